```python
import jax
import jax.numpy as jnp
from jax import lax
import numpy as np

D_MODEL = 1024
BATCH = 8
SEQ = 4096
DEPTH = 4

CTX_LEN = 256
GRID_W = 64
N_EVEN = (DEPTH + 1) // 2
N_ODD = DEPTH // 2
EPS = 1e-6

RET_HEADS = 4
RET_HEAD_DIM = D_MODEL // 8
RET_W = RET_HEADS * RET_HEAD_DIM
RET_CHUNK = 128
ROPE_BASE = 10000.0
ROPE_PAIRS = (RET_HEAD_DIM // 8, 3 * RET_HEAD_DIM // 16, 3 * RET_HEAD_DIM // 16)
CONV_CH = D_MODEL // 2
CONV_K = 31
EVEN_IN = 4 * RET_W + 2 * CONV_CH
EVEN_MIX = RET_W + CONV_CH
POOL_CH = D_MODEL // 2
POOL_WINDOWS = (2, 4, 8, 16)
POOL_GROUPS = len(POOL_WINDOWS)
POOL_GC = POOL_CH // POOL_GROUPS
SG_CH = D_MODEL // 2
SG_GROUPS = 4
SG_GC = SG_CH // SG_GROUPS
SG_CHUNK = 128
ODD_IN = POOL_CH + 2 * SG_CH
ODD_MIX = POOL_CH + SG_CH
D_FF = ((8 * D_MODEL // 3 + 255) // 256) * 256

kernel_name = 'hybrid_retention_conformer_pool_gmlp_prefix_dit'


def rms_norm(x, w):
    x32 = x.astype(jnp.float32)
    y = x32 * lax.rsqrt(jnp.mean(x32 * x32, axis=-1, keepdims=True) + EPS)
    return (y * w.astype(jnp.float32)).astype(x.dtype)


def layer_norm(x, w, b):
    x32 = x.astype(jnp.float32)
    mu = jnp.mean(x32, axis=-1, keepdims=True)
    xc = x32 - mu
    y = xc * lax.rsqrt(jnp.mean(xc * xc, axis=-1, keepdims=True) + EPS)
    return (y * w.astype(jnp.float32) + b.astype(jnp.float32)).astype(x.dtype)


def modulate(h, shift, scale):
    return h * (1.0 + scale) + shift


def swiglu(h, w_gate, w_up, w_down):
    return (jax.nn.silu(h @ w_gate) * (h @ w_up)) @ w_down


def rope_angles(p_seq, p_row, p_col):
    parts = []
    for p, n in zip((p_seq, p_row, p_col), ROPE_PAIRS):
        freq = ROPE_BASE ** (-jnp.arange(n, dtype=jnp.float32) / n)
        parts.append(p[:, None] * freq[None, :])
    return jnp.concatenate(parts, axis=-1)


def apply_rope(t, ang):
    t32 = t.astype(jnp.float32)
    half = t32.shape[-1] // 2
    t1, t2 = t32[..., :half], t32[..., half:]
    cos, sin = jnp.cos(ang), jnp.sin(ang)
    return jnp.concatenate([t1 * cos - t2 * sin, t1 * sin + t2 * cos], axis=-1)


def split_heads(t):
    b, l, _ = t.shape
    return t.reshape(b, l, RET_HEADS, RET_HEAD_DIM).transpose(0, 2, 1, 3)


def retention_chunkwise(q, k, v, log_g, s0, inclusive):
    b, h, l, dk = q.shape
    dv = v.shape[-1]
    n = l // RET_CHUNK
    qc = q.reshape(b, h, n, RET_CHUNK, dk)
    kc = k.reshape(b, h, n, RET_CHUNK, dk)
    vc = v.reshape(b, h, n, RET_CHUNK, dv)
    pos = jnp.arange(RET_CHUNK, dtype=jnp.float32)
    diff = pos[:, None] - pos[None, :]
    if inclusive:
        mask = diff >= 0
        expo = diff
        xi_exp = pos + 1.0
    else:
        mask = diff > 0
        expo = diff - 1.0
        xi_exp = pos
    lg = log_g[:, None, None]
    dmat = jnp.where(mask[None], jnp.exp(lg * jnp.where(mask, expo, 0.0)[None]), 0.0)
    xi = jnp.exp(log_g[:, None] * xi_exp[None, :])
    zeta = jnp.exp(log_g[:, None] * (RET_CHUNK - 1.0 - pos)[None, :])
    chunk_decay = jnp.exp(log_g * RET_CHUNK)[None, :, None, None]
    scores = jnp.einsum('bhncd,bhnmd->bhncm', qc, kc) * dmat[None, :, None]
    intra = jnp.einsum('bhncm,bhnme->bhnce', scores, vc)
    kv = jnp.einsum('bhnmd,hm,bhnme->bhnde', kc, zeta, vc)

    def step(s, kv_n):
        return s * chunk_decay + kv_n, s

    _, s_prev = lax.scan(step, s0, jnp.moveaxis(kv, 2, 0))
    inter = jnp.einsum('bhncd,hc,nbhde->bhnce', qc, xi, s_prev)
    return (intra + inter).reshape(b, h, l, dv)


def retention_final_states(k, v, log_g2):
    l = k.shape[2]
    pos = jnp.arange(l, dtype=jnp.float32)
    w_f = jnp.exp(log_g2[0][:, None] * (l - 1.0 - pos)[None, :])
    w_b = jnp.exp(log_g2[1][:, None] * pos[None, :])
    s_f = jnp.einsum('bhld,hl,bhle->bhde', k, w_f, v)
    s_b = jnp.einsum('bhld,hl,bhle->bhde', k, w_b, v)
    return s_f, s_b


def bidir_retention(q, k, v, log_g2, s_f, s_b):
    flip = lambda t: jnp.flip(t, axis=2)
    fwd = retention_chunkwise(q, k, v, log_g2[0], s_f, True)
    bwd = flip(retention_chunkwise(flip(q), flip(k), flip(v), log_g2[1], s_b, False))
    return fwd + bwd


def retention_kv(h, w_in, ang):
    k = apply_rope(split_heads(h @ w_in[:, RET_W:2 * RET_W]), ang)
    v = split_heads(h @ w_in[:, 2 * RET_W:3 * RET_W]).astype(jnp.float32)
    return k, v


def even_project(h, w_in, ang):
    p = h @ w_in
    q, k, v, g, a, gb = jnp.split(p, [RET_W, 2 * RET_W, 3 * RET_W, 4 * RET_W, 4 * RET_W + CONV_CH], axis=-1)
    q = apply_rope(split_heads(q), ang) * (RET_HEAD_DIM ** -0.5)
    k = apply_rope(split_heads(k), ang)
    v = split_heads(v).astype(jnp.float32)
    return q, k, v, g, a * jax.nn.sigmoid(gb)


def depthwise_conv(u, w):
    return lax.conv_general_dilated(
        u, w.astype(u.dtype)[:, None, :], window_strides=(1,),
        padding=[(CONV_K // 2, CONV_K // 2)],
        dimension_numbers=('NWC', 'WIO', 'NWC'), feature_group_count=u.shape[-1])


def even_mix(q, k, v, g, u, s_f, s_b, log_g2, conv_w, ln_w, ln_b, w_out):
    b, _, l, _ = q.shape
    y = bidir_retention(q, k, v, log_g2, s_f, s_b)
    y = y * lax.rsqrt(jnp.mean(y * y, axis=-1, keepdims=True) + EPS)
    y = y.transpose(0, 2, 1, 3).reshape(b, l, RET_W).astype(g.dtype)
    ret_out = jax.nn.silu(g) * y
    conv_out = jax.nn.silu(layer_norm(depthwise_conv(u, conv_w), ln_w, ln_b))
    return jnp.concatenate([ret_out, conv_out], axis=-1) @ w_out


def pool_minus_token(h32):
    b, l, _ = h32.shape
    cs = jnp.concatenate([jnp.zeros((b, 1, POOL_CH), jnp.float32), lax.cumsum(h32, axis=1)], axis=1)
    t = jnp.arange(l)
    outs = []
    for gi, w in enumerate(POOL_WINDOWS):
        left = w // 2
        right = w - 1 - left
        lo = jnp.clip(t - left, 0, l - 1)
        hi = jnp.clip(t + right, 0, l - 1)
        csg = cs[..., gi * POOL_GC:(gi + 1) * POOL_GC]
        mean = (csg[:, hi + 1] - csg[:, lo]) / (hi - lo + 1).astype(jnp.float32)[None, :, None]
        outs.append(mean - h32[..., gi * POOL_GC:(gi + 1) * POOL_GC])
    return jnp.stack(outs, axis=2)


def odd_stream(h, w_in, w_out, pool_w, pool_scale, sg_ln_w, sg_ln_b, sg_w, sg_b):
    b, l, _ = h.shape
    p = h @ w_in
    pc, pd = p[..., :POOL_CH], p[..., POOL_CH:]
    m = pool_minus_token(pc.astype(jnp.float32))
    pool_out = jnp.einsum('blgc,gcd->blgd', m, pool_w.astype(jnp.float32)).reshape(b, l, POOL_CH)
    pool_out = pool_out.astype(h.dtype) * pool_scale
    z = jax.nn.gelu(pd, approximate=False)
    u, v = jnp.split(z, 2, axis=-1)
    v = layer_norm(v, sg_ln_w, sg_ln_b).reshape(b, l // SG_CHUNK, SG_CHUNK, SG_GROUPS, SG_GC)
    s = jnp.einsum('bnpgc,gqp->bnqgc', v, sg_w) + sg_b.T[None, None, :, :, None]
    sg_out = u * s.reshape(b, l, SG_CH)
    return jnp.concatenate([pool_out, sg_out], axis=-1) @ w_out


def _fwd_setup_inputs(seed: int = 0) -> dict:
    key = jax.random.key(seed)
    ks = jax.random.split(key, 25)
    f32 = jnp.float32
    nrm = lambda k, shape, s: jax.random.normal(k, shape, f32) * s
    gamma0 = 1.0 - 2.0 ** (-5.0 - np.arange(RET_HEADS))
    decay_base = jnp.asarray(np.log(gamma0 / (1.0 - gamma0)), f32)
    return {
        'x': nrm(ks[0], (BATCH, SEQ, D_MODEL), 1.0),
        'c': nrm(ks[1], (BATCH, D_MODEL), 1.0),
        'ctx': nrm(ks[2], (BATCH, CTX_LEN, D_MODEL), 1.0),
        'c_ctx': nrm(ks[3], (D_MODEL,), 1.0),
        'ada_w': nrm(ks[4], (DEPTH, D_MODEL, 6 * D_MODEL), 0.5 * D_MODEL ** -0.5),
        'ada_b': nrm(ks[5], (DEPTH, 6 * D_MODEL), 0.02),
        'norm_w': 1.0 + nrm(ks[6], (DEPTH, 2, D_MODEL), 0.02),
        'even_w_in': nrm(ks[7], (N_EVEN, D_MODEL, EVEN_IN), D_MODEL ** -0.5),
        'even_w_out': nrm(ks[8], (N_EVEN, EVEN_MIX, D_MODEL), EVEN_MIX ** -0.5),
        'ret_decay_logit': decay_base + nrm(ks[9], (N_EVEN, 2, RET_HEADS), 0.1),
        'conv_dw_w': nrm(ks[10], (N_EVEN, CONV_K, CONV_CH), CONV_K ** -0.5),
        'conv_ln_w': 1.0 + nrm(ks[11], (N_EVEN, CONV_CH), 0.02),
        'conv_ln_b': nrm(ks[12], (N_EVEN, CONV_CH), 0.02),
        'odd_w_in': nrm(ks[13], (N_ODD, D_MODEL, ODD_IN), D_MODEL ** -0.5),
        'odd_w_out': nrm(ks[14], (N_ODD, ODD_MIX, D_MODEL), ODD_MIX ** -0.5),
        'pool_w': nrm(ks[15], (N_ODD, POOL_GROUPS, POOL_GC, POOL_GC), POOL_GC ** -0.5),
        'pool_scale': 1.0 + nrm(ks[16], (N_ODD, POOL_CH), 0.02),
        'sg_ln_w': 1.0 + nrm(ks[17], (N_ODD, SG_CH), 0.02),
        'sg_ln_b': nrm(ks[18], (N_ODD, SG_CH), 0.02),
        'sg_w': nrm(ks[19], (N_ODD, SG_GROUPS, SG_CHUNK, SG_CHUNK), 0.5 * SG_CHUNK ** -0.5),
        'sg_b': 1.0 + nrm(ks[20], (N_ODD, SG_GROUPS, SG_CHUNK), 0.02),
        'ffn_w_gate': nrm(ks[21], (DEPTH, D_MODEL, D_FF), D_MODEL ** -0.5),
        'ffn_w_up': nrm(ks[22], (DEPTH, D_MODEL, D_FF), D_MODEL ** -0.5),
        'ffn_w_down': nrm(ks[23], (DEPTH, D_FF, D_MODEL), D_FF ** -0.5),
        'final_norm_w': 1.0 + nrm(ks[24], (D_MODEL,), 0.02),
    }


def _fwd_reference(x, c, ctx, c_ctx, ada_w, ada_b, norm_w, even_w_in, even_w_out, ret_decay_logit,
              conv_dw_w, conv_ln_w, conv_ln_b, odd_w_in, odd_w_out, pool_w, pool_scale,
              sg_ln_w, sg_ln_b, sg_w, sg_b, ffn_w_gate, ffn_w_up, ffn_w_down, final_norm_w):
    b, l, d = x.shape
    lc = ctx.shape[1]
    rows = l // GRID_W
    grid_r = jnp.broadcast_to(jnp.arange(rows, dtype=jnp.float32)[:, None], (rows, GRID_W)).reshape(-1)
    grid_c = jnp.broadcast_to(jnp.arange(GRID_W, dtype=jnp.float32)[None, :], (rows, GRID_W)).reshape(-1)
    ang_x = rope_angles(jnp.full((l,), lc, jnp.float32), grid_r, grid_c)
    zeros_c = jnp.zeros((lc,), jnp.float32)
    ang_c = rope_angles(jnp.arange(lc, dtype=jnp.float32), zeros_c, zeros_c)
    silu_c = jax.nn.silu(c)
    silu_cc = jax.nn.silu(c_ctx)

    for i in range(DEPTH):
        j = i // 2
        even = i % 2 == 0
        ctx_after = any(m % 2 == 0 for m in range(i + 1, DEPTH))
        use_ctx = even or ctx_after
        mod_x = (silu_c @ ada_w[i] + ada_b[i]).reshape(b, 6, 1, d)
        hx = modulate(rms_norm(x, norm_w[i, 0]), mod_x[:, 0], mod_x[:, 1])
        if use_ctx:
            mod_c = (silu_cc @ ada_w[i] + ada_b[i]).reshape(6, 1, d)
            hc = modulate(rms_norm(ctx, norm_w[i, 0]), mod_c[0], mod_c[1])
        if even:
            log_g2 = jax.nn.log_sigmoid(ret_decay_logit[j].astype(jnp.float32))
            if ctx_after:
                parts_c = even_project(hc, even_w_in[j], ang_c)
                kc, vc = parts_c[1], parts_c[2]
            else:
                kc, vc = retention_kv(hc, even_w_in[j], ang_c)
            s_f, s_b = retention_final_states(kc, vc, log_g2)
            yx = even_mix(*even_project(hx, even_w_in[j], ang_x), s_f, s_b, log_g2,
                          conv_dw_w[j], conv_ln_w[j], conv_ln_b[j], even_w_out[j])
            if ctx_after:
                s_zero = jnp.zeros_like(s_f)
                yc = even_mix(*parts_c, s_zero, s_zero, log_g2,
                              conv_dw_w[j], conv_ln_w[j], conv_ln_b[j], even_w_out[j])
        else:
            yx = odd_stream(hx, odd_w_in[j], odd_w_out[j], pool_w[j], pool_scale[j],
                            sg_ln_w[j], sg_ln_b[j], sg_w[j], sg_b[j])
            if ctx_after:
                yc = odd_stream(hc, odd_w_in[j], odd_w_out[j], pool_w[j], pool_scale[j],
                                sg_ln_w[j], sg_ln_b[j], sg_w[j], sg_b[j])
        x = x + mod_x[:, 2] * yx
        x = x + mod_x[:, 5] * swiglu(modulate(rms_norm(x, norm_w[i, 1]), mod_x[:, 3], mod_x[:, 4]),
                                     ffn_w_gate[i], ffn_w_up[i], ffn_w_down[i])
        if ctx_after:
            ctx = ctx + mod_c[2] * yc
            ctx = ctx + mod_c[5] * swiglu(modulate(rms_norm(ctx, norm_w[i, 1]), mod_c[3], mod_c[4]),
                                          ffn_w_gate[i], ffn_w_up[i], ffn_w_down[i])
    return rms_norm(x, final_norm_w)


import jax as _jax
import jax.numpy as _jnp

TWIN_FORMAT = 'train_step'
FWD_PARAMS = ['x', 'c', 'ctx', 'c_ctx', 'ada_w', 'ada_b', 'norm_w', 'even_w_in', 'even_w_out', 'ret_decay_logit', 'conv_dw_w', 'conv_ln_w', 'conv_ln_b', 'odd_w_in', 'odd_w_out', 'pool_w', 'pool_scale', 'sg_ln_w', 'sg_ln_b', 'sg_w', 'sg_b', 'ffn_w_gate', 'ffn_w_up', 'ffn_w_down', 'final_norm_w']
TWIN_WEIGHTS = ['c_ctx', 'ada_w', 'ada_b', 'norm_w', 'even_w_in', 'even_w_out', 'ret_decay_logit', 'conv_dw_w', 'conv_ln_w', 'conv_ln_b', 'odd_w_in', 'odd_w_out', 'pool_w', 'pool_scale', 'sg_ln_w', 'sg_ln_b', 'sg_w', 'sg_b', 'ffn_w_gate', 'ffn_w_up', 'ffn_w_down', 'final_norm_w']
TWIN_DIFF_INPUT = 'x'
TWIN_INPUTS = ['x', 'c', 'ctx', 'c_ctx', 'ada_w', 'ada_b', 'norm_w', 'even_w_in', 'even_w_out', 'ret_decay_logit', 'conv_dw_w', 'conv_ln_w', 'conv_ln_b', 'odd_w_in', 'odd_w_out', 'pool_w', 'pool_scale', 'sg_ln_w', 'sg_ln_b', 'sg_w', 'sg_b', 'ffn_w_gate', 'ffn_w_up', 'ffn_w_down', 'final_norm_w', 'loss_target', 'm_c_ctx', 'm_ada_w', 'm_ada_b', 'm_norm_w', 'm_even_w_in', 'm_even_w_out', 'm_ret_decay_logit', 'm_conv_dw_w', 'm_conv_ln_w', 'm_conv_ln_b', 'm_odd_w_in', 'm_odd_w_out', 'm_pool_w', 'm_pool_scale', 'm_sg_ln_w', 'm_sg_ln_b', 'm_sg_w', 'm_sg_b', 'm_ffn_w_gate', 'm_ffn_w_up', 'm_ffn_w_down', 'm_final_norm_w', 'v_c_ctx', 'v_ada_w', 'v_ada_b', 'v_norm_w', 'v_even_w_in', 'v_even_w_out', 'v_ret_decay_logit', 'v_conv_dw_w', 'v_conv_ln_w', 'v_conv_ln_b', 'v_odd_w_in', 'v_odd_w_out', 'v_pool_w', 'v_pool_scale', 'v_sg_ln_w', 'v_sg_ln_b', 'v_sg_w', 'v_sg_b', 'v_ffn_w_gate', 'v_ffn_w_up', 'v_ffn_w_down', 'v_final_norm_w']
TWIN_OUTPUTS = ['loss', 'grad_x', 'grad_c_ctx', 'grad_ada_w', 'grad_ada_b', 'grad_norm_w', 'grad_even_w_in', 'grad_even_w_out', 'grad_ret_decay_logit', 'grad_conv_dw_w', 'grad_conv_ln_w', 'grad_conv_ln_b', 'grad_odd_w_in', 'grad_odd_w_out', 'grad_pool_w', 'grad_pool_scale', 'grad_sg_ln_w', 'grad_sg_ln_b', 'grad_sg_w', 'grad_sg_b', 'grad_ffn_w_gate', 'grad_ffn_w_up', 'grad_ffn_w_down', 'grad_final_norm_w', 'delta_c_ctx', 'delta_ada_w', 'delta_ada_b', 'delta_norm_w', 'delta_even_w_in', 'delta_even_w_out', 'delta_ret_decay_logit', 'delta_conv_dw_w', 'delta_conv_ln_w', 'delta_conv_ln_b', 'delta_odd_w_in', 'delta_odd_w_out', 'delta_pool_w', 'delta_pool_scale', 'delta_sg_ln_w', 'delta_sg_ln_b', 'delta_sg_w', 'delta_sg_b', 'delta_ffn_w_gate', 'delta_ffn_w_up', 'delta_ffn_w_down', 'delta_final_norm_w', 'new_m_c_ctx', 'new_m_ada_w', 'new_m_ada_b', 'new_m_norm_w', 'new_m_even_w_in', 'new_m_even_w_out', 'new_m_ret_decay_logit', 'new_m_conv_dw_w', 'new_m_conv_ln_w', 'new_m_conv_ln_b', 'new_m_odd_w_in', 'new_m_odd_w_out', 'new_m_pool_w', 'new_m_pool_scale', 'new_m_sg_ln_w', 'new_m_sg_ln_b', 'new_m_sg_w', 'new_m_sg_b', 'new_m_ffn_w_gate', 'new_m_ffn_w_up', 'new_m_ffn_w_down', 'new_m_final_norm_w', 'new_v_c_ctx', 'new_v_ada_w', 'new_v_ada_b', 'new_v_norm_w', 'new_v_even_w_in', 'new_v_even_w_out', 'new_v_ret_decay_logit', 'new_v_conv_dw_w', 'new_v_conv_ln_w', 'new_v_conv_ln_b', 'new_v_odd_w_in', 'new_v_odd_w_out', 'new_v_pool_w', 'new_v_pool_scale', 'new_v_sg_ln_w', 'new_v_sg_ln_b', 'new_v_sg_w', 'new_v_sg_b', 'new_v_ffn_w_gate', 'new_v_ffn_w_up', 'new_v_ffn_w_down', 'new_v_final_norm_w']
TWIN_LEAF_KINDS = {'loss': 'loss', 'grad_x': 'grad_x', 'grad_c_ctx': 'grad_w', 'grad_ada_w': 'grad_w', 'grad_ada_b': 'grad_w', 'grad_norm_w': 'grad_w', 'grad_even_w_in': 'grad_w', 'grad_even_w_out': 'grad_w', 'grad_ret_decay_logit': 'grad_w', 'grad_conv_dw_w': 'grad_w', 'grad_conv_ln_w': 'grad_w', 'grad_conv_ln_b': 'grad_w', 'grad_odd_w_in': 'grad_w', 'grad_odd_w_out': 'grad_w', 'grad_pool_w': 'grad_w', 'grad_pool_scale': 'grad_w', 'grad_sg_ln_w': 'grad_w', 'grad_sg_ln_b': 'grad_w', 'grad_sg_w': 'grad_w', 'grad_sg_b': 'grad_w', 'grad_ffn_w_gate': 'grad_w', 'grad_ffn_w_up': 'grad_w', 'grad_ffn_w_down': 'grad_w', 'grad_final_norm_w': 'grad_w', 'delta_c_ctx': 'delta_w', 'delta_ada_w': 'delta_w', 'delta_ada_b': 'delta_w', 'delta_norm_w': 'delta_w', 'delta_even_w_in': 'delta_w', 'delta_even_w_out': 'delta_w', 'delta_ret_decay_logit': 'delta_w', 'delta_conv_dw_w': 'delta_w', 'delta_conv_ln_w': 'delta_w', 'delta_conv_ln_b': 'delta_w', 'delta_odd_w_in': 'delta_w', 'delta_odd_w_out': 'delta_w', 'delta_pool_w': 'delta_w', 'delta_pool_scale': 'delta_w', 'delta_sg_ln_w': 'delta_w', 'delta_sg_ln_b': 'delta_w', 'delta_sg_w': 'delta_w', 'delta_sg_b': 'delta_w', 'delta_ffn_w_gate': 'delta_w', 'delta_ffn_w_up': 'delta_w', 'delta_ffn_w_down': 'delta_w', 'delta_final_norm_w': 'delta_w', 'new_m_c_ctx': 'new_m', 'new_m_ada_w': 'new_m', 'new_m_ada_b': 'new_m', 'new_m_norm_w': 'new_m', 'new_m_even_w_in': 'new_m', 'new_m_even_w_out': 'new_m', 'new_m_ret_decay_logit': 'new_m', 'new_m_conv_dw_w': 'new_m', 'new_m_conv_ln_w': 'new_m', 'new_m_conv_ln_b': 'new_m', 'new_m_odd_w_in': 'new_m', 'new_m_odd_w_out': 'new_m', 'new_m_pool_w': 'new_m', 'new_m_pool_scale': 'new_m', 'new_m_sg_ln_w': 'new_m', 'new_m_sg_ln_b': 'new_m', 'new_m_sg_w': 'new_m', 'new_m_sg_b': 'new_m', 'new_m_ffn_w_gate': 'new_m', 'new_m_ffn_w_up': 'new_m', 'new_m_ffn_w_down': 'new_m', 'new_m_final_norm_w': 'new_m', 'new_v_c_ctx': 'new_v', 'new_v_ada_w': 'new_v', 'new_v_ada_b': 'new_v', 'new_v_norm_w': 'new_v', 'new_v_even_w_in': 'new_v', 'new_v_even_w_out': 'new_v', 'new_v_ret_decay_logit': 'new_v', 'new_v_conv_dw_w': 'new_v', 'new_v_conv_ln_w': 'new_v', 'new_v_conv_ln_b': 'new_v', 'new_v_odd_w_in': 'new_v', 'new_v_odd_w_out': 'new_v', 'new_v_pool_w': 'new_v', 'new_v_pool_scale': 'new_v', 'new_v_sg_ln_w': 'new_v', 'new_v_sg_ln_b': 'new_v', 'new_v_sg_w': 'new_v', 'new_v_sg_b': 'new_v', 'new_v_ffn_w_gate': 'new_v', 'new_v_ffn_w_up': 'new_v', 'new_v_ffn_w_down': 'new_v', 'new_v_final_norm_w': 'new_v'}


def _forward(args):
    return _fwd_reference(*[args[k] for k in FWD_PARAMS])


def _output_shape():
    def fwd():
        inp = _fwd_setup_inputs(0)
        return _fwd_reference(*[inp[k] for k in FWD_PARAMS])
    out = _jax.eval_shape(fwd)
    return out.shape, out.dtype

N_MICROBATCH = 1
ADAM_LR = 0.001
ADAM_B1 = 0.9
ADAM_B2 = 0.999
ADAM_EPS = 1e-08
ADAM_WD = 0.01
ADAM_STEP = 10
PER_EXAMPLE_BATCH_AXIS = {'x': 0, 'c': 0, 'ctx': 0, 'loss_target': 0}
SHARED_INPUTS = []
_WEIGHT_DTYPES = {'c_ctx': _jnp.float32, 'ada_w': _jnp.float32, 'ada_b': _jnp.float32, 'norm_w': _jnp.float32, 'even_w_in': _jnp.float32, 'even_w_out': _jnp.float32, 'ret_decay_logit': _jnp.float32, 'conv_dw_w': _jnp.float32, 'conv_ln_w': _jnp.float32, 'conv_ln_b': _jnp.float32, 'odd_w_in': _jnp.float32, 'odd_w_out': _jnp.float32, 'pool_w': _jnp.float32, 'pool_scale': _jnp.float32, 'sg_ln_w': _jnp.float32, 'sg_ln_b': _jnp.float32, 'sg_w': _jnp.float32, 'sg_b': _jnp.float32, 'ffn_w_gate': _jnp.float32, 'ffn_w_up': _jnp.float32, 'ffn_w_down': _jnp.float32, 'final_norm_w': _jnp.float32}
MOMENT_SCALE = {'c_ctx': 2.012575e-02, 'ada_w': 5.326389e-02, 'ada_b': 9.032365e-02, 'norm_w': 4.948944e-02, 'even_w_in': 3.533313e-02, 'even_w_out': 3.420564e-02, 'ret_decay_logit': 1.695773e-01, 'conv_dw_w': 3.384688e-02, 'conv_ln_w': 4.059643e-02, 'conv_ln_b': 3.624005e-02, 'odd_w_in': 3.765840e-02, 'odd_w_out': 4.353587e-02, 'pool_w': 4.560640e-02, 'pool_scale': 4.706844e-02, 'sg_ln_w': 1.791493e-02, 'sg_ln_b': 1.753848e-02, 'sg_w': 3.578766e-02, 'sg_b': 3.699494e-02, 'ffn_w_gate': 2.246150e-02, 'ffn_w_up': 2.173751e-02, 'ffn_w_down': 3.611630e-02, 'final_norm_w': 3.206666e+01}


def _to_microbatches(a, axis):
    t = _jnp.moveaxis(a, axis, 0)
    t = t.reshape((N_MICROBATCH, t.shape[0] // N_MICROBATCH) + t.shape[1:])
    return _jnp.moveaxis(t, 1, axis + 1)


def setup_inputs(seed: int = 0) -> dict:
    inp = _fwd_setup_inputs(seed)
    key = _jax.random.fold_in(_jax.random.key(seed), 7919)
    shape, _ = _output_shape()
    out = dict(inp)
    out["loss_target"] = _jax.random.normal(_jax.random.fold_in(key, 0), shape, _jnp.float32)
    for i, name in enumerate(TWIN_WEIGHTS):
        w = inp[name].astype(_jnp.float32)
        if MOMENT_SCALE is None:
            s = _jnp.sqrt(_jnp.mean(_jnp.square(w)) + 1e-30)
        else:
            s = MOMENT_SCALE[name]
        km, kv = _jax.random.split(_jax.random.fold_in(key, i + 1))
        out[name] = w
        out["m_" + name] = s * _jax.random.normal(km, w.shape, _jnp.float32)
        out["v_" + name] = (s * s) * _jax.random.uniform(kv, w.shape, _jnp.float32, 0.5, 1.5)
    if N_MICROBATCH > 1:
        for name, axis in PER_EXAMPLE_BATCH_AXIS.items():
            out[name] = _to_microbatches(out[name], axis)
    return {'x': out['x'], 'c': out['c'], 'ctx': out['ctx'], 'c_ctx': out['c_ctx'], 'ada_w': out['ada_w'], 'ada_b': out['ada_b'], 'norm_w': out['norm_w'], 'even_w_in': out['even_w_in'], 'even_w_out': out['even_w_out'], 'ret_decay_logit': out['ret_decay_logit'], 'conv_dw_w': out['conv_dw_w'], 'conv_ln_w': out['conv_ln_w'], 'conv_ln_b': out['conv_ln_b'], 'odd_w_in': out['odd_w_in'], 'odd_w_out': out['odd_w_out'], 'pool_w': out['pool_w'], 'pool_scale': out['pool_scale'], 'sg_ln_w': out['sg_ln_w'], 'sg_ln_b': out['sg_ln_b'], 'sg_w': out['sg_w'], 'sg_b': out['sg_b'], 'ffn_w_gate': out['ffn_w_gate'], 'ffn_w_up': out['ffn_w_up'], 'ffn_w_down': out['ffn_w_down'], 'final_norm_w': out['final_norm_w'], 'loss_target': out['loss_target'], 'm_c_ctx': out['m_c_ctx'], 'm_ada_w': out['m_ada_w'], 'm_ada_b': out['m_ada_b'], 'm_norm_w': out['m_norm_w'], 'm_even_w_in': out['m_even_w_in'], 'm_even_w_out': out['m_even_w_out'], 'm_ret_decay_logit': out['m_ret_decay_logit'], 'm_conv_dw_w': out['m_conv_dw_w'], 'm_conv_ln_w': out['m_conv_ln_w'], 'm_conv_ln_b': out['m_conv_ln_b'], 'm_odd_w_in': out['m_odd_w_in'], 'm_odd_w_out': out['m_odd_w_out'], 'm_pool_w': out['m_pool_w'], 'm_pool_scale': out['m_pool_scale'], 'm_sg_ln_w': out['m_sg_ln_w'], 'm_sg_ln_b': out['m_sg_ln_b'], 'm_sg_w': out['m_sg_w'], 'm_sg_b': out['m_sg_b'], 'm_ffn_w_gate': out['m_ffn_w_gate'], 'm_ffn_w_up': out['m_ffn_w_up'], 'm_ffn_w_down': out['m_ffn_w_down'], 'm_final_norm_w': out['m_final_norm_w'], 'v_c_ctx': out['v_c_ctx'], 'v_ada_w': out['v_ada_w'], 'v_ada_b': out['v_ada_b'], 'v_norm_w': out['v_norm_w'], 'v_even_w_in': out['v_even_w_in'], 'v_even_w_out': out['v_even_w_out'], 'v_ret_decay_logit': out['v_ret_decay_logit'], 'v_conv_dw_w': out['v_conv_dw_w'], 'v_conv_ln_w': out['v_conv_ln_w'], 'v_conv_ln_b': out['v_conv_ln_b'], 'v_odd_w_in': out['v_odd_w_in'], 'v_odd_w_out': out['v_odd_w_out'], 'v_pool_w': out['v_pool_w'], 'v_pool_scale': out['v_pool_scale'], 'v_sg_ln_w': out['v_sg_ln_w'], 'v_sg_ln_b': out['v_sg_ln_b'], 'v_sg_w': out['v_sg_w'], 'v_sg_b': out['v_sg_b'], 'v_ffn_w_gate': out['v_ffn_w_gate'], 'v_ffn_w_up': out['v_ffn_w_up'], 'v_ffn_w_down': out['v_ffn_w_down'], 'v_final_norm_w': out['v_final_norm_w']}


def _loss(weights, diff, rest, loss_target):
    with _jax.named_scope("forward"):
        args = {**rest, TWIN_DIFF_INPUT: diff, **{k: w.astype(_WEIGHT_DTYPES[k]) for k, w in weights.items()}}
        y = _forward(args)
    with _jax.named_scope("loss_head"):
        err = _jnp.square(y.astype(_jnp.float32) - loss_target)
        return 0.5 * _jnp.sum(_jnp.mean(err, axis=-1)) if err.ndim else 0.5 * err


def _adamw(w, g, m, v):
    m = ADAM_B1 * m + (1.0 - ADAM_B1) * g
    v = ADAM_B2 * v + (1.0 - ADAM_B2) * _jnp.square(g)
    m_hat = m / (1.0 - ADAM_B1 ** ADAM_STEP)
    v_hat = v / (1.0 - ADAM_B2 ** ADAM_STEP)
    delta = -ADAM_LR * (m_hat / (_jnp.sqrt(v_hat) + ADAM_EPS) + ADAM_WD * w)
    return delta, m, v


def reference(x, c, ctx, c_ctx, ada_w, ada_b, norm_w, even_w_in, even_w_out, ret_decay_logit, conv_dw_w, conv_ln_w, conv_ln_b, odd_w_in, odd_w_out, pool_w, pool_scale, sg_ln_w, sg_ln_b, sg_w, sg_b, ffn_w_gate, ffn_w_up, ffn_w_down, final_norm_w, loss_target, m_c_ctx, m_ada_w, m_ada_b, m_norm_w, m_even_w_in, m_even_w_out, m_ret_decay_logit, m_conv_dw_w, m_conv_ln_w, m_conv_ln_b, m_odd_w_in, m_odd_w_out, m_pool_w, m_pool_scale, m_sg_ln_w, m_sg_ln_b, m_sg_w, m_sg_b, m_ffn_w_gate, m_ffn_w_up, m_ffn_w_down, m_final_norm_w, v_c_ctx, v_ada_w, v_ada_b, v_norm_w, v_even_w_in, v_even_w_out, v_ret_decay_logit, v_conv_dw_w, v_conv_ln_w, v_conv_ln_b, v_odd_w_in, v_odd_w_out, v_pool_w, v_pool_scale, v_sg_ln_w, v_sg_ln_b, v_sg_w, v_sg_b, v_ffn_w_gate, v_ffn_w_up, v_ffn_w_down, v_final_norm_w):
    given = dict(x=x, c=c, ctx=ctx, c_ctx=c_ctx, ada_w=ada_w, ada_b=ada_b, norm_w=norm_w, even_w_in=even_w_in, even_w_out=even_w_out, ret_decay_logit=ret_decay_logit, conv_dw_w=conv_dw_w, conv_ln_w=conv_ln_w, conv_ln_b=conv_ln_b, odd_w_in=odd_w_in, odd_w_out=odd_w_out, pool_w=pool_w, pool_scale=pool_scale, sg_ln_w=sg_ln_w, sg_ln_b=sg_ln_b, sg_w=sg_w, sg_b=sg_b, ffn_w_gate=ffn_w_gate, ffn_w_up=ffn_w_up, ffn_w_down=ffn_w_down, final_norm_w=final_norm_w, loss_target=loss_target, m_c_ctx=m_c_ctx, m_ada_w=m_ada_w, m_ada_b=m_ada_b, m_norm_w=m_norm_w, m_even_w_in=m_even_w_in, m_even_w_out=m_even_w_out, m_ret_decay_logit=m_ret_decay_logit, m_conv_dw_w=m_conv_dw_w, m_conv_ln_w=m_conv_ln_w, m_conv_ln_b=m_conv_ln_b, m_odd_w_in=m_odd_w_in, m_odd_w_out=m_odd_w_out, m_pool_w=m_pool_w, m_pool_scale=m_pool_scale, m_sg_ln_w=m_sg_ln_w, m_sg_ln_b=m_sg_ln_b, m_sg_w=m_sg_w, m_sg_b=m_sg_b, m_ffn_w_gate=m_ffn_w_gate, m_ffn_w_up=m_ffn_w_up, m_ffn_w_down=m_ffn_w_down, m_final_norm_w=m_final_norm_w, v_c_ctx=v_c_ctx, v_ada_w=v_ada_w, v_ada_b=v_ada_b, v_norm_w=v_norm_w, v_even_w_in=v_even_w_in, v_even_w_out=v_even_w_out, v_ret_decay_logit=v_ret_decay_logit, v_conv_dw_w=v_conv_dw_w, v_conv_ln_w=v_conv_ln_w, v_conv_ln_b=v_conv_ln_b, v_odd_w_in=v_odd_w_in, v_odd_w_out=v_odd_w_out, v_pool_w=v_pool_w, v_pool_scale=v_pool_scale, v_sg_ln_w=v_sg_ln_w, v_sg_ln_b=v_sg_ln_b, v_sg_w=v_sg_w, v_sg_b=v_sg_b, v_ffn_w_gate=v_ffn_w_gate, v_ffn_w_up=v_ffn_w_up, v_ffn_w_down=v_ffn_w_down, v_final_norm_w=v_final_norm_w)
    weights = {n: given[n] for n in TWIN_WEIGHTS}
    shared = {n: given[n] for n in SHARED_INPUTS}
    per_example = {n: given[n] for n in ['x', 'c', 'ctx']}
    grad_fn = _jax.value_and_grad(_loss, argnums=(0, 1))

    def one_microbatch(ex, loss_target):
        ex = dict(ex)
        diff = ex.pop(TWIN_DIFF_INPUT)
        return grad_fn(weights, diff, {**shared, **ex}, loss_target)

    if N_MICROBATCH == 1:
        loss, (grad_w, grad_x) = one_microbatch(per_example, given["loss_target"])
    else:
        def body(carry, xs):
            loss_sum, grad_sum = carry
            l_k, (gw_k, gx_k) = one_microbatch(xs[0], xs[1])
            with _jax.named_scope("update"):
                return (loss_sum + l_k, _jax.tree.map(_jnp.add, grad_sum, gw_k)), gx_k

        init = (_jnp.zeros((), _jnp.float32), _jax.tree.map(_jnp.zeros_like, weights))
        (loss, grad_w), grad_x = _jax.lax.scan(body, init, (per_example, given["loss_target"]))
    with _jax.named_scope("update"):
        delta_w, new_m, new_v = {}, {}, {}
        for n in TWIN_WEIGHTS:
            delta_w[n], new_m[n], new_v[n] = _adamw(weights[n], grad_w[n], given["m_" + n], given["v_" + n])
    return (loss, grad_x, *[grad_w[n] for n in TWIN_WEIGHTS], *[delta_w[n] for n in TWIN_WEIGHTS],
            *[new_m[n] for n in TWIN_WEIGHTS], *[new_v[n] for n in TWIN_WEIGHTS])
```

```python
import math

import jax
import jax.numpy as jnp
from jax import lax
from jax.experimental import pallas as pl
from jax.experimental.pallas import tpu as pltpu

F32 = jnp.float32
BF16 = jnp.bfloat16

N_DEV = 8
D = 1024
DEPTH = 4
ROW_TILE = 256
HALO = 16
LANES = 128
EPS = 1e-6
RET_HEADS = 4
HEAD_DIM = 128
RET_W = RET_HEADS * HEAD_DIM
CHUNK = 128
Q_SCALE = HEAD_DIM ** -0.5
ROPE_BASE = 10000.0
ROPE_PAIRS = (HEAD_DIM // 8, 3 * HEAD_DIM // 16, 3 * HEAD_DIM // 16)
GRID_W = 64
CONV_CH = 512
CONV_K = 31
EVEN_IN = 4 * RET_W + 2 * CONV_CH
POOL_CH = 512
POOL_WINDOWS = (2, 4, 8, 16)
GROUP_CH = 128
SG_CH = 512
SG_CHUNK = 128
ODD_IN = POOL_CH + 2 * SG_CH
D_FF = 2816
INV_SQRT2 = 1.0 / math.sqrt(2.0)
INV_SQRT_2PI = 1.0 / math.sqrt(2.0 * math.pi)
ADAM_LR, ADAM_B1, ADAM_B2, ADAM_EPS, ADAM_WD, ADAM_STEP = 0.001, 0.9, 0.999, 1e-08, 0.01, 10
ADAM_BC1 = 1.0 - ADAM_B1 ** ADAM_STEP
ADAM_BC2 = 1.0 - ADAM_B2 ** ADAM_STEP
VMEM_LIMIT = 56 * 1024 * 1024
MESH = pl.DeviceIdType.MESH


def _params(sem=None):
    return pltpu.CompilerParams(dimension_semantics=sem, vmem_limit_bytes=VMEM_LIMIT)


def _dot(a, b):
    return jnp.dot(a, b, preferred_element_type=F32)


def _dot_nt(a, b):
    return lax.dot_general(a, b, (((1,), (1,)), ((), ())), preferred_element_type=F32)


def _dot_tn(a, b):
    return lax.dot_general(a, b, (((0,), (0,)), ((), ())), preferred_element_type=F32)


def _sigmoid(x):
    return 1.0 / (1.0 + jnp.exp(-x))


def _colsum(x):
    return jnp.sum(x, axis=0, keepdims=True)


def _rowmean(x):
    return jnp.mean(x, axis=-1, keepdims=True)


def _row(shape_cols):
    return pl.BlockSpec((ROW_TILE, shape_cols), lambda i: (i, 0))


def _const(shape):
    nd = len(shape)
    return pl.BlockSpec(shape, lambda i: (0,) * nd)


def _mod_spec(nxt):
    return pl.BlockSpec((1, 6, D), lambda i: (i // nxt, 0, 0))


def _stat_spec(nxt):
    return pl.BlockSpec((1, 8, D), lambda i: (i // nxt, 0, 0))


def _norm_mod_fwd(X, nw, mods, si, nxt, name):
    T = X.shape[0]

    def body(x_ref, w_ref, m_ref, h_ref):
        x = x_ref[...]
        r = lax.rsqrt(_rowmean(x * x) + EPS)
        m = m_ref[0]
        y = x * r * w_ref[...]
        h_ref[...] = (y * (1.0 + m[si + 1:si + 2, :]) + m[si:si + 1, :]).astype(BF16)

    return pl.pallas_call(
        body, grid=(T // ROW_TILE,), name=name,
        in_specs=[_row(D), _const((1, D)), _mod_spec(nxt)],
        out_specs=_row(D), out_shape=jax.ShapeDtypeStruct((T, D), BF16),
        compiler_params=_params(("parallel",)),
    )(X, nw, mods)


def _norm_mod_bwd(dh, X, nw, mods, dres, si, nxt, name):
    T = X.shape[0]

    def body(dh_ref, x_ref, w_ref, m_ref, dres_ref, dx_ref, st_ref):
        i = pl.program_id(0)

        @pl.when((i == 0) | (i == nxt))
        def _():
            st_ref[...] = jnp.zeros_like(st_ref)

        x = x_ref[...]
        r = lax.rsqrt(_rowmean(x * x) + EPS)
        xn = x * r
        w = w_ref[...]
        scale = m_ref[0][si + 1:si + 2, :]
        dhv = dh_ref[...]
        dy = dhv * (1.0 + scale)
        dxn = dy * w
        dx_ref[...] = dres_ref[...] + r * (dxn - xn * _rowmean(dxn * xn))
        st_ref[0, 0:1, :] += _colsum(dhv)
        st_ref[0, 1:2, :] += _colsum(dhv * xn * w)
        st_ref[0, 2:3, :] += _colsum(dy * xn)

    return pl.pallas_call(
        body, grid=(T // ROW_TILE,), name=name,
        in_specs=[_row(D), _row(D), _const((1, D)), _mod_spec(nxt), _row(D)],
        out_specs=[_row(D), _stat_spec(nxt)],
        out_shape=[jax.ShapeDtypeStruct((T, D), F32), jax.ShapeDtypeStruct((2, 8, D), F32)],
        compiler_params=_params(("arbitrary",)),
    )(dh, X, nw, mods, dres)


def _resid_bwd(dX, y, mods, gi, nxt, name):
    T = dX.shape[0]

    def body(dx_ref, y_ref, m_ref, dy_ref, st_ref):
        i = pl.program_id(0)

        @pl.when((i == 0) | (i == nxt))
        def _():
            st_ref[...] = jnp.zeros_like(st_ref)

        dx = dx_ref[...]
        dy_ref[...] = (dx * m_ref[0][gi:gi + 1, :]).astype(BF16)
        st_ref[0, 0:1, :] += _colsum(dx * y_ref[...].astype(F32))

    return pl.pallas_call(
        body, grid=(T // ROW_TILE,), name=name,
        in_specs=[_row(D), _row(D), _mod_spec(nxt)],
        out_specs=[_row(D), _stat_spec(nxt)],
        out_shape=[jax.ShapeDtypeStruct((T, D), BF16), jax.ShapeDtypeStruct((2, 8, D), F32)],
        compiler_params=_params(("arbitrary",)),
    )(dX, y, mods)


def _loss_head(X, tgt, fw, nxt, name):
    T = X.shape[0]

    def body(x_ref, t_ref, w_ref, dx_ref, st_ref):
        i = pl.program_id(0)

        @pl.when(i == 0)
        def _():
            st_ref[...] = jnp.zeros_like(st_ref)

        @pl.when(i < nxt)
        def _():
            x = x_ref[...]
            r = lax.rsqrt(_rowmean(x * x) + EPS)
            xn = x * r
            w = w_ref[...]
            err = xn * w - t_ref[...]
            dy = err * (1.0 / D)
            dxn = dy * w
            dx_ref[...] = r * (dxn - xn * _rowmean(dxn * xn))
            st_ref[0:1, :] += _colsum(dy * xn)
            st_ref[1:2, :] += (0.5 / D) * _colsum(err * err)

        @pl.when(i >= nxt)
        def _():
            dx_ref[...] = jnp.zeros_like(dx_ref)

    return pl.pallas_call(
        body, grid=(T // ROW_TILE,), name=name,
        in_specs=[_row(D), pl.BlockSpec((ROW_TILE, D), lambda i: (jnp.minimum(i, nxt - 1), 0)), _const((1, D))],
        out_specs=[_row(D), _const((8, D))],
        out_shape=[jax.ShapeDtypeStruct((T, D), F32), jax.ShapeDtypeStruct((8, D), F32)],
        compiler_params=_params(("arbitrary",)),
    )(X, tgt, fw)


def _mm_nn(A, B, out_dtype, name):
    T, K = A.shape
    N = B.shape[1]

    def body(a_ref, b_ref, o_ref):
        o_ref[...] = _dot(a_ref[...], b_ref[...]).astype(out_dtype)

    return pl.pallas_call(
        body, grid=(T // ROW_TILE,), name=name,
        in_specs=[_row(K), _const((K, N))],
        out_specs=_row(N), out_shape=jax.ShapeDtypeStruct((T, N), out_dtype),
        compiler_params=_params(("parallel",)),
    )(A, B)


def _mm_nn_resid(A, B, X, mods, gi, nxt, name):
    T, K = A.shape
    N = B.shape[1]

    def body(a_ref, b_ref, x_ref, m_ref, xo_ref, y_ref):
        acc = _dot(a_ref[...], b_ref[...])
        y_ref[...] = acc.astype(BF16)
        xo_ref[...] = x_ref[...] + m_ref[0][gi:gi + 1, :] * acc

    return pl.pallas_call(
        body, grid=(T // ROW_TILE,), name=name,
        in_specs=[_row(K), _const((K, N)), _row(N), _mod_spec(nxt)],
        out_specs=[_row(N), _row(N)],
        out_shape=[jax.ShapeDtypeStruct((T, N), F32), jax.ShapeDtypeStruct((T, N), BF16)],
        compiler_params=_params(("parallel",)),
    )(A, B, X, mods)


def _mm_nt(A, B, name, add=None):
    T, N = A.shape
    K = B.shape[0]

    if add is None:
        def body(a_ref, b_ref, o_ref):
            o_ref[...] = _dot_nt(a_ref[...], b_ref[...])
        ins, specs = (A, B), [_row(N), _const((K, N))]
    else:
        def body(a_ref, b_ref, c_ref, o_ref):
            o_ref[...] = c_ref[...] + _dot_nt(a_ref[...], b_ref[...])
        ins, specs = (A, B, add), [_row(N), _const((K, N)), _row(K)]

    return pl.pallas_call(
        body, grid=(T // ROW_TILE,), name=name, in_specs=specs,
        out_specs=_row(K), out_shape=jax.ShapeDtypeStruct((T, K), F32),
        compiler_params=_params(("parallel",)),
    )(*ins)


def _mm_tn(A, G, name):
    T, K = A.shape
    N = G.shape[1]
    nt = T // ROW_TILE

    def body(a_ref, g_ref, o_ref, acc_ref):
        i = pl.program_id(0)

        @pl.when(i == 0)
        def _():
            acc_ref[...] = jnp.zeros_like(acc_ref)

        acc_ref[...] += _dot_tn(a_ref[...], g_ref[...])

        @pl.when(i == nt - 1)
        def _():
            o_ref[...] = acc_ref[...].astype(BF16)

    return pl.pallas_call(
        body, grid=(nt,), name=name,
        in_specs=[_row(K), _row(N)],
        out_specs=_const((K, N)), out_shape=jax.ShapeDtypeStruct((K, N), BF16),
        scratch_shapes=[pltpu.VMEM((K, N), F32)],
        compiler_params=_params(("arbitrary",)),
    )(A, G)


def _ffn_up(h, Wg, Wu, name):
    T = h.shape[0]
    F = Wg.shape[1]

    def body(h_ref, wg_ref, wu_ref, gp_ref, up_ref, act_ref):
        hv = h_ref[...]
        gp = _dot(hv, wg_ref[...])
        up = _dot(hv, wu_ref[...])
        gp_ref[...] = gp.astype(BF16)
        up_ref[...] = up.astype(BF16)
        act_ref[...] = (gp * _sigmoid(gp) * up).astype(BF16)

    sd = jax.ShapeDtypeStruct((T, F), BF16)
    return pl.pallas_call(
        body, grid=(T // ROW_TILE,), name=name,
        in_specs=[_row(D), _const((D, F)), _const((D, F))],
        out_specs=[_row(F), _row(F), _row(F)], out_shape=[sd, sd, sd],
        compiler_params=_params(("parallel",)),
    )(h, Wg, Wu)


def _ffn_dact(dy, Wd, gp, up, name):
    T = dy.shape[0]
    F = Wd.shape[0]

    def body(dy_ref, wd_ref, gp_ref, up_ref, dgp_ref, dup_ref):
        dact = _dot_nt(dy_ref[...], wd_ref[...])
        gpv = gp_ref[...].astype(F32)
        upv = up_ref[...].astype(F32)
        sg = _sigmoid(gpv)
        dup_ref[...] = (dact * gpv * sg).astype(BF16)
        dgp_ref[...] = (dact * upv * sg * (1.0 + gpv * (1.0 - sg))).astype(BF16)

    sd = jax.ShapeDtypeStruct((T, F), BF16)
    return pl.pallas_call(
        body, grid=(T // ROW_TILE,), name=name,
        in_specs=[_row(D), _const((F, D)), _row(F), _row(F)],
        out_specs=[_row(F), _row(F)], out_shape=[sd, sd],
        compiler_params=_params(("parallel",)),
    )(dy, Wd, gp, up)


N_TAB = 7


def _ret_tables(tab_ref, lgf, lgb):
    r = lax.broadcasted_iota(jnp.int32, (CHUNK, CHUNK), 0).astype(F32)
    c = lax.broadcasted_iota(jnp.int32, (CHUNK, CHUNK), 1).astype(F32)
    for d, lg in ((0, lgf), (1, lgb)):
        if d == 0:
            mask, expo, xe, ze = r >= c, r - c, r + 1.0, (CHUNK - 1.0) - r
        else:
            mask, expo, xe, ze = c > r, c - r - 1.0, (CHUNK - 1.0) - r, r
        e = jnp.where(mask, expo, 0.0)
        tab_ref[N_TAB * d + 0] = jnp.where(mask, jnp.exp(lg * e), 0.0)
        tab_ref[N_TAB * d + 1] = jnp.exp(lg * xe)
        tab_ref[N_TAB * d + 2] = jnp.exp(lg * ze)
        tab_ref[N_TAB * d + 3] = jnp.exp(jnp.full((CHUNK, CHUNK), lg * float(CHUNK), F32))
        tab_ref[N_TAB * d + 4] = e
        tab_ref[N_TAB * d + 5] = xe
        tab_ref[N_TAB * d + 6] = ze


def _rope(t, cosf, sgn):
    return t * cosf + pltpu.roll(t, HEAD_DIM // 2, 1) * sgn


def _rope_t(d, cosf, sgn):
    return d * cosf + pltpu.roll(d * sgn, HEAD_DIM // 2, 1)


def _chunk_of(d, j, nc, ncx):
    return lax.rem(j + ncx, nc) if d == 0 else nc - 1 - j


def _head_spec(T, col0):
    return pl.BlockSpec((T, HEAD_DIM), lambda h: (0, col0 + h))


def _ret_fwd(p, lg, cosf, sgn, seq, name):
    T = p.shape[0]
    nc, ncx = T // CHUNK, seq // CHUNK

    def body(lg_ref, q_ref, k_ref, v_ref, g_ref, cos_ref, sgn_ref, y_ref, ro_ref, tab_ref, of_ref):
        h = pl.program_id(0)
        _ret_tables(tab_ref, lg_ref[0, h], lg_ref[1, h])

        def load(cidx):
            rows = pl.ds(pl.multiple_of(cidx * CHUNK, CHUNK), CHUNK)
            cs, sn = cos_ref[rows, :], sgn_ref[rows, :]
            return rows, _rope(q_ref[rows, :], cs, sn) * Q_SCALE, _rope(k_ref[rows, :], cs, sn), v_ref[rows, :]

        def chunk(d, q, k, v, S):
            b = N_TAB * d
            kb, vb = k.astype(BF16), v.astype(BF16)
            pm = _dot_nt(q.astype(BF16), kb) * tab_ref[b]
            o = _dot(pm.astype(BF16), vb) + _dot((q * tab_ref[b + 1]).astype(BF16), S.astype(BF16))
            return o, S * tab_ref[b + 3] + _dot_tn((k * tab_ref[b + 2]).astype(BF16), vb)

        def step_f(j, S):
            rows, q, k, v = load(_chunk_of(0, j, nc, ncx))
            o, S = chunk(0, q, k, v, S)
            of_ref[rows, :] = o
            return S

        lax.fori_loop(0, nc, step_f, jnp.zeros((CHUNK, CHUNK), F32))

        def step_b(j, S):
            rows, q, k, v = load(_chunk_of(1, j, nc, ncx))
            o, S = chunk(1, q, k, v, S)
            y = of_ref[rows, :] + o
            y_ref[rows, :] = y
            g = g_ref[rows, :]
            ro_ref[rows, :] = (g * _sigmoid(g) * y * lax.rsqrt(_rowmean(y * y) + EPS)).astype(BF16)
            return S

        lax.fori_loop(0, nc, step_b, jnp.zeros((CHUNK, CHUNK), F32))

    tbl = pl.BlockSpec((T, HEAD_DIM), lambda h: (0, 0))
    return pl.pallas_call(
        body, grid=(RET_HEADS,), name=name,
        in_specs=[pl.BlockSpec(memory_space=pltpu.SMEM), _head_spec(T, 0), _head_spec(T, 4), _head_spec(T, 8),
                  _head_spec(T, 12), tbl, tbl],
        out_specs=[_head_spec(T, 0), _head_spec(T, 0)],
        out_shape=[jax.ShapeDtypeStruct((T, RET_W), F32), jax.ShapeDtypeStruct((T, RET_W), BF16)],
        scratch_shapes=[pltpu.VMEM((2 * N_TAB, CHUNK, CHUNK), F32), pltpu.VMEM((T, HEAD_DIM), F32)],
        compiler_params=_params(("arbitrary",)),
    )(lg, p, p, p, p, cosf, sgn)


def _ret_gate_bwd(yret, p, dmix, name):
    T = yret.shape[0]

    def body(y_ref, g_ref, dm_ref, dy_ref, dg_ref):
        for hh in range(RET_HEADS):
            sl = slice(hh * HEAD_DIM, (hh + 1) * HEAD_DIM)
            y = y_ref[:, sl]
            r = lax.rsqrt(_rowmean(y * y) + EPS)
            yn = y * r
            g = g_ref[:, sl]
            sg = _sigmoid(g)
            dro = dm_ref[:, sl]
            dg_ref[:, sl] = (dro * yn * sg * (1.0 + g * (1.0 - sg))).astype(BF16)
            dyn = dro * g * sg
            dy_ref[:, sl] = r * (dyn - yn * _rowmean(dyn * yn))

    return pl.pallas_call(
        body, grid=(T // ROW_TILE,), name=name,
        in_specs=[_row(RET_W), pl.BlockSpec((ROW_TILE, RET_W), lambda i: (i, 3)), _row(RET_W)],
        out_specs=[_row(RET_W), _row(RET_W)],
        out_shape=[jax.ShapeDtypeStruct((T, RET_W), F32), jax.ShapeDtypeStruct((T, RET_W), BF16)],
        compiler_params=_params(("parallel",)),
    )(yret, p, dmix)


def _ret_bwd(p, lg, dy, cosf, sgn, seq, name):
    T = p.shape[0]
    nc, ncx = T // CHUNK, seq // CHUNK

    def body(lg_ref, q_ref, k_ref, v_ref, dy_ref, cos_ref, sgn_ref, dq_ref, dk_ref, dv_ref, dlg_ref,
             tab_ref, st_ref, dqs, dks, dvs):
        h = pl.program_id(0)
        _ret_tables(tab_ref, lg_ref[0, h], lg_ref[1, h])
        dlg_ref[...] = jnp.zeros_like(dlg_ref)

        def load(cidx):
            rows = pl.ds(pl.multiple_of(cidx * CHUNK, CHUNK), CHUNK)
            cs, sn = cos_ref[rows, :], sgn_ref[rows, :]
            return rows, _rope(q_ref[rows, :], cs, sn) * Q_SCALE, _rope(k_ref[rows, :], cs, sn), v_ref[rows, :]

        for d in (0, 1):
            b = N_TAB * d

            def states(j, S, d=d, b=b):
                _, _, k, v = load(_chunk_of(d, j, nc, ncx))
                st_ref[j] = S
                return S * tab_ref[b + 3] + _dot_tn((k * tab_ref[b + 2]).astype(BF16), v.astype(BF16))

            lax.fori_loop(0, nc, states, jnp.zeros((CHUNK, CHUNK), F32))

            def sweep(jj, carry, d=d, b=b):
                dS, acc = carry
                j = nc - 1 - jj
                rows, q, k, v = load(_chunk_of(d, j, nc, ncx))
                dO = dy_ref[rows, :]
                Sp = st_ref[j]
                qb, kb, vb, dOb = q.astype(BF16), k.astype(BF16), v.astype(BF16), dO.astype(BF16)
                Spb, dSb = Sp.astype(BF16), dS.astype(BF16)
                dmat, xi, ze, cd = tab_ref[b], tab_ref[b + 1], tab_ref[b + 2], tab_ref[b + 3]
                pm = _dot_nt(qb, kb) * dmat
                dpm = _dot_nt(dOb, vb)
                acc = acc + _colsum(dpm * pm * tab_ref[b + 4])
                dsc = (dpm * dmat).astype(BF16)
                qx = (q * xi).astype(BF16)
                kz = (k * ze).astype(BF16)
                dq = _dot(dsc, kb) + _dot_nt(dOb, Spb) * xi
                dk = _dot_tn(dsc, qb) + _dot_nt(vb, dSb) * ze
                dvst = _dot(kz, dSb)
                dv = _dot_tn(pm.astype(BF16), dOb) + dvst
                inter = _dot(qx, Spb)
                acc = acc + _colsum(dO * inter * tab_ref[b + 5])
                acc = acc + float(CHUNK) * _colsum(dS * cd * Sp) + _colsum(v * dvst * tab_ref[b + 6])
                if d == 0:
                    dqs[rows, :] = dq
                    dks[rows, :] = dk
                    dvs[rows, :] = dv
                else:
                    dqs[rows, :] += dq
                    dks[rows, :] += dk
                    dvs[rows, :] += dv
                return dS * cd + _dot_tn(qx, dOb), acc

            _, acc = lax.fori_loop(0, nc, sweep, (jnp.zeros((CHUNK, CHUNK), F32), jnp.zeros((1, CHUNK), F32)))
            dlg_ref[0, d:d + 1, :] = jnp.zeros((1, LANES), F32) + jnp.sum(acc)

        def finish(cidx, carry):
            rows = pl.ds(pl.multiple_of(cidx * CHUNK, CHUNK), CHUNK)
            cs, sn = cos_ref[rows, :], sgn_ref[rows, :]
            dq_ref[rows, :] = (_rope_t(dqs[rows, :], cs, sn) * Q_SCALE).astype(BF16)
            dk_ref[rows, :] = _rope_t(dks[rows, :], cs, sn).astype(BF16)
            dv_ref[rows, :] = dvs[rows, :].astype(BF16)
            return carry

        lax.fori_loop(0, nc, finish, 0)

    tbl = pl.BlockSpec((T, HEAD_DIM), lambda h: (0, 0))
    sd = jax.ShapeDtypeStruct((T, RET_W), BF16)
    return pl.pallas_call(
        body, grid=(RET_HEADS,), name=name,
        in_specs=[pl.BlockSpec(memory_space=pltpu.SMEM), _head_spec(T, 0), _head_spec(T, 4), _head_spec(T, 8),
                  _head_spec(T, 0), tbl, tbl],
        out_specs=[_head_spec(T, 0), _head_spec(T, 0), _head_spec(T, 0),
                   pl.BlockSpec((1, 8, LANES), lambda h: (h, 0, 0))],
        out_shape=[sd, sd, sd, jax.ShapeDtypeStruct((RET_HEADS, 8, LANES), F32)],
        scratch_shapes=[pltpu.VMEM((2 * N_TAB, CHUNK, CHUNK), F32), pltpu.VMEM((nc, CHUNK, CHUNK), F32),
                        pltpu.VMEM((T, HEAD_DIM), F32), pltpu.VMEM((T, HEAD_DIM), F32),
                        pltpu.VMEM((T, HEAD_DIM), F32)],
        compiler_params=_params(("arbitrary",)),
    )(lg, p, p, p, dy, cosf, sgn)


def _halo_specs(T, cols, colblock):
    per = ROW_TILE // HALO
    last = T // HALO - 1
    prv = pl.BlockSpec((HALO, cols), lambda i: (jnp.maximum(i * per - 1, 0), colblock))
    nxt = pl.BlockSpec((HALO, cols), lambda i: (jnp.minimum((i + 1) * per, last), colblock))
    return prv, nxt


def _seq_edges(i, nxt, nt):
    first = (i == 0) | (i == nxt)
    last = (i == nxt - 1) | (i == nt - 1)
    return first, last


def _fill_pad(pad_ref, prv, cur, nxt, first, last):
    pad_ref[0:HALO, :] = jnp.where(first, 0.0, prv)
    pad_ref[HALO:HALO + ROW_TILE, :] = cur
    pad_ref[HALO + ROW_TILE:2 * HALO + ROW_TILE, :] = jnp.where(last, 0.0, nxt)


def _ln_stats(x):
    mu = _rowmean(x)
    xc = x - mu
    rs = lax.rsqrt(_rowmean(xc * xc) + EPS)
    return xc * rs, rs


def _conv_fwd(p, cw, lnw, lnb, nxt, name):
    T = p.shape[0]
    nt = T // ROW_TILE
    a_col, g_col = 4, 5

    def body(a_ref, g_ref, ap_ref, gp_ref, an_ref, gn_ref, w_ref, lw_ref, lb_ref, cv_ref, co_ref, pad_ref):
        i = pl.program_id(0)
        first, last = _seq_edges(i, nxt, nt)
        glu = lambda a, g: a * _sigmoid(g)
        _fill_pad(pad_ref, glu(ap_ref[...], gp_ref[...]), glu(a_ref[...], g_ref[...]),
                  glu(an_ref[...], gn_ref[...]), first, last)
        acc = jnp.zeros((ROW_TILE, CONV_CH), F32)
        for k in range(CONV_K):
            acc = acc + w_ref[k:k + 1, :] * pad_ref[k + 1:k + 1 + ROW_TILE, :]
        cv_ref[...] = acc
        xh, _ = _ln_stats(acc)
        z = xh * lw_ref[...] + lb_ref[...]
        co_ref[...] = (z * _sigmoid(z)).astype(BF16)

    cur = lambda cb: pl.BlockSpec((ROW_TILE, CONV_CH), lambda i: (i, cb))
    ap, an = _halo_specs(T, CONV_CH, a_col)
    gp, gn = _halo_specs(T, CONV_CH, g_col)
    return pl.pallas_call(
        body, grid=(nt,), name=name,
        in_specs=[cur(a_col), cur(g_col), ap, gp, an, gn, _const((32, CONV_CH)), _const((1, CONV_CH)),
                  _const((1, CONV_CH))],
        out_specs=[_row(CONV_CH), _row(CONV_CH)],
        out_shape=[jax.ShapeDtypeStruct((T, CONV_CH), F32), jax.ShapeDtypeStruct((T, CONV_CH), BF16)],
        scratch_shapes=[pltpu.VMEM((ROW_TILE + 2 * HALO, CONV_CH), F32)],
        compiler_params=_params(("parallel",)),
    )(p, p, p, p, p, p, cw, lnw, lnb)


def _conv_bwd(p, cv, dmix, cw, lnw, lnb, nxt, name):
    T = p.shape[0]
    nt = T // ROW_TILE
    a_col, g_col = 4, 5

    def body(a_ref, g_ref, ap_ref, gp_ref, an_ref, gn_ref, cv_ref, cvp_ref, cvn_ref, dc_ref, dcp_ref, dcn_ref,
             w_ref, lw_ref, lb_ref, dp_ref, dw_ref, dl_ref, upad_ref, dpad_ref):
        i = pl.program_id(0)
        first, last = _seq_edges(i, nxt, nt)

        @pl.when(i == 0)
        def _():
            dw_ref[...] = jnp.zeros_like(dw_ref)
            dl_ref[...] = jnp.zeros_like(dl_ref)

        lw, lb = lw_ref[...], lb_ref[...]

        def ln_bwd(cvv, dco):
            xh, rs = _ln_stats(cvv)
            z = xh * lw + lb
            sg = _sigmoid(z)
            dz = dco * sg * (1.0 + z * (1.0 - sg))
            dxh = dz * lw
            return rs * (dxh - _rowmean(dxh) - xh * _rowmean(dxh * xh)), dz, xh

        dcv, dz, xh = ln_bwd(cv_ref[...], dc_ref[...])
        _fill_pad(dpad_ref, ln_bwd(cvp_ref[...], dcp_ref[...])[0], dcv, ln_bwd(cvn_ref[...], dcn_ref[...])[0],
                  first, last)
        a, g = a_ref[...], g_ref[...]
        sg = _sigmoid(g)
        glu = lambda av, gv: av * _sigmoid(gv)
        _fill_pad(upad_ref, glu(ap_ref[...], gp_ref[...]), a * sg, glu(an_ref[...], gn_ref[...]), first, last)
        du = jnp.zeros((ROW_TILE, CONV_CH), F32)
        for k in range(CONV_K):
            du = du + w_ref[k:k + 1, :] * dpad_ref[CONV_K - k:CONV_K - k + ROW_TILE, :]
            dw_ref[k:k + 1, :] += _colsum(dcv * upad_ref[k + 1:k + 1 + ROW_TILE, :])
        dl_ref[0:1, :] += _colsum(dz * xh)
        dl_ref[1:2, :] += _colsum(dz)
        dp_ref[:, 0:CONV_CH] = (du * sg).astype(BF16)
        dp_ref[:, CONV_CH:2 * CONV_CH] = (du * a * sg * (1.0 - sg)).astype(BF16)

    cur = lambda cb: pl.BlockSpec((ROW_TILE, CONV_CH), lambda i: (i, cb))
    ap, an = _halo_specs(T, CONV_CH, a_col)
    gp, gn = _halo_specs(T, CONV_CH, g_col)
    cvp, cvn = _halo_specs(T, CONV_CH, 0)
    dcp, dcn = _halo_specs(T, CONV_CH, 1)
    return pl.pallas_call(
        body, grid=(nt,), name=name,
        in_specs=[cur(a_col), cur(g_col), ap, gp, an, gn, cur(0), cvp, cvn, cur(1), dcp, dcn,
                  _const((32, CONV_CH)), _const((1, CONV_CH)), _const((1, CONV_CH))],
        out_specs=[_row(2 * CONV_CH), _const((32, CONV_CH)), _const((8, CONV_CH))],
        out_shape=[jax.ShapeDtypeStruct((T, 2 * CONV_CH), BF16), jax.ShapeDtypeStruct((32, CONV_CH), F32),
                   jax.ShapeDtypeStruct((8, CONV_CH), F32)],
        scratch_shapes=[pltpu.VMEM((ROW_TILE + 2 * HALO, CONV_CH), F32),
                        pltpu.VMEM((ROW_TILE + 2 * HALO, CONV_CH), F32)],
        compiler_params=_params(("arbitrary",)),
    )(p, p, p, p, p, p, cv, cv, cv, dmix, dmix, dmix, cw, lnw, lnb)


def _tile_positions(i, nxt, seq, ctx, offset, rows):
    is_ctx = i >= nxt
    pos0 = (i - jnp.where(is_ctx, nxt, 0)) * ROW_TILE + offset
    length = jnp.where(is_ctx, ctx, seq).astype(F32)
    pos = (pos0 + lax.broadcasted_iota(jnp.int32, (rows, 1), 0)).astype(F32)
    return pos, length


def _pool_count(pos, length, w):
    left = w // 2
    right = w - 1 - left
    return jnp.minimum(pos + right, length - 1.0) - jnp.maximum(pos - left, 0.0) + 1.0


def _gelu(x):
    return 0.5 * x * (1.0 + lax.erf(x * INV_SQRT2))


def _odd_fwd(p, pw, ps, slw, slb, sw, sbf, nxt, seq, ctx, name):
    T = p.shape[0]
    nt = T // ROW_TILE

    def body(p_ref, pp_ref, pn_ref, pw_ref, ps_ref, lw_ref, lb_ref, sw_ref, sb_ref, o_ref, pad_ref):
        i = pl.program_id(0)
        first, last = _seq_edges(i, nxt, nt)
        _fill_pad(pad_ref, pp_ref[...], p_ref[:, 0:POOL_CH], pn_ref[...], first, last)
        pos, length = _tile_positions(i, nxt, seq, ctx, 0, ROW_TILE)
        for gi, w in enumerate(POOL_WINDOWS):
            sl = slice(gi * GROUP_CH, (gi + 1) * GROUP_CH)
            left = w // 2
            ssum = jnp.zeros((ROW_TILE, GROUP_CH), F32)
            for o in range(-left, w - left):
                ssum = ssum + pad_ref[HALO + o:HALO + o + ROW_TILE, sl]
            m = ssum / _pool_count(pos, length, w) - p_ref[:, sl]
            pre = _dot(m.astype(BF16), pw_ref[gi].astype(BF16))
            o_ref[:, sl] = (pre * ps_ref[:, sl]).astype(BF16)
        u = _gelu(p_ref[:, POOL_CH:POOL_CH + SG_CH])
        xh, _ = _ln_stats(_gelu(p_ref[:, POOL_CH + SG_CH:ODD_IN]))
        vln = xh * lw_ref[...] + lb_ref[...]
        for n in range(ROW_TILE // SG_CHUNK):
            rs = slice(n * SG_CHUNK, (n + 1) * SG_CHUNK)
            for gi in range(4):
                sl = slice(gi * GROUP_CH, (gi + 1) * GROUP_CH)
                s = _dot(sw_ref[gi].astype(BF16), vln[rs, sl].astype(BF16)) + sb_ref[:, sl]
                o_ref[rs, POOL_CH + gi * GROUP_CH:POOL_CH + (gi + 1) * GROUP_CH] = (u[rs, sl] * s).astype(BF16)

    pp, pn = _halo_specs(T, POOL_CH, 0)
    return pl.pallas_call(
        body, grid=(nt,), name=name,
        in_specs=[_row(ODD_IN), pp, pn, _const((4, GROUP_CH, GROUP_CH)), _const((1, POOL_CH)), _const((1, SG_CH)),
                  _const((1, SG_CH)), _const((4, SG_CHUNK, SG_CHUNK)), _const((SG_CHUNK, SG_CH))],
        out_specs=_row(D), out_shape=jax.ShapeDtypeStruct((T, D), BF16),
        scratch_shapes=[pltpu.VMEM((ROW_TILE + 2 * HALO, POOL_CH), F32)],
        compiler_params=_params(("parallel",)),
    )(p, p, p, pw, ps, slw, slb, sw, sbf)


def _odd_bwd(p, dmix, pw, ps, slw, slb, sw, sbf, nxt, seq, ctx, name):
    T = p.shape[0]
    nt = T // ROW_TILE

    def body(p_ref, pp_ref, pn_ref, dm_ref, dmp_ref, dmn_ref, pw_ref, ps_ref, lw_ref, lb_ref, sw_ref, sb_ref,
             dp_ref, dpw_ref, dsw_ref, dv_ref, dsb_ref, pad_ref, dpad_ref):
        i = pl.program_id(0)
        first, last = _seq_edges(i, nxt, nt)

        @pl.when(i == 0)
        def _():
            dpw_ref[...] = jnp.zeros_like(dpw_ref)
            dsw_ref[...] = jnp.zeros_like(dsw_ref)
            dv_ref[...] = jnp.zeros_like(dv_ref)
            dsb_ref[...] = jnp.zeros_like(dsb_ref)

        _fill_pad(pad_ref, pp_ref[...], p_ref[:, 0:POOL_CH], pn_ref[...], first, last)
        pos, length = _tile_positions(i, nxt, seq, ctx, 0, ROW_TILE)
        pos_p, _ = _tile_positions(i, nxt, seq, ctx, -HALO, HALO)
        pos_n, _ = _tile_positions(i, nxt, seq, ctx, ROW_TILE, HALO)
        scale = ps_ref[...]
        for gi, w in enumerate(POOL_WINDOWS):
            sl = slice(gi * GROUP_CH, (gi + 1) * GROUP_CH)
            left = w // 2
            right = w - 1 - left
            ssum = jnp.zeros((ROW_TILE, GROUP_CH), F32)
            for o in range(-left, right + 1):
                ssum = ssum + pad_ref[HALO + o:HALO + o + ROW_TILE, sl]
            cnt = _pool_count(pos, length, w)
            m = ssum / cnt - p_ref[:, sl]
            wg = pw_ref[gi].astype(BF16)
            pre = _dot(m.astype(BF16), wg)
            dpo = dm_ref[:, sl]
            dv_ref[0:1, sl] += _colsum(dpo * pre)
            dpre = (dpo * scale[:, sl]).astype(BF16)
            dpw_ref[gi] += _dot_tn(m.astype(BF16), dpre)
            dmc = _dot_nt(dpre, wg)
            halo_dm = lambda ref, ps_: _dot_nt((ref[:, sl] * scale[:, sl]).astype(BF16), wg) / _pool_count(ps_, length, w)
            dpad_ref[0:HALO, sl] = jnp.where(first, 0.0, halo_dm(dmp_ref, pos_p))
            dpad_ref[HALO:HALO + ROW_TILE, sl] = dmc / cnt
            dpad_ref[HALO + ROW_TILE:2 * HALO + ROW_TILE, sl] = jnp.where(last, 0.0, halo_dm(dmn_ref, pos_n))
            atd = jnp.zeros((ROW_TILE, GROUP_CH), F32)
            for o in range(-right, left + 1):
                atd = atd + dpad_ref[HALO + o:HALO + o + ROW_TILE, sl]
            dp_ref[:, sl] = (atd - dmc).astype(BF16)

        pu = p_ref[:, POOL_CH:POOL_CH + SG_CH]
        pv = p_ref[:, POOL_CH + SG_CH:ODD_IN]
        u = _gelu(pu)
        xh, rs_ = _ln_stats(_gelu(pv))
        lw = lw_ref[...]
        vln = xh * lw + lb_ref[...]
        dgelu = lambda x: 0.5 * (1.0 + lax.erf(x * INV_SQRT2)) + x * jnp.exp(-0.5 * x * x) * INV_SQRT_2PI
        for n in range(ROW_TILE // SG_CHUNK):
            rs = slice(n * SG_CHUNK, (n + 1) * SG_CHUNK)
            dvl = []
            for gi in range(4):
                sl = slice(gi * GROUP_CH, (gi + 1) * GROUP_CH)
                wq = sw_ref[gi].astype(BF16)
                vb = vln[rs, sl].astype(BF16)
                s = _dot(wq, vb) + sb_ref[:, sl]
                dsg = dm_ref[rs, POOL_CH + gi * GROUP_CH:POOL_CH + (gi + 1) * GROUP_CH]
                ds = dsg * u[rs, sl]
                dsb_ref[:, sl] += ds
                dsw_ref[gi] += _dot_nt(ds.astype(BF16), vb)
                dvl.append(_dot_tn(wq, ds.astype(BF16)))
                dp_ref[rs, POOL_CH + gi * GROUP_CH:POOL_CH + (gi + 1) * GROUP_CH] = (
                    dsg * s * dgelu(pu[rs, sl])).astype(BF16)
            dvln = jnp.concatenate(dvl, axis=1)
            xhc = xh[rs, :]
            dv_ref[1:2, :] += _colsum(dvln * xhc)
            dv_ref[2:3, :] += _colsum(dvln)
            dxh = dvln * lw
            dvv = rs_[rs, :] * (dxh - _rowmean(dxh) - xhc * _rowmean(dxh * xhc))
            dp_ref[rs, POOL_CH + SG_CH:ODD_IN] = (dvv * dgelu(pv[rs, :])).astype(BF16)

    pp, pn = _halo_specs(T, POOL_CH, 0)
    dmp, dmn = _halo_specs(T, POOL_CH, 0)
    gsd = jax.ShapeDtypeStruct((4, GROUP_CH, GROUP_CH), F32)
    return pl.pallas_call(
        body, grid=(nt,), name=name,
        in_specs=[_row(ODD_IN), pp, pn, _row(D), dmp, dmn, _const((4, GROUP_CH, GROUP_CH)), _const((1, POOL_CH)),
                  _const((1, SG_CH)), _const((1, SG_CH)), _const((4, SG_CHUNK, SG_CHUNK)), _const((SG_CHUNK, SG_CH))],
        out_specs=[_row(ODD_IN), _const((4, GROUP_CH, GROUP_CH)), _const((4, SG_CHUNK, SG_CHUNK)), _const((8, POOL_CH)),
                   _const((SG_CHUNK, SG_CH))],
        out_shape=[jax.ShapeDtypeStruct((T, ODD_IN), BF16), gsd, gsd, jax.ShapeDtypeStruct((8, POOL_CH), F32),
                   jax.ShapeDtypeStruct((SG_CHUNK, SG_CH), F32)],
        scratch_shapes=[pltpu.VMEM((ROW_TILE + 2 * HALO, POOL_CH), F32),
                        pltpu.VMEM((ROW_TILE + 2 * HALO, POOL_CH), F32)],
        compiler_params=_params(("arbitrary",)),
    )(p, p, p, dmix, dmix, dmix, pw, ps, slw, slb, sw, sbf)


def _ada_fwd(cs, aw, ab, name):
    cols = aw.shape[2]

    def body(c_ref, w_ref, b_ref, o_ref):
        c = c_ref[...]
        s = (c * _sigmoid(c)).astype(BF16)
        o_ref[0] = _dot(s, w_ref[0].astype(BF16)) + b_ref[0]

    return pl.pallas_call(
        body, grid=(DEPTH,), name=name,
        in_specs=[pl.BlockSpec((16, D), lambda i: (0, 0)), pl.BlockSpec((1, D, cols), lambda i: (i, 0, 0)),
                  pl.BlockSpec((1, 1, cols), lambda i: (i, 0, 0))],
        out_specs=pl.BlockSpec((1, 16, cols), lambda i: (i, 0, 0)),
        out_shape=jax.ShapeDtypeStruct((DEPTH, 16, cols), F32),
        compiler_params=_params(("parallel",)),
    )(cs, aw, ab)


def _ada_bwd(cs, G, aw, name):
    cols = aw.shape[2]

    def body(c_ref, g_ref, w_ref, gw_ref, cc_ref):
        i = pl.program_id(0)

        @pl.when(i == 0)
        def _():
            cc_ref[...] = jnp.zeros_like(cc_ref)

        c = c_ref[...]
        s = (c * _sigmoid(c)).astype(BF16)
        g = g_ref[0]
        dc = g[8:9, :]
        for d in range(9, 16):
            dc = dc + g[d:d + 1, :]
        row = lax.broadcasted_iota(jnp.int32, (8, cols), 0)
        dcrows = jnp.where(row == 0, dc, 0.0)
        dm = jnp.concatenate([g[0:8, :], dcrows], axis=0).astype(BF16)
        gw_ref[0] = _dot_tn(s, dm)
        cc_ref[...] += _dot_nt(dcrows.astype(BF16), w_ref[0].astype(BF16))

    return pl.pallas_call(
        body, grid=(DEPTH,), name=name,
        in_specs=[pl.BlockSpec((16, D), lambda i: (0, 0)), pl.BlockSpec((1, 16, cols), lambda i: (i, 0, 0)),
                  pl.BlockSpec((1, D, cols), lambda i: (i, 0, 0))],
        out_specs=[pl.BlockSpec((1, D, cols), lambda i: (i, 0, 0)), pl.BlockSpec((8, D), lambda i: (0, 0))],
        out_shape=[jax.ShapeDtypeStruct((DEPTH, D, cols), F32), jax.ShapeDtypeStruct((8, D), F32)],
        compiler_params=_params(("arbitrary",)),
    )(cs, G, aw)


def _row_tile_for(rows):
    for t in (512, 256, 128, 64, 32, 16, 8):
        if rows % t == 0:
            return t
    raise ValueError(f"rows={rows} is not a multiple of 8")


def _sum_devices(x, name):
    _, rows, cols = x.shape
    tr = _row_tile_for(rows)

    def body(x_ref, o_ref):
        acc = x_ref[0]
        for d in range(1, N_DEV):
            acc = acc + x_ref[d]
        o_ref[...] = acc

    return pl.pallas_call(
        body, grid=(rows // tr,), name=name,
        in_specs=[pl.BlockSpec((N_DEV, tr, cols), lambda i: (0, i, 0))],
        out_specs=pl.BlockSpec((tr, cols), lambda i: (i, 0)),
        out_shape=jax.ShapeDtypeStruct((rows, cols), F32),
        compiler_params=_params(("parallel",)),
    )(x)


def _adamw(w, g, m, v, name):
    rows, cols = w.shape
    tr = _row_tile_for(rows)
    pieces = g.ndim == 3

    def body(w_ref, g_ref, m_ref, v_ref, go_ref, d_ref, mo_ref, vo_ref):
        if pieces:
            gv = g_ref[0].astype(F32)
            for d in range(1, N_DEV):
                gv = gv + g_ref[d].astype(F32)
        else:
            gv = g_ref[...]
        mn = ADAM_B1 * m_ref[...] + (1.0 - ADAM_B1) * gv
        vn = ADAM_B2 * v_ref[...] + (1.0 - ADAM_B2) * (gv * gv)
        go_ref[...] = gv
        mo_ref[...] = mn
        vo_ref[...] = vn
        d_ref[...] = -ADAM_LR * ((mn / ADAM_BC1) / (jnp.sqrt(vn / ADAM_BC2) + ADAM_EPS) + ADAM_WD * w_ref[...])

    blk = pl.BlockSpec((tr, cols), lambda i: (i, 0))
    gblk = pl.BlockSpec((N_DEV, tr, cols), lambda i: (0, i, 0)) if pieces else blk
    sd = jax.ShapeDtypeStruct((rows, cols), F32)
    return pl.pallas_call(
        body, grid=(rows // tr,), name=name,
        in_specs=[blk, gblk, blk, blk], out_specs=[blk, blk, blk, blk], out_shape=[sd, sd, sd, sd],
        compiler_params=_params(("parallel",)),
    )(w, g, m, v)


def _exchange(x, scatter, name):
    shape = x.shape[1:] if scatter else x.shape

    def body(x_ref, out_ref, send_sems, recv_sems, local_sem):
        mx, my, mc = lax.axis_index("x"), lax.axis_index("y"), lax.axis_index("c")
        me = 4 * mx + 2 * my + mc

        def peer(k):
            px, py, pc = mx ^ ((k >> 2) & 1), my ^ ((k >> 1) & 1), mc ^ (k & 1)
            return (px, py, pc), 4 * px + 2 * py + pc

        mine = pltpu.make_async_copy(x_ref.at[me] if scatter else x_ref, out_ref.at[me], local_sem)
        mine.start()
        sends = []
        for k in range(1, N_DEV):
            pid, pidx = peer(k)
            cp = pltpu.make_async_remote_copy(
                src_ref=x_ref.at[pidx] if scatter else x_ref, dst_ref=out_ref.at[me],
                send_sem=send_sems.at[k - 1], recv_sem=recv_sems.at[k - 1], device_id=pid, device_id_type=MESH)
            cp.start()
            sends.append(cp)
        for k in range(1, N_DEV):
            pid, pidx = peer(k)
            pltpu.make_async_remote_copy(
                src_ref=x_ref.at[pidx] if scatter else x_ref, dst_ref=out_ref.at[pidx],
                send_sem=send_sems.at[k - 1], recv_sem=recv_sems.at[k - 1], device_id=pid,
                device_id_type=MESH).wait_recv()
        for cp in sends:
            cp.wait_send()
        mine.wait()

    return pl.pallas_call(
        body, name=name,
        in_specs=[pl.BlockSpec(memory_space=pl.ANY)], out_specs=pl.BlockSpec(memory_space=pl.ANY),
        out_shape=jax.ShapeDtypeStruct((N_DEV,) + tuple(shape), x.dtype),
        scratch_shapes=[pltpu.SemaphoreType.DMA((N_DEV - 1,)), pltpu.SemaphoreType.DMA((N_DEV - 1,)),
                        pltpu.SemaphoreType.DMA],
    )(x)


def _pack(arrs, cols, row_mult, dtype):
    flat = jnp.concatenate([a.reshape(-1).astype(dtype) for a in arrs])
    per = cols * row_mult
    total = -(-flat.shape[0] // per) * per
    return jnp.pad(flat, (0, total - flat.shape[0])).reshape(total // cols, cols)


def _unpack(flat, shapes):
    out, off = [], 0
    for s in shapes:
        n = math.prod(s)
        out.append(flat[..., off:off + n].reshape(flat.shape[:-1] + tuple(s)))
        off += n
    return out


def _rope_tables(seq, ctx):
    def angles(ps):
        parts = []
        for pvec, n in zip(ps, ROPE_PAIRS):
            freq = ROPE_BASE ** (-jnp.arange(n, dtype=F32) / n)
            parts.append(pvec[:, None] * freq[None, :])
        return jnp.concatenate(parts, axis=-1)

    rows = seq // GRID_W
    grid_r = jnp.broadcast_to(jnp.arange(rows, dtype=F32)[:, None], (rows, GRID_W)).reshape(-1)
    grid_c = jnp.broadcast_to(jnp.arange(GRID_W, dtype=F32)[None, :], (rows, GRID_W)).reshape(-1)
    zc = jnp.zeros((ctx,), F32)
    ang = jnp.concatenate([angles((jnp.full((seq,), ctx, F32), grid_r, grid_c)),
                           angles((jnp.arange(ctx, dtype=F32), zc, zc))], axis=0)
    cos, sin = jnp.cos(ang), jnp.sin(ang)
    return jnp.concatenate([cos, cos], axis=1), jnp.concatenate([-sin, sin], axis=1)


def _local_step(X0, tgt, mods, W, S, seq, ctx):
    nxt = seq // ROW_TILE
    cosf, sgn = _rope_tables(seq, ctx)
    X = X0
    saved = []
    for i in range(DEPTH):
        j, even = i // 2, i % 2 == 0
        md = mods[i]
        t = f"l{i}_"
        h1 = _norm_mod_fwd(X, S["norm_w"][i, 0][None], md, 0, nxt, t + "norm1")
        if even:
            p = _mm_nn(h1, W["even_in"][j], F32, t + "proj_in")
            yret, ro = _ret_fwd(p, S["lg"][j], cosf, sgn, seq, t + "ret_fwd")
            cv, co = _conv_fwd(p, S["conv_w"][j], S["conv_ln_w"][j][None], S["conv_ln_b"][j][None], nxt,
                               t + "conv_fwd")
            mix = jnp.concatenate([ro, co], axis=1)
            extra = (yret, cv)
            w_out = W["even_out"][j]
        else:
            p = _mm_nn(h1, W["odd_in"][j], F32, t + "proj_in")
            sbf = jnp.repeat(S["sg_b"][j].T, GROUP_CH, axis=1)
            mix = _odd_fwd(p, S["pool_w"][j], S["pool_scale"][j][None], S["sg_ln_w"][j][None],
                           S["sg_ln_b"][j][None], S["sg_w"][j], sbf, nxt, seq, ctx, t + "odd_fwd")
            extra = (sbf,)
            w_out = W["odd_out"][j]
        X1, y1 = _mm_nn_resid(mix, w_out, X, md, 2, nxt, t + "proj_out")
        h2 = _norm_mod_fwd(X1, S["norm_w"][i, 1][None], md, 3, nxt, t + "norm2")
        gp, up, act = _ffn_up(h2, W["gate"][i], W["up"][i], t + "ffn_up")
        X2, y2 = _mm_nn_resid(act, W["down"][i], X1, md, 5, nxt, t + "ffn_down")
        saved.append((X, h1, p, mix, extra, y1, X1, h2, gp, up, act, y2))
        X = X2

    dX, hst = _loss_head(X, tgt, S["final_norm_w"][None], nxt, "loss_head")
    loss = jnp.sum(hst[1])
    gW = {k: [None] * len(v) for k, v in W.items()}
    gS = {"final_norm_w": hst[0], "norm_w": [None] * DEPTH, "lg": [None] * 2, "conv_w": [None] * 2,
          "conv_ln_w": [None] * 2, "conv_ln_b": [None] * 2, "pool_w": [None] * 2, "pool_scale": [None] * 2,
          "sg_ln_w": [None] * 2, "sg_ln_b": [None] * 2, "sg_w": [None] * 2, "sg_b": [None] * 2}
    dmods = [None] * DEPTH
    for i in reversed(range(DEPTH)):
        j, even = i // 2, i % 2 == 0
        md = mods[i]
        t = f"l{i}_"
        X_in, h1, p, mix, extra, y1, X1, h2, gp, up, act, y2 = saved[i]
        dyf, g2 = _resid_bwd(dX, y2, md, 5, nxt, t + "ffn_resid_bwd")
        dgp, dup = _ffn_dact(dyf, W["down"][i], gp, up, t + "ffn_dact")
        gW["down"][i] = _mm_tn(act, dyf, t + "dw_down")
        gW["gate"][i] = _mm_tn(h2, dgp, t + "dw_gate")
        gW["up"][i] = _mm_tn(h2, dup, t + "dw_up")
        dh2 = _mm_nt(dgp, W["gate"][i], t + "dh2_gate")
        dh2 = _mm_nt(dup, W["up"][i], t + "dh2_up", add=dh2)
        dX1, s2 = _norm_mod_bwd(dh2, X1, S["norm_w"][i, 1][None], md, dX, 3, nxt, t + "norm2_bwd")
        dym, g1 = _resid_bwd(dX1, y1, md, 2, nxt, t + "mix_resid_bwd")
        if even:
            w_out, w_in = W["even_out"][j], W["even_in"][j]
        else:
            w_out, w_in = W["odd_out"][j], W["odd_in"][j]
        dmix = _mm_nt(dym, w_out, t + "dmix")
        g_out = _mm_tn(mix, dym, t + "dw_out")
        if even:
            yret, cv = extra
            dyr, dg = _ret_gate_bwd(yret, p, dmix, t + "ret_gate_bwd")
            dq, dk, dv, dlg = _ret_bwd(p, S["lg"][j], dyr, cosf, sgn, seq, t + "ret_bwd")
            dpc, dcw, dln = _conv_bwd(p, cv, dmix, S["conv_w"][j], S["conv_ln_w"][j][None],
                                      S["conv_ln_b"][j][None], nxt, t + "conv_bwd")
            dp = jnp.concatenate([dq, dk, dv, dg, dpc], axis=1)
            gW["even_out"][j] = g_out
            gS["lg"][j] = dlg[:, 0:2, 0].T
            gS["conv_w"][j], gS["conv_ln_w"][j], gS["conv_ln_b"][j] = dcw, dln[0], dln[1]
        else:
            (sbf,) = extra
            dp, dpw, dsw, dvec, dsb = _odd_bwd(p, dmix, S["pool_w"][j], S["pool_scale"][j][None],
                                               S["sg_ln_w"][j][None], S["sg_ln_b"][j][None], S["sg_w"][j], sbf,
                                               nxt, seq, ctx, t + "odd_bwd")
            gW["odd_out"][j] = g_out
            gS["pool_w"][j], gS["sg_w"][j] = dpw, dsw
            gS["pool_scale"][j], gS["sg_ln_w"][j], gS["sg_ln_b"][j] = dvec[0], dvec[1], dvec[2]
            gS["sg_b"][j] = jnp.sum(dsb.reshape(SG_CHUNK, 4, GROUP_CH), axis=2).T
        g_in = _mm_tn(h1, dp, t + "dw_in")
        if even:
            gW["even_in"][j] = g_in
        else:
            gW["odd_in"][j] = g_in
        dh1 = _mm_nt(dp, w_in, t + "dh1")
        dX, s1 = _norm_mod_bwd(dh1, X_in, S["norm_w"][i, 0][None], md, dX1, 0, nxt, t + "norm1_bwd")
        gS["norm_w"][i] = jnp.stack([s1[0, 2] + s1[1, 2], s2[0, 2] + s2[1, 2]])
        dmods[i] = jnp.stack([s1[:, 0], s1[:, 1], g1[:, 0], s2[:, 0], s2[:, 1], g2[:, 0]], axis=1)
    gS = {k: (jnp.stack(v) if isinstance(v, list) else v) for k, v in gS.items()}
    return loss, dX, gW, gS, jnp.stack(dmods)


PACK_COLS = 1024
LAYER_KEYS = ("in", "out", "gate", "up", "down")


def _layer_shard_shapes(even):
    n_in = (EVEN_IN if even else ODD_IN) // N_DEV
    return [(D, n_in), (D // N_DEV, D), (D, D_FF // N_DEV), (D, D_FF // N_DEV), (D_FF // N_DEV, D)]


def _layer_pack_rows():
    per = PACK_COLS * 16
    n = sum(math.prod(s) for s in _layer_shard_shapes(True))
    return -(-n // per) * per // PACK_COLS


def _shards_to_full(parts):
    cat_cols = lambda a: a.transpose(1, 0, 2).reshape(a.shape[1], -1)
    cat_rows = lambda a: a.reshape(-1, a.shape[2])
    return [cat_cols(parts[0]), cat_rows(parts[1]), cat_cols(parts[2]), cat_cols(parts[3]), cat_rows(parts[4])]


def _full_to_shards(full):
    split_cols = lambda a: a.reshape(a.shape[0], N_DEV, -1).transpose(1, 0, 2)
    split_rows = lambda a: a.reshape(N_DEV, -1, a.shape[1])
    return [split_cols(full[0]), split_rows(full[1]), split_cols(full[2]), split_cols(full[3]), split_rows(full[4])]


def kernel(x, c, ctx, c_ctx, ada_w, ada_b, norm_w, even_w_in, even_w_out, ret_decay_logit, conv_dw_w, conv_ln_w, conv_ln_b, odd_w_in, odd_w_out, pool_w, pool_scale, sg_ln_w, sg_ln_b, sg_w, sg_b, ffn_w_gate, ffn_w_up, ffn_w_down, final_norm_w, loss_target, m_c_ctx, m_ada_w, m_ada_b, m_norm_w, m_even_w_in, m_even_w_out, m_ret_decay_logit, m_conv_dw_w, m_conv_ln_w, m_conv_ln_b, m_odd_w_in, m_odd_w_out, m_pool_w, m_pool_scale, m_sg_ln_w, m_sg_ln_b, m_sg_w, m_sg_b, m_ffn_w_gate, m_ffn_w_up, m_ffn_w_down, m_final_norm_w, v_c_ctx, v_ada_w, v_ada_b, v_norm_w, v_even_w_in, v_even_w_out, v_ret_decay_logit, v_conv_dw_w, v_conv_ln_w, v_conv_ln_b, v_odd_w_in, v_odd_w_out, v_pool_w, v_pool_scale, v_sg_ln_w, v_sg_ln_b, v_sg_w, v_sg_b, v_ffn_w_gate, v_ffn_w_up, v_ffn_w_down, v_final_norm_w):
    seq, n_ctx = x.shape[1], ctx.shape[1]
    me = 4 * lax.axis_index("x") + 2 * lax.axis_index("y") + lax.axis_index("c")
    mcols = ada_w.shape[2]

    small_shapes = [(D,), norm_w.shape, conv_dw_w.shape, pool_scale.shape, sg_ln_w.shape, sg_ln_b.shape]
    sm = _exchange(_pack([c, norm_w, conv_dw_w, pool_scale, sg_ln_w, sg_ln_b], LANES, 8, F32), False, "gather_small")
    c_all, nw_s, cw_s, ps_s, slw_s, slb_s = _unpack(sm.reshape(N_DEV, -1), small_shapes)
    cat_last = lambda a: jnp.moveaxis(a, 0, -2).reshape(a.shape[1:-1] + (-1,))
    conv_w_full = cat_last(cw_s)
    S = {"norm_w": cat_last(nw_s), "lg": jax.nn.log_sigmoid(ret_decay_logit),
         "conv_w": jnp.pad(conv_w_full, ((0, 0), (0, 32 - CONV_K), (0, 0))),
         "conv_ln_w": conv_ln_w, "conv_ln_b": conv_ln_b, "pool_w": pool_w, "pool_scale": cat_last(ps_s),
         "sg_ln_w": cat_last(slw_s), "sg_ln_b": cat_last(slb_s), "sg_w": sg_w, "sg_b": sg_b,
         "final_norm_w": final_norm_w}

    cs = jnp.concatenate([c_all, c_ctx[None], jnp.zeros((7, D), F32)], axis=0)
    ab_loc = lax.dynamic_slice_in_dim(ada_b, me * mcols, mcols, axis=1)
    mod_loc = _ada_fwd(cs, ada_w, ab_loc[:, None, :], "ada_fwd")
    mod_all = _exchange(mod_loc.reshape(DEPTH * 16, mcols), False, "gather_mod")
    mod_all = mod_all.reshape(N_DEV, DEPTH, 16, mcols).transpose(1, 2, 0, 3).reshape(DEPTH, 16, 6, D)
    mod_x = lax.dynamic_index_in_dim(mod_all, me, axis=1, keepdims=False)
    mods = jnp.stack([mod_x, mod_all[:, 8]], axis=1)

    W = {"even_in": [], "even_out": [], "odd_in": [], "odd_out": [], "gate": [], "up": [], "down": []}
    local = []
    for i in range(DEPTH):
        j, even = i // 2, i % 2 == 0
        w_in, w_out = (even_w_in[j], even_w_out[j]) if even else (odd_w_in[j], odd_w_out[j])
        shards = [w_in, w_out, ffn_w_gate[i], ffn_w_up[i], ffn_w_down[i]]
        local.append(shards)
        shapes = _layer_shard_shapes(even)
        packed = _pack(shards, PACK_COLS, 16, BF16)
        packed = jnp.pad(packed, ((0, _layer_pack_rows() - packed.shape[0]), (0, 0)))
        got = _exchange(packed, False, f"gather_w{i}")
        full = _shards_to_full(_unpack(got.reshape(N_DEV, -1), shapes))
        W["even_in" if even else "odd_in"].append(full[0])
        W["even_out" if even else "odd_out"].append(full[1])
        W["gate"].append(full[2])
        W["up"].append(full[3])
        W["down"].append(full[4])

    X0 = jnp.concatenate([x[0], ctx[0]], axis=0)
    loss, dX, gW, gS, dmods = _local_step(X0, loss_target[0], mods, W, S, seq, n_ctx)
    loss = lax.psum(loss, ("x", "y", "c"))
    grad_x = dX[:seq][None]

    dm_all = _exchange(dmods.reshape(DEPTH * 2, 6 * D), False, "gather_dmod").reshape(N_DEV, DEPTH, 2, 6 * D)
    dm_sum = _sum_devices(dm_all.reshape(N_DEV, DEPTH * 2, 6 * D), "sum_dmod").reshape(DEPTH, 2, 6 * D)
    g_ada_b = dm_sum[:, 0] + dm_sum[:, 1]
    G = lax.dynamic_slice_in_dim(dm_all, me * mcols, mcols, axis=3).transpose(1, 2, 0, 3).reshape(DEPTH, 16, mcols)
    g_ada_w, ccp = _ada_bwd(cs, G, ada_w, "ada_bwd")
    sg_cc = _sigmoid(c_ctx)
    g_cctx_part = ccp[0] * sg_cc * (1.0 + c_ctx * (1.0 - sg_cc))

    dsig = _sigmoid(-ret_decay_logit)
    part = [g_cctx_part, gS["norm_w"], gS["lg"] * dsig, gS["conv_w"][:, :CONV_K], gS["conv_ln_w"], gS["conv_ln_b"],
            gS["pool_w"], gS["pool_scale"], gS["sg_ln_w"], gS["sg_ln_b"], gS["sg_w"], gS["sg_b"],
            gS["final_norm_w"]]
    part_shapes = [a.shape for a in part]
    red = _sum_devices(_exchange(_pack(part, LANES, 8, F32), False, "gather_small_grads"), "sum_small_grads")
    (g_cctx, g_nw, g_rdl, g_cw, g_clw, g_clb, g_pw, g_ps, g_slw, g_slb, g_sw, g_sb, g_fnw) = _unpack(
        red.reshape(-1), part_shapes)
    mine_last = lambda a, n: lax.dynamic_slice_in_dim(a, me * n, n, axis=a.ndim - 1)
    g_nw, g_cw = mine_last(g_nw, norm_w.shape[2]), mine_last(g_cw, conv_dw_w.shape[2])
    g_ps, g_slw, g_slb = (mine_last(a, pool_scale.shape[1]) for a in (g_ps, g_slw, g_slb))

    big = {}
    for i in range(DEPTH):
        j, even = i // 2, i % 2 == 0
        full = [gW["even_in" if even else "odd_in"][j], gW["even_out" if even else "odd_out"][j],
                gW["gate"][i], gW["up"][i], gW["down"][i]]
        shards = _full_to_shards(full)
        flat = jnp.concatenate([s.reshape(N_DEV, -1) for s in shards], axis=1)
        rows = _layer_pack_rows()
        flat = jnp.pad(flat, ((0, 0), (0, rows * PACK_COLS - flat.shape[1])))
        got = _exchange(flat.reshape(N_DEV, rows, PACK_COLS), True, f"scatter_g{i}")
        pieces = _unpack(got.reshape(N_DEV, -1), _layer_shard_shapes(even))
        for key, pc in zip(LAYER_KEYS, pieces):
            big[(key, i)] = pc

    def adam_big(key, layers, w, m, v, name):
        L, a, b = w.shape
        g = jnp.concatenate([big[(key, l)] for l in layers], axis=1)
        outs = _adamw(w.reshape(L * a, b), g, m.reshape(L * a, b), v.reshape(L * a, b), name)
        return [o.reshape(L, a, b) for o in outs]

    res = {}
    res["even_w_in"] = adam_big("in", (0, 2), even_w_in, m_even_w_in, v_even_w_in, "adam_even_in")
    res["even_w_out"] = adam_big("out", (0, 2), even_w_out, m_even_w_out, v_even_w_out, "adam_even_out")
    res["odd_w_in"] = adam_big("in", (1, 3), odd_w_in, m_odd_w_in, v_odd_w_in, "adam_odd_in")
    res["odd_w_out"] = adam_big("out", (1, 3), odd_w_out, m_odd_w_out, v_odd_w_out, "adam_odd_out")
    res["ffn_w_gate"] = adam_big("gate", (0, 1, 2, 3), ffn_w_gate, m_ffn_w_gate, v_ffn_w_gate, "adam_gate")
    res["ffn_w_up"] = adam_big("up", (0, 1, 2, 3), ffn_w_up, m_ffn_w_up, v_ffn_w_up, "adam_up")
    res["ffn_w_down"] = adam_big("down", (0, 1, 2, 3), ffn_w_down, m_ffn_w_down, v_ffn_w_down, "adam_down")
    flat2 = lambda a: a.reshape(-1, a.shape[-1])
    res["ada_w"] = [o.reshape(ada_w.shape) for o in
                    _adamw(flat2(ada_w), flat2(g_ada_w), flat2(m_ada_w), flat2(v_ada_w), "adam_ada_w")]

    names = ["c_ctx", "ada_b", "norm_w", "ret_decay_logit", "conv_dw_w", "conv_ln_w", "conv_ln_b", "pool_w",
             "pool_scale", "sg_ln_w", "sg_ln_b", "sg_w", "sg_b", "final_norm_w"]
    ws = [c_ctx, ada_b, norm_w, ret_decay_logit, conv_dw_w, conv_ln_w, conv_ln_b, pool_w, pool_scale, sg_ln_w,
          sg_ln_b, sg_w, sg_b, final_norm_w]
    gs = [g_cctx, g_ada_b, g_nw, g_rdl, g_cw, g_clw, g_clb, g_pw, g_ps, g_slw, g_slb, g_sw, g_sb, g_fnw]
    ms = [m_c_ctx, m_ada_b, m_norm_w, m_ret_decay_logit, m_conv_dw_w, m_conv_ln_w, m_conv_ln_b, m_pool_w,
          m_pool_scale, m_sg_ln_w, m_sg_ln_b, m_sg_w, m_sg_b, m_final_norm_w]
    vs = [v_c_ctx, v_ada_b, v_norm_w, v_ret_decay_logit, v_conv_dw_w, v_conv_ln_w, v_conv_ln_b, v_pool_w,
          v_pool_scale, v_sg_ln_w, v_sg_ln_b, v_sg_w, v_sg_b, v_final_norm_w]
    shapes = [a.shape for a in ws]
    gs = [g.reshape(s) for g, s in zip(gs, shapes)]
    pk = lambda arrs: _pack(arrs, LANES, 8, F32)
    outs = _adamw(pk(ws), pk(gs), pk(ms), pk(vs), "adam_small")
    for k, o in enumerate(outs):
        for nm, arr in zip(names, _unpack(o.reshape(-1), shapes)):
            res.setdefault(nm, [None] * 4)[k] = arr

    order = ["c_ctx", "ada_w", "ada_b", "norm_w", "even_w_in", "even_w_out", "ret_decay_logit", "conv_dw_w",
             "conv_ln_w", "conv_ln_b", "odd_w_in", "odd_w_out", "pool_w", "pool_scale", "sg_ln_w", "sg_ln_b",
             "sg_w", "sg_b", "ffn_w_gate", "ffn_w_up", "ffn_w_down", "final_norm_w"]
    return (loss, grad_x, *[res[n][0] for n in order], *[res[n][1] for n in order],
            *[res[n][2] for n in order], *[res[n][3] for n in order])
```

```python
import math

import jax
import jax.numpy as jnp
from jax import lax
from jax.experimental import pallas as pl
from jax.experimental.pallas import tpu as pltpu

F32 = jnp.float32
BF16 = jnp.bfloat16

N_DEV = 8
D = 1024
DEPTH = 4
ROW_TILE = 256
HALO = 16
LANES = 128
EPS = 1e-6
RET_HEADS = 4
HEAD_DIM = 128
RET_W = RET_HEADS * HEAD_DIM
CHUNK = 128
Q_SCALE = HEAD_DIM ** -0.5
ROPE_BASE = 10000.0
ROPE_PAIRS = (HEAD_DIM // 8, 3 * HEAD_DIM // 16, 3 * HEAD_DIM // 16)
GRID_W = 64
CONV_CH = 512
CONV_K = 31
EVEN_IN = 4 * RET_W + 2 * CONV_CH
POOL_CH = 512
POOL_WINDOWS = (2, 4, 8, 16)
GROUP_CH = 128
SG_CH = 512
SG_CHUNK = 128
ODD_IN = POOL_CH + 2 * SG_CH
D_FF = 2816
INV_SQRT2 = 1.0 / math.sqrt(2.0)
INV_SQRT_2PI = 1.0 / math.sqrt(2.0 * math.pi)
ADAM_LR, ADAM_B1, ADAM_B2, ADAM_EPS, ADAM_WD, ADAM_STEP = 0.001, 0.9, 0.999, 1e-08, 0.01, 10
ADAM_BC1 = 1.0 - ADAM_B1 ** ADAM_STEP
ADAM_BC2 = 1.0 - ADAM_B2 ** ADAM_STEP
VMEM_LIMIT = 56 * 1024 * 1024
MESH = pl.DeviceIdType.MESH


def _params(sem=None):
    return pltpu.CompilerParams(dimension_semantics=sem, vmem_limit_bytes=VMEM_LIMIT)


def _dot(a, b):
    return jnp.dot(a, b, preferred_element_type=F32)


def _dot_nt(a, b):
    return lax.dot_general(a, b, (((1,), (1,)), ((), ())), preferred_element_type=F32)


def _dot_tn(a, b):
    return lax.dot_general(a, b, (((0,), (0,)), ((), ())), preferred_element_type=F32)


def _sigmoid(x):
    return 1.0 / (1.0 + jnp.exp(-x))


def _colsum(x):
    return jnp.sum(x, axis=0, keepdims=True)


def _rowmean(x):
    return jnp.mean(x, axis=-1, keepdims=True)


def _row(shape_cols):
    return pl.BlockSpec((ROW_TILE, shape_cols), lambda i: (i, 0))


def _const(shape):
    nd = len(shape)
    return pl.BlockSpec(shape, lambda i: (0,) * nd)


def _mod_spec(nxt):
    return pl.BlockSpec((1, 6, D), lambda i: (i // nxt, 0, 0))


def _stat_spec(nxt):
    return pl.BlockSpec((1, 8, D), lambda i: (i // nxt, 0, 0))


def _xfer_out_shape(kind, a):
    s = a.shape
    shape = {"ag_blk": (N_DEV,) + s, "ag_cols": (s[0], N_DEV * s[1]) if len(s) == 2 else None,
             "ag_rows": (N_DEV * s[0],) + s[1:], "a2a_blk": s,
             "a2a_cols": (N_DEV, s[0], s[1] // N_DEV) if len(s) == 2 else None,
             "a2a_rows": (N_DEV, s[0] // N_DEV) + s[1:]}[kind]
    return jax.ShapeDtypeStruct(shape, a.dtype)


def _src_view(kind, ref, who):
    if kind == "a2a_blk":
        return ref.at[who]
    if kind == "a2a_cols":
        n = ref.shape[1] // N_DEV
        return ref.at[:, pl.ds(pl.multiple_of(who * n, LANES), n)]
    if kind == "a2a_rows":
        r = ref.shape[0] // N_DEV
        return ref.at[pl.ds(pl.multiple_of(who * r, 16), r), :]
    return ref


def _dst_view(kind, ref, who):
    if kind == "ag_cols":
        n = ref.shape[1] // N_DEV
        return ref.at[:, pl.ds(pl.multiple_of(who * n, LANES), n)]
    if kind == "ag_rows":
        r = ref.shape[0] // N_DEV
        return ref.at[pl.ds(pl.multiple_of(who * r, 16), r), :]
    return ref.at[who]


def _xfers(kinds, srcs, dsts, send_sems, recv_sems, local_sems, start):
    mx, my, mc = lax.axis_index("x"), lax.axis_index("y"), lax.axis_index("c")
    me = 4 * mx + 2 * my + mc
    peers = []
    for k in range(1, N_DEV):
        px, py, pc = mx ^ ((k >> 2) & 1), my ^ ((k >> 1) & 1), mc ^ (k & 1)
        peers.append(((px, py, pc), 4 * px + 2 * py + pc))
    for j, (kind, src, dst) in enumerate(zip(kinds, srcs, dsts)):
        own = pltpu.make_async_copy(_src_view(kind, src, me), _dst_view(kind, dst, me), local_sems.at[j])
        if start:
            own.start()
        for k, (pid, pidx) in enumerate(peers):
            sem = (N_DEV - 1) * j + k
            cp = pltpu.make_async_remote_copy(
                src_ref=_src_view(kind, src, pidx), dst_ref=_dst_view(kind, dst, me if start else pidx),
                send_sem=send_sems.at[sem], recv_sem=recv_sems.at[sem], device_id=pid, device_id_type=MESH)
            if start:
                cp.start()
            else:
                cp.wait_recv()
                cp.wait_send()
        if not start:
            own.wait()


def _xfer_sems(n):
    return [pltpu.SemaphoreType.DMA(((N_DEV - 1) * n,)), pltpu.SemaphoreType.DMA(((N_DEV - 1) * n,)),
            pltpu.SemaphoreType.DMA((n,))]


def _exchange(comm, name):
    kinds = [k for k, _ in comm]
    n = len(comm)

    def body(*refs):
        sems = refs[2 * n:]
        _xfers(kinds, refs[:n], refs[n:2 * n], *sems, start=True)
        _xfers(kinds, refs[:n], refs[n:2 * n], *sems, start=False)

    hbm = pl.BlockSpec(memory_space=pl.ANY)
    return pl.pallas_call(
        body, name=name, in_specs=[hbm] * n, out_specs=[hbm] * n,
        out_shape=[_xfer_out_shape(k, a) for k, a in comm], scratch_shapes=_xfer_sems(n),
    )(*[a for _, a in comm])


def _call(body, *, grid, in_specs, out_specs, out_shape, args, name, sem, scratch=(), comm=()):
    n_in, n_out, n_scr, n_c = len(in_specs), len(out_specs), len(scratch), len(comm)
    if not comm:
        outs = pl.pallas_call(body, grid=grid, in_specs=list(in_specs), out_specs=list(out_specs),
                              out_shape=list(out_shape), scratch_shapes=list(scratch), name=name,
                              compiler_params=_params(sem))(*args)
        return list(outs), []
    kinds = [k for k, _ in comm]
    n_steps = grid[0]

    def wrapped(*refs):
        ins, csrc = refs[:n_in], refs[n_in:n_in + n_c]
        o0 = n_in + n_c
        outs, cdst = refs[o0:o0 + n_out], refs[o0 + n_out:o0 + n_out + n_c]
        s0 = o0 + n_out + n_c
        scr, sems = refs[s0:s0 + n_scr], refs[s0 + n_scr:]
        i = pl.program_id(0)

        @pl.when(i == 0)
        def _():
            _xfers(kinds, csrc, cdst, *sems, start=True)

        body(*ins, *outs, *scr)

        @pl.when(i == n_steps - 1)
        def _():
            _xfers(kinds, csrc, cdst, *sems, start=False)

    hbm = pl.BlockSpec(memory_space=pl.ANY)
    outs = pl.pallas_call(
        wrapped, grid=grid, in_specs=list(in_specs) + [hbm] * n_c, out_specs=list(out_specs) + [hbm] * n_c,
        out_shape=list(out_shape) + [_xfer_out_shape(k, a) for k, a in comm],
        scratch_shapes=list(scratch) + _xfer_sems(n_c), name=name, compiler_params=_params(("arbitrary",)),
    )(*args, *[a for _, a in comm])
    return list(outs[:n_out]), list(outs[n_out:])


def _norm_mod_fwd(X, nw, mods, si, nxt, name):
    T = X.shape[0]

    def body(x_ref, w_ref, m_ref, h_ref):
        x = x_ref[...]
        r = lax.rsqrt(_rowmean(x * x) + EPS)
        m = m_ref[0]
        y = x * r * w_ref[...]
        h_ref[...] = (y * (1.0 + m[si + 1:si + 2, :]) + m[si:si + 1, :]).astype(BF16)

    return pl.pallas_call(
        body, grid=(T // ROW_TILE,), name=name,
        in_specs=[_row(D), _const((1, D)), _mod_spec(nxt)],
        out_specs=_row(D), out_shape=jax.ShapeDtypeStruct((T, D), BF16),
        compiler_params=_params(("parallel",)),
    )(X, nw, mods)


def _norm_mod_bwd(dh, X, nw, mods, dres, si, nxt, name):
    T = X.shape[0]

    def body(dh_ref, x_ref, w_ref, m_ref, dres_ref, dx_ref, st_ref):
        i = pl.program_id(0)

        @pl.when((i == 0) | (i == nxt))
        def _():
            st_ref[...] = jnp.zeros_like(st_ref)

        x = x_ref[...]
        r = lax.rsqrt(_rowmean(x * x) + EPS)
        xn = x * r
        w = w_ref[...]
        scale = m_ref[0][si + 1:si + 2, :]
        dhv = dh_ref[...]
        dy = dhv * (1.0 + scale)
        dxn = dy * w
        dx_ref[...] = dres_ref[...] + r * (dxn - xn * _rowmean(dxn * xn))
        st_ref[0, 0:1, :] += _colsum(dhv)
        st_ref[0, 1:2, :] += _colsum(dhv * xn * w)
        st_ref[0, 2:3, :] += _colsum(dy * xn)

    return pl.pallas_call(
        body, grid=(T // ROW_TILE,), name=name,
        in_specs=[_row(D), _row(D), _const((1, D)), _mod_spec(nxt), _row(D)],
        out_specs=[_row(D), _stat_spec(nxt)],
        out_shape=[jax.ShapeDtypeStruct((T, D), F32), jax.ShapeDtypeStruct((2, 8, D), F32)],
        compiler_params=_params(("arbitrary",)),
    )(dh, X, nw, mods, dres)


def _resid_bwd(dX, y, mods, gi, nxt, name):
    T = dX.shape[0]

    def body(dx_ref, y_ref, m_ref, dy_ref, st_ref):
        i = pl.program_id(0)

        @pl.when((i == 0) | (i == nxt))
        def _():
            st_ref[...] = jnp.zeros_like(st_ref)

        dx = dx_ref[...]
        dy_ref[...] = (dx * m_ref[0][gi:gi + 1, :]).astype(BF16)
        st_ref[0, 0:1, :] += _colsum(dx * y_ref[...].astype(F32))

    return pl.pallas_call(
        body, grid=(T // ROW_TILE,), name=name,
        in_specs=[_row(D), _row(D), _mod_spec(nxt)],
        out_specs=[_row(D), _stat_spec(nxt)],
        out_shape=[jax.ShapeDtypeStruct((T, D), BF16), jax.ShapeDtypeStruct((2, 8, D), F32)],
        compiler_params=_params(("arbitrary",)),
    )(dX, y, mods)


def _loss_head(X, tgt, fw, nxt, name):
    T = X.shape[0]

    def body(x_ref, t_ref, w_ref, dx_ref, st_ref):
        i = pl.program_id(0)

        @pl.when(i == 0)
        def _():
            st_ref[...] = jnp.zeros_like(st_ref)

        @pl.when(i < nxt)
        def _():
            x = x_ref[...]
            r = lax.rsqrt(_rowmean(x * x) + EPS)
            xn = x * r
            w = w_ref[...]
            err = xn * w - t_ref[...]
            dy = err * (1.0 / D)
            dxn = dy * w
            dx_ref[...] = r * (dxn - xn * _rowmean(dxn * xn))
            st_ref[0:1, :] += _colsum(dy * xn)
            st_ref[1:2, :] += (0.5 / D) * _colsum(err * err)

        @pl.when(i >= nxt)
        def _():
            dx_ref[...] = jnp.zeros_like(dx_ref)

    return pl.pallas_call(
        body, grid=(T // ROW_TILE,), name=name,
        in_specs=[_row(D), pl.BlockSpec((ROW_TILE, D), lambda i: (jnp.minimum(i, nxt - 1), 0)), _const((1, D))],
        out_specs=[_row(D), _const((8, D))],
        out_shape=[jax.ShapeDtypeStruct((T, D), F32), jax.ShapeDtypeStruct((8, D), F32)],
        compiler_params=_params(("arbitrary",)),
    )(X, tgt, fw)


def _mm_nn(A, B, out_dtype, name, comm=()):
    T, K = A.shape
    N = B.shape[1]

    def body(a_ref, b_ref, o_ref):
        o_ref[...] = _dot(a_ref[...], b_ref[...]).astype(out_dtype)

    return _call(body, grid=(T // ROW_TILE,), name=name, in_specs=[_row(K), _const((K, N))],
                 out_specs=[_row(N)], out_shape=[jax.ShapeDtypeStruct((T, N), out_dtype)],
                 sem=("parallel",), args=(A, B), comm=comm)


def _mm_nn_resid(A, B, X, mods, gi, nxt, name, comm=()):
    T, K = A.shape
    N = B.shape[1]

    def body(a_ref, b_ref, x_ref, m_ref, xo_ref, y_ref):
        acc = _dot(a_ref[...], b_ref[...])
        y_ref[...] = acc.astype(BF16)
        xo_ref[...] = x_ref[...] + m_ref[0][gi:gi + 1, :] * acc

    return _call(body, grid=(T // ROW_TILE,), name=name,
                 in_specs=[_row(K), _const((K, N)), _row(N), _mod_spec(nxt)], out_specs=[_row(N), _row(N)],
                 out_shape=[jax.ShapeDtypeStruct((T, N), F32), jax.ShapeDtypeStruct((T, N), BF16)],
                 sem=("parallel",), args=(A, B, X, mods), comm=comm)


def _mm_nt(A, B, name, add=None, comm=()):
    T, N = A.shape
    K = B.shape[0]

    if add is None:
        def body(a_ref, b_ref, o_ref):
            o_ref[...] = _dot_nt(a_ref[...], b_ref[...])
        ins, specs = (A, B), [_row(N), _const((K, N))]
    else:
        def body(a_ref, b_ref, c_ref, o_ref):
            o_ref[...] = c_ref[...] + _dot_nt(a_ref[...], b_ref[...])
        ins, specs = (A, B, add), [_row(N), _const((K, N)), _row(K)]

    return _call(body, grid=(T // ROW_TILE,), name=name, in_specs=specs, out_specs=[_row(K)],
                 out_shape=[jax.ShapeDtypeStruct((T, K), F32)], sem=("parallel",), args=ins, comm=comm)


def _mm_tn(A, G, name):
    T, K = A.shape
    N = G.shape[1]
    nt = T // ROW_TILE

    def body(a_ref, g_ref, o_ref, acc_ref):
        i = pl.program_id(0)

        @pl.when(i == 0)
        def _():
            acc_ref[...] = jnp.zeros_like(acc_ref)

        acc_ref[...] += _dot_tn(a_ref[...], g_ref[...])

        @pl.when(i == nt - 1)
        def _():
            o_ref[...] = acc_ref[...].astype(BF16)

    return pl.pallas_call(
        body, grid=(nt,), name=name,
        in_specs=[_row(K), _row(N)],
        out_specs=_const((K, N)), out_shape=jax.ShapeDtypeStruct((K, N), BF16),
        scratch_shapes=[pltpu.VMEM((K, N), F32)],
        compiler_params=_params(("arbitrary",)),
    )(A, G)


def _ffn_up(h, Wg, Wu, name, comm=()):
    T = h.shape[0]
    F = Wg.shape[1]

    def body(h_ref, wg_ref, wu_ref, gp_ref, up_ref, act_ref):
        hv = h_ref[...]
        gp = _dot(hv, wg_ref[...])
        up = _dot(hv, wu_ref[...])
        gp_ref[...] = gp.astype(BF16)
        up_ref[...] = up.astype(BF16)
        act_ref[...] = (gp * _sigmoid(gp) * up).astype(BF16)

    sd = jax.ShapeDtypeStruct((T, F), BF16)
    return _call(body, grid=(T // ROW_TILE,), name=name, in_specs=[_row(D), _const((D, F)), _const((D, F))],
                 out_specs=[_row(F), _row(F), _row(F)], out_shape=[sd, sd, sd], sem=("parallel",),
                 args=(h, Wg, Wu), comm=comm)


def _ffn_dact(dy, Wd, gp, up, name, comm=()):
    T = dy.shape[0]
    F = Wd.shape[0]

    def body(dy_ref, wd_ref, gp_ref, up_ref, dgp_ref, dup_ref):
        dact = _dot_nt(dy_ref[...], wd_ref[...])
        gpv = gp_ref[...].astype(F32)
        upv = up_ref[...].astype(F32)
        sg = _sigmoid(gpv)
        dup_ref[...] = (dact * gpv * sg).astype(BF16)
        dgp_ref[...] = (dact * upv * sg * (1.0 + gpv * (1.0 - sg))).astype(BF16)

    sd = jax.ShapeDtypeStruct((T, F), BF16)
    return _call(body, grid=(T // ROW_TILE,), name=name, in_specs=[_row(D), _const((F, D)), _row(F), _row(F)],
                 out_specs=[_row(F), _row(F)], out_shape=[sd, sd], sem=("parallel",),
                 args=(dy, Wd, gp, up), comm=comm)


N_TAB = 7


def _ret_tables(tab_ref, lgf, lgb):
    r = lax.broadcasted_iota(jnp.int32, (CHUNK, CHUNK), 0).astype(F32)
    c = lax.broadcasted_iota(jnp.int32, (CHUNK, CHUNK), 1).astype(F32)
    for d, lg in ((0, lgf), (1, lgb)):
        if d == 0:
            mask, expo, xe, ze = r >= c, r - c, r + 1.0, (CHUNK - 1.0) - r
        else:
            mask, expo, xe, ze = c > r, c - r - 1.0, (CHUNK - 1.0) - r, r
        e = jnp.where(mask, expo, 0.0)
        tab_ref[N_TAB * d + 0] = jnp.where(mask, jnp.exp(lg * e), 0.0)
        tab_ref[N_TAB * d + 1] = jnp.exp(lg * xe)
        tab_ref[N_TAB * d + 2] = jnp.exp(lg * ze)
        tab_ref[N_TAB * d + 3] = jnp.exp(jnp.full((CHUNK, CHUNK), lg * float(CHUNK), F32))
        tab_ref[N_TAB * d + 4] = e
        tab_ref[N_TAB * d + 5] = xe
        tab_ref[N_TAB * d + 6] = ze


def _rope(t, cosf, sgn):
    return t * cosf + pltpu.roll(t, HEAD_DIM // 2, 1) * sgn


def _rope_t(d, cosf, sgn):
    return d * cosf + pltpu.roll(d * sgn, HEAD_DIM // 2, 1)


def _chunk_of(d, j, nc, ncx):
    return lax.rem(j + ncx, nc) if d == 0 else nc - 1 - j


def _head_spec(T, col0):
    return pl.BlockSpec((T, HEAD_DIM), lambda h: (0, col0 + h))


def _ret_fwd(p, lg, cosf, sgn, seq, name, comm=()):
    T = p.shape[0]
    nc, ncx = T // CHUNK, seq // CHUNK

    def body(lg_ref, q_ref, k_ref, v_ref, g_ref, cos_ref, sgn_ref, y_ref, ro_ref, tab_ref, of_ref):
        h = pl.program_id(0)
        _ret_tables(tab_ref, lg_ref[0, h], lg_ref[1, h])

        def load(cidx):
            rows = pl.ds(pl.multiple_of(cidx * CHUNK, CHUNK), CHUNK)
            cs, sn = cos_ref[rows, :], sgn_ref[rows, :]
            return rows, _rope(q_ref[rows, :], cs, sn) * Q_SCALE, _rope(k_ref[rows, :], cs, sn), v_ref[rows, :]

        def chunk(d, q, k, v, S):
            b = N_TAB * d
            kb, vb = k.astype(BF16), v.astype(BF16)
            pm = _dot_nt(q.astype(BF16), kb) * tab_ref[b]
            o = _dot(pm.astype(BF16), vb) + _dot((q * tab_ref[b + 1]).astype(BF16), S.astype(BF16))
            return o, S * tab_ref[b + 3] + _dot_tn((k * tab_ref[b + 2]).astype(BF16), vb)

        def step_f(j, S):
            rows, q, k, v = load(_chunk_of(0, j, nc, ncx))
            o, S = chunk(0, q, k, v, S)
            of_ref[rows, :] = o
            return S

        lax.fori_loop(0, nc, step_f, jnp.zeros((CHUNK, CHUNK), F32))

        def step_b(j, S):
            rows, q, k, v = load(_chunk_of(1, j, nc, ncx))
            o, S = chunk(1, q, k, v, S)
            y = of_ref[rows, :] + o
            y_ref[rows, :] = y
            g = g_ref[rows, :]
            ro_ref[rows, :] = (g * _sigmoid(g) * y * lax.rsqrt(_rowmean(y * y) + EPS)).astype(BF16)
            return S

        lax.fori_loop(0, nc, step_b, jnp.zeros((CHUNK, CHUNK), F32))

    tbl = pl.BlockSpec((T, HEAD_DIM), lambda h: (0, 0))
    return _call(
        body, grid=(RET_HEADS,), name=name,
        in_specs=[pl.BlockSpec(memory_space=pltpu.SMEM), _head_spec(T, 0), _head_spec(T, 4), _head_spec(T, 8),
                  _head_spec(T, 12), tbl, tbl],
        out_specs=[_head_spec(T, 0), _head_spec(T, 0)],
        out_shape=[jax.ShapeDtypeStruct((T, RET_W), F32), jax.ShapeDtypeStruct((T, RET_W), BF16)],
        scratch=[pltpu.VMEM((2 * N_TAB, CHUNK, CHUNK), F32), pltpu.VMEM((T, HEAD_DIM), F32)],
        sem=("arbitrary",), args=(lg, p, p, p, p, cosf, sgn), comm=comm)


def _ret_gate_bwd(yret, p, dmix, name):
    T = yret.shape[0]

    def body(y_ref, g_ref, dm_ref, dy_ref, dg_ref):
        for hh in range(RET_HEADS):
            sl = slice(hh * HEAD_DIM, (hh + 1) * HEAD_DIM)
            y = y_ref[:, sl]
            r = lax.rsqrt(_rowmean(y * y) + EPS)
            yn = y * r
            g = g_ref[:, sl]
            sg = _sigmoid(g)
            dro = dm_ref[:, sl]
            dg_ref[:, sl] = (dro * yn * sg * (1.0 + g * (1.0 - sg))).astype(BF16)
            dyn = dro * g * sg
            dy_ref[:, sl] = r * (dyn - yn * _rowmean(dyn * yn))

    return pl.pallas_call(
        body, grid=(T // ROW_TILE,), name=name,
        in_specs=[_row(RET_W), pl.BlockSpec((ROW_TILE, RET_W), lambda i: (i, 3)), _row(RET_W)],
        out_specs=[_row(RET_W), _row(RET_W)],
        out_shape=[jax.ShapeDtypeStruct((T, RET_W), F32), jax.ShapeDtypeStruct((T, RET_W), BF16)],
        compiler_params=_params(("parallel",)),
    )(yret, p, dmix)


def _ret_bwd(p, lg, dy, cosf, sgn, seq, name, comm=()):
    T = p.shape[0]
    nc, ncx = T // CHUNK, seq // CHUNK

    def body(lg_ref, q_ref, k_ref, v_ref, dy_ref, cos_ref, sgn_ref, dq_ref, dk_ref, dv_ref, dlg_ref,
             tab_ref, st_ref, dqs, dks, dvs):
        h = pl.program_id(0)
        _ret_tables(tab_ref, lg_ref[0, h], lg_ref[1, h])
        dlg_ref[...] = jnp.zeros_like(dlg_ref)

        def load(cidx):
            rows = pl.ds(pl.multiple_of(cidx * CHUNK, CHUNK), CHUNK)
            cs, sn = cos_ref[rows, :], sgn_ref[rows, :]
            return rows, _rope(q_ref[rows, :], cs, sn) * Q_SCALE, _rope(k_ref[rows, :], cs, sn), v_ref[rows, :]

        for d in (0, 1):
            b = N_TAB * d

            def states(j, S, d=d, b=b):
                _, _, k, v = load(_chunk_of(d, j, nc, ncx))
                st_ref[j] = S
                return S * tab_ref[b + 3] + _dot_tn((k * tab_ref[b + 2]).astype(BF16), v.astype(BF16))

            lax.fori_loop(0, nc, states, jnp.zeros((CHUNK, CHUNK), F32))

            def sweep(jj, carry, d=d, b=b):
                dS, acc = carry
                j = nc - 1 - jj
                rows, q, k, v = load(_chunk_of(d, j, nc, ncx))
                dO = dy_ref[rows, :]
                Sp = st_ref[j]
                qb, kb, vb, dOb = q.astype(BF16), k.astype(BF16), v.astype(BF16), dO.astype(BF16)
                Spb, dSb = Sp.astype(BF16), dS.astype(BF16)
                dmat, xi, ze, cd = tab_ref[b], tab_ref[b + 1], tab_ref[b + 2], tab_ref[b + 3]
                pm = _dot_nt(qb, kb) * dmat
                dpm = _dot_nt(dOb, vb)
                acc = acc + _colsum(dpm * pm * tab_ref[b + 4])
                dsc = (dpm * dmat).astype(BF16)
                qx = (q * xi).astype(BF16)
                kz = (k * ze).astype(BF16)
                dq = _dot(dsc, kb) + _dot_nt(dOb, Spb) * xi
                dk = _dot_tn(dsc, qb) + _dot_nt(vb, dSb) * ze
                dvst = _dot(kz, dSb)
                dv = _dot_tn(pm.astype(BF16), dOb) + dvst
                inter = _dot(qx, Spb)
                acc = acc + _colsum(dO * inter * tab_ref[b + 5])
                acc = acc + float(CHUNK) * _colsum(dS * cd * Sp) + _colsum(v * dvst * tab_ref[b + 6])
                if d == 0:
                    dqs[rows, :] = dq
                    dks[rows, :] = dk
                    dvs[rows, :] = dv
                else:
                    dqs[rows, :] += dq
                    dks[rows, :] += dk
                    dvs[rows, :] += dv
                return dS * cd + _dot_tn(qx, dOb), acc

            _, acc = lax.fori_loop(0, nc, sweep, (jnp.zeros((CHUNK, CHUNK), F32), jnp.zeros((1, CHUNK), F32)))
            dlg_ref[0, d:d + 1, :] = jnp.zeros((1, LANES), F32) + jnp.sum(acc)

        def finish(cidx, carry):
            rows = pl.ds(pl.multiple_of(cidx * CHUNK, CHUNK), CHUNK)
            cs, sn = cos_ref[rows, :], sgn_ref[rows, :]
            dq_ref[rows, :] = (_rope_t(dqs[rows, :], cs, sn) * Q_SCALE).astype(BF16)
            dk_ref[rows, :] = _rope_t(dks[rows, :], cs, sn).astype(BF16)
            dv_ref[rows, :] = dvs[rows, :].astype(BF16)
            return carry

        lax.fori_loop(0, nc, finish, 0)

    tbl = pl.BlockSpec((T, HEAD_DIM), lambda h: (0, 0))
    sd = jax.ShapeDtypeStruct((T, RET_W), BF16)
    return _call(
        body, grid=(RET_HEADS,), name=name,
        in_specs=[pl.BlockSpec(memory_space=pltpu.SMEM), _head_spec(T, 0), _head_spec(T, 4), _head_spec(T, 8),
                  _head_spec(T, 0), tbl, tbl],
        out_specs=[_head_spec(T, 0), _head_spec(T, 0), _head_spec(T, 0),
                   pl.BlockSpec((1, 8, LANES), lambda h: (h, 0, 0))],
        out_shape=[sd, sd, sd, jax.ShapeDtypeStruct((RET_HEADS, 8, LANES), F32)],
        scratch=[pltpu.VMEM((2 * N_TAB, CHUNK, CHUNK), F32), pltpu.VMEM((nc, CHUNK, CHUNK), F32),
                 pltpu.VMEM((T, HEAD_DIM), F32), pltpu.VMEM((T, HEAD_DIM), F32), pltpu.VMEM((T, HEAD_DIM), F32)],
        sem=("arbitrary",), args=(lg, p, p, p, dy, cosf, sgn), comm=comm)


def _halo_specs(T, cols, colblock):
    per = ROW_TILE // HALO
    last = T // HALO - 1
    prv = pl.BlockSpec((HALO, cols), lambda i: (jnp.maximum(i * per - 1, 0), colblock))
    nxt = pl.BlockSpec((HALO, cols), lambda i: (jnp.minimum((i + 1) * per, last), colblock))
    return prv, nxt


def _seq_edges(i, nxt, nt):
    first = (i == 0) | (i == nxt)
    last = (i == nxt - 1) | (i == nt - 1)
    return first, last


def _fill_pad(pad_ref, prv, cur, nxt, first, last):
    pad_ref[0:HALO, :] = jnp.where(first, 0.0, prv)
    pad_ref[HALO:HALO + ROW_TILE, :] = cur
    pad_ref[HALO + ROW_TILE:2 * HALO + ROW_TILE, :] = jnp.where(last, 0.0, nxt)


def _ln_stats(x):
    mu = _rowmean(x)
    xc = x - mu
    rs = lax.rsqrt(_rowmean(xc * xc) + EPS)
    return xc * rs, rs


def _conv_fwd(p, cw, lnw, lnb, nxt, name, comm=()):
    T = p.shape[0]
    nt = T // ROW_TILE
    a_col, g_col = 4, 5

    def body(a_ref, g_ref, ap_ref, gp_ref, an_ref, gn_ref, w_ref, lw_ref, lb_ref, cv_ref, co_ref, pad_ref):
        i = pl.program_id(0)
        first, last = _seq_edges(i, nxt, nt)
        glu = lambda a, g: a * _sigmoid(g)
        _fill_pad(pad_ref, glu(ap_ref[...], gp_ref[...]), glu(a_ref[...], g_ref[...]),
                  glu(an_ref[...], gn_ref[...]), first, last)
        acc = jnp.zeros((ROW_TILE, CONV_CH), F32)
        for k in range(CONV_K):
            acc = acc + w_ref[k:k + 1, :] * pad_ref[k + 1:k + 1 + ROW_TILE, :]
        cv_ref[...] = acc
        xh, _ = _ln_stats(acc)
        z = xh * lw_ref[...] + lb_ref[...]
        co_ref[...] = (z * _sigmoid(z)).astype(BF16)

    cur = lambda cb: pl.BlockSpec((ROW_TILE, CONV_CH), lambda i: (i, cb))
    ap, an = _halo_specs(T, CONV_CH, a_col)
    gp, gn = _halo_specs(T, CONV_CH, g_col)
    return _call(
        body, grid=(nt,), name=name,
        in_specs=[cur(a_col), cur(g_col), ap, gp, an, gn, _const((32, CONV_CH)), _const((1, CONV_CH)),
                  _const((1, CONV_CH))],
        out_specs=[_row(CONV_CH), _row(CONV_CH)],
        out_shape=[jax.ShapeDtypeStruct((T, CONV_CH), F32), jax.ShapeDtypeStruct((T, CONV_CH), BF16)],
        scratch=[pltpu.VMEM((ROW_TILE + 2 * HALO, CONV_CH), F32)],
        sem=("parallel",), args=(p, p, p, p, p, p, cw, lnw, lnb), comm=comm)


def _conv_bwd(p, cv, dmix, cw, lnw, lnb, nxt, name, comm=()):
    T = p.shape[0]
    nt = T // ROW_TILE
    a_col, g_col = 4, 5

    def body(a_ref, g_ref, ap_ref, gp_ref, an_ref, gn_ref, cv_ref, cvp_ref, cvn_ref, dc_ref, dcp_ref, dcn_ref,
             w_ref, lw_ref, lb_ref, dp_ref, dw_ref, dl_ref, upad_ref, dpad_ref):
        i = pl.program_id(0)
        first, last = _seq_edges(i, nxt, nt)

        @pl.when(i == 0)
        def _():
            dw_ref[...] = jnp.zeros_like(dw_ref)
            dl_ref[...] = jnp.zeros_like(dl_ref)

        lw, lb = lw_ref[...], lb_ref[...]

        def ln_bwd(cvv, dco):
            xh, rs = _ln_stats(cvv)
            z = xh * lw + lb
            sg = _sigmoid(z)
            dz = dco * sg * (1.0 + z * (1.0 - sg))
            dxh = dz * lw
            return rs * (dxh - _rowmean(dxh) - xh * _rowmean(dxh * xh)), dz, xh

        dcv, dz, xh = ln_bwd(cv_ref[...], dc_ref[...])
        _fill_pad(dpad_ref, ln_bwd(cvp_ref[...], dcp_ref[...])[0], dcv, ln_bwd(cvn_ref[...], dcn_ref[...])[0],
                  first, last)
        a, g = a_ref[...], g_ref[...]
        sg = _sigmoid(g)
        glu = lambda av, gv: av * _sigmoid(gv)
        _fill_pad(upad_ref, glu(ap_ref[...], gp_ref[...]), a * sg, glu(an_ref[...], gn_ref[...]), first, last)
        du = jnp.zeros((ROW_TILE, CONV_CH), F32)
        for k in range(CONV_K):
            du = du + w_ref[k:k + 1, :] * dpad_ref[CONV_K - k:CONV_K - k + ROW_TILE, :]
            dw_ref[k:k + 1, :] += _colsum(dcv * upad_ref[k + 1:k + 1 + ROW_TILE, :])
        dl_ref[0:1, :] += _colsum(dz * xh)
        dl_ref[1:2, :] += _colsum(dz)
        dp_ref[:, 0:CONV_CH] = (du * sg).astype(BF16)
        dp_ref[:, CONV_CH:2 * CONV_CH] = (du * a * sg * (1.0 - sg)).astype(BF16)

    cur = lambda cb: pl.BlockSpec((ROW_TILE, CONV_CH), lambda i: (i, cb))
    ap, an = _halo_specs(T, CONV_CH, a_col)
    gp, gn = _halo_specs(T, CONV_CH, g_col)
    cvp, cvn = _halo_specs(T, CONV_CH, 0)
    dcp, dcn = _halo_specs(T, CONV_CH, 1)
    return _call(
        body, grid=(nt,), name=name,
        in_specs=[cur(a_col), cur(g_col), ap, gp, an, gn, cur(0), cvp, cvn, cur(1), dcp, dcn,
                  _const((32, CONV_CH)), _const((1, CONV_CH)), _const((1, CONV_CH))],
        out_specs=[_row(2 * CONV_CH), _const((32, CONV_CH)), _const((8, CONV_CH))],
        out_shape=[jax.ShapeDtypeStruct((T, 2 * CONV_CH), BF16), jax.ShapeDtypeStruct((32, CONV_CH), F32),
                   jax.ShapeDtypeStruct((8, CONV_CH), F32)],
        scratch=[pltpu.VMEM((ROW_TILE + 2 * HALO, CONV_CH), F32), pltpu.VMEM((ROW_TILE + 2 * HALO, CONV_CH), F32)],
        sem=("arbitrary",), args=(p, p, p, p, p, p, cv, cv, cv, dmix, dmix, dmix, cw, lnw, lnb), comm=comm)


def _tile_positions(i, nxt, seq, ctx, offset, rows):
    is_ctx = i >= nxt
    pos0 = (i - jnp.where(is_ctx, nxt, 0)) * ROW_TILE + offset
    length = jnp.where(is_ctx, ctx, seq).astype(F32)
    pos = (pos0 + lax.broadcasted_iota(jnp.int32, (rows, 1), 0)).astype(F32)
    return pos, length


def _pool_count(pos, length, w):
    left = w // 2
    right = w - 1 - left
    return jnp.minimum(pos + right, length - 1.0) - jnp.maximum(pos - left, 0.0) + 1.0


def _gelu(x):
    return 0.5 * x * (1.0 + lax.erf(x * INV_SQRT2))


def _odd_fwd(p, pw, ps, slw, slb, sw, sbf, nxt, seq, ctx, name, comm=()):
    T = p.shape[0]
    nt = T // ROW_TILE

    def body(p_ref, pp_ref, pn_ref, pw_ref, ps_ref, lw_ref, lb_ref, sw_ref, sb_ref, o_ref, pad_ref):
        i = pl.program_id(0)
        first, last = _seq_edges(i, nxt, nt)
        _fill_pad(pad_ref, pp_ref[...], p_ref[:, 0:POOL_CH], pn_ref[...], first, last)
        pos, length = _tile_positions(i, nxt, seq, ctx, 0, ROW_TILE)
        for gi, w in enumerate(POOL_WINDOWS):
            sl = slice(gi * GROUP_CH, (gi + 1) * GROUP_CH)
            left = w // 2
            ssum = jnp.zeros((ROW_TILE, GROUP_CH), F32)
            for o in range(-left, w - left):
                ssum = ssum + pad_ref[HALO + o:HALO + o + ROW_TILE, sl]
            m = ssum / _pool_count(pos, length, w) - p_ref[:, sl]
            pre = _dot(m.astype(BF16), pw_ref[gi].astype(BF16))
            o_ref[:, sl] = (pre * ps_ref[:, sl]).astype(BF16)
        u = _gelu(p_ref[:, POOL_CH:POOL_CH + SG_CH])
        xh, _ = _ln_stats(_gelu(p_ref[:, POOL_CH + SG_CH:ODD_IN]))
        vln = xh * lw_ref[...] + lb_ref[...]
        for n in range(ROW_TILE // SG_CHUNK):
            rs = slice(n * SG_CHUNK, (n + 1) * SG_CHUNK)
            for gi in range(4):
                sl = slice(gi * GROUP_CH, (gi + 1) * GROUP_CH)
                s = _dot(sw_ref[gi].astype(BF16), vln[rs, sl].astype(BF16)) + sb_ref[:, sl]
                o_ref[rs, POOL_CH + gi * GROUP_CH:POOL_CH + (gi + 1) * GROUP_CH] = (u[rs, sl] * s).astype(BF16)

    pp, pn = _halo_specs(T, POOL_CH, 0)
    return _call(
        body, grid=(nt,), name=name,
        in_specs=[_row(ODD_IN), pp, pn, _const((4, GROUP_CH, GROUP_CH)), _const((1, POOL_CH)), _const((1, SG_CH)),
                  _const((1, SG_CH)), _const((4, SG_CHUNK, SG_CHUNK)), _const((SG_CHUNK, SG_CH))],
        out_specs=[_row(D)], out_shape=[jax.ShapeDtypeStruct((T, D), BF16)],
        scratch=[pltpu.VMEM((ROW_TILE + 2 * HALO, POOL_CH), F32)],
        sem=("parallel",), args=(p, p, p, pw, ps, slw, slb, sw, sbf), comm=comm)


def _odd_bwd(p, dmix, pw, ps, slw, slb, sw, sbf, nxt, seq, ctx, name, comm=()):
    T = p.shape[0]
    nt = T // ROW_TILE

    def body(p_ref, pp_ref, pn_ref, dm_ref, dmp_ref, dmn_ref, pw_ref, ps_ref, lw_ref, lb_ref, sw_ref, sb_ref,
             dp_ref, dpw_ref, dsw_ref, dv_ref, dsb_ref, pad_ref, dpad_ref):
        i = pl.program_id(0)
        first, last = _seq_edges(i, nxt, nt)

        @pl.when(i == 0)
        def _():
            dpw_ref[...] = jnp.zeros_like(dpw_ref)
            dsw_ref[...] = jnp.zeros_like(dsw_ref)
            dv_ref[...] = jnp.zeros_like(dv_ref)
            dsb_ref[...] = jnp.zeros_like(dsb_ref)

        _fill_pad(pad_ref, pp_ref[...], p_ref[:, 0:POOL_CH], pn_ref[...], first, last)
        pos, length = _tile_positions(i, nxt, seq, ctx, 0, ROW_TILE)
        pos_p, _ = _tile_positions(i, nxt, seq, ctx, -HALO, HALO)
        pos_n, _ = _tile_positions(i, nxt, seq, ctx, ROW_TILE, HALO)
        scale = ps_ref[...]
        for gi, w in enumerate(POOL_WINDOWS):
            sl = slice(gi * GROUP_CH, (gi + 1) * GROUP_CH)
            left = w // 2
            right = w - 1 - left
            ssum = jnp.zeros((ROW_TILE, GROUP_CH), F32)
            for o in range(-left, right + 1):
                ssum = ssum + pad_ref[HALO + o:HALO + o + ROW_TILE, sl]
            cnt = _pool_count(pos, length, w)
            m = ssum / cnt - p_ref[:, sl]
            wg = pw_ref[gi].astype(BF16)
            pre = _dot(m.astype(BF16), wg)
            dpo = dm_ref[:, sl]
            dv_ref[0:1, sl] += _colsum(dpo * pre)
            dpre = (dpo * scale[:, sl]).astype(BF16)
            dpw_ref[gi] += _dot_tn(m.astype(BF16), dpre)
            dmc = _dot_nt(dpre, wg)
            halo_dm = lambda ref, ps_: _dot_nt((ref[:, sl] * scale[:, sl]).astype(BF16), wg) / _pool_count(ps_, length, w)
            dpad_ref[0:HALO, sl] = jnp.where(first, 0.0, halo_dm(dmp_ref, pos_p))
            dpad_ref[HALO:HALO + ROW_TILE, sl] = dmc / cnt
            dpad_ref[HALO + ROW_TILE:2 * HALO + ROW_TILE, sl] = jnp.where(last, 0.0, halo_dm(dmn_ref, pos_n))
            atd = jnp.zeros((ROW_TILE, GROUP_CH), F32)
            for o in range(-right, left + 1):
                atd = atd + dpad_ref[HALO + o:HALO + o + ROW_TILE, sl]
            dp_ref[:, sl] = (atd - dmc).astype(BF16)

        pu = p_ref[:, POOL_CH:POOL_CH + SG_CH]
        pv = p_ref[:, POOL_CH + SG_CH:ODD_IN]
        u = _gelu(pu)
        xh, rs_ = _ln_stats(_gelu(pv))
        lw = lw_ref[...]
        vln = xh * lw + lb_ref[...]
        dgelu = lambda x: 0.5 * (1.0 + lax.erf(x * INV_SQRT2)) + x * jnp.exp(-0.5 * x * x) * INV_SQRT_2PI
        for n in range(ROW_TILE // SG_CHUNK):
            rs = slice(n * SG_CHUNK, (n + 1) * SG_CHUNK)
            dvl = []
            for gi in range(4):
                sl = slice(gi * GROUP_CH, (gi + 1) * GROUP_CH)
                wq = sw_ref[gi].astype(BF16)
                vb = vln[rs, sl].astype(BF16)
                s = _dot(wq, vb) + sb_ref[:, sl]
                dsg = dm_ref[rs, POOL_CH + gi * GROUP_CH:POOL_CH + (gi + 1) * GROUP_CH]
                ds = dsg * u[rs, sl]
                dsb_ref[:, sl] += ds
                dsw_ref[gi] += _dot_nt(ds.astype(BF16), vb)
                dvl.append(_dot_tn(wq, ds.astype(BF16)))
                dp_ref[rs, POOL_CH + gi * GROUP_CH:POOL_CH + (gi + 1) * GROUP_CH] = (
                    dsg * s * dgelu(pu[rs, sl])).astype(BF16)
            dvln = jnp.concatenate(dvl, axis=1)
            xhc = xh[rs, :]
            dv_ref[1:2, :] += _colsum(dvln * xhc)
            dv_ref[2:3, :] += _colsum(dvln)
            dxh = dvln * lw
            dvv = rs_[rs, :] * (dxh - _rowmean(dxh) - xhc * _rowmean(dxh * xhc))
            dp_ref[rs, POOL_CH + SG_CH:ODD_IN] = (dvv * dgelu(pv[rs, :])).astype(BF16)

    pp, pn = _halo_specs(T, POOL_CH, 0)
    dmp, dmn = _halo_specs(T, POOL_CH, 0)
    gsd = jax.ShapeDtypeStruct((4, GROUP_CH, GROUP_CH), F32)
    return _call(
        body, grid=(nt,), name=name,
        in_specs=[_row(ODD_IN), pp, pn, _row(D), dmp, dmn, _const((4, GROUP_CH, GROUP_CH)), _const((1, POOL_CH)),
                  _const((1, SG_CH)), _const((1, SG_CH)), _const((4, SG_CHUNK, SG_CHUNK)), _const((SG_CHUNK, SG_CH))],
        out_specs=[_row(ODD_IN), _const((4, GROUP_CH, GROUP_CH)), _const((4, SG_CHUNK, SG_CHUNK)), _const((8, POOL_CH)),
                   _const((SG_CHUNK, SG_CH))],
        out_shape=[jax.ShapeDtypeStruct((T, ODD_IN), BF16), gsd, gsd, jax.ShapeDtypeStruct((8, POOL_CH), F32),
                   jax.ShapeDtypeStruct((SG_CHUNK, SG_CH), F32)],
        scratch=[pltpu.VMEM((ROW_TILE + 2 * HALO, POOL_CH), F32), pltpu.VMEM((ROW_TILE + 2 * HALO, POOL_CH), F32)],
        sem=("arbitrary",), args=(p, p, p, dmix, dmix, dmix, pw, ps, slw, slb, sw, sbf), comm=comm)


def _ada_fwd(cs, aw, ab, name):
    cols = aw.shape[2]

    def body(c_ref, w_ref, b_ref, o_ref):
        c = c_ref[...]
        s = (c * _sigmoid(c)).astype(BF16)
        o_ref[0] = _dot(s, w_ref[0].astype(BF16)) + b_ref[0]

    return pl.pallas_call(
        body, grid=(DEPTH,), name=name,
        in_specs=[pl.BlockSpec((16, D), lambda i: (0, 0)), pl.BlockSpec((1, D, cols), lambda i: (i, 0, 0)),
                  pl.BlockSpec((1, 1, cols), lambda i: (i, 0, 0))],
        out_specs=pl.BlockSpec((1, 16, cols), lambda i: (i, 0, 0)),
        out_shape=jax.ShapeDtypeStruct((DEPTH, 16, cols), F32),
        compiler_params=_params(("parallel",)),
    )(cs, aw, ab)


def _ada_bwd(cs, G, aw, name):
    cols = aw.shape[2]

    def body(c_ref, g_ref, w_ref, gw_ref, cc_ref):
        i = pl.program_id(0)

        @pl.when(i == 0)
        def _():
            cc_ref[...] = jnp.zeros_like(cc_ref)

        c = c_ref[...]
        s = (c * _sigmoid(c)).astype(BF16)
        g = g_ref[0]
        dc = g[8:9, :]
        for d in range(9, 16):
            dc = dc + g[d:d + 1, :]
        row = lax.broadcasted_iota(jnp.int32, (8, cols), 0)
        dcrows = jnp.where(row == 0, dc, 0.0)
        dm = jnp.concatenate([g[0:8, :], dcrows], axis=0).astype(BF16)
        gw_ref[0] = _dot_tn(s, dm)
        cc_ref[...] += _dot_nt(dcrows.astype(BF16), w_ref[0].astype(BF16))

    return pl.pallas_call(
        body, grid=(DEPTH,), name=name,
        in_specs=[pl.BlockSpec((16, D), lambda i: (0, 0)), pl.BlockSpec((1, 16, cols), lambda i: (i, 0, 0)),
                  pl.BlockSpec((1, D, cols), lambda i: (i, 0, 0))],
        out_specs=[pl.BlockSpec((1, D, cols), lambda i: (i, 0, 0)), pl.BlockSpec((8, D), lambda i: (0, 0))],
        out_shape=[jax.ShapeDtypeStruct((DEPTH, D, cols), F32), jax.ShapeDtypeStruct((8, D), F32)],
        compiler_params=_params(("arbitrary",)),
    )(cs, G, aw)


def _row_tile_for(rows):
    for t in (512, 256, 128, 64, 32, 16, 8):
        if rows % t == 0:
            return t
    raise ValueError(f"rows={rows} is not a multiple of 8")


def _sum_devices(x, name):
    _, rows, cols = x.shape
    tr = _row_tile_for(rows)

    def body(x_ref, o_ref):
        acc = x_ref[0]
        for d in range(1, N_DEV):
            acc = acc + x_ref[d]
        o_ref[...] = acc

    return pl.pallas_call(
        body, grid=(rows // tr,), name=name,
        in_specs=[pl.BlockSpec((N_DEV, tr, cols), lambda i: (0, i, 0))],
        out_specs=pl.BlockSpec((tr, cols), lambda i: (i, 0)),
        out_shape=jax.ShapeDtypeStruct((rows, cols), F32),
        compiler_params=_params(("parallel",)),
    )(x)


def _adamw(w, g, m, v, name):
    rows, cols = w.shape
    tr = _row_tile_for(rows)
    pieces = g.ndim == 3

    def body(w_ref, g_ref, m_ref, v_ref, go_ref, d_ref, mo_ref, vo_ref):
        if pieces:
            gv = g_ref[0].astype(F32)
            for d in range(1, N_DEV):
                gv = gv + g_ref[d].astype(F32)
        else:
            gv = g_ref[...]
        mn = ADAM_B1 * m_ref[...] + (1.0 - ADAM_B1) * gv
        vn = ADAM_B2 * v_ref[...] + (1.0 - ADAM_B2) * (gv * gv)
        go_ref[...] = gv
        mo_ref[...] = mn
        vo_ref[...] = vn
        d_ref[...] = -ADAM_LR * ((mn / ADAM_BC1) / (jnp.sqrt(vn / ADAM_BC2) + ADAM_EPS) + ADAM_WD * w_ref[...])

    blk = pl.BlockSpec((tr, cols), lambda i: (i, 0))
    gblk = pl.BlockSpec((N_DEV, tr, cols), lambda i: (0, i, 0)) if pieces else blk
    sd = jax.ShapeDtypeStruct((rows, cols), F32)
    return pl.pallas_call(
        body, grid=(rows // tr,), name=name,
        in_specs=[blk, gblk, blk, blk], out_specs=[blk, blk, blk, blk], out_shape=[sd, sd, sd, sd],
        compiler_params=_params(("parallel",)),
    )(w, g, m, v)


def _adamw_layers(w, pieces, m, v, name):
    L, a, b = w.shape
    bp = pieces[0].shape[2]
    tr = next(t for t in (256, 128, 64, 32, 16) if a % t == 0)

    def body(w_ref, m_ref, v_ref, *rest):
        p_refs, (go_ref, d_ref, mo_ref, vo_ref) = rest[:L], rest[L:]
        layer = pl.program_id(0)
        for k in range(L):
            @pl.when(layer == k)
            def _(k=k):
                gv = p_refs[k][0].astype(F32)
                for d in range(1, N_DEV):
                    gv = gv + p_refs[k][d].astype(F32)
                gv = gv[:, :b]
                mn = ADAM_B1 * m_ref[0] + (1.0 - ADAM_B1) * gv
                vn = ADAM_B2 * v_ref[0] + (1.0 - ADAM_B2) * (gv * gv)
                go_ref[0] = gv
                mo_ref[0] = mn
                vo_ref[0] = vn
                d_ref[0] = -ADAM_LR * ((mn / ADAM_BC1) / (jnp.sqrt(vn / ADAM_BC2) + ADAM_EPS) + ADAM_WD * w_ref[0])

    blk = pl.BlockSpec((1, tr, b), lambda l, i: (l, i, 0))
    pspecs = [pl.BlockSpec((N_DEV, tr, bp), lambda l, i, k=k: (0, jnp.where(l == k, i, 0), 0)) for k in range(L)]
    sd = jax.ShapeDtypeStruct((L, a, b), F32)
    return pl.pallas_call(
        body, grid=(L, a // tr), name=name,
        in_specs=[blk, blk, blk] + pspecs, out_specs=[blk, blk, blk, blk], out_shape=[sd, sd, sd, sd],
        compiler_params=_params(("arbitrary", "arbitrary")),
    )(w, m, v, *pieces)


def _pack(arrs, cols, row_mult, dtype):
    flat = jnp.concatenate([a.reshape(-1).astype(dtype) for a in arrs])
    per = cols * row_mult
    total = -(-flat.shape[0] // per) * per
    return jnp.pad(flat, (0, total - flat.shape[0])).reshape(total // cols, cols)


def _unpack(flat, shapes):
    out, off = [], 0
    for s in shapes:
        n = math.prod(s)
        out.append(flat[..., off:off + n].reshape(flat.shape[:-1] + tuple(s)))
        off += n
    return out


def _rope_tables(seq, ctx):
    def angles(ps):
        parts = []
        for pvec, n in zip(ps, ROPE_PAIRS):
            freq = ROPE_BASE ** (-jnp.arange(n, dtype=F32) / n)
            parts.append(pvec[:, None] * freq[None, :])
        return jnp.concatenate(parts, axis=-1)

    rows = seq // GRID_W
    grid_r = jnp.broadcast_to(jnp.arange(rows, dtype=F32)[:, None], (rows, GRID_W)).reshape(-1)
    grid_c = jnp.broadcast_to(jnp.arange(GRID_W, dtype=F32)[None, :], (rows, GRID_W)).reshape(-1)
    zc = jnp.zeros((ctx,), F32)
    ang = jnp.concatenate([angles((jnp.full((seq,), ctx, F32), grid_r, grid_c)),
                           angles((jnp.arange(ctx, dtype=F32), zc, zc))], axis=0)
    cos, sin = jnp.cos(ang), jnp.sin(ang)
    return jnp.concatenate([cos, cos], axis=1), jnp.concatenate([-sin, sin], axis=1)


def _in_gather(shard, even):
    return ("ag_cols" if even else "ag_blk", shard)


def _in_full(got, even):
    return got if even else got.transpose(1, 0, 2).reshape(D, ODD_IN)


def _sample_step(X0, tgt, mods, shards, S, seq, ctx):
    nxt = seq // ROW_TILE
    cosf, sgn = _rope_tables(seq, ctx)
    X = X0
    saved = []
    w_in, w_out = _exchange([_in_gather(shards[0]["in"], True), ("ag_rows", shards[0]["out"])], "gather_l0")
    for i in range(DEPTH):
        j, even = i // 2, i % 2 == 0
        md, sh = mods[i], shards[i]
        t = f"l{i}_"
        h1 = _norm_mod_fwd(X, S["norm_w"][i, 0][None], md, 0, nxt, t + "norm1")
        if even:
            (p,), _ = _mm_nn(h1, w_in, F32, t + "proj_in")
            (yret, ro), (wg, wu) = _ret_fwd(p, S["lg"][j], cosf, sgn, seq, t + "ret_fwd",
                                            comm=[("ag_cols", sh["gate"]), ("ag_cols", sh["up"])])
            (cv, co), (wd,) = _conv_fwd(p, S["conv_w"][j], S["conv_ln_w"][j][None], S["conv_ln_b"][j][None], nxt,
                                        t + "conv_fwd", comm=[("ag_rows", sh["down"])])
            mix = jnp.concatenate([ro, co], axis=1)
            extra = (yret, cv)
            (X1, y1), _ = _mm_nn_resid(mix, w_out, X, md, 2, nxt, t + "proj_out")
        else:
            (p,), (wg,) = _mm_nn(h1, w_in, F32, t + "proj_in", comm=[("ag_cols", sh["gate"])])
            sbf = jnp.repeat(S["sg_b"][j].T, GROUP_CH, axis=1)
            (mix,), (wu,) = _odd_fwd(p, S["pool_w"][j], S["pool_scale"][j][None], S["sg_ln_w"][j][None],
                                     S["sg_ln_b"][j][None], S["sg_w"][j], sbf, nxt, seq, ctx, t + "odd_fwd",
                                     comm=[("ag_cols", sh["up"])])
            extra = (sbf,)
            (X1, y1), (wd,) = _mm_nn_resid(mix, w_out, X, md, 2, nxt, t + "proj_out",
                                           comm=[("ag_rows", sh["down"])])
        h2 = _norm_mod_fwd(X1, S["norm_w"][i, 1][None], md, 3, nxt, t + "norm2")
        last = i == DEPTH - 1
        (gp, up, act), got_in = _ffn_up(h2, wg, wu, t + "ffn_up",
                                        comm=[] if last else [_in_gather(shards[i + 1]["in"], not even)])
        (X2, y2), got_out = _mm_nn_resid(act, wd, X1, md, 5, nxt, t + "ffn_down",
                                         comm=[] if last else [("ag_rows", shards[i + 1]["out"])])
        saved.append((X, h1, p, mix, extra, y1, X1, h2, gp, up, act, y2, (w_in, w_out, wg, wu, wd)))
        X = X2
        if not last:
            w_in, w_out = _in_full(got_in[0], not even), got_out[0]

    dX, hst = _loss_head(X, tgt, S["final_norm_w"][None], nxt, "loss_head")
    loss = jnp.sum(hst[1])
    pieces = {}
    gS = {"final_norm_w": hst[0], "norm_w": [None] * DEPTH, "lg": [None] * 2, "conv_w": [None] * 2,
          "conv_ln_w": [None] * 2, "conv_ln_b": [None] * 2, "pool_w": [None] * 2, "pool_scale": [None] * 2,
          "sg_ln_w": [None] * 2, "sg_ln_b": [None] * 2, "sg_w": [None] * 2, "sg_b": [None] * 2}
    dmods = [None] * DEPTH
    send_in = None
    for i in reversed(range(DEPTH)):
        j, even = i // 2, i % 2 == 0
        md = mods[i]
        t = f"l{i}_"
        X_in, h1, p, mix, extra, y1, X1, h2, gp, up, act, y2, (w_in, w_out, wg, wu, wd) = saved[i]
        dyf, g2 = _resid_bwd(dX, y2, md, 5, nxt, t + "ffn_resid_bwd")
        (dgp, dup), got = _ffn_dact(dyf, wd, gp, up, t + "ffn_dact", comm=[] if send_in is None else [send_in])
        if got:
            pieces[("in", i + 1)] = got[0]
        g_down = _mm_tn(act, dyf, t + "dw_down")
        g_gate = _mm_tn(h2, dgp, t + "dw_gate")
        g_up = _mm_tn(h2, dup, t + "dw_up")
        ffn_sends = [("a2a_rows", g_down), ("a2a_cols", g_gate), ("a2a_cols", g_up)]
        if even:
            (dh2,), _ = _mm_nt(dgp, wg, t + "dh2_gate")
            (dh2,), _ = _mm_nt(dup, wu, t + "dh2_up", add=dh2)
        else:
            (dh2,), (pieces[("down", i)],) = _mm_nt(dgp, wg, t + "dh2_gate", comm=ffn_sends[0:1])
            (dh2,), (pieces[("gate", i)],) = _mm_nt(dup, wu, t + "dh2_up", add=dh2, comm=ffn_sends[1:2])
        dX1, s2 = _norm_mod_bwd(dh2, X1, S["norm_w"][i, 1][None], md, dX, 3, nxt, t + "norm2_bwd")
        dym, g1 = _resid_bwd(dX1, y1, md, 2, nxt, t + "mix_resid_bwd")
        (dmix,), _ = _mm_nt(dym, w_out, t + "dmix")
        g_out = _mm_tn(mix, dym, t + "dw_out")
        if even:
            yret, cv = extra
            dyr, dg = _ret_gate_bwd(yret, p, dmix, t + "ret_gate_bwd")
            (dq, dk, dv, dlg), got = _ret_bwd(p, S["lg"][j], dyr, cosf, sgn, seq, t + "ret_bwd", comm=ffn_sends)
            pieces[("down", i)], pieces[("gate", i)], pieces[("up", i)] = got
            (dpc, dcw, dln), (pieces[("out", i)],) = _conv_bwd(
                p, cv, dmix, S["conv_w"][j], S["conv_ln_w"][j][None], S["conv_ln_b"][j][None], nxt,
                t + "conv_bwd", comm=[("a2a_rows", g_out)])
            dp = jnp.concatenate([dq, dk, dv, dg, dpc], axis=1)
            gS["lg"][j] = dlg[:, 0:2, 0].T
            gS["conv_w"][j], gS["conv_ln_w"][j], gS["conv_ln_b"][j] = dcw, dln[0], dln[1]
        else:
            (sbf,) = extra
            (dp, dpw, dsw, dvec, dsb), (pieces[("up", i)],) = _odd_bwd(
                p, dmix, S["pool_w"][j], S["pool_scale"][j][None], S["sg_ln_w"][j][None], S["sg_ln_b"][j][None],
                S["sg_w"][j], sbf, nxt, seq, ctx, t + "odd_bwd", comm=ffn_sends[2:3])
            gS["pool_w"][j], gS["sg_w"][j] = dpw, dsw
            gS["pool_scale"][j], gS["sg_ln_w"][j], gS["sg_ln_b"][j] = dvec[0], dvec[1], dvec[2]
            gS["sg_b"][j] = jnp.sum(dsb.reshape(SG_CHUNK, 4, GROUP_CH), axis=2).T
        g_in = _mm_tn(h1, dp, t + "dw_in")
        if even:
            (dh1,), _ = _mm_nt(dp, w_in, t + "dh1")
            send_in = ("a2a_cols", g_in)
        else:
            (dh1,), (pieces[("out", i)],) = _mm_nt(dp, w_in, t + "dh1", comm=[("a2a_rows", g_out)])
            send_in = ("a2a_blk", g_in.reshape(D, N_DEV, ODD_IN // N_DEV).transpose(1, 0, 2))
        dX, s1 = _norm_mod_bwd(dh1, X_in, S["norm_w"][i, 0][None], md, dX1, 0, nxt, t + "norm1_bwd")
        gS["norm_w"][i] = jnp.stack([s1[0, 2] + s1[1, 2], s2[0, 2] + s2[1, 2]])
        dmods[i] = jnp.stack([s1[:, 0], s1[:, 1], g1[:, 0], s2[:, 0], s2[:, 1], g2[:, 0]], axis=1)
    (pieces[("in", 0)],) = _exchange([send_in], "scatter_in0")
    gS = {k: (jnp.stack(v) if isinstance(v, list) else v) for k, v in gS.items()}
    return loss, dX, pieces, gS, jnp.stack(dmods)


FF_SHARD = D_FF // N_DEV
FF_SHARD_PAD = 384


def kernel(x, c, ctx, c_ctx, ada_w, ada_b, norm_w, even_w_in, even_w_out, ret_decay_logit, conv_dw_w, conv_ln_w, conv_ln_b, odd_w_in, odd_w_out, pool_w, pool_scale, sg_ln_w, sg_ln_b, sg_w, sg_b, ffn_w_gate, ffn_w_up, ffn_w_down, final_norm_w, loss_target, m_c_ctx, m_ada_w, m_ada_b, m_norm_w, m_even_w_in, m_even_w_out, m_ret_decay_logit, m_conv_dw_w, m_conv_ln_w, m_conv_ln_b, m_odd_w_in, m_odd_w_out, m_pool_w, m_pool_scale, m_sg_ln_w, m_sg_ln_b, m_sg_w, m_sg_b, m_ffn_w_gate, m_ffn_w_up, m_ffn_w_down, m_final_norm_w, v_c_ctx, v_ada_w, v_ada_b, v_norm_w, v_even_w_in, v_even_w_out, v_ret_decay_logit, v_conv_dw_w, v_conv_ln_w, v_conv_ln_b, v_odd_w_in, v_odd_w_out, v_pool_w, v_pool_scale, v_sg_ln_w, v_sg_ln_b, v_sg_w, v_sg_b, v_ffn_w_gate, v_ffn_w_up, v_ffn_w_down, v_final_norm_w):
    seq, n_ctx = x.shape[1], ctx.shape[1]
    me = 4 * lax.axis_index("x") + 2 * lax.axis_index("y") + lax.axis_index("c")
    mcols = ada_w.shape[2]

    small_shapes = [(D,), norm_w.shape, conv_dw_w.shape, pool_scale.shape, sg_ln_w.shape, sg_ln_b.shape]
    (sm,) = _exchange([("ag_blk", _pack([c, norm_w, conv_dw_w, pool_scale, sg_ln_w, sg_ln_b], LANES, 8, F32))],
                      "gather_small")
    c_all, nw_s, cw_s, ps_s, slw_s, slb_s = _unpack(sm.reshape(N_DEV, -1), small_shapes)
    cat_last = lambda a: jnp.moveaxis(a, 0, -2).reshape(a.shape[1:-1] + (-1,))
    conv_w_full = cat_last(cw_s)
    S = {"norm_w": cat_last(nw_s), "lg": jax.nn.log_sigmoid(ret_decay_logit),
         "conv_w": jnp.pad(conv_w_full, ((0, 0), (0, 32 - CONV_K), (0, 0))),
         "conv_ln_w": conv_ln_w, "conv_ln_b": conv_ln_b, "pool_w": pool_w, "pool_scale": cat_last(ps_s),
         "sg_ln_w": cat_last(slw_s), "sg_ln_b": cat_last(slb_s), "sg_w": sg_w, "sg_b": sg_b,
         "final_norm_w": final_norm_w}

    cs = jnp.concatenate([c_all, c_ctx[None], jnp.zeros((7, D), F32)], axis=0)
    ab_loc = lax.dynamic_slice_in_dim(ada_b, me * mcols, mcols, axis=1)
    mod_loc = _ada_fwd(cs, ada_w, ab_loc[:, None, :], "ada_fwd")
    (mod_all,) = _exchange([("ag_blk", mod_loc.reshape(DEPTH * 16, mcols))], "gather_mod")
    mod_all = mod_all.reshape(N_DEV, DEPTH, 16, mcols).transpose(1, 2, 0, 3).reshape(DEPTH, 16, 6, D)
    mod_x = lax.dynamic_index_in_dim(mod_all, me, axis=1, keepdims=False)
    mods = jnp.stack([mod_x, mod_all[:, 8]], axis=1)

    fpad = FF_SHARD_PAD - FF_SHARD
    shards = []
    for i in range(DEPTH):
        j, even = i // 2, i % 2 == 0
        w_in, w_out = (even_w_in[j], even_w_out[j]) if even else (odd_w_in[j], odd_w_out[j])
        shards.append({"in": w_in.astype(BF16), "out": w_out.astype(BF16),
                       "gate": jnp.pad(ffn_w_gate[i], ((0, 0), (0, fpad))).astype(BF16),
                       "up": jnp.pad(ffn_w_up[i], ((0, 0), (0, fpad))).astype(BF16),
                       "down": jnp.pad(ffn_w_down[i], ((0, fpad), (0, 0))).astype(BF16)})

    X0 = jnp.concatenate([x[0], ctx[0]], axis=0)
    loss, dX, big, gS, dmods = _sample_step(X0, loss_target[0], mods, shards, S, seq, n_ctx)
    loss = lax.psum(loss, ("x", "y", "c"))
    grad_x = dX[:seq][None]

    (dm_all,) = _exchange([("ag_blk", dmods.reshape(DEPTH * 2, 6 * D))], "gather_dmod")
    dm_all = dm_all.reshape(N_DEV, DEPTH, 2, 6 * D)
    dm_sum = _sum_devices(dm_all.reshape(N_DEV, DEPTH * 2, 6 * D), "sum_dmod").reshape(DEPTH, 2, 6 * D)
    g_ada_b = dm_sum[:, 0] + dm_sum[:, 1]
    G = lax.dynamic_slice_in_dim(dm_all, me * mcols, mcols, axis=3).transpose(1, 2, 0, 3).reshape(DEPTH, 16, mcols)
    g_ada_w, ccp = _ada_bwd(cs, G, ada_w, "ada_bwd")
    sg_cc = _sigmoid(c_ctx)
    g_cctx_part = ccp[0] * sg_cc * (1.0 + c_ctx * (1.0 - sg_cc))

    dsig = _sigmoid(-ret_decay_logit)
    part = [g_cctx_part, gS["norm_w"], gS["lg"] * dsig, gS["conv_w"][:, :CONV_K], gS["conv_ln_w"], gS["conv_ln_b"],
            gS["pool_w"], gS["pool_scale"], gS["sg_ln_w"], gS["sg_ln_b"], gS["sg_w"], gS["sg_b"],
            gS["final_norm_w"]]
    part_shapes = [a.shape for a in part]
    (part_all,) = _exchange([("ag_blk", _pack(part, LANES, 8, F32))], "gather_small_grads")
    red = _sum_devices(part_all, "sum_small_grads")
    (g_cctx, g_nw, g_rdl, g_cw, g_clw, g_clb, g_pw, g_ps, g_slw, g_slb, g_sw, g_sb, g_fnw) = _unpack(
        red.reshape(-1), part_shapes)
    mine_last = lambda a, n: lax.dynamic_slice_in_dim(a, me * n, n, axis=a.ndim - 1)
    g_nw, g_cw = mine_last(g_nw, norm_w.shape[2]), mine_last(g_cw, conv_dw_w.shape[2])
    g_ps, g_slw, g_slb = (mine_last(a, pool_scale.shape[1]) for a in (g_ps, g_slw, g_slb))

    def adam_big(key, layers, w, m, v, name):
        return _adamw_layers(w, [big[(key, l)] for l in layers], m, v, name)

    res = {}
    res["even_w_in"] = adam_big("in", (0, 2), even_w_in, m_even_w_in, v_even_w_in, "adam_even_in")
    res["even_w_out"] = adam_big("out", (0, 2), even_w_out, m_even_w_out, v_even_w_out, "adam_even_out")
    res["odd_w_in"] = adam_big("in", (1, 3), odd_w_in, m_odd_w_in, v_odd_w_in, "adam_odd_in")
    res["odd_w_out"] = adam_big("out", (1, 3), odd_w_out, m_odd_w_out, v_odd_w_out, "adam_odd_out")
    res["ffn_w_gate"] = adam_big("gate", (0, 1, 2, 3), ffn_w_gate, m_ffn_w_gate, v_ffn_w_gate, "adam_gate")
    res["ffn_w_up"] = adam_big("up", (0, 1, 2, 3), ffn_w_up, m_ffn_w_up, v_ffn_w_up, "adam_up")
    res["ffn_w_down"] = adam_big("down", (0, 1, 2, 3), ffn_w_down, m_ffn_w_down, v_ffn_w_down, "adam_down")
    flat2 = lambda a: a.reshape(-1, a.shape[-1])
    res["ada_w"] = [o.reshape(ada_w.shape) for o in
                    _adamw(flat2(ada_w), flat2(g_ada_w), flat2(m_ada_w), flat2(v_ada_w), "adam_ada_w")]

    names = ["c_ctx", "ada_b", "norm_w", "ret_decay_logit", "conv_dw_w", "conv_ln_w", "conv_ln_b", "pool_w",
             "pool_scale", "sg_ln_w", "sg_ln_b", "sg_w", "sg_b", "final_norm_w"]
    ws = [c_ctx, ada_b, norm_w, ret_decay_logit, conv_dw_w, conv_ln_w, conv_ln_b, pool_w, pool_scale, sg_ln_w,
          sg_ln_b, sg_w, sg_b, final_norm_w]
    gs = [g_cctx, g_ada_b, g_nw, g_rdl, g_cw, g_clw, g_clb, g_pw, g_ps, g_slw, g_slb, g_sw, g_sb, g_fnw]
    ms = [m_c_ctx, m_ada_b, m_norm_w, m_ret_decay_logit, m_conv_dw_w, m_conv_ln_w, m_conv_ln_b, m_pool_w,
          m_pool_scale, m_sg_ln_w, m_sg_ln_b, m_sg_w, m_sg_b, m_final_norm_w]
    vs = [v_c_ctx, v_ada_b, v_norm_w, v_ret_decay_logit, v_conv_dw_w, v_conv_ln_w, v_conv_ln_b, v_pool_w,
          v_pool_scale, v_sg_ln_w, v_sg_ln_b, v_sg_w, v_sg_b, v_final_norm_w]
    shapes = [a.shape for a in ws]
    gs = [g.reshape(s) for g, s in zip(gs, shapes)]
    pk = lambda arrs: _pack(arrs, LANES, 512, F32)
    outs = _adamw(pk(ws), pk(gs), pk(ms), pk(vs), "adam_small")
    for k, o in enumerate(outs):
        for nm, arr in zip(names, _unpack(o.reshape(-1), shapes)):
            res.setdefault(nm, [None] * 4)[k] = arr

    order = ["c_ctx", "ada_w", "ada_b", "norm_w", "even_w_in", "even_w_out", "ret_decay_logit", "conv_dw_w",
             "conv_ln_w", "conv_ln_b", "odd_w_in", "odd_w_out", "pool_w", "pool_scale", "sg_ln_w", "sg_ln_b",
             "sg_w", "sg_b", "ffn_w_gate", "ffn_w_up", "ffn_w_down", "final_norm_w"]
    return (loss, grad_x, *[res[n][0] for n in order], *[res[n][1] for n in order],
            *[res[n][2] for n in order], *[res[n][3] for n in order])
```

```python
import math

import jax
import jax.numpy as jnp
from jax import lax
from jax.experimental import pallas as pl
from jax.experimental.pallas import tpu as pltpu

F32 = jnp.float32
BF16 = jnp.bfloat16

N_DEV = 8
D = 1024
DEPTH = 4
ROW_TILE = 256
HALO = 16
LANES = 128
EPS = 1e-6
RET_HEADS = 4
HEAD_DIM = 128
RET_W = RET_HEADS * HEAD_DIM
CHUNK = 128
Q_SCALE = HEAD_DIM ** -0.5
ROPE_BASE = 10000.0
ROPE_PAIRS = (HEAD_DIM // 8, 3 * HEAD_DIM // 16, 3 * HEAD_DIM // 16)
GRID_W = 64
CONV_CH = 512
CONV_K = 31
EVEN_IN = 4 * RET_W + 2 * CONV_CH
POOL_CH = 512
POOL_WINDOWS = (2, 4, 8, 16)
GROUP_CH = 128
SG_CH = 512
SG_CHUNK = 128
ODD_IN = POOL_CH + 2 * SG_CH
D_FF = 2816
INV_SQRT2 = 1.0 / math.sqrt(2.0)
INV_SQRT_2PI = 1.0 / math.sqrt(2.0 * math.pi)
ADAM_LR, ADAM_B1, ADAM_B2, ADAM_EPS, ADAM_WD, ADAM_STEP = 0.001, 0.9, 0.999, 1e-08, 0.01, 10
ADAM_BC1 = 1.0 - ADAM_B1 ** ADAM_STEP
ADAM_BC2 = 1.0 - ADAM_B2 ** ADAM_STEP
VMEM_LIMIT = 56 * 1024 * 1024
MESH = pl.DeviceIdType.MESH


def _params(sem=None):
    return pltpu.CompilerParams(dimension_semantics=sem, vmem_limit_bytes=VMEM_LIMIT)


def _dot(a, b):
    return jnp.dot(a, b, preferred_element_type=F32)


def _dot_nt(a, b):
    return lax.dot_general(a, b, (((1,), (1,)), ((), ())), preferred_element_type=F32)


def _dot_tn(a, b):
    return lax.dot_general(a, b, (((0,), (0,)), ((), ())), preferred_element_type=F32)


def _sigmoid(x):
    return 1.0 / (1.0 + jnp.exp(-x))


def _colsum(x):
    return jnp.sum(x, axis=0, keepdims=True)


def _rowmean(x):
    return jnp.mean(x, axis=-1, keepdims=True)


def _row(shape_cols):
    return pl.BlockSpec((ROW_TILE, shape_cols), lambda i: (i, 0))


def _const(shape):
    nd = len(shape)
    return pl.BlockSpec(shape, lambda i: (0,) * nd)


def _mod_spec(nxt):
    return pl.BlockSpec((1, 6, D), lambda i: (i // nxt, 0, 0))


def _stat_spec(nxt):
    return pl.BlockSpec((1, 8, D), lambda i: (i // nxt, 0, 0))


def _xfer_out_shape(kind, a):
    r, c = a.shape
    shape = {"ag_blk": (N_DEV, r, c), "ag_cols": (r, N_DEV * c), "ag_rows": (N_DEV * r, c),
             "a2a_cols": (N_DEV, r, c // N_DEV), "a2a_rows": (N_DEV, r // N_DEV, c)}[kind]
    return jax.ShapeDtypeStruct(shape, a.dtype)


def _src_view(kind, ref, who):
    if kind == "a2a_cols":
        n = ref.shape[1] // N_DEV
        return ref.at[:, pl.ds(pl.multiple_of(who * n, LANES), n)]
    if kind == "a2a_rows":
        r = ref.shape[0] // N_DEV
        return ref.at[pl.ds(pl.multiple_of(who * r, 16), r), :]
    return ref


def _dst_view(kind, ref, who):
    if kind == "ag_cols":
        n = ref.shape[1] // N_DEV
        return ref.at[:, pl.ds(pl.multiple_of(who * n, LANES), n)]
    if kind == "ag_rows":
        r = ref.shape[0] // N_DEV
        return ref.at[pl.ds(pl.multiple_of(who * r, 16), r), :]
    return ref.at[who]


def _xfers(kinds, srcs, dsts, send_sems, recv_sems, local_sems, start):
    mx, my, mc = lax.axis_index("x"), lax.axis_index("y"), lax.axis_index("c")
    me = 4 * mx + 2 * my + mc
    peers = []
    for k in range(1, N_DEV):
        px, py, pc = mx ^ ((k >> 2) & 1), my ^ ((k >> 1) & 1), mc ^ (k & 1)
        peers.append(((px, py, pc), 4 * px + 2 * py + pc))
    for j, (kind, src, dst) in enumerate(zip(kinds, srcs, dsts)):
        own = pltpu.make_async_copy(_src_view(kind, src, me), _dst_view(kind, dst, me), local_sems.at[j])
        if start:
            own.start()
        for k, (pid, pidx) in enumerate(peers):
            sem = (N_DEV - 1) * j + k
            cp = pltpu.make_async_remote_copy(
                src_ref=_src_view(kind, src, pidx), dst_ref=_dst_view(kind, dst, me if start else pidx),
                send_sem=send_sems.at[sem], recv_sem=recv_sems.at[sem], device_id=pid, device_id_type=MESH)
            if start:
                cp.start()
            else:
                cp.wait_recv()
                cp.wait_send()
        if not start:
            own.wait()


def _xfer_sems(n):
    return [pltpu.SemaphoreType.DMA(((N_DEV - 1) * n,)), pltpu.SemaphoreType.DMA(((N_DEV - 1) * n,)),
            pltpu.SemaphoreType.DMA((n,))]


def _exchange(comm, name):
    kinds = [k for k, _ in comm]
    n = len(comm)

    def body(*refs):
        sems = refs[2 * n:]
        _xfers(kinds, refs[:n], refs[n:2 * n], *sems, start=True)
        _xfers(kinds, refs[:n], refs[n:2 * n], *sems, start=False)

    hbm = pl.BlockSpec(memory_space=pl.ANY)
    return pl.pallas_call(
        body, name=name, in_specs=[hbm] * n, out_specs=[hbm] * n,
        out_shape=[_xfer_out_shape(k, a) for k, a in comm], scratch_shapes=_xfer_sems(n),
    )(*[a for _, a in comm])


def _call(body, *, grid, in_specs, out_specs, out_shape, args, name, sem, scratch=(), comm=()):
    n_in, n_out, n_scr, n_c = len(in_specs), len(out_specs), len(scratch), len(comm)
    if not comm:
        outs = pl.pallas_call(body, grid=grid, in_specs=list(in_specs), out_specs=list(out_specs),
                              out_shape=list(out_shape), scratch_shapes=list(scratch), name=name,
                              compiler_params=_params(sem))(*args)
        return list(outs), []
    kinds = [k for k, _ in comm]
    n_steps = grid[0]

    def wrapped(*refs):
        ins, csrc = refs[:n_in], refs[n_in:n_in + n_c]
        o0 = n_in + n_c
        outs, cdst = refs[o0:o0 + n_out], refs[o0 + n_out:o0 + n_out + n_c]
        s0 = o0 + n_out + n_c
        scr, sems = refs[s0:s0 + n_scr], refs[s0 + n_scr:]
        i = pl.program_id(0)

        @pl.when(i == 0)
        def _():
            _xfers(kinds, csrc, cdst, *sems, start=True)

        body(*ins, *outs, *scr)

        @pl.when(i == n_steps - 1)
        def _():
            _xfers(kinds, csrc, cdst, *sems, start=False)

    hbm = pl.BlockSpec(memory_space=pl.ANY)
    outs = pl.pallas_call(
        wrapped, grid=grid, in_specs=list(in_specs) + [hbm] * n_c, out_specs=list(out_specs) + [hbm] * n_c,
        out_shape=list(out_shape) + [_xfer_out_shape(k, a) for k, a in comm],
        scratch_shapes=list(scratch) + _xfer_sems(n_c), name=name, compiler_params=_params(("arbitrary",)),
    )(*args, *[a for _, a in comm])
    return list(outs[:n_out]), list(outs[n_out:])


def _norm_mod_fwd(X, nw, mods, si, nxt, name):
    T = X.shape[0]

    def body(x_ref, w_ref, m_ref, h_ref):
        x = x_ref[...]
        r = lax.rsqrt(_rowmean(x * x) + EPS)
        m = m_ref[0]
        y = x * r * w_ref[...]
        h_ref[...] = (y * (1.0 + m[si + 1:si + 2, :]) + m[si:si + 1, :]).astype(BF16)

    return pl.pallas_call(
        body, grid=(T // ROW_TILE,), name=name,
        in_specs=[_row(D), _const((1, D)), _mod_spec(nxt)],
        out_specs=_row(D), out_shape=jax.ShapeDtypeStruct((T, D), BF16),
        compiler_params=_params(("parallel",)),
    )(X, nw, mods)


def _norm_mod_bwd(dh, X, nw, mods, dres, si, nxt, name):
    T = X.shape[0]

    def body(dh_ref, x_ref, w_ref, m_ref, dres_ref, dx_ref, st_ref):
        i = pl.program_id(0)

        @pl.when((i == 0) | (i == nxt))
        def _():
            st_ref[...] = jnp.zeros_like(st_ref)

        x = x_ref[...]
        r = lax.rsqrt(_rowmean(x * x) + EPS)
        xn = x * r
        w = w_ref[...]
        scale = m_ref[0][si + 1:si + 2, :]
        dhv = dh_ref[...]
        dy = dhv * (1.0 + scale)
        dxn = dy * w
        dx_ref[...] = dres_ref[...] + r * (dxn - xn * _rowmean(dxn * xn))
        st_ref[0, 0:1, :] += _colsum(dhv)
        st_ref[0, 1:2, :] += _colsum(dhv * xn * w)
        st_ref[0, 2:3, :] += _colsum(dy * xn)

    return pl.pallas_call(
        body, grid=(T // ROW_TILE,), name=name,
        in_specs=[_row(D), _row(D), _const((1, D)), _mod_spec(nxt), _row(D)],
        out_specs=[_row(D), _stat_spec(nxt)],
        out_shape=[jax.ShapeDtypeStruct((T, D), F32), jax.ShapeDtypeStruct((2, 8, D), F32)],
        compiler_params=_params(("arbitrary",)),
    )(dh, X, nw, mods, dres)


def _resid_bwd(dX, y, mods, gi, nxt, name):
    T = dX.shape[0]

    def body(dx_ref, y_ref, m_ref, dy_ref, st_ref):
        i = pl.program_id(0)

        @pl.when((i == 0) | (i == nxt))
        def _():
            st_ref[...] = jnp.zeros_like(st_ref)

        dx = dx_ref[...]
        dy_ref[...] = (dx * m_ref[0][gi:gi + 1, :]).astype(BF16)
        st_ref[0, 0:1, :] += _colsum(dx * y_ref[...].astype(F32))

    return pl.pallas_call(
        body, grid=(T // ROW_TILE,), name=name,
        in_specs=[_row(D), _row(D), _mod_spec(nxt)],
        out_specs=[_row(D), _stat_spec(nxt)],
        out_shape=[jax.ShapeDtypeStruct((T, D), BF16), jax.ShapeDtypeStruct((2, 8, D), F32)],
        compiler_params=_params(("arbitrary",)),
    )(dX, y, mods)


def _loss_head(X, tgt, fw, nxt, name):
    T = X.shape[0]

    def body(x_ref, t_ref, w_ref, dx_ref, st_ref):
        i = pl.program_id(0)

        @pl.when(i == 0)
        def _():
            st_ref[...] = jnp.zeros_like(st_ref)

        @pl.when(i < nxt)
        def _():
            x = x_ref[...]
            r = lax.rsqrt(_rowmean(x * x) + EPS)
            xn = x * r
            w = w_ref[...]
            err = xn * w - t_ref[...]
            dy = err * (1.0 / D)
            dxn = dy * w
            dx_ref[...] = r * (dxn - xn * _rowmean(dxn * xn))
            st_ref[0:1, :] += _colsum(dy * xn)
            st_ref[1:2, :] += (0.5 / D) * _colsum(err * err)

        @pl.when(i >= nxt)
        def _():
            dx_ref[...] = jnp.zeros_like(dx_ref)

    return pl.pallas_call(
        body, grid=(T // ROW_TILE,), name=name,
        in_specs=[_row(D), pl.BlockSpec((ROW_TILE, D), lambda i: (jnp.minimum(i, nxt - 1), 0)), _const((1, D))],
        out_specs=[_row(D), _const((8, D))],
        out_shape=[jax.ShapeDtypeStruct((T, D), F32), jax.ShapeDtypeStruct((8, D), F32)],
        compiler_params=_params(("arbitrary",)),
    )(X, tgt, fw)


def _mm_nn(A, B, out_dtype, name, add=None, comm=()):
    T, K = A.shape
    N = B.shape[1]

    if add is None:
        def body(a_ref, b_ref, o_ref):
            o_ref[...] = _dot(a_ref[...], b_ref[...]).astype(out_dtype)
        ins, specs = (A, B), [_row(K), _const((K, N))]
    else:
        def body(a_ref, b_ref, c_ref, o_ref):
            o_ref[...] = (c_ref[...] + _dot(a_ref[...], b_ref[...])).astype(out_dtype)
        ins, specs = (A, B, add), [_row(K), _const((K, N)), _row(N)]

    return _call(body, grid=(T // ROW_TILE,), name=name, in_specs=specs,
                 out_specs=[_row(N)], out_shape=[jax.ShapeDtypeStruct((T, N), out_dtype)],
                 sem=("parallel",), args=ins, comm=comm)


def _mm_nn_resid(A, B, X, mods, gi, nxt, name, comm=()):
    T, K = A.shape
    N = B.shape[1]

    def body(a_ref, b_ref, x_ref, m_ref, xo_ref, y_ref):
        acc = _dot(a_ref[...], b_ref[...])
        y_ref[...] = acc.astype(BF16)
        xo_ref[...] = x_ref[...] + m_ref[0][gi:gi + 1, :] * acc

    return _call(body, grid=(T // ROW_TILE,), name=name,
                 in_specs=[_row(K), _const((K, N)), _row(N), _mod_spec(nxt)], out_specs=[_row(N), _row(N)],
                 out_shape=[jax.ShapeDtypeStruct((T, N), F32), jax.ShapeDtypeStruct((T, N), BF16)],
                 sem=("parallel",), args=(A, B, X, mods), comm=comm)


def _mm_nt(A, B, name, comm=()):
    T, N = A.shape
    K = B.shape[0]

    def body(a_ref, b_ref, o_ref):
        o_ref[...] = _dot_nt(a_ref[...], b_ref[...])

    return _call(body, grid=(T // ROW_TILE,), name=name, in_specs=[_row(N), _const((K, N))], out_specs=[_row(K)],
                 out_shape=[jax.ShapeDtypeStruct((T, K), F32)], sem=("parallel",), args=(A, B), comm=comm)


def _mm_tn(A, G, name):
    T, K = A.shape
    N = G.shape[1]
    nt = T // ROW_TILE

    def body(a_ref, g_ref, o_ref, acc_ref):
        i = pl.program_id(0)

        @pl.when(i == 0)
        def _():
            acc_ref[...] = jnp.zeros_like(acc_ref)

        acc_ref[...] += _dot_tn(a_ref[...], g_ref[...])

        @pl.when(i == nt - 1)
        def _():
            o_ref[...] = acc_ref[...].astype(BF16)

    return pl.pallas_call(
        body, grid=(nt,), name=name,
        in_specs=[_row(K), _row(N)],
        out_specs=_const((K, N)), out_shape=jax.ShapeDtypeStruct((K, N), BF16),
        scratch_shapes=[pltpu.VMEM((K, N), F32)],
        compiler_params=_params(("arbitrary",)),
    )(A, G)


def _ffn_up(h, WgT, WuT, name, comm=()):
    T = h.shape[0]
    F = WgT.shape[0]

    def body(h_ref, wg_ref, wu_ref, gp_ref, up_ref, act_ref):
        hv = h_ref[...]
        gp = _dot_nt(hv, wg_ref[...])
        up = _dot_nt(hv, wu_ref[...])
        gp_ref[...] = gp.astype(BF16)
        up_ref[...] = up.astype(BF16)
        act_ref[...] = (gp * _sigmoid(gp) * up).astype(BF16)

    sd = jax.ShapeDtypeStruct((T, F), BF16)
    return _call(body, grid=(T // ROW_TILE,), name=name, in_specs=[_row(D), _const((F, D)), _const((F, D))],
                 out_specs=[_row(F), _row(F), _row(F)], out_shape=[sd, sd, sd], sem=("parallel",),
                 args=(h, WgT, WuT), comm=comm)


def _ffn_dact(dy, Wd, gp, up, name, comm=()):
    T = dy.shape[0]
    F = Wd.shape[0]

    def body(dy_ref, wd_ref, gp_ref, up_ref, dgp_ref, dup_ref):
        dact = _dot_nt(dy_ref[...], wd_ref[...])
        gpv = gp_ref[...].astype(F32)
        upv = up_ref[...].astype(F32)
        sg = _sigmoid(gpv)
        dup_ref[...] = (dact * gpv * sg).astype(BF16)
        dgp_ref[...] = (dact * upv * sg * (1.0 + gpv * (1.0 - sg))).astype(BF16)

    sd = jax.ShapeDtypeStruct((T, F), BF16)
    return _call(body, grid=(T // ROW_TILE,), name=name, in_specs=[_row(D), _const((F, D)), _row(F), _row(F)],
                 out_specs=[_row(F), _row(F)], out_shape=[sd, sd], sem=("parallel",),
                 args=(dy, Wd, gp, up), comm=comm)


N_TAB = 7


def _ret_tables(tab_ref, lgf, lgb):
    r = lax.broadcasted_iota(jnp.int32, (CHUNK, CHUNK), 0).astype(F32)
    c = lax.broadcasted_iota(jnp.int32, (CHUNK, CHUNK), 1).astype(F32)
    for d, lg in ((0, lgf), (1, lgb)):
        if d == 0:
            mask, expo, xe, ze = r >= c, r - c, r + 1.0, (CHUNK - 1.0) - r
        else:
            mask, expo, xe, ze = c > r, c - r - 1.0, (CHUNK - 1.0) - r, r
        e = jnp.where(mask, expo, 0.0)
        tab_ref[N_TAB * d + 0] = jnp.where(mask, jnp.exp(lg * e), 0.0)
        tab_ref[N_TAB * d + 1] = jnp.exp(lg * xe)
        tab_ref[N_TAB * d + 2] = jnp.exp(lg * ze)
        tab_ref[N_TAB * d + 3] = jnp.exp(jnp.full((CHUNK, CHUNK), lg * float(CHUNK), F32))
        tab_ref[N_TAB * d + 4] = e
        tab_ref[N_TAB * d + 5] = xe
        tab_ref[N_TAB * d + 6] = ze


def _rope(t, cosf, sgn):
    return t * cosf + pltpu.roll(t, HEAD_DIM // 2, 1) * sgn


def _rope_t(d, cosf, sgn):
    return d * cosf + pltpu.roll(d * sgn, HEAD_DIM // 2, 1)


def _chunk_of(d, j, nc, ncx):
    return lax.rem(j + ncx, nc) if d == 0 else nc - 1 - j


def _head_spec(T, col0):
    return pl.BlockSpec((T, HEAD_DIM), lambda h: (0, col0 + h))


def _ret_fwd(p, lg, cosf, sgn, seq, name, comm=()):
    T = p.shape[0]
    nc, ncx = T // CHUNK, seq // CHUNK

    def body(lg_ref, q_ref, k_ref, v_ref, g_ref, cos_ref, sgn_ref, y_ref, ro_ref, tab_ref, of_ref):
        h = pl.program_id(0)
        _ret_tables(tab_ref, lg_ref[0, h], lg_ref[1, h])

        def load(cidx):
            rows = pl.ds(pl.multiple_of(cidx * CHUNK, CHUNK), CHUNK)
            cs, sn = cos_ref[rows, :], sgn_ref[rows, :]
            return rows, _rope(q_ref[rows, :], cs, sn) * Q_SCALE, _rope(k_ref[rows, :], cs, sn), v_ref[rows, :]

        def chunk(d, q, k, v, S):
            b = N_TAB * d
            kb, vb = k.astype(BF16), v.astype(BF16)
            pm = _dot_nt(q.astype(BF16), kb) * tab_ref[b]
            o = _dot(pm.astype(BF16), vb) + _dot((q * tab_ref[b + 1]).astype(BF16), S.astype(BF16))
            return o, S * tab_ref[b + 3] + _dot_tn((k * tab_ref[b + 2]).astype(BF16), vb)

        def step_f(j, S):
            rows, q, k, v = load(_chunk_of(0, j, nc, ncx))
            o, S = chunk(0, q, k, v, S)
            of_ref[rows, :] = o
            return S

        lax.fori_loop(0, nc, step_f, jnp.zeros((CHUNK, CHUNK), F32))

        def step_b(j, S):
            rows, q, k, v = load(_chunk_of(1, j, nc, ncx))
            o, S = chunk(1, q, k, v, S)
            y = of_ref[rows, :] + o
            y_ref[rows, :] = y
            g = g_ref[rows, :]
            ro_ref[rows, :] = (g * _sigmoid(g) * y * lax.rsqrt(_rowmean(y * y) + EPS)).astype(BF16)
            return S

        lax.fori_loop(0, nc, step_b, jnp.zeros((CHUNK, CHUNK), F32))

    tbl = pl.BlockSpec((T, HEAD_DIM), lambda h: (0, 0))
    return _call(
        body, grid=(RET_HEADS,), name=name,
        in_specs=[pl.BlockSpec(memory_space=pltpu.SMEM), _head_spec(T, 0), _head_spec(T, 4), _head_spec(T, 8),
                  _head_spec(T, 12), tbl, tbl],
        out_specs=[_head_spec(T, 0), _head_spec(T, 0)],
        out_shape=[jax.ShapeDtypeStruct((T, RET_W), F32), jax.ShapeDtypeStruct((T, RET_W), BF16)],
        scratch=[pltpu.VMEM((2 * N_TAB, CHUNK, CHUNK), F32), pltpu.VMEM((T, HEAD_DIM), F32)],
        sem=("arbitrary",), args=(lg, p, p, p, p, cosf, sgn), comm=comm)


def _ret_gate_bwd(yret, p, dmix, name):
    T = yret.shape[0]

    def body(y_ref, g_ref, dm_ref, dy_ref, dg_ref):
        for hh in range(RET_HEADS):
            sl = slice(hh * HEAD_DIM, (hh + 1) * HEAD_DIM)
            y = y_ref[:, sl]
            r = lax.rsqrt(_rowmean(y * y) + EPS)
            yn = y * r
            g = g_ref[:, sl]
            sg = _sigmoid(g)
            dro = dm_ref[:, sl]
            dg_ref[:, sl] = (dro * yn * sg * (1.0 + g * (1.0 - sg))).astype(BF16)
            dyn = dro * g * sg
            dy_ref[:, sl] = r * (dyn - yn * _rowmean(dyn * yn))

    return pl.pallas_call(
        body, grid=(T // ROW_TILE,), name=name,
        in_specs=[_row(RET_W), pl.BlockSpec((ROW_TILE, RET_W), lambda i: (i, 3)), _row(RET_W)],
        out_specs=[_row(RET_W), _row(RET_W)],
        out_shape=[jax.ShapeDtypeStruct((T, RET_W), F32), jax.ShapeDtypeStruct((T, RET_W), BF16)],
        compiler_params=_params(("parallel",)),
    )(yret, p, dmix)


def _ret_bwd(p, lg, dy, cosf, sgn, seq, name, comm=()):
    T = p.shape[0]
    nc, ncx = T // CHUNK, seq // CHUNK

    def body(lg_ref, q_ref, k_ref, v_ref, dy_ref, cos_ref, sgn_ref, dq_ref, dk_ref, dv_ref, dlg_ref,
             tab_ref, st_ref, dqs, dks, dvs):
        h = pl.program_id(0)
        _ret_tables(tab_ref, lg_ref[0, h], lg_ref[1, h])
        dlg_ref[...] = jnp.zeros_like(dlg_ref)

        def load(cidx):
            rows = pl.ds(pl.multiple_of(cidx * CHUNK, CHUNK), CHUNK)
            cs, sn = cos_ref[rows, :], sgn_ref[rows, :]
            return rows, _rope(q_ref[rows, :], cs, sn) * Q_SCALE, _rope(k_ref[rows, :], cs, sn), v_ref[rows, :]

        for d in (0, 1):
            b = N_TAB * d

            def states(j, S, d=d, b=b):
                _, _, k, v = load(_chunk_of(d, j, nc, ncx))
                st_ref[j] = S
                return S * tab_ref[b + 3] + _dot_tn((k * tab_ref[b + 2]).astype(BF16), v.astype(BF16))

            lax.fori_loop(0, nc, states, jnp.zeros((CHUNK, CHUNK), F32))

            def sweep(jj, carry, d=d, b=b):
                dS, acc = carry
                j = nc - 1 - jj
                rows, q, k, v = load(_chunk_of(d, j, nc, ncx))
                dO = dy_ref[rows, :]
                Sp = st_ref[j]
                qb, kb, vb, dOb = q.astype(BF16), k.astype(BF16), v.astype(BF16), dO.astype(BF16)
                Spb, dSb = Sp.astype(BF16), dS.astype(BF16)
                dmat, xi, ze, cd = tab_ref[b], tab_ref[b + 1], tab_ref[b + 2], tab_ref[b + 3]
                pm = _dot_nt(qb, kb) * dmat
                dpm = _dot_nt(dOb, vb)
                acc = acc + _colsum(dpm * pm * tab_ref[b + 4])
                dsc = (dpm * dmat).astype(BF16)
                qx = (q * xi).astype(BF16)
                kz = (k * ze).astype(BF16)
                dq = _dot(dsc, kb) + _dot_nt(dOb, Spb) * xi
                dk = _dot_tn(dsc, qb) + _dot_nt(vb, dSb) * ze
                dvst = _dot(kz, dSb)
                dv = _dot_tn(pm.astype(BF16), dOb) + dvst
                inter = _dot(qx, Spb)
                acc = acc + _colsum(dO * inter * tab_ref[b + 5])
                acc = acc + float(CHUNK) * _colsum(dS * cd * Sp) + _colsum(v * dvst * tab_ref[b + 6])
                if d == 0:
                    dqs[rows, :] = dq
                    dks[rows, :] = dk
                    dvs[rows, :] = dv
                else:
                    dqs[rows, :] += dq
                    dks[rows, :] += dk
                    dvs[rows, :] += dv
                return dS * cd + _dot_tn(qx, dOb), acc

            _, acc = lax.fori_loop(0, nc, sweep, (jnp.zeros((CHUNK, CHUNK), F32), jnp.zeros((1, CHUNK), F32)))
            dlg_ref[0, d:d + 1, :] = jnp.zeros((1, LANES), F32) + jnp.sum(acc)

        def finish(cidx, carry):
            rows = pl.ds(pl.multiple_of(cidx * CHUNK, CHUNK), CHUNK)
            cs, sn = cos_ref[rows, :], sgn_ref[rows, :]
            dq_ref[rows, :] = (_rope_t(dqs[rows, :], cs, sn) * Q_SCALE).astype(BF16)
            dk_ref[rows, :] = _rope_t(dks[rows, :], cs, sn).astype(BF16)
            dv_ref[rows, :] = dvs[rows, :].astype(BF16)
            return carry

        lax.fori_loop(0, nc, finish, 0)

    tbl = pl.BlockSpec((T, HEAD_DIM), lambda h: (0, 0))
    sd = jax.ShapeDtypeStruct((T, RET_W), BF16)
    return _call(
        body, grid=(RET_HEADS,), name=name,
        in_specs=[pl.BlockSpec(memory_space=pltpu.SMEM), _head_spec(T, 0), _head_spec(T, 4), _head_spec(T, 8),
                  _head_spec(T, 0), tbl, tbl],
        out_specs=[_head_spec(T, 0), _head_spec(T, 0), _head_spec(T, 0),
                   pl.BlockSpec((1, 8, LANES), lambda h: (h, 0, 0))],
        out_shape=[sd, sd, sd, jax.ShapeDtypeStruct((RET_HEADS, 8, LANES), F32)],
        scratch=[pltpu.VMEM((2 * N_TAB, CHUNK, CHUNK), F32), pltpu.VMEM((nc, CHUNK, CHUNK), F32),
                 pltpu.VMEM((T, HEAD_DIM), F32), pltpu.VMEM((T, HEAD_DIM), F32), pltpu.VMEM((T, HEAD_DIM), F32)],
        sem=("arbitrary",), args=(lg, p, p, p, dy, cosf, sgn), comm=comm)


def _halo_specs(T, cols, colblock):
    per = ROW_TILE // HALO
    last = T // HALO - 1
    prv = pl.BlockSpec((HALO, cols), lambda i: (jnp.maximum(i * per - 1, 0), colblock))
    nxt = pl.BlockSpec((HALO, cols), lambda i: (jnp.minimum((i + 1) * per, last), colblock))
    return prv, nxt


def _seq_edges(i, nxt, nt):
    first = (i == 0) | (i == nxt)
    last = (i == nxt - 1) | (i == nt - 1)
    return first, last


def _fill_pad(pad_ref, prv, cur, nxt, first, last):
    pad_ref[0:HALO, :] = jnp.where(first, 0.0, prv)
    pad_ref[HALO:HALO + ROW_TILE, :] = cur
    pad_ref[HALO + ROW_TILE:2 * HALO + ROW_TILE, :] = jnp.where(last, 0.0, nxt)


def _ln_stats(x):
    mu = _rowmean(x)
    xc = x - mu
    rs = lax.rsqrt(_rowmean(xc * xc) + EPS)
    return xc * rs, rs


def _conv_fwd(p, cw, lnw, lnb, nxt, name, comm=()):
    T = p.shape[0]
    nt = T // ROW_TILE
    a_col, g_col = 4, 5

    def body(a_ref, g_ref, ap_ref, gp_ref, an_ref, gn_ref, w_ref, lw_ref, lb_ref, cv_ref, co_ref, pad_ref):
        i = pl.program_id(0)
        first, last = _seq_edges(i, nxt, nt)
        glu = lambda a, g: a * _sigmoid(g)
        _fill_pad(pad_ref, glu(ap_ref[...], gp_ref[...]), glu(a_ref[...], g_ref[...]),
                  glu(an_ref[...], gn_ref[...]), first, last)
        acc = jnp.zeros((ROW_TILE, CONV_CH), F32)
        for k in range(CONV_K):
            acc = acc + w_ref[k:k + 1, :] * pad_ref[k + 1:k + 1 + ROW_TILE, :]
        cv_ref[...] = acc
        xh, _ = _ln_stats(acc)
        z = xh * lw_ref[...] + lb_ref[...]
        co_ref[...] = (z * _sigmoid(z)).astype(BF16)

    cur = lambda cb: pl.BlockSpec((ROW_TILE, CONV_CH), lambda i: (i, cb))
    ap, an = _halo_specs(T, CONV_CH, a_col)
    gp, gn = _halo_specs(T, CONV_CH, g_col)
    return _call(
        body, grid=(nt,), name=name,
        in_specs=[cur(a_col), cur(g_col), ap, gp, an, gn, _const((32, CONV_CH)), _const((1, CONV_CH)),
                  _const((1, CONV_CH))],
        out_specs=[_row(CONV_CH), _row(CONV_CH)],
        out_shape=[jax.ShapeDtypeStruct((T, CONV_CH), F32), jax.ShapeDtypeStruct((T, CONV_CH), BF16)],
        scratch=[pltpu.VMEM((ROW_TILE + 2 * HALO, CONV_CH), F32)],
        sem=("parallel",), args=(p, p, p, p, p, p, cw, lnw, lnb), comm=comm)


def _conv_bwd(p, cv, dmix, cw, lnw, lnb, nxt, name, comm=()):
    T = p.shape[0]
    nt = T // ROW_TILE
    a_col, g_col = 4, 5

    def body(a_ref, g_ref, ap_ref, gp_ref, an_ref, gn_ref, cv_ref, cvp_ref, cvn_ref, dc_ref, dcp_ref, dcn_ref,
             w_ref, lw_ref, lb_ref, dp_ref, dw_ref, dl_ref, upad_ref, dpad_ref):
        i = pl.program_id(0)
        first, last = _seq_edges(i, nxt, nt)

        @pl.when(i == 0)
        def _():
            dw_ref[...] = jnp.zeros_like(dw_ref)
            dl_ref[...] = jnp.zeros_like(dl_ref)

        lw, lb = lw_ref[...], lb_ref[...]

        def ln_bwd(cvv, dco):
            xh, rs = _ln_stats(cvv)
            z = xh * lw + lb
            sg = _sigmoid(z)
            dz = dco * sg * (1.0 + z * (1.0 - sg))
            dxh = dz * lw
            return rs * (dxh - _rowmean(dxh) - xh * _rowmean(dxh * xh)), dz, xh

        dcv, dz, xh = ln_bwd(cv_ref[...], dc_ref[...])
        _fill_pad(dpad_ref, ln_bwd(cvp_ref[...], dcp_ref[...])[0], dcv, ln_bwd(cvn_ref[...], dcn_ref[...])[0],
                  first, last)
        a, g = a_ref[...], g_ref[...]
        sg = _sigmoid(g)
        glu = lambda av, gv: av * _sigmoid(gv)
        _fill_pad(upad_ref, glu(ap_ref[...], gp_ref[...]), a * sg, glu(an_ref[...], gn_ref[...]), first, last)
        du = jnp.zeros((ROW_TILE, CONV_CH), F32)
        for k in range(CONV_K):
            du = du + w_ref[k:k + 1, :] * dpad_ref[CONV_K - k:CONV_K - k + ROW_TILE, :]
            dw_ref[k:k + 1, :] += _colsum(dcv * upad_ref[k + 1:k + 1 + ROW_TILE, :])
        dl_ref[0:1, :] += _colsum(dz * xh)
        dl_ref[1:2, :] += _colsum(dz)
        dp_ref[:, 0:CONV_CH] = (du * sg).astype(BF16)
        dp_ref[:, CONV_CH:2 * CONV_CH] = (du * a * sg * (1.0 - sg)).astype(BF16)

    cur = lambda cb: pl.BlockSpec((ROW_TILE, CONV_CH), lambda i: (i, cb))
    ap, an = _halo_specs(T, CONV_CH, a_col)
    gp, gn = _halo_specs(T, CONV_CH, g_col)
    cvp, cvn = _halo_specs(T, CONV_CH, 0)
    dcp, dcn = _halo_specs(T, CONV_CH, 1)
    return _call(
        body, grid=(nt,), name=name,
        in_specs=[cur(a_col), cur(g_col), ap, gp, an, gn, cur(0), cvp, cvn, cur(1), dcp, dcn,
                  _const((32, CONV_CH)), _const((1, CONV_CH)), _const((1, CONV_CH))],
        out_specs=[_row(2 * CONV_CH), _const((32, CONV_CH)), _const((8, CONV_CH))],
        out_shape=[jax.ShapeDtypeStruct((T, 2 * CONV_CH), BF16), jax.ShapeDtypeStruct((32, CONV_CH), F32),
                   jax.ShapeDtypeStruct((8, CONV_CH), F32)],
        scratch=[pltpu.VMEM((ROW_TILE + 2 * HALO, CONV_CH), F32), pltpu.VMEM((ROW_TILE + 2 * HALO, CONV_CH), F32)],
        sem=("arbitrary",), args=(p, p, p, p, p, p, cv, cv, cv, dmix, dmix, dmix, cw, lnw, lnb), comm=comm)


def _tile_positions(i, nxt, seq, ctx, offset, rows):
    is_ctx = i >= nxt
    pos0 = (i - jnp.where(is_ctx, nxt, 0)) * ROW_TILE + offset
    length = jnp.where(is_ctx, ctx, seq).astype(F32)
    pos = (pos0 + lax.broadcasted_iota(jnp.int32, (rows, 1), 0)).astype(F32)
    return pos, length


def _pool_count(pos, length, w):
    left = w // 2
    right = w - 1 - left
    return jnp.minimum(pos + right, length - 1.0) - jnp.maximum(pos - left, 0.0) + 1.0


def _gelu(x):
    return 0.5 * x * (1.0 + lax.erf(x * INV_SQRT2))


def _odd_fwd(p, pw, ps, slw, slb, sw, sbf, nxt, seq, ctx, name, comm=()):
    T = p.shape[0]
    nt = T // ROW_TILE

    def body(p_ref, pp_ref, pn_ref, pw_ref, ps_ref, lw_ref, lb_ref, sw_ref, sb_ref, o_ref, pad_ref):
        i = pl.program_id(0)
        first, last = _seq_edges(i, nxt, nt)
        _fill_pad(pad_ref, pp_ref[...], p_ref[:, 0:POOL_CH], pn_ref[...], first, last)
        pos, length = _tile_positions(i, nxt, seq, ctx, 0, ROW_TILE)
        for gi, w in enumerate(POOL_WINDOWS):
            sl = slice(gi * GROUP_CH, (gi + 1) * GROUP_CH)
            left = w // 2
            ssum = jnp.zeros((ROW_TILE, GROUP_CH), F32)
            for o in range(-left, w - left):
                ssum = ssum + pad_ref[HALO + o:HALO + o + ROW_TILE, sl]
            m = ssum / _pool_count(pos, length, w) - p_ref[:, sl]
            pre = _dot(m.astype(BF16), pw_ref[gi].astype(BF16))
            o_ref[:, sl] = (pre * ps_ref[:, sl]).astype(BF16)
        u = _gelu(p_ref[:, POOL_CH:POOL_CH + SG_CH])
        xh, _ = _ln_stats(_gelu(p_ref[:, POOL_CH + SG_CH:ODD_IN]))
        vln = xh * lw_ref[...] + lb_ref[...]
        for n in range(ROW_TILE // SG_CHUNK):
            rs = slice(n * SG_CHUNK, (n + 1) * SG_CHUNK)
            for gi in range(4):
                sl = slice(gi * GROUP_CH, (gi + 1) * GROUP_CH)
                s = _dot(sw_ref[gi].astype(BF16), vln[rs, sl].astype(BF16)) + sb_ref[:, sl]
                o_ref[rs, POOL_CH + gi * GROUP_CH:POOL_CH + (gi + 1) * GROUP_CH] = (u[rs, sl] * s).astype(BF16)

    pp, pn = _halo_specs(T, POOL_CH, 0)
    return _call(
        body, grid=(nt,), name=name,
        in_specs=[_row(ODD_IN), pp, pn, _const((4, GROUP_CH, GROUP_CH)), _const((1, POOL_CH)), _const((1, SG_CH)),
                  _const((1, SG_CH)), _const((4, SG_CHUNK, SG_CHUNK)), _const((SG_CHUNK, SG_CH))],
        out_specs=[_row(D)], out_shape=[jax.ShapeDtypeStruct((T, D), BF16)],
        scratch=[pltpu.VMEM((ROW_TILE + 2 * HALO, POOL_CH), F32)],
        sem=("parallel",), args=(p, p, p, pw, ps, slw, slb, sw, sbf), comm=comm)


def _odd_bwd(p, dmix, pw, ps, slw, slb, sw, sbf, nxt, seq, ctx, name, comm=()):
    T = p.shape[0]
    nt = T // ROW_TILE

    def body(p_ref, pp_ref, pn_ref, dm_ref, dmp_ref, dmn_ref, pw_ref, ps_ref, lw_ref, lb_ref, sw_ref, sb_ref,
             dp_ref, dpw_ref, dsw_ref, dv_ref, dsb_ref, pad_ref, dpad_ref):
        i = pl.program_id(0)
        first, last = _seq_edges(i, nxt, nt)

        @pl.when(i == 0)
        def _():
            dpw_ref[...] = jnp.zeros_like(dpw_ref)
            dsw_ref[...] = jnp.zeros_like(dsw_ref)
            dv_ref[...] = jnp.zeros_like(dv_ref)
            dsb_ref[...] = jnp.zeros_like(dsb_ref)

        _fill_pad(pad_ref, pp_ref[...], p_ref[:, 0:POOL_CH], pn_ref[...], first, last)
        pos, length = _tile_positions(i, nxt, seq, ctx, 0, ROW_TILE)
        pos_p, _ = _tile_positions(i, nxt, seq, ctx, -HALO, HALO)
        pos_n, _ = _tile_positions(i, nxt, seq, ctx, ROW_TILE, HALO)
        scale = ps_ref[...]
        for gi, w in enumerate(POOL_WINDOWS):
            sl = slice(gi * GROUP_CH, (gi + 1) * GROUP_CH)
            left = w // 2
            right = w - 1 - left
            ssum = jnp.zeros((ROW_TILE, GROUP_CH), F32)
            for o in range(-left, right + 1):
                ssum = ssum + pad_ref[HALO + o:HALO + o + ROW_TILE, sl]
            cnt = _pool_count(pos, length, w)
            m = ssum / cnt - p_ref[:, sl]
            wg = pw_ref[gi].astype(BF16)
            pre = _dot(m.astype(BF16), wg)
            dpo = dm_ref[:, sl]
            dv_ref[0:1, sl] += _colsum(dpo * pre)
            dpre = (dpo * scale[:, sl]).astype(BF16)
            dpw_ref[gi] += _dot_tn(m.astype(BF16), dpre)
            dmc = _dot_nt(dpre, wg)
            halo_dm = lambda ref, ps_: _dot_nt((ref[:, sl] * scale[:, sl]).astype(BF16), wg) / _pool_count(ps_, length, w)
            dpad_ref[0:HALO, sl] = jnp.where(first, 0.0, halo_dm(dmp_ref, pos_p))
            dpad_ref[HALO:HALO + ROW_TILE, sl] = dmc / cnt
            dpad_ref[HALO + ROW_TILE:2 * HALO + ROW_TILE, sl] = jnp.where(last, 0.0, halo_dm(dmn_ref, pos_n))
            atd = jnp.zeros((ROW_TILE, GROUP_CH), F32)
            for o in range(-right, left + 1):
                atd = atd + dpad_ref[HALO + o:HALO + o + ROW_TILE, sl]
            dp_ref[:, sl] = (atd - dmc).astype(BF16)

        pu = p_ref[:, POOL_CH:POOL_CH + SG_CH]
        pv = p_ref[:, POOL_CH + SG_CH:ODD_IN]
        u = _gelu(pu)
        xh, rs_ = _ln_stats(_gelu(pv))
        lw = lw_ref[...]
        vln = xh * lw + lb_ref[...]
        dgelu = lambda x: 0.5 * (1.0 + lax.erf(x * INV_SQRT2)) + x * jnp.exp(-0.5 * x * x) * INV_SQRT_2PI
        for n in range(ROW_TILE // SG_CHUNK):
            rs = slice(n * SG_CHUNK, (n + 1) * SG_CHUNK)
            dvl = []
            for gi in range(4):
                sl = slice(gi * GROUP_CH, (gi + 1) * GROUP_CH)
                wq = sw_ref[gi].astype(BF16)
                vb = vln[rs, sl].astype(BF16)
                s = _dot(wq, vb) + sb_ref[:, sl]
                dsg = dm_ref[rs, POOL_CH + gi * GROUP_CH:POOL_CH + (gi + 1) * GROUP_CH]
                ds = dsg * u[rs, sl]
                dsb_ref[:, sl] += ds
                dsw_ref[gi] += _dot_nt(ds.astype(BF16), vb)
                dvl.append(_dot_tn(wq, ds.astype(BF16)))
                dp_ref[rs, POOL_CH + gi * GROUP_CH:POOL_CH + (gi + 1) * GROUP_CH] = (
                    dsg * s * dgelu(pu[rs, sl])).astype(BF16)
            dvln = jnp.concatenate(dvl, axis=1)
            xhc = xh[rs, :]
            dv_ref[1:2, :] += _colsum(dvln * xhc)
            dv_ref[2:3, :] += _colsum(dvln)
            dxh = dvln * lw
            dvv = rs_[rs, :] * (dxh - _rowmean(dxh) - xhc * _rowmean(dxh * xhc))
            dp_ref[rs, POOL_CH + SG_CH:ODD_IN] = (dvv * dgelu(pv[rs, :])).astype(BF16)

    pp, pn = _halo_specs(T, POOL_CH, 0)
    dmp, dmn = _halo_specs(T, POOL_CH, 0)
    gsd = jax.ShapeDtypeStruct((4, GROUP_CH, GROUP_CH), F32)
    return _call(
        body, grid=(nt,), name=name,
        in_specs=[_row(ODD_IN), pp, pn, _row(D), dmp, dmn, _const((4, GROUP_CH, GROUP_CH)), _const((1, POOL_CH)),
                  _const((1, SG_CH)), _const((1, SG_CH)), _const((4, SG_CHUNK, SG_CHUNK)), _const((SG_CHUNK, SG_CH))],
        out_specs=[_row(ODD_IN), _const((4, GROUP_CH, GROUP_CH)), _const((4, SG_CHUNK, SG_CHUNK)), _const((8, POOL_CH)),
                   _const((SG_CHUNK, SG_CH))],
        out_shape=[jax.ShapeDtypeStruct((T, ODD_IN), BF16), gsd, gsd, jax.ShapeDtypeStruct((8, POOL_CH), F32),
                   jax.ShapeDtypeStruct((SG_CHUNK, SG_CH), F32)],
        scratch=[pltpu.VMEM((ROW_TILE + 2 * HALO, POOL_CH), F32), pltpu.VMEM((ROW_TILE + 2 * HALO, POOL_CH), F32)],
        sem=("arbitrary",), args=(p, p, p, dmix, dmix, dmix, pw, ps, slw, slb, sw, sbf), comm=comm)


def _ada_fwd(cs, aw, ab, name):
    cols = aw.shape[2]

    def body(c_ref, w_ref, b_ref, o_ref):
        c = c_ref[...]
        s = (c * _sigmoid(c)).astype(BF16)
        o_ref[0] = _dot(s, w_ref[0].astype(BF16)) + b_ref[0]

    return pl.pallas_call(
        body, grid=(DEPTH,), name=name,
        in_specs=[pl.BlockSpec((16, D), lambda i: (0, 0)), pl.BlockSpec((1, D, cols), lambda i: (i, 0, 0)),
                  pl.BlockSpec((1, 1, cols), lambda i: (i, 0, 0))],
        out_specs=pl.BlockSpec((1, 16, cols), lambda i: (i, 0, 0)),
        out_shape=jax.ShapeDtypeStruct((DEPTH, 16, cols), F32),
        compiler_params=_params(("parallel",)),
    )(cs, aw, ab)


def _ada_bwd(cs, G, aw, name):
    cols = aw.shape[2]

    def body(c_ref, g_ref, w_ref, gw_ref, cc_ref):
        i = pl.program_id(0)

        @pl.when(i == 0)
        def _():
            cc_ref[...] = jnp.zeros_like(cc_ref)

        c = c_ref[...]
        s = (c * _sigmoid(c)).astype(BF16)
        g = g_ref[0]
        dc = g[8:9, :]
        for d in range(9, 16):
            dc = dc + g[d:d + 1, :]
        row = lax.broadcasted_iota(jnp.int32, (8, cols), 0)
        dcrows = jnp.where(row == 0, dc, 0.0)
        dm = jnp.concatenate([g[0:8, :], dcrows], axis=0).astype(BF16)
        gw_ref[0] = _dot_tn(s, dm)
        cc_ref[...] += _dot_nt(dcrows.astype(BF16), w_ref[0].astype(BF16))

    return pl.pallas_call(
        body, grid=(DEPTH,), name=name,
        in_specs=[pl.BlockSpec((16, D), lambda i: (0, 0)), pl.BlockSpec((1, 16, cols), lambda i: (i, 0, 0)),
                  pl.BlockSpec((1, D, cols), lambda i: (i, 0, 0))],
        out_specs=[pl.BlockSpec((1, D, cols), lambda i: (i, 0, 0)), pl.BlockSpec((8, D), lambda i: (0, 0))],
        out_shape=[jax.ShapeDtypeStruct((DEPTH, D, cols), F32), jax.ShapeDtypeStruct((8, D), F32)],
        compiler_params=_params(("arbitrary",)),
    )(cs, G, aw)


def _row_tile_for(rows):
    for t in (512, 256, 128, 64, 32, 16, 8):
        if rows % t == 0:
            return t
    raise ValueError(f"rows={rows} is not a multiple of 8")


def _sum_devices(x, name):
    _, rows, cols = x.shape
    tr = _row_tile_for(rows)

    def body(x_ref, o_ref):
        acc = x_ref[0]
        for d in range(1, N_DEV):
            acc = acc + x_ref[d]
        o_ref[...] = acc

    return pl.pallas_call(
        body, grid=(rows // tr,), name=name,
        in_specs=[pl.BlockSpec((N_DEV, tr, cols), lambda i: (0, i, 0))],
        out_specs=pl.BlockSpec((tr, cols), lambda i: (i, 0)),
        out_shape=jax.ShapeDtypeStruct((rows, cols), F32),
        compiler_params=_params(("parallel",)),
    )(x)


def _adamw(w, g, m, v, name):
    rows, cols = w.shape
    tr = _row_tile_for(rows)

    def body(w_ref, g_ref, m_ref, v_ref, go_ref, d_ref, mo_ref, vo_ref):
        gv = g_ref[...]
        mn = ADAM_B1 * m_ref[...] + (1.0 - ADAM_B1) * gv
        vn = ADAM_B2 * v_ref[...] + (1.0 - ADAM_B2) * (gv * gv)
        go_ref[...] = gv
        mo_ref[...] = mn
        vo_ref[...] = vn
        d_ref[...] = -ADAM_LR * ((mn / ADAM_BC1) / (jnp.sqrt(vn / ADAM_BC2) + ADAM_EPS) + ADAM_WD * w_ref[...])

    blk = pl.BlockSpec((tr, cols), lambda i: (i, 0))
    sd = jax.ShapeDtypeStruct((rows, cols), F32)
    return pl.pallas_call(
        body, grid=(rows // tr,), name=name,
        in_specs=[blk, blk, blk, blk], out_specs=[blk, blk, blk, blk], out_shape=[sd, sd, sd, sd],
        compiler_params=_params(("parallel",)),
    )(w, g, m, v)


def _adamw_layers(w, pieces, m, v, name):
    L, a, b = w.shape
    bp = pieces[0].shape[2]
    tr = next(t for t in (256, 128, 64, 32, 16) if a % t == 0)

    def body(w_ref, m_ref, v_ref, *rest):
        p_refs, (go_ref, d_ref, mo_ref, vo_ref) = rest[:L], rest[L:]
        layer = pl.program_id(0)
        for k in range(L):
            @pl.when(layer == k)
            def _(k=k):
                gv = p_refs[k][0].astype(F32)
                for d in range(1, N_DEV):
                    gv = gv + p_refs[k][d].astype(F32)
                gv = gv[:, :b]
                mn = ADAM_B1 * m_ref[0] + (1.0 - ADAM_B1) * gv
                vn = ADAM_B2 * v_ref[0] + (1.0 - ADAM_B2) * (gv * gv)
                go_ref[0] = gv
                mo_ref[0] = mn
                vo_ref[0] = vn
                d_ref[0] = -ADAM_LR * ((mn / ADAM_BC1) / (jnp.sqrt(vn / ADAM_BC2) + ADAM_EPS) + ADAM_WD * w_ref[0])

    blk = pl.BlockSpec((1, tr, b), lambda l, i: (l, i, 0))
    pspecs = [pl.BlockSpec((N_DEV, tr, bp), lambda l, i, k=k: (0, jnp.where(l == k, i, 0), 0)) for k in range(L)]
    sd = jax.ShapeDtypeStruct((L, a, b), F32)
    return pl.pallas_call(
        body, grid=(L, a // tr), name=name,
        in_specs=[blk, blk, blk] + pspecs, out_specs=[blk, blk, blk, blk], out_shape=[sd, sd, sd, sd],
        compiler_params=_params(("arbitrary", "arbitrary")),
    )(w, m, v, *pieces)


def _pack(arrs, row_mult):
    parts = []
    for a in arrs:
        n = math.prod(a.shape)
        rows = -(-n // LANES)
        parts.append(jnp.pad(a.reshape(-1).astype(F32), (0, rows * LANES - n)).reshape(rows, LANES))
    total = sum(p.shape[0] for p in parts)
    parts.append(jnp.zeros((-total % row_mult, LANES), F32))
    return jnp.concatenate(parts, axis=0)


def _unpack(packed, shapes):
    out, r0 = [], 0
    lead = packed.shape[:-2]
    for s in shapes:
        n = math.prod(s)
        rows = -(-n // LANES)
        piece = packed[..., r0:r0 + rows, :].reshape(lead + (rows * LANES,))
        out.append(piece[..., :n].reshape(lead + tuple(s)))
        r0 += rows
    return out


def _rope_tables(seq, ctx):
    def angles(ps):
        parts = []
        for pvec, n in zip(ps, ROPE_PAIRS):
            freq = ROPE_BASE ** (-jnp.arange(n, dtype=F32) / n)
            parts.append(pvec[:, None] * freq[None, :])
        return jnp.concatenate(parts, axis=-1)

    rows = seq // GRID_W
    grid_r = jnp.broadcast_to(jnp.arange(rows, dtype=F32)[:, None], (rows, GRID_W)).reshape(-1)
    grid_c = jnp.broadcast_to(jnp.arange(GRID_W, dtype=F32)[None, :], (rows, GRID_W)).reshape(-1)
    zc = jnp.zeros((ctx,), F32)
    ang = jnp.concatenate([angles((jnp.full((seq,), ctx, F32), grid_r, grid_c)),
                           angles((jnp.arange(ctx, dtype=F32), zc, zc))], axis=0)
    cos, sin = jnp.cos(ang), jnp.sin(ang)
    return jnp.concatenate([cos, cos], axis=1), jnp.concatenate([-sin, sin], axis=1)


def _sample_step(X0, tgt, mods, shards, S, seq, ctx):
    nxt = seq // ROW_TILE
    cosf, sgn = _rope_tables(seq, ctx)
    X = X0
    saved = []
    w_in, w_out = _exchange([("ag_cols", shards[0]["in"]), ("ag_rows", shards[0]["out"])], "gather_l0")
    for i in range(DEPTH):
        j, even = i // 2, i % 2 == 0
        md, sh = mods[i], shards[i]
        t = f"l{i}_"
        h1 = _norm_mod_fwd(X, S["norm_w"][i, 0][None], md, 0, nxt, t + "norm1")
        if even:
            (p,), _ = _mm_nn(h1, w_in, F32, t + "proj_in")
            (yret, ro), (wg, wu) = _ret_fwd(p, S["lg"][j], cosf, sgn, seq, t + "ret_fwd",
                                            comm=[("ag_rows", sh["gate"]), ("ag_rows", sh["up"])])
            (cv, co), (wd,) = _conv_fwd(p, S["conv_w"][j], S["conv_ln_w"][j][None], S["conv_ln_b"][j][None], nxt,
                                        t + "conv_fwd", comm=[("ag_rows", sh["down"])])
            mix = jnp.concatenate([ro, co], axis=1)
            extra = (yret, cv)
            (X1, y1), _ = _mm_nn_resid(mix, w_out, X, md, 2, nxt, t + "proj_out")
        else:
            (p,), (wg,) = _mm_nt(h1, w_in, t + "proj_in", comm=[("ag_rows", sh["gate"])])
            sbf = jnp.repeat(S["sg_b"][j].T, GROUP_CH, axis=1)
            (mix,), (wu,) = _odd_fwd(p, S["pool_w"][j], S["pool_scale"][j][None], S["sg_ln_w"][j][None],
                                     S["sg_ln_b"][j][None], S["sg_w"][j], sbf, nxt, seq, ctx, t + "odd_fwd",
                                     comm=[("ag_rows", sh["up"])])
            extra = (sbf,)
            (X1, y1), (wd,) = _mm_nn_resid(mix, w_out, X, md, 2, nxt, t + "proj_out",
                                           comm=[("ag_rows", sh["down"])])
        h2 = _norm_mod_fwd(X1, S["norm_w"][i, 1][None], md, 3, nxt, t + "norm2")
        last = i == DEPTH - 1
        next_in = [] if last else [("ag_rows" if even else "ag_cols", shards[i + 1]["in"])]
        (gp, up, act), got_in = _ffn_up(h2, wg, wu, t + "ffn_up", comm=next_in)
        (X2, y2), got_out = _mm_nn_resid(act, wd, X1, md, 5, nxt, t + "ffn_down",
                                         comm=[] if last else [("ag_rows", shards[i + 1]["out"])])
        saved.append((X, h1, p, mix, extra, y1, X1, h2, gp, up, act, y2, (w_in, w_out, wg, wu, wd)))
        X = X2
        if not last:
            w_in, w_out = got_in[0], got_out[0]

    dX, hst = _loss_head(X, tgt, S["final_norm_w"][None], nxt, "loss_head")
    loss = jnp.sum(hst[1])
    pieces = {}
    gS = {"final_norm_w": hst[0], "norm_w": [None] * DEPTH, "lg": [None] * 2, "conv_w": [None] * 2,
          "conv_ln_w": [None] * 2, "conv_ln_b": [None] * 2, "pool_w": [None] * 2, "pool_scale": [None] * 2,
          "sg_ln_w": [None] * 2, "sg_ln_b": [None] * 2, "sg_w": [None] * 2, "sg_b": [None] * 2}
    dmods = [None] * DEPTH
    send_in = None
    for i in reversed(range(DEPTH)):
        j, even = i // 2, i % 2 == 0
        md = mods[i]
        t = f"l{i}_"
        X_in, h1, p, mix, extra, y1, X1, h2, gp, up, act, y2, (w_in, w_out, wg, wu, wd) = saved[i]
        dyf, g2 = _resid_bwd(dX, y2, md, 5, nxt, t + "ffn_resid_bwd")
        (dgp, dup), got = _ffn_dact(dyf, wd, gp, up, t + "ffn_dact", comm=[] if send_in is None else [send_in])
        if got:
            pieces[("in", i + 1)] = got[0]
        g_down = _mm_tn(act, dyf, t + "dw_down")
        g_gate = _mm_tn(dgp, h2, t + "dw_gate")
        g_up = _mm_tn(dup, h2, t + "dw_up")
        ffn_sends = [("a2a_rows", g_down), ("a2a_rows", g_gate), ("a2a_rows", g_up)]
        if even:
            (dh2,), _ = _mm_nn(dgp, wg, F32, t + "dh2_gate")
            (dh2,), _ = _mm_nn(dup, wu, F32, t + "dh2_up", add=dh2)
        else:
            (dh2,), (pieces[("down", i)],) = _mm_nn(dgp, wg, F32, t + "dh2_gate", comm=ffn_sends[0:1])
            (dh2,), (pieces[("gate", i)],) = _mm_nn(dup, wu, F32, t + "dh2_up", add=dh2, comm=ffn_sends[1:2])
        dX1, s2 = _norm_mod_bwd(dh2, X1, S["norm_w"][i, 1][None], md, dX, 3, nxt, t + "norm2_bwd")
        dym, g1 = _resid_bwd(dX1, y1, md, 2, nxt, t + "mix_resid_bwd")
        (dmix,), _ = _mm_nt(dym, w_out, t + "dmix")
        g_out = _mm_tn(mix, dym, t + "dw_out")
        if even:
            yret, cv = extra
            dyr, dg = _ret_gate_bwd(yret, p, dmix, t + "ret_gate_bwd")
            (dq, dk, dv, dlg), got = _ret_bwd(p, S["lg"][j], dyr, cosf, sgn, seq, t + "ret_bwd", comm=ffn_sends)
            pieces[("down", i)], pieces[("gate", i)], pieces[("up", i)] = got
            early = [("ag_blk", _pack([jnp.stack(gS["pool_w"]), jnp.stack(gS["sg_w"])], 8))] if i == 0 else []
            (dpc, dcw, dln), got = _conv_bwd(
                p, cv, dmix, S["conv_w"][j], S["conv_ln_w"][j][None], S["conv_ln_b"][j][None], nxt,
                t + "conv_bwd", comm=[("a2a_rows", g_out)] + early)
            pieces[("out", i)] = got[0]
            if i == 0:
                early_all = got[1]
            dp = jnp.concatenate([dq, dk, dv, dg, dpc], axis=1)
            gS["lg"][j] = dlg[:, 0:2, 0].T
            gS["conv_w"][j], gS["conv_ln_w"][j], gS["conv_ln_b"][j] = dcw, dln[0], dln[1]
        else:
            (sbf,) = extra
            (dp, dpw, dsw, dvec, dsb), (pieces[("up", i)],) = _odd_bwd(
                p, dmix, S["pool_w"][j], S["pool_scale"][j][None], S["sg_ln_w"][j][None], S["sg_ln_b"][j][None],
                S["sg_w"][j], sbf, nxt, seq, ctx, t + "odd_bwd", comm=ffn_sends[2:3])
            gS["pool_w"][j], gS["sg_w"][j] = dpw, dsw
            gS["pool_scale"][j], gS["sg_ln_w"][j], gS["sg_ln_b"][j] = dvec[0], dvec[1], dvec[2]
            gS["sg_b"][j] = jnp.sum(dsb.reshape(SG_CHUNK, 4, GROUP_CH), axis=2).T
        if even:
            g_in = _mm_tn(h1, dp, t + "dw_in")
            send_in = ("a2a_cols", g_in)
            (dh1,), got = _mm_nt(dp, w_in, t + "dh1", comm=[send_in] if i == 0 else [])
            if i == 0:
                pieces[("in", 0)] = got[0]
        else:
            g_in = _mm_tn(dp, h1, t + "dw_in")
            send_in = ("a2a_rows", g_in)
            (dh1,), (pieces[("out", i)],) = _mm_nn(dp, w_in, F32, t + "dh1", comm=[("a2a_rows", g_out)])
        dX, s1 = _norm_mod_bwd(dh1, X_in, S["norm_w"][i, 0][None], md, dX1, 0, nxt, t + "norm1_bwd")
        gS["norm_w"][i] = jnp.stack([s1[0, 2] + s1[1, 2], s2[0, 2] + s2[1, 2]])
        dmods[i] = jnp.stack([s1[:, 0], s1[:, 1], g1[:, 0], s2[:, 0], s2[:, 1], g2[:, 0]], axis=1)
    gS = {k: (jnp.stack(v) if isinstance(v, list) else v) for k, v in gS.items()}
    return loss, dX, pieces, gS, jnp.stack(dmods), early_all


FF_SHARD = D_FF // N_DEV
FF_SHARD_PAD = 384


def kernel(x, c, ctx, c_ctx, ada_w, ada_b, norm_w, even_w_in, even_w_out, ret_decay_logit, conv_dw_w, conv_ln_w, conv_ln_b, odd_w_in, odd_w_out, pool_w, pool_scale, sg_ln_w, sg_ln_b, sg_w, sg_b, ffn_w_gate, ffn_w_up, ffn_w_down, final_norm_w, loss_target, m_c_ctx, m_ada_w, m_ada_b, m_norm_w, m_even_w_in, m_even_w_out, m_ret_decay_logit, m_conv_dw_w, m_conv_ln_w, m_conv_ln_b, m_odd_w_in, m_odd_w_out, m_pool_w, m_pool_scale, m_sg_ln_w, m_sg_ln_b, m_sg_w, m_sg_b, m_ffn_w_gate, m_ffn_w_up, m_ffn_w_down, m_final_norm_w, v_c_ctx, v_ada_w, v_ada_b, v_norm_w, v_even_w_in, v_even_w_out, v_ret_decay_logit, v_conv_dw_w, v_conv_ln_w, v_conv_ln_b, v_odd_w_in, v_odd_w_out, v_pool_w, v_pool_scale, v_sg_ln_w, v_sg_ln_b, v_sg_w, v_sg_b, v_ffn_w_gate, v_ffn_w_up, v_ffn_w_down, v_final_norm_w):
    seq, n_ctx = x.shape[1], ctx.shape[1]
    me = 4 * lax.axis_index("x") + 2 * lax.axis_index("y") + lax.axis_index("c")
    mcols = ada_w.shape[2]

    small_shapes = [(D,), norm_w.shape, conv_dw_w.shape, pool_scale.shape, sg_ln_w.shape, sg_ln_b.shape]
    (sm,) = _exchange([("ag_blk", _pack([c, norm_w, conv_dw_w, pool_scale, sg_ln_w, sg_ln_b], 8))], "gather_small")
    c_all, nw_s, cw_s, ps_s, slw_s, slb_s = _unpack(sm, small_shapes)
    cat_last = lambda a: jnp.moveaxis(a, 0, -2).reshape(a.shape[1:-1] + (-1,))
    conv_w_full = cat_last(cw_s)
    S = {"norm_w": cat_last(nw_s), "lg": jax.nn.log_sigmoid(ret_decay_logit),
         "conv_w": jnp.pad(conv_w_full, ((0, 0), (0, 32 - CONV_K), (0, 0))),
         "conv_ln_w": conv_ln_w, "conv_ln_b": conv_ln_b, "pool_w": pool_w, "pool_scale": cat_last(ps_s),
         "sg_ln_w": cat_last(slw_s), "sg_ln_b": cat_last(slb_s), "sg_w": sg_w, "sg_b": sg_b,
         "final_norm_w": final_norm_w}

    cs = jnp.concatenate([c_all, c_ctx[None], jnp.zeros((7, D), F32)], axis=0)
    ab_loc = lax.dynamic_slice_in_dim(ada_b, me * mcols, mcols, axis=1)
    mod_loc = _ada_fwd(cs, ada_w, ab_loc[:, None, :], "ada_fwd")
    (mod_all,) = _exchange([("ag_blk", mod_loc.reshape(DEPTH * 16, mcols))], "gather_mod")
    mod_all = mod_all.reshape(N_DEV, DEPTH, 16, mcols).transpose(1, 2, 0, 3).reshape(DEPTH, 16, 6, D)
    mod_x = lax.dynamic_index_in_dim(mod_all, me, axis=1, keepdims=False)
    mods = jnp.stack([mod_x, mod_all[:, 8]], axis=1)

    tr = lambda a: jnp.swapaxes(a, 1, 2)
    pad_rows = lambda a: jnp.pad(a, ((0, FF_SHARD_PAD - FF_SHARD), (0, 0))).astype(BF16)
    gate_t, up_t, odd_in_t = tr(ffn_w_gate), tr(ffn_w_up), tr(odd_w_in)
    shards = []
    for i in range(DEPTH):
        j, even = i // 2, i % 2 == 0
        w_in, w_out = (even_w_in[j], even_w_out[j]) if even else (odd_in_t[j], odd_w_out[j])
        shards.append({"in": w_in.astype(BF16), "out": w_out.astype(BF16), "gate": pad_rows(gate_t[i]),
                       "up": pad_rows(up_t[i]), "down": pad_rows(ffn_w_down[i])})

    X0 = jnp.concatenate([x[0], ctx[0]], axis=0)
    loss, dX, big, gS, dmods, early_all = _sample_step(X0, loss_target[0], mods, shards, S, seq, n_ctx)
    loss = lax.psum(loss, ("x", "y", "c"))
    grad_x = dX[:seq][None]

    (dm_all,) = _exchange([("ag_blk", dmods.reshape(DEPTH * 2, 6 * D))], "gather_dmod")
    dm_all = dm_all.reshape(N_DEV, DEPTH, 2, 6 * D)
    dm_sum = _sum_devices(dm_all.reshape(N_DEV, DEPTH * 2, 6 * D), "sum_dmod").reshape(DEPTH, 2, 6 * D)
    g_ada_b = dm_sum[:, 0] + dm_sum[:, 1]
    G = lax.dynamic_slice_in_dim(dm_all, me * mcols, mcols, axis=3).transpose(1, 2, 0, 3).reshape(DEPTH, 16, mcols)
    g_ada_w, ccp = _ada_bwd(cs, G, ada_w, "ada_bwd")
    sg_cc = _sigmoid(c_ctx)
    g_cctx_part = ccp[0] * sg_cc * (1.0 + c_ctx * (1.0 - sg_cc))

    dsig = _sigmoid(-ret_decay_logit)
    part = [g_cctx_part, gS["norm_w"], gS["lg"] * dsig, gS["conv_w"][:, :CONV_K], gS["conv_ln_w"], gS["conv_ln_b"],
            gS["pool_scale"], gS["sg_ln_w"], gS["sg_ln_b"], gS["sg_b"], gS["final_norm_w"]]
    part_shapes = [a.shape for a in part]
    (part_all,) = _exchange([("ag_blk", _pack(part, 8))], "gather_small_grads")
    red = _sum_devices(part_all, "sum_small_grads")
    g_cctx, g_nw, g_rdl, g_cw, g_clw, g_clb, g_ps, g_slw, g_slb, g_sb, g_fnw = _unpack(red, part_shapes)
    g_pw, g_sw = _unpack(_sum_devices(early_all, "sum_early_grads"), [pool_w.shape, sg_w.shape])
    mine_last = lambda a, n: lax.dynamic_slice_in_dim(a, me * n, n, axis=a.ndim - 1)
    g_nw, g_cw = mine_last(g_nw, norm_w.shape[2]), mine_last(g_cw, conv_dw_w.shape[2])
    g_ps, g_slw, g_slb = (mine_last(a, pool_scale.shape[1]) for a in (g_ps, g_slw, g_slb))

    def adam_big(key, layers, w, m, v, name):
        return _adamw_layers(w, [big[(key, l)] for l in layers], m, v, name)

    res = {}
    res["even_w_in"] = adam_big("in", (0, 2), even_w_in, m_even_w_in, v_even_w_in, "adam_even_in")
    res["even_w_out"] = adam_big("out", (0, 2), even_w_out, m_even_w_out, v_even_w_out, "adam_even_out")
    adam_t = lambda key, layers, w, m, v, name: [tr(o) for o in adam_big(key, layers, tr(w), tr(m), tr(v), name)]
    res["odd_w_in"] = adam_t("in", (1, 3), odd_w_in, m_odd_w_in, v_odd_w_in, "adam_odd_in")
    res["odd_w_out"] = adam_big("out", (1, 3), odd_w_out, m_odd_w_out, v_odd_w_out, "adam_odd_out")
    res["ffn_w_gate"] = adam_t("gate", (0, 1, 2, 3), ffn_w_gate, m_ffn_w_gate, v_ffn_w_gate, "adam_gate")
    res["ffn_w_up"] = adam_t("up", (0, 1, 2, 3), ffn_w_up, m_ffn_w_up, v_ffn_w_up, "adam_up")
    res["ffn_w_down"] = adam_big("down", (0, 1, 2, 3), ffn_w_down, m_ffn_w_down, v_ffn_w_down, "adam_down")
    flat2 = lambda a: a.reshape(-1, a.shape[-1])
    res["ada_w"] = [o.reshape(ada_w.shape) for o in
                    _adamw(flat2(ada_w), flat2(g_ada_w), flat2(m_ada_w), flat2(v_ada_w), "adam_ada_w")]

    names = ["c_ctx", "ada_b", "norm_w", "ret_decay_logit", "conv_dw_w", "conv_ln_w", "conv_ln_b", "pool_w",
             "pool_scale", "sg_ln_w", "sg_ln_b", "sg_w", "sg_b", "final_norm_w"]
    ws = [c_ctx, ada_b, norm_w, ret_decay_logit, conv_dw_w, conv_ln_w, conv_ln_b, pool_w, pool_scale, sg_ln_w,
          sg_ln_b, sg_w, sg_b, final_norm_w]
    gs = [g_cctx, g_ada_b, g_nw, g_rdl, g_cw, g_clw, g_clb, g_pw, g_ps, g_slw, g_slb, g_sw, g_sb, g_fnw]
    ms = [m_c_ctx, m_ada_b, m_norm_w, m_ret_decay_logit, m_conv_dw_w, m_conv_ln_w, m_conv_ln_b, m_pool_w,
          m_pool_scale, m_sg_ln_w, m_sg_ln_b, m_sg_w, m_sg_b, m_final_norm_w]
    vs = [v_c_ctx, v_ada_b, v_norm_w, v_ret_decay_logit, v_conv_dw_w, v_conv_ln_w, v_conv_ln_b, v_pool_w,
          v_pool_scale, v_sg_ln_w, v_sg_ln_b, v_sg_w, v_sg_b, v_final_norm_w]
    shapes = [a.shape for a in ws]
    gs = [g.reshape(s) for g, s in zip(gs, shapes)]
    pk = lambda arrs: _pack(arrs, 512)
    outs = _adamw(pk(ws), pk(gs), pk(ms), pk(vs), "adam_small")
    for k, o in enumerate(outs):
        for nm, arr in zip(names, _unpack(o, shapes)):
            res.setdefault(nm, [None] * 4)[k] = arr

    order = ["c_ctx", "ada_w", "ada_b", "norm_w", "even_w_in", "even_w_out", "ret_decay_logit", "conv_dw_w",
             "conv_ln_w", "conv_ln_b", "odd_w_in", "odd_w_out", "pool_w", "pool_scale", "sg_ln_w", "sg_ln_b",
             "sg_w", "sg_b", "ffn_w_gate", "ffn_w_up", "ffn_w_down", "final_norm_w"]
    return (loss, grad_x, *[res[n][0] for n in order], *[res[n][1] for n in order],
            *[res[n][2] for n in order], *[res[n][3] for n in order])
```

```python
import math

import jax
import jax.numpy as jnp
from jax import lax
from jax.experimental import pallas as pl
from jax.experimental.pallas import tpu as pltpu

F32 = jnp.float32
BF16 = jnp.bfloat16

N_DEV = 8
D = 1024
DEPTH = 4
ROW_TILE = 256
HALO = 16
LANES = 128
EPS = 1e-6
RET_HEADS = 4
HEAD_DIM = 128
RET_W = RET_HEADS * HEAD_DIM
CHUNK = 128
Q_SCALE = HEAD_DIM ** -0.5
ROPE_BASE = 10000.0
ROPE_PAIRS = (HEAD_DIM // 8, 3 * HEAD_DIM // 16, 3 * HEAD_DIM // 16)
GRID_W = 64
CONV_CH = 512
CONV_K = 31
EVEN_IN = 4 * RET_W + 2 * CONV_CH
POOL_CH = 512
POOL_WINDOWS = (2, 4, 8, 16)
GROUP_CH = 128
SG_CH = 512
SG_CHUNK = 128
ODD_IN = POOL_CH + 2 * SG_CH
D_FF = 2816
INV_SQRT2 = 1.0 / math.sqrt(2.0)
INV_SQRT_2PI = 1.0 / math.sqrt(2.0 * math.pi)
ADAM_LR, ADAM_B1, ADAM_B2, ADAM_EPS, ADAM_WD, ADAM_STEP = 0.001, 0.9, 0.999, 1e-08, 0.01, 10
ADAM_BC1 = 1.0 - ADAM_B1 ** ADAM_STEP
ADAM_BC2 = 1.0 - ADAM_B2 ** ADAM_STEP
VMEM_LIMIT = 56 * 1024 * 1024
MESH = pl.DeviceIdType.MESH


def _params(sem=None):
    return pltpu.CompilerParams(dimension_semantics=sem, vmem_limit_bytes=VMEM_LIMIT)


def _dot(a, b):
    return jnp.dot(a, b, preferred_element_type=F32)


def _dot_nt(a, b):
    return lax.dot_general(a, b, (((1,), (1,)), ((), ())), preferred_element_type=F32)


def _dot_tn(a, b):
    return lax.dot_general(a, b, (((0,), (0,)), ((), ())), preferred_element_type=F32)


def _sigmoid(x):
    return 1.0 / (1.0 + jnp.exp(-x))


def _colsum(x):
    return jnp.sum(x, axis=0, keepdims=True)


def _rowmean(x):
    return jnp.mean(x, axis=-1, keepdims=True)


def _row(shape_cols):
    return pl.BlockSpec((ROW_TILE, shape_cols), lambda i: (i, 0))


def _const(shape):
    nd = len(shape)
    return pl.BlockSpec(shape, lambda i: (0,) * nd)


def _mod_spec(nxt):
    return pl.BlockSpec((1, 6, D), lambda i: (i // nxt, 0, 0))


def _stat_spec(nxt):
    return pl.BlockSpec((1, 8, D), lambda i: (i // nxt, 0, 0))


def _xfer_out_shape(kind, a):
    r, c = a.shape
    shape = {"ag_blk": (N_DEV, r, c), "ag_cols": (r, N_DEV * c), "ag_rows": (N_DEV * r, c),
             "a2a_cols": (N_DEV, r, c // N_DEV), "a2a_rows": (N_DEV, r // N_DEV, c)}[kind]
    return jax.ShapeDtypeStruct(shape, a.dtype)


def _src_view(kind, ref, who):
    if kind == "a2a_cols":
        n = ref.shape[1] // N_DEV
        return ref.at[:, pl.ds(pl.multiple_of(who * n, LANES), n)]
    if kind == "a2a_rows":
        r = ref.shape[0] // N_DEV
        return ref.at[pl.ds(pl.multiple_of(who * r, 16), r), :]
    return ref


def _dst_view(kind, ref, who):
    if kind == "ag_cols":
        n = ref.shape[1] // N_DEV
        return ref.at[:, pl.ds(pl.multiple_of(who * n, LANES), n)]
    if kind == "ag_rows":
        r = ref.shape[0] // N_DEV
        return ref.at[pl.ds(pl.multiple_of(who * r, 16), r), :]
    return ref.at[who]


def _xfers(kinds, srcs, dsts, send_sems, recv_sems, local_sems, start):
    mx, my, mc = lax.axis_index("x"), lax.axis_index("y"), lax.axis_index("c")
    me = 4 * mx + 2 * my + mc
    peers = []
    for k in range(1, N_DEV):
        px, py, pc = mx ^ ((k >> 2) & 1), my ^ ((k >> 1) & 1), mc ^ (k & 1)
        peers.append(((px, py, pc), 4 * px + 2 * py + pc))
    for j, (kind, src, dst) in enumerate(zip(kinds, srcs, dsts)):
        own = pltpu.make_async_copy(_src_view(kind, src, me), _dst_view(kind, dst, me), local_sems.at[j])
        if start:
            own.start()
        for k, (pid, pidx) in enumerate(peers):
            sem = (N_DEV - 1) * j + k
            cp = pltpu.make_async_remote_copy(
                src_ref=_src_view(kind, src, pidx), dst_ref=_dst_view(kind, dst, me if start else pidx),
                send_sem=send_sems.at[sem], recv_sem=recv_sems.at[sem], device_id=pid, device_id_type=MESH)
            if start:
                cp.start()
            else:
                cp.wait_recv()
                cp.wait_send()
        if not start:
            own.wait()


def _xfer_sems(n):
    return [pltpu.SemaphoreType.DMA(((N_DEV - 1) * n,)), pltpu.SemaphoreType.DMA(((N_DEV - 1) * n,)),
            pltpu.SemaphoreType.DMA((n,))]


def _exchange(comm, name):
    kinds = [k for k, _ in comm]
    n = len(comm)

    def body(*refs):
        sems = refs[2 * n:]
        _xfers(kinds, refs[:n], refs[n:2 * n], *sems, start=True)
        _xfers(kinds, refs[:n], refs[n:2 * n], *sems, start=False)

    hbm = pl.BlockSpec(memory_space=pl.ANY)
    return pl.pallas_call(
        body, name=name, in_specs=[hbm] * n, out_specs=[hbm] * n,
        out_shape=[_xfer_out_shape(k, a) for k, a in comm], scratch_shapes=_xfer_sems(n),
    )(*[a for _, a in comm])


_HBM = pl.BlockSpec(memory_space=pltpu.HBM)
_SEM = pl.BlockSpec(memory_space=pltpu.SEMAPHORE)
_EFFECT = pltpu.SideEffectType.DATAFLOW_SIDE_EFFECTING


def _mesh_peers():
    mx, my, mc = lax.axis_index("x"), lax.axis_index("y"), lax.axis_index("c")
    peers = []
    for k in range(1, N_DEV):
        px, py, pc = mx ^ ((k >> 2) & 1), my ^ ((k >> 1) & 1), mc ^ (k & 1)
        peers.append(((px, py, pc), 4 * px + 2 * py + pc))
    return 4 * mx + 2 * my + mc, peers


def _landing(kind, a, me):
    sd = _xfer_out_shape(kind, a)
    land = lax.empty(sd.shape, sd.dtype)
    r, c = a.shape
    if kind == "ag_rows":
        return lax.dynamic_update_slice(land, a, (me * r, 0))
    if kind == "ag_cols":
        return lax.dynamic_update_slice(land, a, (0, me * c))
    if kind == "a2a_rows":
        own = lax.dynamic_slice(a, (me * (r // N_DEV), 0), (r // N_DEV, c))
    elif kind == "a2a_cols":
        own = lax.dynamic_slice(a, (0, me * (c // N_DEV)), (r, c // N_DEV))
    else:
        own = a
    return lax.dynamic_update_slice(land, own[None], (me, 0, 0))


def _xfer_start(comm, name):
    kinds = [k for k, _ in comm]
    n = len(comm)
    me = 4 * lax.axis_index("x") + 2 * lax.axis_index("y") + lax.axis_index("c")
    srcs = [pltpu.with_memory_space_constraint(a, pltpu.HBM) for _, a in comm]
    lands = [pltpu.with_memory_space_constraint(_landing(k, a, me), pltpu.HBM) for k, a in comm]

    def body(*refs):
        src_refs, land_refs = refs[:n], refs[n:2 * n]
        send_sems, recv_sems, token = refs[2 * n], refs[2 * n + 1], refs[-1]
        my, peers = _mesh_peers()
        for j, kind in enumerate(kinds):
            for k, (pid, pidx) in enumerate(peers):
                sem = (N_DEV - 1) * j + k
                pltpu.make_async_remote_copy(
                    src_ref=_src_view(kind, src_refs[j], pidx), dst_ref=_dst_view(kind, land_refs[j], my),
                    send_sem=send_sems.at[sem], recv_sem=recv_sems.at[sem], device_id=pid,
                    device_id_type=MESH).start()
        token[...] = jnp.zeros_like(token)

    sems = pltpu.SemaphoreType.DMA(((N_DEV - 1) * n,))
    outs = pl.pallas_call(
        body, name=name,
        out_shape=(sems, sems, *[pltpu.HBM(a.shape, a.dtype) for a in srcs + lands],
                   jax.ShapeDtypeStruct((8, LANES), F32)),
        in_specs=[_HBM] * (2 * n),
        out_specs=(_SEM, _SEM, *[_HBM] * (2 * n), pl.BlockSpec(memory_space=pltpu.VMEM)),
        input_output_aliases={k: 2 + k for k in range(2 * n)},
        compiler_params=pltpu.CompilerParams(has_side_effects=_EFFECT),
    )(*srcs, *lands)
    return (kinds, outs[0], outs[1], list(outs[2:2 + 2 * n])), outs[-1]


def _xfer_wait(handle, after, name):
    kinds, send_sems, recv_sems, bufs = handle
    n = len(kinds)

    def body(*refs):
        src_refs, land_refs = refs[:n], refs[n:2 * n]
        send_ref, recv_ref = refs[2 * n], refs[2 * n + 1]
        _, peers = _mesh_peers()
        for j, kind in enumerate(kinds):
            for k, (pid, pidx) in enumerate(peers):
                sem = (N_DEV - 1) * j + k
                cp = pltpu.make_async_remote_copy(
                    src_ref=_src_view(kind, src_refs[j], pidx), dst_ref=_dst_view(kind, land_refs[j], pidx),
                    send_sem=send_ref.at[sem], recv_sem=recv_ref.at[sem], device_id=pid, device_id_type=MESH)
                cp.wait_send()
                cp.wait_recv()

    outs = pl.pallas_call(
        body, name=name, out_shape=[pltpu.HBM(a.shape, a.dtype) for a in bufs],
        in_specs=[_HBM] * (2 * n) + [_SEM, _SEM, pl.BlockSpec(memory_space=pl.ANY)], out_specs=[_HBM] * (2 * n),
        input_output_aliases={k: k for k in range(2 * n)},
        compiler_params=pltpu.CompilerParams(has_side_effects=_EFFECT),
    )(*bufs, send_sems, recv_sems, after)
    return list(outs[n:])


def _call(body, *, grid, in_specs, out_specs, out_shape, args, name, sem, scratch=(), comm=()):
    n_in, n_out, n_scr, n_c = len(in_specs), len(out_specs), len(scratch), len(comm)
    if not comm:
        outs = pl.pallas_call(body, grid=grid, in_specs=list(in_specs), out_specs=list(out_specs),
                              out_shape=list(out_shape), scratch_shapes=list(scratch), name=name,
                              compiler_params=_params(sem))(*args)
        return list(outs), []
    kinds = [k for k, _ in comm]
    n_steps = grid[0]

    def wrapped(*refs):
        ins, csrc = refs[:n_in], refs[n_in:n_in + n_c]
        o0 = n_in + n_c
        outs, cdst = refs[o0:o0 + n_out], refs[o0 + n_out:o0 + n_out + n_c]
        s0 = o0 + n_out + n_c
        scr, sems = refs[s0:s0 + n_scr], refs[s0 + n_scr:]
        i = pl.program_id(0)

        @pl.when(i == 0)
        def _():
            _xfers(kinds, csrc, cdst, *sems, start=True)

        body(*ins, *outs, *scr)

        @pl.when(i == n_steps - 1)
        def _():
            _xfers(kinds, csrc, cdst, *sems, start=False)

    hbm = pl.BlockSpec(memory_space=pl.ANY)
    outs = pl.pallas_call(
        wrapped, grid=grid, in_specs=list(in_specs) + [hbm] * n_c, out_specs=list(out_specs) + [hbm] * n_c,
        out_shape=list(out_shape) + [_xfer_out_shape(k, a) for k, a in comm],
        scratch_shapes=list(scratch) + _xfer_sems(n_c), name=name, compiler_params=_params(("arbitrary",)),
    )(*args, *[a for _, a in comm])
    return list(outs[:n_out]), list(outs[n_out:])


def _norm_mod_fwd(X, nw, mods, si, nxt, name):
    T = X.shape[0]

    def body(x_ref, w_ref, m_ref, h_ref):
        x = x_ref[...]
        r = lax.rsqrt(_rowmean(x * x) + EPS)
        m = m_ref[0]
        y = x * r * w_ref[...]
        h_ref[...] = (y * (1.0 + m[si + 1:si + 2, :]) + m[si:si + 1, :]).astype(BF16)

    return pl.pallas_call(
        body, grid=(T // ROW_TILE,), name=name,
        in_specs=[_row(D), _const((1, D)), _mod_spec(nxt)],
        out_specs=_row(D), out_shape=jax.ShapeDtypeStruct((T, D), BF16),
        compiler_params=_params(("parallel",)),
    )(X, nw, mods)


def _norm_mod_bwd(dh, X, nw, mods, dres, si, nxt, name):
    T = X.shape[0]

    def body(dh_ref, x_ref, w_ref, m_ref, dres_ref, dx_ref, st_ref):
        i = pl.program_id(0)

        @pl.when((i == 0) | (i == nxt))
        def _():
            st_ref[...] = jnp.zeros_like(st_ref)

        x = x_ref[...]
        r = lax.rsqrt(_rowmean(x * x) + EPS)
        xn = x * r
        w = w_ref[...]
        scale = m_ref[0][si + 1:si + 2, :]
        dhv = dh_ref[...]
        dy = dhv * (1.0 + scale)
        dxn = dy * w
        dx_ref[...] = dres_ref[...] + r * (dxn - xn * _rowmean(dxn * xn))
        st_ref[0, 0:1, :] += _colsum(dhv)
        st_ref[0, 1:2, :] += _colsum(dhv * xn * w)
        st_ref[0, 2:3, :] += _colsum(dy * xn)

    return pl.pallas_call(
        body, grid=(T // ROW_TILE,), name=name,
        in_specs=[_row(D), _row(D), _const((1, D)), _mod_spec(nxt), _row(D)],
        out_specs=[_row(D), _stat_spec(nxt)],
        out_shape=[jax.ShapeDtypeStruct((T, D), F32), jax.ShapeDtypeStruct((2, 8, D), F32)],
        compiler_params=_params(("arbitrary",)),
    )(dh, X, nw, mods, dres)


def _resid_bwd(dX, y, mods, gi, nxt, name):
    T = dX.shape[0]

    def body(dx_ref, y_ref, m_ref, dy_ref, st_ref):
        i = pl.program_id(0)

        @pl.when((i == 0) | (i == nxt))
        def _():
            st_ref[...] = jnp.zeros_like(st_ref)

        dx = dx_ref[...]
        dy_ref[...] = (dx * m_ref[0][gi:gi + 1, :]).astype(BF16)
        st_ref[0, 0:1, :] += _colsum(dx * y_ref[...].astype(F32))

    return pl.pallas_call(
        body, grid=(T // ROW_TILE,), name=name,
        in_specs=[_row(D), _row(D), _mod_spec(nxt)],
        out_specs=[_row(D), _stat_spec(nxt)],
        out_shape=[jax.ShapeDtypeStruct((T, D), BF16), jax.ShapeDtypeStruct((2, 8, D), F32)],
        compiler_params=_params(("arbitrary",)),
    )(dX, y, mods)


def _loss_head(X, tgt, fw, nxt, name):
    T = X.shape[0]

    def body(x_ref, t_ref, w_ref, dx_ref, st_ref):
        i = pl.program_id(0)

        @pl.when(i == 0)
        def _():
            st_ref[...] = jnp.zeros_like(st_ref)

        @pl.when(i < nxt)
        def _():
            x = x_ref[...]
            r = lax.rsqrt(_rowmean(x * x) + EPS)
            xn = x * r
            w = w_ref[...]
            err = xn * w - t_ref[...]
            dy = err * (1.0 / D)
            dxn = dy * w
            dx_ref[...] = r * (dxn - xn * _rowmean(dxn * xn))
            st_ref[0:1, :] += _colsum(dy * xn)
            st_ref[1:2, :] += (0.5 / D) * _colsum(err * err)

        @pl.when(i >= nxt)
        def _():
            dx_ref[...] = jnp.zeros_like(dx_ref)

    return pl.pallas_call(
        body, grid=(T // ROW_TILE,), name=name,
        in_specs=[_row(D), pl.BlockSpec((ROW_TILE, D), lambda i: (jnp.minimum(i, nxt - 1), 0)), _const((1, D))],
        out_specs=[_row(D), _const((8, D))],
        out_shape=[jax.ShapeDtypeStruct((T, D), F32), jax.ShapeDtypeStruct((8, D), F32)],
        compiler_params=_params(("arbitrary",)),
    )(X, tgt, fw)


def _mm_nn(A, B, out_dtype, name, add=None, comm=()):
    T, K = A.shape
    N = B.shape[1]

    if add is None:
        def body(a_ref, b_ref, o_ref):
            o_ref[...] = _dot(a_ref[...], b_ref[...]).astype(out_dtype)
        ins, specs = (A, B), [_row(K), _const((K, N))]
    else:
        def body(a_ref, b_ref, c_ref, o_ref):
            o_ref[...] = (c_ref[...] + _dot(a_ref[...], b_ref[...])).astype(out_dtype)
        ins, specs = (A, B, add), [_row(K), _const((K, N)), _row(N)]

    return _call(body, grid=(T // ROW_TILE,), name=name, in_specs=specs,
                 out_specs=[_row(N)], out_shape=[jax.ShapeDtypeStruct((T, N), out_dtype)],
                 sem=("parallel",), args=ins, comm=comm)


def _mm_nn_resid(A, B, X, mods, gi, nxt, name, comm=()):
    T, K = A.shape
    N = B.shape[1]

    def body(a_ref, b_ref, x_ref, m_ref, xo_ref, y_ref):
        acc = _dot(a_ref[...], b_ref[...])
        y_ref[...] = acc.astype(BF16)
        xo_ref[...] = x_ref[...] + m_ref[0][gi:gi + 1, :] * acc

    return _call(body, grid=(T // ROW_TILE,), name=name,
                 in_specs=[_row(K), _const((K, N)), _row(N), _mod_spec(nxt)], out_specs=[_row(N), _row(N)],
                 out_shape=[jax.ShapeDtypeStruct((T, N), F32), jax.ShapeDtypeStruct((T, N), BF16)],
                 sem=("parallel",), args=(A, B, X, mods), comm=comm)


def _mm_nt(A, B, name, comm=()):
    T, N = A.shape
    K = B.shape[0]

    def body(a_ref, b_ref, o_ref):
        o_ref[...] = _dot_nt(a_ref[...], b_ref[...])

    return _call(body, grid=(T // ROW_TILE,), name=name, in_specs=[_row(N), _const((K, N))], out_specs=[_row(K)],
                 out_shape=[jax.ShapeDtypeStruct((T, K), F32)], sem=("parallel",), args=(A, B), comm=comm)


def _mm_tn(A, G, name):
    T, K = A.shape
    N = G.shape[1]
    nt = T // ROW_TILE

    def body(a_ref, g_ref, o_ref, acc_ref):
        i = pl.program_id(0)

        @pl.when(i == 0)
        def _():
            acc_ref[...] = jnp.zeros_like(acc_ref)

        acc_ref[...] += _dot_tn(a_ref[...], g_ref[...])

        @pl.when(i == nt - 1)
        def _():
            o_ref[...] = acc_ref[...].astype(BF16)

    return pl.pallas_call(
        body, grid=(nt,), name=name,
        in_specs=[_row(K), _row(N)],
        out_specs=_const((K, N)), out_shape=jax.ShapeDtypeStruct((K, N), BF16),
        scratch_shapes=[pltpu.VMEM((K, N), F32)],
        compiler_params=_params(("arbitrary",)),
    )(A, G)


def _ffn_up(h, WgT, WuT, name, comm=()):
    T = h.shape[0]
    F = WgT.shape[0]

    def body(h_ref, wg_ref, wu_ref, gp_ref, up_ref, act_ref):
        hv = h_ref[...]
        gp = _dot_nt(hv, wg_ref[...])
        up = _dot_nt(hv, wu_ref[...])
        gp_ref[...] = gp.astype(BF16)
        up_ref[...] = up.astype(BF16)
        act_ref[...] = (gp * _sigmoid(gp) * up).astype(BF16)

    sd = jax.ShapeDtypeStruct((T, F), BF16)
    return _call(body, grid=(T // ROW_TILE,), name=name, in_specs=[_row(D), _const((F, D)), _const((F, D))],
                 out_specs=[_row(F), _row(F), _row(F)], out_shape=[sd, sd, sd], sem=("parallel",),
                 args=(h, WgT, WuT), comm=comm)


def _ffn_dact(dy, Wd, gp, up, name, comm=()):
    T = dy.shape[0]
    F = Wd.shape[0]

    def body(dy_ref, wd_ref, gp_ref, up_ref, dgp_ref, dup_ref):
        dact = _dot_nt(dy_ref[...], wd_ref[...])
        gpv = gp_ref[...].astype(F32)
        upv = up_ref[...].astype(F32)
        sg = _sigmoid(gpv)
        dup_ref[...] = (dact * gpv * sg).astype(BF16)
        dgp_ref[...] = (dact * upv * sg * (1.0 + gpv * (1.0 - sg))).astype(BF16)

    sd = jax.ShapeDtypeStruct((T, F), BF16)
    return _call(body, grid=(T // ROW_TILE,), name=name, in_specs=[_row(D), _const((F, D)), _row(F), _row(F)],
                 out_specs=[_row(F), _row(F)], out_shape=[sd, sd], sem=("parallel",),
                 args=(dy, Wd, gp, up), comm=comm)


N_TAB = 7


def _ret_tables(tab_ref, lgf, lgb):
    r = lax.broadcasted_iota(jnp.int32, (CHUNK, CHUNK), 0).astype(F32)
    c = lax.broadcasted_iota(jnp.int32, (CHUNK, CHUNK), 1).astype(F32)
    for d, lg in ((0, lgf), (1, lgb)):
        if d == 0:
            mask, expo, xe, ze = r >= c, r - c, r + 1.0, (CHUNK - 1.0) - r
        else:
            mask, expo, xe, ze = c > r, c - r - 1.0, (CHUNK - 1.0) - r, r
        e = jnp.where(mask, expo, 0.0)
        tab_ref[N_TAB * d + 0] = jnp.where(mask, jnp.exp(lg * e), 0.0)
        tab_ref[N_TAB * d + 1] = jnp.exp(lg * xe)
        tab_ref[N_TAB * d + 2] = jnp.exp(lg * ze)
        tab_ref[N_TAB * d + 3] = jnp.exp(jnp.full((CHUNK, CHUNK), lg * float(CHUNK), F32))
        tab_ref[N_TAB * d + 4] = e
        tab_ref[N_TAB * d + 5] = xe
        tab_ref[N_TAB * d + 6] = ze


def _rope(t, cosf, sgn):
    return t * cosf + pltpu.roll(t, HEAD_DIM // 2, 1) * sgn


def _rope_t(d, cosf, sgn):
    return d * cosf + pltpu.roll(d * sgn, HEAD_DIM // 2, 1)


def _chunk_of(d, j, nc, ncx):
    return lax.rem(j + ncx, nc) if d == 0 else nc - 1 - j


def _head_spec(T, col0):
    return pl.BlockSpec((T, HEAD_DIM), lambda h: (0, col0 + h))


def _ret_fwd(p, lg, cosf, sgn, seq, name, comm=()):
    T = p.shape[0]
    nc, ncx = T // CHUNK, seq // CHUNK

    def body(lg_ref, q_ref, k_ref, v_ref, g_ref, cos_ref, sgn_ref, y_ref, ro_ref, tab_ref, of_ref):
        h = pl.program_id(0)
        _ret_tables(tab_ref, lg_ref[0, h], lg_ref[1, h])

        def load(cidx):
            rows = pl.ds(pl.multiple_of(cidx * CHUNK, CHUNK), CHUNK)
            cs, sn = cos_ref[rows, :], sgn_ref[rows, :]
            return rows, _rope(q_ref[rows, :], cs, sn) * Q_SCALE, _rope(k_ref[rows, :], cs, sn), v_ref[rows, :]

        def chunk(d, q, k, v, S):
            b = N_TAB * d
            kb, vb = k.astype(BF16), v.astype(BF16)
            pm = _dot_nt(q.astype(BF16), kb) * tab_ref[b]
            o = _dot(pm.astype(BF16), vb) + _dot((q * tab_ref[b + 1]).astype(BF16), S.astype(BF16))
            return o, S * tab_ref[b + 3] + _dot_tn((k * tab_ref[b + 2]).astype(BF16), vb)

        def step_f(j, S):
            rows, q, k, v = load(_chunk_of(0, j, nc, ncx))
            o, S = chunk(0, q, k, v, S)
            of_ref[rows, :] = o
            return S

        lax.fori_loop(0, nc, step_f, jnp.zeros((CHUNK, CHUNK), F32))

        def step_b(j, S):
            rows, q, k, v = load(_chunk_of(1, j, nc, ncx))
            o, S = chunk(1, q, k, v, S)
            y = of_ref[rows, :] + o
            y_ref[rows, :] = y
            g = g_ref[rows, :]
            ro_ref[rows, :] = (g * _sigmoid(g) * y * lax.rsqrt(_rowmean(y * y) + EPS)).astype(BF16)
            return S

        lax.fori_loop(0, nc, step_b, jnp.zeros((CHUNK, CHUNK), F32))

    tbl = pl.BlockSpec((T, HEAD_DIM), lambda h: (0, 0))
    return _call(
        body, grid=(RET_HEADS,), name=name,
        in_specs=[pl.BlockSpec(memory_space=pltpu.SMEM), _head_spec(T, 0), _head_spec(T, 4), _head_spec(T, 8),
                  _head_spec(T, 12), tbl, tbl],
        out_specs=[_head_spec(T, 0), _head_spec(T, 0)],
        out_shape=[jax.ShapeDtypeStruct((T, RET_W), F32), jax.ShapeDtypeStruct((T, RET_W), BF16)],
        scratch=[pltpu.VMEM((2 * N_TAB, CHUNK, CHUNK), F32), pltpu.VMEM((T, HEAD_DIM), F32)],
        sem=("arbitrary",), args=(lg, p, p, p, p, cosf, sgn), comm=comm)


def _ret_gate_bwd(yret, p, dmix, name):
    T = yret.shape[0]

    def body(y_ref, g_ref, dm_ref, dy_ref, dg_ref):
        for hh in range(RET_HEADS):
            sl = slice(hh * HEAD_DIM, (hh + 1) * HEAD_DIM)
            y = y_ref[:, sl]
            r = lax.rsqrt(_rowmean(y * y) + EPS)
            yn = y * r
            g = g_ref[:, sl]
            sg = _sigmoid(g)
            dro = dm_ref[:, sl]
            dg_ref[:, sl] = (dro * yn * sg * (1.0 + g * (1.0 - sg))).astype(BF16)
            dyn = dro * g * sg
            dy_ref[:, sl] = r * (dyn - yn * _rowmean(dyn * yn))

    return pl.pallas_call(
        body, grid=(T // ROW_TILE,), name=name,
        in_specs=[_row(RET_W), pl.BlockSpec((ROW_TILE, RET_W), lambda i: (i, 3)), _row(RET_W)],
        out_specs=[_row(RET_W), _row(RET_W)],
        out_shape=[jax.ShapeDtypeStruct((T, RET_W), F32), jax.ShapeDtypeStruct((T, RET_W), BF16)],
        compiler_params=_params(("parallel",)),
    )(yret, p, dmix)


def _ret_bwd(p, lg, dy, cosf, sgn, seq, name, comm=()):
    T = p.shape[0]
    nc, ncx = T // CHUNK, seq // CHUNK

    def body(lg_ref, q_ref, k_ref, v_ref, dy_ref, cos_ref, sgn_ref, dq_ref, dk_ref, dv_ref, dlg_ref,
             tab_ref, st_ref, dqs, dks, dvs):
        h = pl.program_id(0)
        _ret_tables(tab_ref, lg_ref[0, h], lg_ref[1, h])
        dlg_ref[...] = jnp.zeros_like(dlg_ref)

        def load(cidx):
            rows = pl.ds(pl.multiple_of(cidx * CHUNK, CHUNK), CHUNK)
            cs, sn = cos_ref[rows, :], sgn_ref[rows, :]
            return rows, _rope(q_ref[rows, :], cs, sn) * Q_SCALE, _rope(k_ref[rows, :], cs, sn), v_ref[rows, :]

        for d in (0, 1):
            b = N_TAB * d

            def states(j, S, d=d, b=b):
                _, _, k, v = load(_chunk_of(d, j, nc, ncx))
                st_ref[j] = S
                return S * tab_ref[b + 3] + _dot_tn((k * tab_ref[b + 2]).astype(BF16), v.astype(BF16))

            lax.fori_loop(0, nc, states, jnp.zeros((CHUNK, CHUNK), F32))

            def sweep(jj, carry, d=d, b=b):
                dS, acc = carry
                j = nc - 1 - jj
                rows, q, k, v = load(_chunk_of(d, j, nc, ncx))
                dO = dy_ref[rows, :]
                Sp = st_ref[j]
                qb, kb, vb, dOb = q.astype(BF16), k.astype(BF16), v.astype(BF16), dO.astype(BF16)
                Spb, dSb = Sp.astype(BF16), dS.astype(BF16)
                dmat, xi, ze, cd = tab_ref[b], tab_ref[b + 1], tab_ref[b + 2], tab_ref[b + 3]
                pm = _dot_nt(qb, kb) * dmat
                dpm = _dot_nt(dOb, vb)
                acc = acc + _colsum(dpm * pm * tab_ref[b + 4])
                dsc = (dpm * dmat).astype(BF16)
                qx = (q * xi).astype(BF16)
                kz = (k * ze).astype(BF16)
                dq = _dot(dsc, kb) + _dot_nt(dOb, Spb) * xi
                dk = _dot_tn(dsc, qb) + _dot_nt(vb, dSb) * ze
                dvst = _dot(kz, dSb)
                dv = _dot_tn(pm.astype(BF16), dOb) + dvst
                inter = _dot(qx, Spb)
                acc = acc + _colsum(dO * inter * tab_ref[b + 5])
                acc = acc + float(CHUNK) * _colsum(dS * cd * Sp) + _colsum(v * dvst * tab_ref[b + 6])
                if d == 0:
                    dqs[rows, :] = dq
                    dks[rows, :] = dk
                    dvs[rows, :] = dv
                else:
                    dqs[rows, :] += dq
                    dks[rows, :] += dk
                    dvs[rows, :] += dv
                return dS * cd + _dot_tn(qx, dOb), acc

            _, acc = lax.fori_loop(0, nc, sweep, (jnp.zeros((CHUNK, CHUNK), F32), jnp.zeros((1, CHUNK), F32)))
            dlg_ref[0, d:d + 1, :] = jnp.zeros((1, LANES), F32) + jnp.sum(acc)

        def finish(cidx, carry):
            rows = pl.ds(pl.multiple_of(cidx * CHUNK, CHUNK), CHUNK)
            cs, sn = cos_ref[rows, :], sgn_ref[rows, :]
            dq_ref[rows, :] = (_rope_t(dqs[rows, :], cs, sn) * Q_SCALE).astype(BF16)
            dk_ref[rows, :] = _rope_t(dks[rows, :], cs, sn).astype(BF16)
            dv_ref[rows, :] = dvs[rows, :].astype(BF16)
            return carry

        lax.fori_loop(0, nc, finish, 0)

    tbl = pl.BlockSpec((T, HEAD_DIM), lambda h: (0, 0))
    sd = jax.ShapeDtypeStruct((T, RET_W), BF16)
    return _call(
        body, grid=(RET_HEADS,), name=name,
        in_specs=[pl.BlockSpec(memory_space=pltpu.SMEM), _head_spec(T, 0), _head_spec(T, 4), _head_spec(T, 8),
                  _head_spec(T, 0), tbl, tbl],
        out_specs=[_head_spec(T, 0), _head_spec(T, 0), _head_spec(T, 0),
                   pl.BlockSpec((1, 8, LANES), lambda h: (h, 0, 0))],
        out_shape=[sd, sd, sd, jax.ShapeDtypeStruct((RET_HEADS, 8, LANES), F32)],
        scratch=[pltpu.VMEM((2 * N_TAB, CHUNK, CHUNK), F32), pltpu.VMEM((nc, CHUNK, CHUNK), F32),
                 pltpu.VMEM((T, HEAD_DIM), F32), pltpu.VMEM((T, HEAD_DIM), F32), pltpu.VMEM((T, HEAD_DIM), F32)],
        sem=("arbitrary",), args=(lg, p, p, p, dy, cosf, sgn), comm=comm)


def _halo_specs(T, cols, colblock):
    per = ROW_TILE // HALO
    last = T // HALO - 1
    prv = pl.BlockSpec((HALO, cols), lambda i: (jnp.maximum(i * per - 1, 0), colblock))
    nxt = pl.BlockSpec((HALO, cols), lambda i: (jnp.minimum((i + 1) * per, last), colblock))
    return prv, nxt


def _seq_edges(i, nxt, nt):
    first = (i == 0) | (i == nxt)
    last = (i == nxt - 1) | (i == nt - 1)
    return first, last


def _fill_pad(pad_ref, prv, cur, nxt, first, last):
    pad_ref[0:HALO, :] = jnp.where(first, 0.0, prv)
    pad_ref[HALO:HALO + ROW_TILE, :] = cur
    pad_ref[HALO + ROW_TILE:2 * HALO + ROW_TILE, :] = jnp.where(last, 0.0, nxt)


def _ln_stats(x):
    mu = _rowmean(x)
    xc = x - mu
    rs = lax.rsqrt(_rowmean(xc * xc) + EPS)
    return xc * rs, rs


def _conv_fwd(p, cw, lnw, lnb, nxt, name, comm=()):
    T = p.shape[0]
    nt = T // ROW_TILE
    a_col, g_col = 4, 5

    def body(a_ref, g_ref, ap_ref, gp_ref, an_ref, gn_ref, w_ref, lw_ref, lb_ref, cv_ref, co_ref, pad_ref):
        i = pl.program_id(0)
        first, last = _seq_edges(i, nxt, nt)
        glu = lambda a, g: a * _sigmoid(g)
        _fill_pad(pad_ref, glu(ap_ref[...], gp_ref[...]), glu(a_ref[...], g_ref[...]),
                  glu(an_ref[...], gn_ref[...]), first, last)
        acc = jnp.zeros((ROW_TILE, CONV_CH), F32)
        for k in range(CONV_K):
            acc = acc + w_ref[k:k + 1, :] * pad_ref[k + 1:k + 1 + ROW_TILE, :]
        cv_ref[...] = acc
        xh, _ = _ln_stats(acc)
        z = xh * lw_ref[...] + lb_ref[...]
        co_ref[...] = (z * _sigmoid(z)).astype(BF16)

    cur = lambda cb: pl.BlockSpec((ROW_TILE, CONV_CH), lambda i: (i, cb))
    ap, an = _halo_specs(T, CONV_CH, a_col)
    gp, gn = _halo_specs(T, CONV_CH, g_col)
    return _call(
        body, grid=(nt,), name=name,
        in_specs=[cur(a_col), cur(g_col), ap, gp, an, gn, _const((32, CONV_CH)), _const((1, CONV_CH)),
                  _const((1, CONV_CH))],
        out_specs=[_row(CONV_CH), _row(CONV_CH)],
        out_shape=[jax.ShapeDtypeStruct((T, CONV_CH), F32), jax.ShapeDtypeStruct((T, CONV_CH), BF16)],
        scratch=[pltpu.VMEM((ROW_TILE + 2 * HALO, CONV_CH), F32)],
        sem=("parallel",), args=(p, p, p, p, p, p, cw, lnw, lnb), comm=comm)


def _conv_bwd(p, cv, dmix, cw, lnw, lnb, nxt, name, comm=()):
    T = p.shape[0]
    nt = T // ROW_TILE
    a_col, g_col = 4, 5

    def body(a_ref, g_ref, ap_ref, gp_ref, an_ref, gn_ref, cv_ref, cvp_ref, cvn_ref, dc_ref, dcp_ref, dcn_ref,
             w_ref, lw_ref, lb_ref, dp_ref, dw_ref, dl_ref, upad_ref, dpad_ref):
        i = pl.program_id(0)
        first, last = _seq_edges(i, nxt, nt)

        @pl.when(i == 0)
        def _():
            dw_ref[...] = jnp.zeros_like(dw_ref)
            dl_ref[...] = jnp.zeros_like(dl_ref)

        lw, lb = lw_ref[...], lb_ref[...]

        def ln_bwd(cvv, dco):
            xh, rs = _ln_stats(cvv)
            z = xh * lw + lb
            sg = _sigmoid(z)
            dz = dco * sg * (1.0 + z * (1.0 - sg))
            dxh = dz * lw
            return rs * (dxh - _rowmean(dxh) - xh * _rowmean(dxh * xh)), dz, xh

        dcv, dz, xh = ln_bwd(cv_ref[...], dc_ref[...])
        _fill_pad(dpad_ref, ln_bwd(cvp_ref[...], dcp_ref[...])[0], dcv, ln_bwd(cvn_ref[...], dcn_ref[...])[0],
                  first, last)
        a, g = a_ref[...], g_ref[...]
        sg = _sigmoid(g)
        glu = lambda av, gv: av * _sigmoid(gv)
        _fill_pad(upad_ref, glu(ap_ref[...], gp_ref[...]), a * sg, glu(an_ref[...], gn_ref[...]), first, last)
        du = jnp.zeros((ROW_TILE, CONV_CH), F32)
        for k in range(CONV_K):
            du = du + w_ref[k:k + 1, :] * dpad_ref[CONV_K - k:CONV_K - k + ROW_TILE, :]
            dw_ref[k:k + 1, :] += _colsum(dcv * upad_ref[k + 1:k + 1 + ROW_TILE, :])
        dl_ref[0:1, :] += _colsum(dz * xh)
        dl_ref[1:2, :] += _colsum(dz)
        dp_ref[:, 0:CONV_CH] = (du * sg).astype(BF16)
        dp_ref[:, CONV_CH:2 * CONV_CH] = (du * a * sg * (1.0 - sg)).astype(BF16)

    cur = lambda cb: pl.BlockSpec((ROW_TILE, CONV_CH), lambda i: (i, cb))
    ap, an = _halo_specs(T, CONV_CH, a_col)
    gp, gn = _halo_specs(T, CONV_CH, g_col)
    cvp, cvn = _halo_specs(T, CONV_CH, 0)
    dcp, dcn = _halo_specs(T, CONV_CH, 1)
    return _call(
        body, grid=(nt,), name=name,
        in_specs=[cur(a_col), cur(g_col), ap, gp, an, gn, cur(0), cvp, cvn, cur(1), dcp, dcn,
                  _const((32, CONV_CH)), _const((1, CONV_CH)), _const((1, CONV_CH))],
        out_specs=[_row(2 * CONV_CH), _const((32, CONV_CH)), _const((8, CONV_CH))],
        out_shape=[jax.ShapeDtypeStruct((T, 2 * CONV_CH), BF16), jax.ShapeDtypeStruct((32, CONV_CH), F32),
                   jax.ShapeDtypeStruct((8, CONV_CH), F32)],
        scratch=[pltpu.VMEM((ROW_TILE + 2 * HALO, CONV_CH), F32), pltpu.VMEM((ROW_TILE + 2 * HALO, CONV_CH), F32)],
        sem=("arbitrary",), args=(p, p, p, p, p, p, cv, cv, cv, dmix, dmix, dmix, cw, lnw, lnb), comm=comm)


def _tile_positions(i, nxt, seq, ctx, offset, rows):
    is_ctx = i >= nxt
    pos0 = (i - jnp.where(is_ctx, nxt, 0)) * ROW_TILE + offset
    length = jnp.where(is_ctx, ctx, seq).astype(F32)
    pos = (pos0 + lax.broadcasted_iota(jnp.int32, (rows, 1), 0)).astype(F32)
    return pos, length


def _pool_count(pos, length, w):
    left = w // 2
    right = w - 1 - left
    return jnp.minimum(pos + right, length - 1.0) - jnp.maximum(pos - left, 0.0) + 1.0


def _gelu(x):
    return 0.5 * x * (1.0 + lax.erf(x * INV_SQRT2))


def _odd_fwd(p, pw, ps, slw, slb, sw, sbf, nxt, seq, ctx, name, comm=()):
    T = p.shape[0]
    nt = T // ROW_TILE

    def body(p_ref, pp_ref, pn_ref, pw_ref, ps_ref, lw_ref, lb_ref, sw_ref, sb_ref, o_ref, pad_ref):
        i = pl.program_id(0)
        first, last = _seq_edges(i, nxt, nt)
        _fill_pad(pad_ref, pp_ref[...], p_ref[:, 0:POOL_CH], pn_ref[...], first, last)
        pos, length = _tile_positions(i, nxt, seq, ctx, 0, ROW_TILE)
        for gi, w in enumerate(POOL_WINDOWS):
            sl = slice(gi * GROUP_CH, (gi + 1) * GROUP_CH)
            left = w // 2
            ssum = jnp.zeros((ROW_TILE, GROUP_CH), F32)
            for o in range(-left, w - left):
                ssum = ssum + pad_ref[HALO + o:HALO + o + ROW_TILE, sl]
            m = ssum / _pool_count(pos, length, w) - p_ref[:, sl]
            pre = _dot(m.astype(BF16), pw_ref[gi].astype(BF16))
            o_ref[:, sl] = (pre * ps_ref[:, sl]).astype(BF16)
        u = _gelu(p_ref[:, POOL_CH:POOL_CH + SG_CH])
        xh, _ = _ln_stats(_gelu(p_ref[:, POOL_CH + SG_CH:ODD_IN]))
        vln = xh * lw_ref[...] + lb_ref[...]
        for n in range(ROW_TILE // SG_CHUNK):
            rs = slice(n * SG_CHUNK, (n + 1) * SG_CHUNK)
            for gi in range(4):
                sl = slice(gi * GROUP_CH, (gi + 1) * GROUP_CH)
                s = _dot(sw_ref[gi].astype(BF16), vln[rs, sl].astype(BF16)) + sb_ref[:, sl]
                o_ref[rs, POOL_CH + gi * GROUP_CH:POOL_CH + (gi + 1) * GROUP_CH] = (u[rs, sl] * s).astype(BF16)

    pp, pn = _halo_specs(T, POOL_CH, 0)
    return _call(
        body, grid=(nt,), name=name,
        in_specs=[_row(ODD_IN), pp, pn, _const((4, GROUP_CH, GROUP_CH)), _const((1, POOL_CH)), _const((1, SG_CH)),
                  _const((1, SG_CH)), _const((4, SG_CHUNK, SG_CHUNK)), _const((SG_CHUNK, SG_CH))],
        out_specs=[_row(D)], out_shape=[jax.ShapeDtypeStruct((T, D), BF16)],
        scratch=[pltpu.VMEM((ROW_TILE + 2 * HALO, POOL_CH), F32)],
        sem=("parallel",), args=(p, p, p, pw, ps, slw, slb, sw, sbf), comm=comm)


def _odd_bwd(p, dmix, pw, ps, slw, slb, sw, sbf, nxt, seq, ctx, name, comm=()):
    T = p.shape[0]
    nt = T // ROW_TILE

    def body(p_ref, pp_ref, pn_ref, dm_ref, dmp_ref, dmn_ref, pw_ref, ps_ref, lw_ref, lb_ref, sw_ref, sb_ref,
             dp_ref, dpw_ref, dsw_ref, dv_ref, dsb_ref, pad_ref, dpad_ref):
        i = pl.program_id(0)
        first, last = _seq_edges(i, nxt, nt)

        @pl.when(i == 0)
        def _():
            dpw_ref[...] = jnp.zeros_like(dpw_ref)
            dsw_ref[...] = jnp.zeros_like(dsw_ref)
            dv_ref[...] = jnp.zeros_like(dv_ref)
            dsb_ref[...] = jnp.zeros_like(dsb_ref)

        _fill_pad(pad_ref, pp_ref[...], p_ref[:, 0:POOL_CH], pn_ref[...], first, last)
        pos, length = _tile_positions(i, nxt, seq, ctx, 0, ROW_TILE)
        pos_p, _ = _tile_positions(i, nxt, seq, ctx, -HALO, HALO)
        pos_n, _ = _tile_positions(i, nxt, seq, ctx, ROW_TILE, HALO)
        scale = ps_ref[...]
        for gi, w in enumerate(POOL_WINDOWS):
            sl = slice(gi * GROUP_CH, (gi + 1) * GROUP_CH)
            left = w // 2
            right = w - 1 - left
            ssum = jnp.zeros((ROW_TILE, GROUP_CH), F32)
            for o in range(-left, right + 1):
                ssum = ssum + pad_ref[HALO + o:HALO + o + ROW_TILE, sl]
            cnt = _pool_count(pos, length, w)
            m = ssum / cnt - p_ref[:, sl]
            wg = pw_ref[gi].astype(BF16)
            pre = _dot(m.astype(BF16), wg)
            dpo = dm_ref[:, sl]
            dv_ref[0:1, sl] += _colsum(dpo * pre)
            dpre = (dpo * scale[:, sl]).astype(BF16)
            dpw_ref[gi] += _dot_tn(m.astype(BF16), dpre)
            dmc = _dot_nt(dpre, wg)
            halo_dm = lambda ref, ps_: _dot_nt((ref[:, sl] * scale[:, sl]).astype(BF16), wg) / _pool_count(ps_, length, w)
            dpad_ref[0:HALO, sl] = jnp.where(first, 0.0, halo_dm(dmp_ref, pos_p))
            dpad_ref[HALO:HALO + ROW_TILE, sl] = dmc / cnt
            dpad_ref[HALO + ROW_TILE:2 * HALO + ROW_TILE, sl] = jnp.where(last, 0.0, halo_dm(dmn_ref, pos_n))
            atd = jnp.zeros((ROW_TILE, GROUP_CH), F32)
            for o in range(-right, left + 1):
                atd = atd + dpad_ref[HALO + o:HALO + o + ROW_TILE, sl]
            dp_ref[:, sl] = (atd - dmc).astype(BF16)

        pu = p_ref[:, POOL_CH:POOL_CH + SG_CH]
        pv = p_ref[:, POOL_CH + SG_CH:ODD_IN]
        u = _gelu(pu)
        xh, rs_ = _ln_stats(_gelu(pv))
        lw = lw_ref[...]
        vln = xh * lw + lb_ref[...]
        dgelu = lambda x: 0.5 * (1.0 + lax.erf(x * INV_SQRT2)) + x * jnp.exp(-0.5 * x * x) * INV_SQRT_2PI
        for n in range(ROW_TILE // SG_CHUNK):
            rs = slice(n * SG_CHUNK, (n + 1) * SG_CHUNK)
            dvl = []
            for gi in range(4):
                sl = slice(gi * GROUP_CH, (gi + 1) * GROUP_CH)
                wq = sw_ref[gi].astype(BF16)
                vb = vln[rs, sl].astype(BF16)
                s = _dot(wq, vb) + sb_ref[:, sl]
                dsg = dm_ref[rs, POOL_CH + gi * GROUP_CH:POOL_CH + (gi + 1) * GROUP_CH]
                ds = dsg * u[rs, sl]
                dsb_ref[:, sl] += ds
                dsw_ref[gi] += _dot_nt(ds.astype(BF16), vb)
                dvl.append(_dot_tn(wq, ds.astype(BF16)))
                dp_ref[rs, POOL_CH + gi * GROUP_CH:POOL_CH + (gi + 1) * GROUP_CH] = (
                    dsg * s * dgelu(pu[rs, sl])).astype(BF16)
            dvln = jnp.concatenate(dvl, axis=1)
            xhc = xh[rs, :]
            dv_ref[1:2, :] += _colsum(dvln * xhc)
            dv_ref[2:3, :] += _colsum(dvln)
            dxh = dvln * lw
            dvv = rs_[rs, :] * (dxh - _rowmean(dxh) - xhc * _rowmean(dxh * xhc))
            dp_ref[rs, POOL_CH + SG_CH:ODD_IN] = (dvv * dgelu(pv[rs, :])).astype(BF16)

    pp, pn = _halo_specs(T, POOL_CH, 0)
    dmp, dmn = _halo_specs(T, POOL_CH, 0)
    gsd = jax.ShapeDtypeStruct((4, GROUP_CH, GROUP_CH), F32)
    return _call(
        body, grid=(nt,), name=name,
        in_specs=[_row(ODD_IN), pp, pn, _row(D), dmp, dmn, _const((4, GROUP_CH, GROUP_CH)), _const((1, POOL_CH)),
                  _const((1, SG_CH)), _const((1, SG_CH)), _const((4, SG_CHUNK, SG_CHUNK)), _const((SG_CHUNK, SG_CH))],
        out_specs=[_row(ODD_IN), _const((4, GROUP_CH, GROUP_CH)), _const((4, SG_CHUNK, SG_CHUNK)), _const((8, POOL_CH)),
                   _const((SG_CHUNK, SG_CH))],
        out_shape=[jax.ShapeDtypeStruct((T, ODD_IN), BF16), gsd, gsd, jax.ShapeDtypeStruct((8, POOL_CH), F32),
                   jax.ShapeDtypeStruct((SG_CHUNK, SG_CH), F32)],
        scratch=[pltpu.VMEM((ROW_TILE + 2 * HALO, POOL_CH), F32), pltpu.VMEM((ROW_TILE + 2 * HALO, POOL_CH), F32)],
        sem=("arbitrary",), args=(p, p, p, dmix, dmix, dmix, pw, ps, slw, slb, sw, sbf), comm=comm)


def _ada_fwd(cs, aw, ab, name):
    cols = aw.shape[2]

    def body(c_ref, w_ref, b_ref, o_ref):
        c = c_ref[...]
        s = (c * _sigmoid(c)).astype(BF16)
        o_ref[0] = _dot(s, w_ref[0].astype(BF16)) + b_ref[0]

    return pl.pallas_call(
        body, grid=(DEPTH,), name=name,
        in_specs=[pl.BlockSpec((16, D), lambda i: (0, 0)), pl.BlockSpec((1, D, cols), lambda i: (i, 0, 0)),
                  pl.BlockSpec((1, 1, cols), lambda i: (i, 0, 0))],
        out_specs=pl.BlockSpec((1, 16, cols), lambda i: (i, 0, 0)),
        out_shape=jax.ShapeDtypeStruct((DEPTH, 16, cols), F32),
        compiler_params=_params(("parallel",)),
    )(cs, aw, ab)


def _ada_bwd(cs, G, aw, name):
    cols = aw.shape[2]

    def body(c_ref, g_ref, w_ref, gw_ref, cc_ref):
        i = pl.program_id(0)

        @pl.when(i == 0)
        def _():
            cc_ref[...] = jnp.zeros_like(cc_ref)

        c = c_ref[...]
        s = (c * _sigmoid(c)).astype(BF16)
        g = g_ref[0]
        dc = g[8:9, :]
        for d in range(9, 16):
            dc = dc + g[d:d + 1, :]
        row = lax.broadcasted_iota(jnp.int32, (8, cols), 0)
        dcrows = jnp.where(row == 0, dc, 0.0)
        dm = jnp.concatenate([g[0:8, :], dcrows], axis=0).astype(BF16)
        gw_ref[0] = _dot_tn(s, dm)
        cc_ref[...] += _dot_nt(dcrows.astype(BF16), w_ref[0].astype(BF16))

    return pl.pallas_call(
        body, grid=(DEPTH,), name=name,
        in_specs=[pl.BlockSpec((16, D), lambda i: (0, 0)), pl.BlockSpec((1, 16, cols), lambda i: (i, 0, 0)),
                  pl.BlockSpec((1, D, cols), lambda i: (i, 0, 0))],
        out_specs=[pl.BlockSpec((1, D, cols), lambda i: (i, 0, 0)), pl.BlockSpec((8, D), lambda i: (0, 0))],
        out_shape=[jax.ShapeDtypeStruct((DEPTH, D, cols), F32), jax.ShapeDtypeStruct((8, D), F32)],
        compiler_params=_params(("arbitrary",)),
    )(cs, G, aw)


def _row_tile_for(rows):
    for t in (512, 256, 128, 64, 32, 16, 8):
        if rows % t == 0:
            return t
    raise ValueError(f"rows={rows} is not a multiple of 8")


def _sum_devices(x, name):
    _, rows, cols = x.shape
    tr = _row_tile_for(rows)

    def body(x_ref, o_ref):
        acc = x_ref[0]
        for d in range(1, N_DEV):
            acc = acc + x_ref[d]
        o_ref[...] = acc

    return pl.pallas_call(
        body, grid=(rows // tr,), name=name,
        in_specs=[pl.BlockSpec((N_DEV, tr, cols), lambda i: (0, i, 0))],
        out_specs=pl.BlockSpec((tr, cols), lambda i: (i, 0)),
        out_shape=jax.ShapeDtypeStruct((rows, cols), F32),
        compiler_params=_params(("parallel",)),
    )(x)


def _adamw(w, g, m, v, name):
    rows, cols = w.shape
    tr = _row_tile_for(rows)

    def body(w_ref, g_ref, m_ref, v_ref, go_ref, d_ref, mo_ref, vo_ref):
        gv = g_ref[...]
        mn = ADAM_B1 * m_ref[...] + (1.0 - ADAM_B1) * gv
        vn = ADAM_B2 * v_ref[...] + (1.0 - ADAM_B2) * (gv * gv)
        go_ref[...] = gv
        mo_ref[...] = mn
        vo_ref[...] = vn
        d_ref[...] = -ADAM_LR * ((mn / ADAM_BC1) / (jnp.sqrt(vn / ADAM_BC2) + ADAM_EPS) + ADAM_WD * w_ref[...])

    blk = pl.BlockSpec((tr, cols), lambda i: (i, 0))
    sd = jax.ShapeDtypeStruct((rows, cols), F32)
    return pl.pallas_call(
        body, grid=(rows // tr,), name=name,
        in_specs=[blk, blk, blk, blk], out_specs=[blk, blk, blk, blk], out_shape=[sd, sd, sd, sd],
        compiler_params=_params(("parallel",)),
    )(w, g, m, v)


def _adamw_layers(w, pieces, m, v, name):
    L, a, b = w.shape
    bp = pieces[0].shape[2]
    tr = next(t for t in (256, 128, 64, 32, 16) if a % t == 0)

    def body(w_ref, m_ref, v_ref, *rest):
        p_refs, (go_ref, d_ref, mo_ref, vo_ref) = rest[:L], rest[L:]
        layer = pl.program_id(0)
        for k in range(L):
            @pl.when(layer == k)
            def _(k=k):
                gv = p_refs[k][0].astype(F32)
                for d in range(1, N_DEV):
                    gv = gv + p_refs[k][d].astype(F32)
                gv = gv[:, :b]
                mn = ADAM_B1 * m_ref[0] + (1.0 - ADAM_B1) * gv
                vn = ADAM_B2 * v_ref[0] + (1.0 - ADAM_B2) * (gv * gv)
                go_ref[0] = gv
                mo_ref[0] = mn
                vo_ref[0] = vn
                d_ref[0] = -ADAM_LR * ((mn / ADAM_BC1) / (jnp.sqrt(vn / ADAM_BC2) + ADAM_EPS) + ADAM_WD * w_ref[0])

    blk = pl.BlockSpec((1, tr, b), lambda l, i: (l, i, 0))
    pspecs = [pl.BlockSpec((N_DEV, tr, bp), lambda l, i, k=k: (0, jnp.where(l == k, i, 0), 0)) for k in range(L)]
    sd = jax.ShapeDtypeStruct((L, a, b), F32)
    return pl.pallas_call(
        body, grid=(L, a // tr), name=name,
        in_specs=[blk, blk, blk] + pspecs, out_specs=[blk, blk, blk, blk], out_shape=[sd, sd, sd, sd],
        compiler_params=_params(("arbitrary", "arbitrary")),
    )(w, m, v, *pieces)


def _pack_rows(shape):
    return -(-math.prod(shape) // (8 * LANES)) * 8


def _pack(arrs, row_mult):
    parts = []
    for a in arrs:
        n, rows = math.prod(a.shape), _pack_rows(a.shape)
        parts.append(jnp.pad(a.reshape(-1).astype(F32), (0, rows * LANES - n)).reshape(rows, LANES))
    total = sum(p.shape[0] for p in parts)
    if total % row_mult:
        parts.append(jnp.zeros((-total % row_mult, LANES), F32))
    return jnp.concatenate(parts, axis=0)


def _unpack(packed, shapes):
    out, r0 = [], 0
    lead = packed.shape[:-2]
    for s in shapes:
        n, rows = math.prod(s), _pack_rows(s)
        piece = packed[..., r0:r0 + rows, :].reshape(lead + (rows * LANES,))
        out.append(piece[..., :n].reshape(lead + tuple(s)))
        r0 += rows
    return out


def _rope_tables(seq, ctx):
    def angles(ps):
        parts = []
        for pvec, n in zip(ps, ROPE_PAIRS):
            freq = ROPE_BASE ** (-jnp.arange(n, dtype=F32) / n)
            parts.append(pvec[:, None] * freq[None, :])
        return jnp.concatenate(parts, axis=-1)

    rows = seq // GRID_W
    grid_r = jnp.broadcast_to(jnp.arange(rows, dtype=F32)[:, None], (rows, GRID_W)).reshape(-1)
    grid_c = jnp.broadcast_to(jnp.arange(GRID_W, dtype=F32)[None, :], (rows, GRID_W)).reshape(-1)
    zc = jnp.zeros((ctx,), F32)
    ang = jnp.concatenate([angles((jnp.full((seq,), ctx, F32), grid_r, grid_c)),
                           angles((jnp.arange(ctx, dtype=F32), zc, zc))], axis=0)
    cos, sin = jnp.cos(ang), jnp.sin(ang)
    return jnp.concatenate([cos, cos], axis=1), jnp.concatenate([-sin, sin], axis=1)


def _layer_gather(sh, even):
    return [("ag_cols" if even else "ag_rows", sh["in"]), ("ag_rows", sh["out"]), ("ag_rows", sh["gate"]),
            ("ag_rows", sh["up"]), ("ag_rows", sh["down"])]


def _sample_step(X0, tgt, mods, shards, gather0, S, seq, ctx):
    nxt = seq // ROW_TILE
    cosf, sgn = _rope_tables(seq, ctx)
    X = X0
    saved = []
    weights = _xfer_wait(gather0, mods, "gather_w0_wait")
    for i in range(DEPTH):
        j, even = i // 2, i % 2 == 0
        md = mods[i]
        t = f"l{i}_"
        w_in, w_out, wg, wu, wd = weights
        last = i == DEPTH - 1
        if not last:
            sh = dict(shards[i + 1])
            sh["out"] = sh["out"] + (w_out[0, 0] * 0).astype(BF16)
            gather, token = _xfer_start(_layer_gather(sh, not even), f"gather_w{i + 1}_start")
            md = md + token[0, 0]
        h1 = _norm_mod_fwd(X, S["norm_w"][i, 0][None], md, 0, nxt, t + "norm1")
        if even:
            (p,), _ = _mm_nn(h1, w_in, F32, t + "proj_in")
            (yret, ro), _ = _ret_fwd(p, S["lg"][j], cosf, sgn, seq, t + "ret_fwd")
            (cv, co), _ = _conv_fwd(p, S["conv_w"][j], S["conv_ln_w"][j][None], S["conv_ln_b"][j][None], nxt,
                                    t + "conv_fwd")
            mix = jnp.concatenate([ro, co], axis=1)
            extra = (yret, cv)
        else:
            (p,), _ = _mm_nt(h1, w_in, t + "proj_in")
            sbf = jnp.repeat(S["sg_b"][j].T, GROUP_CH, axis=1)
            (mix,), _ = _odd_fwd(p, S["pool_w"][j], S["pool_scale"][j][None], S["sg_ln_w"][j][None],
                                 S["sg_ln_b"][j][None], S["sg_w"][j], sbf, nxt, seq, ctx, t + "odd_fwd")
            extra = (sbf,)
        (X1, y1), _ = _mm_nn_resid(mix, w_out, X, md, 2, nxt, t + "proj_out")
        h2 = _norm_mod_fwd(X1, S["norm_w"][i, 1][None], md, 3, nxt, t + "norm2")
        (gp, up, act), _ = _ffn_up(h2, wg, wu, t + "ffn_up")
        (X2, y2), _ = _mm_nn_resid(act, wd, X1, md, 5, nxt, t + "ffn_down")
        saved.append((X, h1, p, mix, extra, y1, X1, h2, gp, up, act, y2, weights))
        X = X2
        if not last:
            weights = _xfer_wait(gather, X2, f"gather_w{i + 1}_wait")

    dX, hst = _loss_head(X, tgt, S["final_norm_w"][None], nxt, "loss_head")
    loss = jnp.sum(hst[1])
    pieces = {}
    gS = {"final_norm_w": hst[0], "norm_w": [None] * DEPTH, "lg": [None] * 2, "conv_w": [None] * 2,
          "conv_ln_w": [None] * 2, "conv_ln_b": [None] * 2, "pool_w": [None] * 2, "pool_scale": [None] * 2,
          "sg_ln_w": [None] * 2, "sg_ln_b": [None] * 2, "sg_w": [None] * 2, "sg_b": [None] * 2}
    dmods = [None] * DEPTH
    in_flight = []

    def land(flights, after, tag):
        for handle, keys, name in flights:
            for key, got in zip(keys, _xfer_wait(handle, after, name + tag)):
                pieces[key] = got

    for i in reversed(range(DEPTH)):
        j, even = i // 2, i % 2 == 0
        md = mods[i]
        t = f"l{i}_"
        X_in, h1, p, mix, extra, y1, X1, h2, gp, up, act, y2, (w_in, w_out, wg, wu, wd) = saved[i]
        dyf, g2 = _resid_bwd(dX, y2, md, 5, nxt, t + "ffn_resid_bwd")
        (dgp, dup), _ = _ffn_dact(dyf, wd, gp, up, t + "ffn_dact")
        g_down = _mm_tn(act, dyf, t + "dw_down")
        g_gate = _mm_tn(dgp, h2, t + "dw_gate")
        g_up = _mm_tn(dup, h2, t + "dw_up")
        ffn_flight, token = _xfer_start([("a2a_rows", g_down), ("a2a_rows", g_gate), ("a2a_rows", g_up)],
                                        t + "scatter_ffn_start")
        (dh2,), _ = _mm_nn(dgp, wg, F32, t + "dh2_gate")
        (dh2,), _ = _mm_nn(dup, wu, F32, t + "dh2_up", add=dh2)
        dX1, s2 = _norm_mod_bwd(dh2, X1, S["norm_w"][i, 1][None], md + token[0, 0], dX, 3, nxt, t + "norm2_bwd")
        dym, g1 = _resid_bwd(dX1, y1, md, 2, nxt, t + "mix_resid_bwd")
        (dmix,), _ = _mm_nt(dym, w_out, t + "dmix")
        g_out = _mm_tn(mix, dym, t + "dw_out")
        if even:
            yret, cv = extra
            dyr, dg = _ret_gate_bwd(yret, p, dmix, t + "ret_gate_bwd")
            (dq, dk, dv, dlg), _ = _ret_bwd(p, S["lg"][j], dyr, cosf, sgn, seq, t + "ret_bwd")
            early = [("ag_blk", _pack([jnp.stack(gS["pool_w"]), jnp.stack(gS["sg_w"])], 8))] if i == 0 else []
            (dpc, dcw, dln), got = _conv_bwd(
                p, cv, dmix, S["conv_w"][j], S["conv_ln_w"][j][None], S["conv_ln_b"][j][None], nxt,
                t + "conv_bwd", comm=early)
            if i == 0:
                early_all = got[0]
            dp = jnp.concatenate([dq, dk, dv, dg, dpc], axis=1)
            gS["lg"][j] = dlg[:, 0:2, 0].T
            gS["conv_w"][j], gS["conv_ln_w"][j], gS["conv_ln_b"][j] = dcw, dln[0], dln[1]
            g_in = _mm_tn(h1, dp, t + "dw_in")
            mix_flight, token = _xfer_start([("a2a_rows", g_out), ("a2a_cols", g_in)], t + "scatter_mix_start")
            (dh1,), _ = _mm_nt(dp, w_in, t + "dh1")
        else:
            (sbf,) = extra
            (dp, dpw, dsw, dvec, dsb), _ = _odd_bwd(
                p, dmix, S["pool_w"][j], S["pool_scale"][j][None], S["sg_ln_w"][j][None], S["sg_ln_b"][j][None],
                S["sg_w"][j], sbf, nxt, seq, ctx, t + "odd_bwd")
            gS["pool_w"][j], gS["sg_w"][j] = dpw, dsw
            gS["pool_scale"][j], gS["sg_ln_w"][j], gS["sg_ln_b"][j] = dvec[0], dvec[1], dvec[2]
            gS["sg_b"][j] = jnp.sum(dsb.reshape(SG_CHUNK, 4, GROUP_CH), axis=2).T
            g_in = _mm_tn(dp, h1, t + "dw_in")
            mix_flight, token = _xfer_start([("a2a_rows", g_out), ("a2a_rows", g_in)], t + "scatter_mix_start")
            (dh1,), _ = _mm_nn(dp, w_in, F32, t + "dh1")
        dX, s1 = _norm_mod_bwd(dh1, X_in, S["norm_w"][i, 0][None], md + token[0, 0], dX1, 0, nxt, t + "norm1_bwd")
        gS["norm_w"][i] = jnp.stack([s1[0, 2] + s1[1, 2], s2[0, 2] + s2[1, 2]])
        dmods[i] = jnp.stack([s1[:, 0], s1[:, 1], g1[:, 0], s2[:, 0], s2[:, 1], g2[:, 0]], axis=1)
        land(in_flight, dX, "_wait")
        in_flight = [(ffn_flight, [("down", i), ("gate", i), ("up", i)], t + "scatter_ffn"),
                     (mix_flight, [("out", i), ("in", i)], t + "scatter_mix")]
    land(in_flight, dX, "_wait")
    gS = {k: (jnp.stack(v) if isinstance(v, list) else v) for k, v in gS.items()}
    return loss, dX, pieces, gS, jnp.stack(dmods), early_all


FF_SHARD = D_FF // N_DEV
FF_SHARD_PAD = 384


def kernel(x, c, ctx, c_ctx, ada_w, ada_b, norm_w, even_w_in, even_w_out, ret_decay_logit, conv_dw_w, conv_ln_w, conv_ln_b, odd_w_in, odd_w_out, pool_w, pool_scale, sg_ln_w, sg_ln_b, sg_w, sg_b, ffn_w_gate, ffn_w_up, ffn_w_down, final_norm_w, loss_target, m_c_ctx, m_ada_w, m_ada_b, m_norm_w, m_even_w_in, m_even_w_out, m_ret_decay_logit, m_conv_dw_w, m_conv_ln_w, m_conv_ln_b, m_odd_w_in, m_odd_w_out, m_pool_w, m_pool_scale, m_sg_ln_w, m_sg_ln_b, m_sg_w, m_sg_b, m_ffn_w_gate, m_ffn_w_up, m_ffn_w_down, m_final_norm_w, v_c_ctx, v_ada_w, v_ada_b, v_norm_w, v_even_w_in, v_even_w_out, v_ret_decay_logit, v_conv_dw_w, v_conv_ln_w, v_conv_ln_b, v_odd_w_in, v_odd_w_out, v_pool_w, v_pool_scale, v_sg_ln_w, v_sg_ln_b, v_sg_w, v_sg_b, v_ffn_w_gate, v_ffn_w_up, v_ffn_w_down, v_final_norm_w):
    seq, n_ctx = x.shape[1], ctx.shape[1]
    me = 4 * lax.axis_index("x") + 2 * lax.axis_index("y") + lax.axis_index("c")
    mcols = ada_w.shape[2]

    tr = lambda a: jnp.swapaxes(a, 1, 2)
    pad_rows = lambda a: jnp.pad(a, ((0, FF_SHARD_PAD - FF_SHARD), (0, 0))).astype(BF16)
    gate_t, up_t, odd_in_t = tr(ffn_w_gate), tr(ffn_w_up), tr(odd_w_in)
    shards = []
    for i in range(DEPTH):
        j, even = i // 2, i % 2 == 0
        w_in, w_out = (even_w_in[j], even_w_out[j]) if even else (odd_in_t[j], odd_w_out[j])
        shards.append({"in": w_in.astype(BF16), "out": w_out.astype(BF16), "gate": pad_rows(gate_t[i]),
                       "up": pad_rows(up_t[i]), "down": pad_rows(ffn_w_down[i])})
    gather0, token = _xfer_start(_layer_gather(shards[0], True), "gather_w0_start")

    small_shapes = [(D,), norm_w.shape, conv_dw_w.shape, pool_scale.shape, sg_ln_w.shape, sg_ln_b.shape]
    (sm,) = _exchange([("ag_blk", _pack([c + token[0, 0], norm_w, conv_dw_w, pool_scale, sg_ln_w, sg_ln_b], 8))],
                      "gather_small")
    c_all, nw_s, cw_s, ps_s, slw_s, slb_s = _unpack(sm, small_shapes)
    cat_last = lambda a: jnp.moveaxis(a, 0, -2).reshape(a.shape[1:-1] + (-1,))
    conv_w_full = cat_last(cw_s)
    S = {"norm_w": cat_last(nw_s), "lg": jax.nn.log_sigmoid(ret_decay_logit),
         "conv_w": jnp.pad(conv_w_full, ((0, 0), (0, 32 - CONV_K), (0, 0))),
         "conv_ln_w": conv_ln_w, "conv_ln_b": conv_ln_b, "pool_w": pool_w, "pool_scale": cat_last(ps_s),
         "sg_ln_w": cat_last(slw_s), "sg_ln_b": cat_last(slb_s), "sg_w": sg_w, "sg_b": sg_b,
         "final_norm_w": final_norm_w}

    cs = jnp.concatenate([c_all, c_ctx[None], jnp.zeros((7, D), F32)], axis=0)
    ab_loc = lax.dynamic_slice_in_dim(ada_b, me * mcols, mcols, axis=1)
    mod_loc = _ada_fwd(cs, ada_w, ab_loc[:, None, :], "ada_fwd")
    (mod_all,) = _exchange([("ag_blk", mod_loc.reshape(DEPTH * 16, mcols))], "gather_mod")
    mod_all = mod_all.reshape(N_DEV, DEPTH, 16, mcols).transpose(1, 2, 0, 3).reshape(DEPTH, 16, 6, D)
    mod_x = lax.dynamic_index_in_dim(mod_all, me, axis=1, keepdims=False)
    mods = jnp.stack([mod_x, mod_all[:, 8]], axis=1)

    X0 = jnp.concatenate([x[0], ctx[0]], axis=0)
    loss, dX, big, gS, dmods, early_all = _sample_step(X0, loss_target[0], mods, shards, gather0, S, seq, n_ctx)
    loss = lax.psum(loss, ("x", "y", "c"))
    grad_x = dX[:seq][None]

    (dm_all,) = _exchange([("ag_blk", dmods.reshape(DEPTH * 2, 6 * D))], "gather_dmod")
    dm_all = dm_all.reshape(N_DEV, DEPTH, 2, 6 * D)
    dm_sum = _sum_devices(dm_all.reshape(N_DEV, DEPTH * 2, 6 * D), "sum_dmod").reshape(DEPTH, 2, 6 * D)
    g_ada_b = dm_sum[:, 0] + dm_sum[:, 1]
    G = lax.dynamic_slice_in_dim(dm_all, me * mcols, mcols, axis=3).transpose(1, 2, 0, 3).reshape(DEPTH, 16, mcols)
    g_ada_w, ccp = _ada_bwd(cs, G, ada_w, "ada_bwd")
    sg_cc = _sigmoid(c_ctx)
    g_cctx_part = ccp[0] * sg_cc * (1.0 + c_ctx * (1.0 - sg_cc))

    dsig = _sigmoid(-ret_decay_logit)
    part = [g_cctx_part, gS["norm_w"], gS["lg"] * dsig, gS["conv_w"][:, :CONV_K], gS["conv_ln_w"], gS["conv_ln_b"],
            gS["pool_scale"], gS["sg_ln_w"], gS["sg_ln_b"], gS["sg_b"], gS["final_norm_w"]]
    part_shapes = [a.shape for a in part]
    (part_all,) = _exchange([("ag_blk", _pack(part, 8))], "gather_small_grads")
    red = _sum_devices(part_all, "sum_small_grads")
    g_cctx, g_nw, g_rdl, g_cw, g_clw, g_clb, g_ps, g_slw, g_slb, g_sb, g_fnw = _unpack(red, part_shapes)
    g_pw, g_sw = _unpack(_sum_devices(early_all, "sum_early_grads"), [pool_w.shape, sg_w.shape])
    mine_last = lambda a, n: lax.dynamic_slice_in_dim(a, me * n, n, axis=a.ndim - 1)
    g_nw, g_cw = mine_last(g_nw, norm_w.shape[2]), mine_last(g_cw, conv_dw_w.shape[2])
    g_ps, g_slw, g_slb = (mine_last(a, pool_scale.shape[1]) for a in (g_ps, g_slw, g_slb))

    def adam_big(key, layers, w, m, v, name):
        return _adamw_layers(w, [big[(key, l)] for l in layers], m, v, name)

    res = {}
    res["even_w_in"] = adam_big("in", (0, 2), even_w_in, m_even_w_in, v_even_w_in, "adam_even_in")
    res["even_w_out"] = adam_big("out", (0, 2), even_w_out, m_even_w_out, v_even_w_out, "adam_even_out")
    adam_t = lambda key, layers, w, m, v, name: [tr(o) for o in adam_big(key, layers, tr(w), tr(m), tr(v), name)]
    res["odd_w_in"] = adam_t("in", (1, 3), odd_w_in, m_odd_w_in, v_odd_w_in, "adam_odd_in")
    res["odd_w_out"] = adam_big("out", (1, 3), odd_w_out, m_odd_w_out, v_odd_w_out, "adam_odd_out")
    res["ffn_w_gate"] = adam_t("gate", (0, 1, 2, 3), ffn_w_gate, m_ffn_w_gate, v_ffn_w_gate, "adam_gate")
    res["ffn_w_up"] = adam_t("up", (0, 1, 2, 3), ffn_w_up, m_ffn_w_up, v_ffn_w_up, "adam_up")
    res["ffn_w_down"] = adam_big("down", (0, 1, 2, 3), ffn_w_down, m_ffn_w_down, v_ffn_w_down, "adam_down")
    flat2 = lambda a: a.reshape(-1, a.shape[-1])
    res["ada_w"] = [o.reshape(ada_w.shape) for o in
                    _adamw(flat2(ada_w), flat2(g_ada_w), flat2(m_ada_w), flat2(v_ada_w), "adam_ada_w")]

    names = ["c_ctx", "ada_b", "norm_w", "ret_decay_logit", "conv_dw_w", "conv_ln_w", "conv_ln_b", "pool_w",
             "pool_scale", "sg_ln_w", "sg_ln_b", "sg_w", "sg_b", "final_norm_w"]
    ws = [c_ctx, ada_b, norm_w, ret_decay_logit, conv_dw_w, conv_ln_w, conv_ln_b, pool_w, pool_scale, sg_ln_w,
          sg_ln_b, sg_w, sg_b, final_norm_w]
    gs = [g_cctx, g_ada_b, g_nw, g_rdl, g_cw, g_clw, g_clb, g_pw, g_ps, g_slw, g_slb, g_sw, g_sb, g_fnw]
    ms = [m_c_ctx, m_ada_b, m_norm_w, m_ret_decay_logit, m_conv_dw_w, m_conv_ln_w, m_conv_ln_b, m_pool_w,
          m_pool_scale, m_sg_ln_w, m_sg_ln_b, m_sg_w, m_sg_b, m_final_norm_w]
    vs = [v_c_ctx, v_ada_b, v_norm_w, v_ret_decay_logit, v_conv_dw_w, v_conv_ln_w, v_conv_ln_b, v_pool_w,
          v_pool_scale, v_sg_ln_w, v_sg_ln_b, v_sg_w, v_sg_b, v_final_norm_w]
    shapes = [a.shape for a in ws]
    gs = [g.reshape(s) for g, s in zip(gs, shapes)]
    pk = lambda arrs: _pack(arrs, 512)
    outs = _adamw(pk(ws), pk(gs), pk(ms), pk(vs), "adam_small")
    for k, o in enumerate(outs):
        for nm, arr in zip(names, _unpack(o, shapes)):
            res.setdefault(nm, [None] * 4)[k] = arr

    order = ["c_ctx", "ada_w", "ada_b", "norm_w", "even_w_in", "even_w_out", "ret_decay_logit", "conv_dw_w",
             "conv_ln_w", "conv_ln_b", "odd_w_in", "odd_w_out", "pool_w", "pool_scale", "sg_ln_w", "sg_ln_b",
             "sg_w", "sg_b", "ffn_w_gate", "ffn_w_up", "ffn_w_down", "final_norm_w"]
    return (loss, grad_x, *[res[n][0] for n in order], *[res[n][1] for n in order],
            *[res[n][2] for n in order], *[res[n][3] for n in order])
```

```python
import math

import jax
import jax.numpy as jnp
from jax import lax
from jax.experimental import pallas as pl
from jax.experimental.pallas import tpu as pltpu

F32 = jnp.float32
BF16 = jnp.bfloat16

N_DEV = 8
D = 1024
DEPTH = 4
ROW_TILE = 256
HALO = 16
LANES = 128
EPS = 1e-6
RET_HEADS = 4
HEAD_DIM = 128
RET_W = RET_HEADS * HEAD_DIM
CHUNK = 128
Q_SCALE = HEAD_DIM ** -0.5
ROPE_BASE = 10000.0
ROPE_PAIRS = (HEAD_DIM // 8, 3 * HEAD_DIM // 16, 3 * HEAD_DIM // 16)
GRID_W = 64
CONV_CH = 512
CONV_K = 31
EVEN_IN = 4 * RET_W + 2 * CONV_CH
POOL_CH = 512
POOL_WINDOWS = (2, 4, 8, 16)
GROUP_CH = 128
SG_CH = 512
SG_CHUNK = 128
ODD_IN = POOL_CH + 2 * SG_CH
D_FF = 2816
INV_SQRT2 = 1.0 / math.sqrt(2.0)
INV_SQRT_2PI = 1.0 / math.sqrt(2.0 * math.pi)
ADAM_LR, ADAM_B1, ADAM_B2, ADAM_EPS, ADAM_WD, ADAM_STEP = 0.001, 0.9, 0.999, 1e-08, 0.01, 10
ADAM_BC1 = 1.0 - ADAM_B1 ** ADAM_STEP
ADAM_BC2 = 1.0 - ADAM_B2 ** ADAM_STEP
VMEM_LIMIT = 56 * 1024 * 1024
MESH = pl.DeviceIdType.MESH


def _params(sem=None):
    return pltpu.CompilerParams(dimension_semantics=sem, vmem_limit_bytes=VMEM_LIMIT)


def _dot(a, b):
    return jnp.dot(a, b, preferred_element_type=F32)


def _dot_nt(a, b):
    return lax.dot_general(a, b, (((1,), (1,)), ((), ())), preferred_element_type=F32)


def _dot_tn(a, b):
    return lax.dot_general(a, b, (((0,), (0,)), ((), ())), preferred_element_type=F32)


def _sigmoid(x):
    return 1.0 / (1.0 + jnp.exp(-x))


def _colsum(x):
    return jnp.sum(x, axis=0, keepdims=True)


def _rowmean(x):
    return jnp.mean(x, axis=-1, keepdims=True)


def _row(shape_cols):
    return pl.BlockSpec((ROW_TILE, shape_cols), lambda i: (i, 0))


def _const(shape):
    nd = len(shape)
    return pl.BlockSpec(shape, lambda i: (0,) * nd)


def _mod_spec(nxt):
    return pl.BlockSpec((1, 6, D), lambda i: (i // nxt, 0, 0))


def _stat_spec(nxt):
    return pl.BlockSpec((1, 8, D), lambda i: (i // nxt, 0, 0))


def _xfer_out_shape(kind, a):
    r, c = a.shape
    shape = {"ag_blk": (N_DEV, r, c), "ag_cols": (r, N_DEV * c), "ag_rows": (N_DEV * r, c),
             "a2a_cols": (N_DEV, r, c // N_DEV), "a2a_rows": (N_DEV, r // N_DEV, c)}[kind]
    return jax.ShapeDtypeStruct(shape, a.dtype)


def _src_view(kind, ref, who):
    if kind == "a2a_cols":
        n = ref.shape[1] // N_DEV
        return ref.at[:, pl.ds(pl.multiple_of(who * n, LANES), n)]
    if kind == "a2a_rows":
        r = ref.shape[0] // N_DEV
        return ref.at[pl.ds(pl.multiple_of(who * r, 16), r), :]
    return ref


def _dst_view(kind, ref, who):
    if kind == "ag_cols":
        n = ref.shape[1] // N_DEV
        return ref.at[:, pl.ds(pl.multiple_of(who * n, LANES), n)]
    if kind == "ag_rows":
        r = ref.shape[0] // N_DEV
        return ref.at[pl.ds(pl.multiple_of(who * r, 16), r), :]
    return ref.at[who]


def _xfers(kinds, srcs, dsts, send_sems, recv_sems, local_sems, start):
    mx, my, mc = lax.axis_index("x"), lax.axis_index("y"), lax.axis_index("c")
    me = 4 * mx + 2 * my + mc
    peers = []
    for k in range(1, N_DEV):
        px, py, pc = mx ^ ((k >> 2) & 1), my ^ ((k >> 1) & 1), mc ^ (k & 1)
        peers.append(((px, py, pc), 4 * px + 2 * py + pc))
    for j, (kind, src, dst) in enumerate(zip(kinds, srcs, dsts)):
        own = pltpu.make_async_copy(_src_view(kind, src, me), _dst_view(kind, dst, me), local_sems.at[j])
        if start:
            own.start()
        for k, (pid, pidx) in enumerate(peers):
            sem = (N_DEV - 1) * j + k
            cp = pltpu.make_async_remote_copy(
                src_ref=_src_view(kind, src, pidx), dst_ref=_dst_view(kind, dst, me if start else pidx),
                send_sem=send_sems.at[sem], recv_sem=recv_sems.at[sem], device_id=pid, device_id_type=MESH)
            if start:
                cp.start()
            else:
                cp.wait_recv()
                cp.wait_send()
        if not start:
            own.wait()


def _xfer_sems(n):
    return [pltpu.SemaphoreType.DMA(((N_DEV - 1) * n,)), pltpu.SemaphoreType.DMA(((N_DEV - 1) * n,)),
            pltpu.SemaphoreType.DMA((n,))]


def _exchange(comm, name):
    kinds = [k for k, _ in comm]
    n = len(comm)

    def body(*refs):
        sems = refs[2 * n:]
        _xfers(kinds, refs[:n], refs[n:2 * n], *sems, start=True)
        _xfers(kinds, refs[:n], refs[n:2 * n], *sems, start=False)

    hbm = pl.BlockSpec(memory_space=pl.ANY)
    return pl.pallas_call(
        body, name=name, in_specs=[hbm] * n, out_specs=[hbm] * n,
        out_shape=[_xfer_out_shape(k, a) for k, a in comm], scratch_shapes=_xfer_sems(n),
    )(*[a for _, a in comm])


_HBM = pl.BlockSpec(memory_space=pltpu.HBM)
_SEM = pl.BlockSpec(memory_space=pltpu.SEMAPHORE)
_EFFECT = pltpu.SideEffectType.DATAFLOW_SIDE_EFFECTING


def _mesh_peers():
    mx, my, mc = lax.axis_index("x"), lax.axis_index("y"), lax.axis_index("c")
    peers = []
    for k in range(1, N_DEV):
        px, py, pc = mx ^ ((k >> 2) & 1), my ^ ((k >> 1) & 1), mc ^ (k & 1)
        peers.append(((px, py, pc), 4 * px + 2 * py + pc))
    return 4 * mx + 2 * my + mc, peers


def _xfer_start(comm, name):
    kinds = [k for k, _ in comm]
    n = len(comm)
    srcs = [pltpu.with_memory_space_constraint(a, pltpu.HBM) for _, a in comm]
    lands = []
    for k, a in comm:
        sd = _xfer_out_shape(k, a)
        lands.append(pltpu.with_memory_space_constraint(lax.empty(sd.shape, sd.dtype), pltpu.HBM))

    def body(*refs):
        src_refs, land_refs = refs[:n], refs[n:2 * n]
        send_sems, recv_sems, local_sems, token = refs[2 * n], refs[2 * n + 1], refs[2 * n + 2], refs[-1]
        my, peers = _mesh_peers()
        for j, kind in enumerate(kinds):
            pltpu.make_async_copy(_src_view(kind, src_refs[j], my), _dst_view(kind, land_refs[j], my),
                                  local_sems.at[j]).start()
            for k, (pid, pidx) in enumerate(peers):
                sem = (N_DEV - 1) * j + k
                pltpu.make_async_remote_copy(
                    src_ref=_src_view(kind, src_refs[j], pidx), dst_ref=_dst_view(kind, land_refs[j], my),
                    send_sem=send_sems.at[sem], recv_sem=recv_sems.at[sem], device_id=pid,
                    device_id_type=MESH).start()
        token[...] = jnp.zeros_like(token)

    sems = pltpu.SemaphoreType.DMA(((N_DEV - 1) * n,))
    outs = pl.pallas_call(
        body, name=name,
        out_shape=(sems, sems, pltpu.SemaphoreType.DMA((n,)), *[pltpu.HBM(a.shape, a.dtype) for a in srcs + lands],
                   jax.ShapeDtypeStruct((8, LANES), F32)),
        in_specs=[_HBM] * (2 * n),
        out_specs=(_SEM, _SEM, _SEM, *[_HBM] * (2 * n), pl.BlockSpec(memory_space=pltpu.VMEM)),
        input_output_aliases={k: 3 + k for k in range(2 * n)},
        compiler_params=pltpu.CompilerParams(has_side_effects=_EFFECT),
    )(*srcs, *lands)
    return (kinds, outs[0], outs[1], outs[2], list(outs[3:3 + 2 * n])), outs[-1]


def _xfer_wait(handle, after, name):
    kinds, send_sems, recv_sems, local_sems, bufs = handle
    n = len(kinds)

    def body(*refs):
        src_refs, land_refs = refs[:n], refs[n:2 * n]
        send_ref, recv_ref, local_ref = refs[2 * n], refs[2 * n + 1], refs[2 * n + 2]
        my, peers = _mesh_peers()
        for j, kind in enumerate(kinds):
            pltpu.make_async_copy(_src_view(kind, src_refs[j], my), _dst_view(kind, land_refs[j], my),
                                  local_ref.at[j]).wait()
            for k, (pid, pidx) in enumerate(peers):
                sem = (N_DEV - 1) * j + k
                cp = pltpu.make_async_remote_copy(
                    src_ref=_src_view(kind, src_refs[j], pidx), dst_ref=_dst_view(kind, land_refs[j], pidx),
                    send_sem=send_ref.at[sem], recv_sem=recv_ref.at[sem], device_id=pid, device_id_type=MESH)
                cp.wait_send()
                cp.wait_recv()

    outs = pl.pallas_call(
        body, name=name, out_shape=[pltpu.HBM(a.shape, a.dtype) for a in bufs],
        in_specs=[_HBM] * (2 * n) + [_SEM, _SEM, _SEM, pl.BlockSpec(memory_space=pl.ANY)],
        out_specs=[_HBM] * (2 * n), input_output_aliases={k: k for k in range(2 * n)},
        compiler_params=pltpu.CompilerParams(has_side_effects=_EFFECT),
    )(*bufs, send_sems, recv_sems, local_sems, after)
    return list(outs[n:])


def _call(body, *, grid, in_specs, out_specs, out_shape, args, name, sem, scratch=(), comm=()):
    n_in, n_out, n_scr, n_c = len(in_specs), len(out_specs), len(scratch), len(comm)
    if not comm:
        outs = pl.pallas_call(body, grid=grid, in_specs=list(in_specs), out_specs=list(out_specs),
                              out_shape=list(out_shape), scratch_shapes=list(scratch), name=name,
                              compiler_params=_params(sem))(*args)
        return list(outs), []
    kinds = [k for k, _ in comm]
    n_steps = grid[0]

    def wrapped(*refs):
        ins, csrc = refs[:n_in], refs[n_in:n_in + n_c]
        o0 = n_in + n_c
        outs, cdst = refs[o0:o0 + n_out], refs[o0 + n_out:o0 + n_out + n_c]
        s0 = o0 + n_out + n_c
        scr, sems = refs[s0:s0 + n_scr], refs[s0 + n_scr:]
        i = pl.program_id(0)

        @pl.when(i == 0)
        def _():
            _xfers(kinds, csrc, cdst, *sems, start=True)

        body(*ins, *outs, *scr)

        @pl.when(i == n_steps - 1)
        def _():
            _xfers(kinds, csrc, cdst, *sems, start=False)

    hbm = pl.BlockSpec(memory_space=pl.ANY)
    outs = pl.pallas_call(
        wrapped, grid=grid, in_specs=list(in_specs) + [hbm] * n_c, out_specs=list(out_specs) + [hbm] * n_c,
        out_shape=list(out_shape) + [_xfer_out_shape(k, a) for k, a in comm],
        scratch_shapes=list(scratch) + _xfer_sems(n_c), name=name, compiler_params=_params(("arbitrary",)),
    )(*args, *[a for _, a in comm])
    return list(outs[:n_out]), list(outs[n_out:])


def _norm_mod_fwd(X, nw, mods, si, nxt, name):
    T = X.shape[0]

    def body(x_ref, w_ref, m_ref, h_ref):
        x = x_ref[...]
        r = lax.rsqrt(_rowmean(x * x) + EPS)
        m = m_ref[0]
        y = x * r * w_ref[...]
        h_ref[...] = (y * (1.0 + m[si + 1:si + 2, :]) + m[si:si + 1, :]).astype(BF16)

    return pl.pallas_call(
        body, grid=(T // ROW_TILE,), name=name,
        in_specs=[_row(D), _const((1, D)), _mod_spec(nxt)],
        out_specs=_row(D), out_shape=jax.ShapeDtypeStruct((T, D), BF16),
        compiler_params=_params(("parallel",)),
    )(X, nw, mods)


def _norm_mod_bwd(dh, X, nw, mods, dres, si, nxt, name):
    T = X.shape[0]

    def body(dh_ref, x_ref, w_ref, m_ref, dres_ref, dx_ref, st_ref):
        i = pl.program_id(0)

        @pl.when((i == 0) | (i == nxt))
        def _():
            st_ref[...] = jnp.zeros_like(st_ref)

        x = x_ref[...]
        r = lax.rsqrt(_rowmean(x * x) + EPS)
        xn = x * r
        w = w_ref[...]
        scale = m_ref[0][si + 1:si + 2, :]
        dhv = dh_ref[...]
        dy = dhv * (1.0 + scale)
        dxn = dy * w
        dx_ref[...] = dres_ref[...] + r * (dxn - xn * _rowmean(dxn * xn))
        st_ref[0, 0:1, :] += _colsum(dhv)
        st_ref[0, 1:2, :] += _colsum(dhv * xn * w)
        st_ref[0, 2:3, :] += _colsum(dy * xn)

    return pl.pallas_call(
        body, grid=(T // ROW_TILE,), name=name,
        in_specs=[_row(D), _row(D), _const((1, D)), _mod_spec(nxt), _row(D)],
        out_specs=[_row(D), _stat_spec(nxt)],
        out_shape=[jax.ShapeDtypeStruct((T, D), F32), jax.ShapeDtypeStruct((2, 8, D), F32)],
        compiler_params=_params(("arbitrary",)),
    )(dh, X, nw, mods, dres)


def _resid_bwd(dX, y, mods, gi, nxt, name):
    T = dX.shape[0]

    def body(dx_ref, y_ref, m_ref, dy_ref, st_ref):
        i = pl.program_id(0)

        @pl.when((i == 0) | (i == nxt))
        def _():
            st_ref[...] = jnp.zeros_like(st_ref)

        dx = dx_ref[...]
        dy_ref[...] = (dx * m_ref[0][gi:gi + 1, :]).astype(BF16)
        st_ref[0, 0:1, :] += _colsum(dx * y_ref[...].astype(F32))

    return pl.pallas_call(
        body, grid=(T // ROW_TILE,), name=name,
        in_specs=[_row(D), _row(D), _mod_spec(nxt)],
        out_specs=[_row(D), _stat_spec(nxt)],
        out_shape=[jax.ShapeDtypeStruct((T, D), BF16), jax.ShapeDtypeStruct((2, 8, D), F32)],
        compiler_params=_params(("arbitrary",)),
    )(dX, y, mods)


def _loss_head(X, tgt, fw, nxt, name):
    T = X.shape[0]

    def body(x_ref, t_ref, w_ref, dx_ref, st_ref):
        i = pl.program_id(0)

        @pl.when(i == 0)
        def _():
            st_ref[...] = jnp.zeros_like(st_ref)

        @pl.when(i < nxt)
        def _():
            x = x_ref[...]
            r = lax.rsqrt(_rowmean(x * x) + EPS)
            xn = x * r
            w = w_ref[...]
            err = xn * w - t_ref[...]
            dy = err * (1.0 / D)
            dxn = dy * w
            dx_ref[...] = r * (dxn - xn * _rowmean(dxn * xn))
            st_ref[0:1, :] += _colsum(dy * xn)
            st_ref[1:2, :] += (0.5 / D) * _colsum(err * err)

        @pl.when(i >= nxt)
        def _():
            dx_ref[...] = jnp.zeros_like(dx_ref)

    return pl.pallas_call(
        body, grid=(T // ROW_TILE,), name=name,
        in_specs=[_row(D), pl.BlockSpec((ROW_TILE, D), lambda i: (jnp.minimum(i, nxt - 1), 0)), _const((1, D))],
        out_specs=[_row(D), _const((8, D))],
        out_shape=[jax.ShapeDtypeStruct((T, D), F32), jax.ShapeDtypeStruct((8, D), F32)],
        compiler_params=_params(("arbitrary",)),
    )(X, tgt, fw)


def _mm_nn(A, B, out_dtype, name, add=None, comm=()):
    T, K = A.shape
    N = B.shape[1]

    if add is None:
        def body(a_ref, b_ref, o_ref):
            o_ref[...] = _dot(a_ref[...], b_ref[...]).astype(out_dtype)
        ins, specs = (A, B), [_row(K), _const((K, N))]
    else:
        def body(a_ref, b_ref, c_ref, o_ref):
            o_ref[...] = (c_ref[...] + _dot(a_ref[...], b_ref[...])).astype(out_dtype)
        ins, specs = (A, B, add), [_row(K), _const((K, N)), _row(N)]

    return _call(body, grid=(T // ROW_TILE,), name=name, in_specs=specs,
                 out_specs=[_row(N)], out_shape=[jax.ShapeDtypeStruct((T, N), out_dtype)],
                 sem=("parallel",), args=ins, comm=comm)


def _mm_nn_resid(A, B, X, mods, gi, nxt, name, comm=()):
    T, K = A.shape
    N = B.shape[1]

    def body(a_ref, b_ref, x_ref, m_ref, xo_ref, y_ref):
        acc = _dot(a_ref[...], b_ref[...])
        y_ref[...] = acc.astype(BF16)
        xo_ref[...] = x_ref[...] + m_ref[0][gi:gi + 1, :] * acc

    return _call(body, grid=(T // ROW_TILE,), name=name,
                 in_specs=[_row(K), _const((K, N)), _row(N), _mod_spec(nxt)], out_specs=[_row(N), _row(N)],
                 out_shape=[jax.ShapeDtypeStruct((T, N), F32), jax.ShapeDtypeStruct((T, N), BF16)],
                 sem=("parallel",), args=(A, B, X, mods), comm=comm)


def _mm_nt(A, B, name, comm=()):
    T, N = A.shape
    K = B.shape[0]

    def body(a_ref, b_ref, o_ref):
        o_ref[...] = _dot_nt(a_ref[...], b_ref[...])

    return _call(body, grid=(T // ROW_TILE,), name=name, in_specs=[_row(N), _const((K, N))], out_specs=[_row(K)],
                 out_shape=[jax.ShapeDtypeStruct((T, K), F32)], sem=("parallel",), args=(A, B), comm=comm)


def _mm_tn(A, G, name):
    T, K = A.shape
    N = G.shape[1]
    nt = T // ROW_TILE

    def body(a_ref, g_ref, o_ref, acc_ref):
        i = pl.program_id(0)

        @pl.when(i == 0)
        def _():
            acc_ref[...] = jnp.zeros_like(acc_ref)

        acc_ref[...] += _dot_tn(a_ref[...], g_ref[...])

        @pl.when(i == nt - 1)
        def _():
            o_ref[...] = acc_ref[...].astype(BF16)

    return pl.pallas_call(
        body, grid=(nt,), name=name,
        in_specs=[_row(K), _row(N)],
        out_specs=_const((K, N)), out_shape=jax.ShapeDtypeStruct((K, N), BF16),
        scratch_shapes=[pltpu.VMEM((K, N), F32)],
        compiler_params=_params(("arbitrary",)),
    )(A, G)


def _ffn_up(h, WgT, WuT, name, comm=()):
    T = h.shape[0]
    F = WgT.shape[0]

    def body(h_ref, wg_ref, wu_ref, gp_ref, up_ref, act_ref):
        hv = h_ref[...]
        gp = _dot_nt(hv, wg_ref[...])
        up = _dot_nt(hv, wu_ref[...])
        gp_ref[...] = gp.astype(BF16)
        up_ref[...] = up.astype(BF16)
        act_ref[...] = (gp * _sigmoid(gp) * up).astype(BF16)

    sd = jax.ShapeDtypeStruct((T, F), BF16)
    return _call(body, grid=(T // ROW_TILE,), name=name, in_specs=[_row(D), _const((F, D)), _const((F, D))],
                 out_specs=[_row(F), _row(F), _row(F)], out_shape=[sd, sd, sd], sem=("parallel",),
                 args=(h, WgT, WuT), comm=comm)


def _ffn_dact(dy, Wd, gp, up, name, comm=()):
    T = dy.shape[0]
    F = Wd.shape[0]

    def body(dy_ref, wd_ref, gp_ref, up_ref, dgp_ref, dup_ref):
        dact = _dot_nt(dy_ref[...], wd_ref[...])
        gpv = gp_ref[...].astype(F32)
        upv = up_ref[...].astype(F32)
        sg = _sigmoid(gpv)
        dup_ref[...] = (dact * gpv * sg).astype(BF16)
        dgp_ref[...] = (dact * upv * sg * (1.0 + gpv * (1.0 - sg))).astype(BF16)

    sd = jax.ShapeDtypeStruct((T, F), BF16)
    return _call(body, grid=(T // ROW_TILE,), name=name, in_specs=[_row(D), _const((F, D)), _row(F), _row(F)],
                 out_specs=[_row(F), _row(F)], out_shape=[sd, sd], sem=("parallel",),
                 args=(dy, Wd, gp, up), comm=comm)


N_TAB = 7


def _ret_tables(tab_ref, lgf, lgb):
    r = lax.broadcasted_iota(jnp.int32, (CHUNK, CHUNK), 0).astype(F32)
    c = lax.broadcasted_iota(jnp.int32, (CHUNK, CHUNK), 1).astype(F32)
    for d, lg in ((0, lgf), (1, lgb)):
        if d == 0:
            mask, expo, xe, ze = r >= c, r - c, r + 1.0, (CHUNK - 1.0) - r
        else:
            mask, expo, xe, ze = c > r, c - r - 1.0, (CHUNK - 1.0) - r, r
        e = jnp.where(mask, expo, 0.0)
        tab_ref[N_TAB * d + 0] = jnp.where(mask, jnp.exp(lg * e), 0.0)
        tab_ref[N_TAB * d + 1] = jnp.exp(lg * xe)
        tab_ref[N_TAB * d + 2] = jnp.exp(lg * ze)
        tab_ref[N_TAB * d + 3] = jnp.exp(jnp.full((CHUNK, CHUNK), lg * float(CHUNK), F32))
        tab_ref[N_TAB * d + 4] = e
        tab_ref[N_TAB * d + 5] = xe
        tab_ref[N_TAB * d + 6] = ze


def _rope(t, cosf, sgn):
    return t * cosf + pltpu.roll(t, HEAD_DIM // 2, 1) * sgn


def _rope_t(d, cosf, sgn):
    return d * cosf + pltpu.roll(d * sgn, HEAD_DIM // 2, 1)


def _chunk_of(d, j, nc, ncx):
    return lax.rem(j + ncx, nc) if d == 0 else nc - 1 - j


def _head_spec(T, col0):
    return pl.BlockSpec((T, HEAD_DIM), lambda h: (0, col0 + h))


def _ret_fwd(p, lg, cosf, sgn, seq, name, comm=()):
    T = p.shape[0]
    nc, ncx = T // CHUNK, seq // CHUNK

    def body(lg_ref, q_ref, k_ref, v_ref, g_ref, cos_ref, sgn_ref, y_ref, ro_ref, tab_ref, of_ref):
        h = pl.program_id(0)
        _ret_tables(tab_ref, lg_ref[0, h], lg_ref[1, h])

        def load(cidx):
            rows = pl.ds(pl.multiple_of(cidx * CHUNK, CHUNK), CHUNK)
            cs, sn = cos_ref[rows, :], sgn_ref[rows, :]
            return rows, _rope(q_ref[rows, :], cs, sn) * Q_SCALE, _rope(k_ref[rows, :], cs, sn), v_ref[rows, :]

        def chunk(d, q, k, v, S):
            b = N_TAB * d
            kb, vb = k.astype(BF16), v.astype(BF16)
            pm = _dot_nt(q.astype(BF16), kb) * tab_ref[b]
            o = _dot(pm.astype(BF16), vb) + _dot((q * tab_ref[b + 1]).astype(BF16), S.astype(BF16))
            return o, S * tab_ref[b + 3] + _dot_tn((k * tab_ref[b + 2]).astype(BF16), vb)

        def step_f(j, S):
            rows, q, k, v = load(_chunk_of(0, j, nc, ncx))
            o, S = chunk(0, q, k, v, S)
            of_ref[rows, :] = o
            return S

        lax.fori_loop(0, nc, step_f, jnp.zeros((CHUNK, CHUNK), F32))

        def step_b(j, S):
            rows, q, k, v = load(_chunk_of(1, j, nc, ncx))
            o, S = chunk(1, q, k, v, S)
            y = of_ref[rows, :] + o
            y_ref[rows, :] = y
            g = g_ref[rows, :]
            ro_ref[rows, :] = (g * _sigmoid(g) * y * lax.rsqrt(_rowmean(y * y) + EPS)).astype(BF16)
            return S

        lax.fori_loop(0, nc, step_b, jnp.zeros((CHUNK, CHUNK), F32))

    tbl = pl.BlockSpec((T, HEAD_DIM), lambda h: (0, 0))
    return _call(
        body, grid=(RET_HEADS,), name=name,
        in_specs=[pl.BlockSpec(memory_space=pltpu.SMEM), _head_spec(T, 0), _head_spec(T, 4), _head_spec(T, 8),
                  _head_spec(T, 12), tbl, tbl],
        out_specs=[_head_spec(T, 0), _head_spec(T, 0)],
        out_shape=[jax.ShapeDtypeStruct((T, RET_W), F32), jax.ShapeDtypeStruct((T, RET_W), BF16)],
        scratch=[pltpu.VMEM((2 * N_TAB, CHUNK, CHUNK), F32), pltpu.VMEM((T, HEAD_DIM), F32)],
        sem=("arbitrary",), args=(lg, p, p, p, p, cosf, sgn), comm=comm)


def _ret_gate_bwd(yret, p, dmix, name):
    T = yret.shape[0]

    def body(y_ref, g_ref, dm_ref, dy_ref, dg_ref):
        for hh in range(RET_HEADS):
            sl = slice(hh * HEAD_DIM, (hh + 1) * HEAD_DIM)
            y = y_ref[:, sl]
            r = lax.rsqrt(_rowmean(y * y) + EPS)
            yn = y * r
            g = g_ref[:, sl]
            sg = _sigmoid(g)
            dro = dm_ref[:, sl]
            dg_ref[:, sl] = (dro * yn * sg * (1.0 + g * (1.0 - sg))).astype(BF16)
            dyn = dro * g * sg
            dy_ref[:, sl] = r * (dyn - yn * _rowmean(dyn * yn))

    return pl.pallas_call(
        body, grid=(T // ROW_TILE,), name=name,
        in_specs=[_row(RET_W), pl.BlockSpec((ROW_TILE, RET_W), lambda i: (i, 3)), _row(RET_W)],
        out_specs=[_row(RET_W), _row(RET_W)],
        out_shape=[jax.ShapeDtypeStruct((T, RET_W), F32), jax.ShapeDtypeStruct((T, RET_W), BF16)],
        compiler_params=_params(("parallel",)),
    )(yret, p, dmix)


def _ret_bwd(p, lg, dy, cosf, sgn, seq, name, comm=()):
    T = p.shape[0]
    nc, ncx = T // CHUNK, seq // CHUNK

    def body(lg_ref, q_ref, k_ref, v_ref, dy_ref, cos_ref, sgn_ref, dq_ref, dk_ref, dv_ref, dlg_ref,
             tab_ref, st_ref, dqs, dks, dvs):
        h = pl.program_id(0)
        _ret_tables(tab_ref, lg_ref[0, h], lg_ref[1, h])
        dlg_ref[...] = jnp.zeros_like(dlg_ref)

        def load(cidx):
            rows = pl.ds(pl.multiple_of(cidx * CHUNK, CHUNK), CHUNK)
            cs, sn = cos_ref[rows, :], sgn_ref[rows, :]
            return rows, _rope(q_ref[rows, :], cs, sn) * Q_SCALE, _rope(k_ref[rows, :], cs, sn), v_ref[rows, :]

        for d in (0, 1):
            b = N_TAB * d

            def states(j, S, d=d, b=b):
                _, _, k, v = load(_chunk_of(d, j, nc, ncx))
                st_ref[j] = S
                return S * tab_ref[b + 3] + _dot_tn((k * tab_ref[b + 2]).astype(BF16), v.astype(BF16))

            lax.fori_loop(0, nc, states, jnp.zeros((CHUNK, CHUNK), F32))

            def sweep(jj, carry, d=d, b=b):
                dS, acc = carry
                j = nc - 1 - jj
                rows, q, k, v = load(_chunk_of(d, j, nc, ncx))
                dO = dy_ref[rows, :]
                Sp = st_ref[j]
                qb, kb, vb, dOb = q.astype(BF16), k.astype(BF16), v.astype(BF16), dO.astype(BF16)
                Spb, dSb = Sp.astype(BF16), dS.astype(BF16)
                dmat, xi, ze, cd = tab_ref[b], tab_ref[b + 1], tab_ref[b + 2], tab_ref[b + 3]
                pm = _dot_nt(qb, kb) * dmat
                dpm = _dot_nt(dOb, vb)
                acc = acc + _colsum(dpm * pm * tab_ref[b + 4])
                dsc = (dpm * dmat).astype(BF16)
                qx = (q * xi).astype(BF16)
                kz = (k * ze).astype(BF16)
                dq = _dot(dsc, kb) + _dot_nt(dOb, Spb) * xi
                dk = _dot_tn(dsc, qb) + _dot_nt(vb, dSb) * ze
                dvst = _dot(kz, dSb)
                dv = _dot_tn(pm.astype(BF16), dOb) + dvst
                inter = _dot(qx, Spb)
                acc = acc + _colsum(dO * inter * tab_ref[b + 5])
                acc = acc + float(CHUNK) * _colsum(dS * cd * Sp) + _colsum(v * dvst * tab_ref[b + 6])
                if d == 0:
                    dqs[rows, :] = dq
                    dks[rows, :] = dk
                    dvs[rows, :] = dv
                else:
                    dqs[rows, :] += dq
                    dks[rows, :] += dk
                    dvs[rows, :] += dv
                return dS * cd + _dot_tn(qx, dOb), acc

            _, acc = lax.fori_loop(0, nc, sweep, (jnp.zeros((CHUNK, CHUNK), F32), jnp.zeros((1, CHUNK), F32)))
            dlg_ref[0, d:d + 1, :] = jnp.zeros((1, LANES), F32) + jnp.sum(acc)

        def finish(cidx, carry):
            rows = pl.ds(pl.multiple_of(cidx * CHUNK, CHUNK), CHUNK)
            cs, sn = cos_ref[rows, :], sgn_ref[rows, :]
            dq_ref[rows, :] = (_rope_t(dqs[rows, :], cs, sn) * Q_SCALE).astype(BF16)
            dk_ref[rows, :] = _rope_t(dks[rows, :], cs, sn).astype(BF16)
            dv_ref[rows, :] = dvs[rows, :].astype(BF16)
            return carry

        lax.fori_loop(0, nc, finish, 0)

    tbl = pl.BlockSpec((T, HEAD_DIM), lambda h: (0, 0))
    sd = jax.ShapeDtypeStruct((T, RET_W), BF16)
    return _call(
        body, grid=(RET_HEADS,), name=name,
        in_specs=[pl.BlockSpec(memory_space=pltpu.SMEM), _head_spec(T, 0), _head_spec(T, 4), _head_spec(T, 8),
                  _head_spec(T, 0), tbl, tbl],
        out_specs=[_head_spec(T, 0), _head_spec(T, 0), _head_spec(T, 0),
                   pl.BlockSpec((1, 8, LANES), lambda h: (h, 0, 0))],
        out_shape=[sd, sd, sd, jax.ShapeDtypeStruct((RET_HEADS, 8, LANES), F32)],
        scratch=[pltpu.VMEM((2 * N_TAB, CHUNK, CHUNK), F32), pltpu.VMEM((nc, CHUNK, CHUNK), F32),
                 pltpu.VMEM((T, HEAD_DIM), F32), pltpu.VMEM((T, HEAD_DIM), F32), pltpu.VMEM((T, HEAD_DIM), F32)],
        sem=("arbitrary",), args=(lg, p, p, p, dy, cosf, sgn), comm=comm)


def _halo_specs(T, cols, colblock):
    per = ROW_TILE // HALO
    last = T // HALO - 1
    prv = pl.BlockSpec((HALO, cols), lambda i: (jnp.maximum(i * per - 1, 0), colblock))
    nxt = pl.BlockSpec((HALO, cols), lambda i: (jnp.minimum((i + 1) * per, last), colblock))
    return prv, nxt


def _seq_edges(i, nxt, nt):
    first = (i == 0) | (i == nxt)
    last = (i == nxt - 1) | (i == nt - 1)
    return first, last


def _fill_pad(pad_ref, prv, cur, nxt, first, last):
    pad_ref[0:HALO, :] = jnp.where(first, 0.0, prv)
    pad_ref[HALO:HALO + ROW_TILE, :] = cur
    pad_ref[HALO + ROW_TILE:2 * HALO + ROW_TILE, :] = jnp.where(last, 0.0, nxt)


def _ln_stats(x):
    mu = _rowmean(x)
    xc = x - mu
    rs = lax.rsqrt(_rowmean(xc * xc) + EPS)
    return xc * rs, rs


def _conv_fwd(p, cw, lnw, lnb, nxt, name, comm=()):
    T = p.shape[0]
    nt = T // ROW_TILE
    a_col, g_col = 4, 5

    def body(a_ref, g_ref, ap_ref, gp_ref, an_ref, gn_ref, w_ref, lw_ref, lb_ref, cv_ref, co_ref, pad_ref):
        i = pl.program_id(0)
        first, last = _seq_edges(i, nxt, nt)
        glu = lambda a, g: a * _sigmoid(g)
        _fill_pad(pad_ref, glu(ap_ref[...], gp_ref[...]), glu(a_ref[...], g_ref[...]),
                  glu(an_ref[...], gn_ref[...]), first, last)
        acc = jnp.zeros((ROW_TILE, CONV_CH), F32)
        for k in range(CONV_K):
            acc = acc + w_ref[k:k + 1, :] * pad_ref[k + 1:k + 1 + ROW_TILE, :]
        cv_ref[...] = acc
        xh, _ = _ln_stats(acc)
        z = xh * lw_ref[...] + lb_ref[...]
        co_ref[...] = (z * _sigmoid(z)).astype(BF16)

    cur = lambda cb: pl.BlockSpec((ROW_TILE, CONV_CH), lambda i: (i, cb))
    ap, an = _halo_specs(T, CONV_CH, a_col)
    gp, gn = _halo_specs(T, CONV_CH, g_col)
    return _call(
        body, grid=(nt,), name=name,
        in_specs=[cur(a_col), cur(g_col), ap, gp, an, gn, _const((32, CONV_CH)), _const((1, CONV_CH)),
                  _const((1, CONV_CH))],
        out_specs=[_row(CONV_CH), _row(CONV_CH)],
        out_shape=[jax.ShapeDtypeStruct((T, CONV_CH), F32), jax.ShapeDtypeStruct((T, CONV_CH), BF16)],
        scratch=[pltpu.VMEM((ROW_TILE + 2 * HALO, CONV_CH), F32)],
        sem=("parallel",), args=(p, p, p, p, p, p, cw, lnw, lnb), comm=comm)


def _conv_bwd(p, cv, dmix, cw, lnw, lnb, nxt, name, comm=()):
    T = p.shape[0]
    nt = T // ROW_TILE
    a_col, g_col = 4, 5

    def body(a_ref, g_ref, ap_ref, gp_ref, an_ref, gn_ref, cv_ref, cvp_ref, cvn_ref, dc_ref, dcp_ref, dcn_ref,
             w_ref, lw_ref, lb_ref, dp_ref, dw_ref, dl_ref, upad_ref, dpad_ref):
        i = pl.program_id(0)
        first, last = _seq_edges(i, nxt, nt)

        @pl.when(i == 0)
        def _():
            dw_ref[...] = jnp.zeros_like(dw_ref)
            dl_ref[...] = jnp.zeros_like(dl_ref)

        lw, lb = lw_ref[...], lb_ref[...]

        def ln_bwd(cvv, dco):
            xh, rs = _ln_stats(cvv)
            z = xh * lw + lb
            sg = _sigmoid(z)
            dz = dco * sg * (1.0 + z * (1.0 - sg))
            dxh = dz * lw
            return rs * (dxh - _rowmean(dxh) - xh * _rowmean(dxh * xh)), dz, xh

        dcv, dz, xh = ln_bwd(cv_ref[...], dc_ref[...])
        _fill_pad(dpad_ref, ln_bwd(cvp_ref[...], dcp_ref[...])[0], dcv, ln_bwd(cvn_ref[...], dcn_ref[...])[0],
                  first, last)
        a, g = a_ref[...], g_ref[...]
        sg = _sigmoid(g)
        glu = lambda av, gv: av * _sigmoid(gv)
        _fill_pad(upad_ref, glu(ap_ref[...], gp_ref[...]), a * sg, glu(an_ref[...], gn_ref[...]), first, last)
        du = jnp.zeros((ROW_TILE, CONV_CH), F32)
        for k in range(CONV_K):
            du = du + w_ref[k:k + 1, :] * dpad_ref[CONV_K - k:CONV_K - k + ROW_TILE, :]
            dw_ref[k:k + 1, :] += _colsum(dcv * upad_ref[k + 1:k + 1 + ROW_TILE, :])
        dl_ref[0:1, :] += _colsum(dz * xh)
        dl_ref[1:2, :] += _colsum(dz)
        dp_ref[:, 0:CONV_CH] = (du * sg).astype(BF16)
        dp_ref[:, CONV_CH:2 * CONV_CH] = (du * a * sg * (1.0 - sg)).astype(BF16)

    cur = lambda cb: pl.BlockSpec((ROW_TILE, CONV_CH), lambda i: (i, cb))
    ap, an = _halo_specs(T, CONV_CH, a_col)
    gp, gn = _halo_specs(T, CONV_CH, g_col)
    cvp, cvn = _halo_specs(T, CONV_CH, 0)
    dcp, dcn = _halo_specs(T, CONV_CH, 1)
    return _call(
        body, grid=(nt,), name=name,
        in_specs=[cur(a_col), cur(g_col), ap, gp, an, gn, cur(0), cvp, cvn, cur(1), dcp, dcn,
                  _const((32, CONV_CH)), _const((1, CONV_CH)), _const((1, CONV_CH))],
        out_specs=[_row(2 * CONV_CH), _const((32, CONV_CH)), _const((8, CONV_CH))],
        out_shape=[jax.ShapeDtypeStruct((T, 2 * CONV_CH), BF16), jax.ShapeDtypeStruct((32, CONV_CH), F32),
                   jax.ShapeDtypeStruct((8, CONV_CH), F32)],
        scratch=[pltpu.VMEM((ROW_TILE + 2 * HALO, CONV_CH), F32), pltpu.VMEM((ROW_TILE + 2 * HALO, CONV_CH), F32)],
        sem=("arbitrary",), args=(p, p, p, p, p, p, cv, cv, cv, dmix, dmix, dmix, cw, lnw, lnb), comm=comm)


def _tile_positions(i, nxt, seq, ctx, offset, rows):
    is_ctx = i >= nxt
    pos0 = (i - jnp.where(is_ctx, nxt, 0)) * ROW_TILE + offset
    length = jnp.where(is_ctx, ctx, seq).astype(F32)
    pos = (pos0 + lax.broadcasted_iota(jnp.int32, (rows, 1), 0)).astype(F32)
    return pos, length


def _pool_count(pos, length, w):
    left = w // 2
    right = w - 1 - left
    return jnp.minimum(pos + right, length - 1.0) - jnp.maximum(pos - left, 0.0) + 1.0


def _gelu(x):
    return 0.5 * x * (1.0 + lax.erf(x * INV_SQRT2))


def _odd_fwd(p, pw, ps, slw, slb, sw, sbf, nxt, seq, ctx, name, comm=()):
    T = p.shape[0]
    nt = T // ROW_TILE

    def body(p_ref, pp_ref, pn_ref, pw_ref, ps_ref, lw_ref, lb_ref, sw_ref, sb_ref, o_ref, pad_ref):
        i = pl.program_id(0)
        first, last = _seq_edges(i, nxt, nt)
        _fill_pad(pad_ref, pp_ref[...], p_ref[:, 0:POOL_CH], pn_ref[...], first, last)
        pos, length = _tile_positions(i, nxt, seq, ctx, 0, ROW_TILE)
        for gi, w in enumerate(POOL_WINDOWS):
            sl = slice(gi * GROUP_CH, (gi + 1) * GROUP_CH)
            left = w // 2
            ssum = jnp.zeros((ROW_TILE, GROUP_CH), F32)
            for o in range(-left, w - left):
                ssum = ssum + pad_ref[HALO + o:HALO + o + ROW_TILE, sl]
            m = ssum / _pool_count(pos, length, w) - p_ref[:, sl]
            pre = _dot(m.astype(BF16), pw_ref[gi].astype(BF16))
            o_ref[:, sl] = (pre * ps_ref[:, sl]).astype(BF16)
        u = _gelu(p_ref[:, POOL_CH:POOL_CH + SG_CH])
        xh, _ = _ln_stats(_gelu(p_ref[:, POOL_CH + SG_CH:ODD_IN]))
        vln = xh * lw_ref[...] + lb_ref[...]
        for n in range(ROW_TILE // SG_CHUNK):
            rs = slice(n * SG_CHUNK, (n + 1) * SG_CHUNK)
            for gi in range(4):
                sl = slice(gi * GROUP_CH, (gi + 1) * GROUP_CH)
                s = _dot(sw_ref[gi].astype(BF16), vln[rs, sl].astype(BF16)) + sb_ref[:, sl]
                o_ref[rs, POOL_CH + gi * GROUP_CH:POOL_CH + (gi + 1) * GROUP_CH] = (u[rs, sl] * s).astype(BF16)

    pp, pn = _halo_specs(T, POOL_CH, 0)
    return _call(
        body, grid=(nt,), name=name,
        in_specs=[_row(ODD_IN), pp, pn, _const((4, GROUP_CH, GROUP_CH)), _const((1, POOL_CH)), _const((1, SG_CH)),
                  _const((1, SG_CH)), _const((4, SG_CHUNK, SG_CHUNK)), _const((SG_CHUNK, SG_CH))],
        out_specs=[_row(D)], out_shape=[jax.ShapeDtypeStruct((T, D), BF16)],
        scratch=[pltpu.VMEM((ROW_TILE + 2 * HALO, POOL_CH), F32)],
        sem=("parallel",), args=(p, p, p, pw, ps, slw, slb, sw, sbf), comm=comm)


def _odd_bwd(p, dmix, pw, ps, slw, slb, sw, sbf, nxt, seq, ctx, name, comm=()):
    T = p.shape[0]
    nt = T // ROW_TILE

    def body(p_ref, pp_ref, pn_ref, dm_ref, dmp_ref, dmn_ref, pw_ref, ps_ref, lw_ref, lb_ref, sw_ref, sb_ref,
             dp_ref, dpw_ref, dsw_ref, dv_ref, dsb_ref, pad_ref, dpad_ref):
        i = pl.program_id(0)
        first, last = _seq_edges(i, nxt, nt)

        @pl.when(i == 0)
        def _():
            dpw_ref[...] = jnp.zeros_like(dpw_ref)
            dsw_ref[...] = jnp.zeros_like(dsw_ref)
            dv_ref[...] = jnp.zeros_like(dv_ref)
            dsb_ref[...] = jnp.zeros_like(dsb_ref)

        _fill_pad(pad_ref, pp_ref[...], p_ref[:, 0:POOL_CH], pn_ref[...], first, last)
        pos, length = _tile_positions(i, nxt, seq, ctx, 0, ROW_TILE)
        pos_p, _ = _tile_positions(i, nxt, seq, ctx, -HALO, HALO)
        pos_n, _ = _tile_positions(i, nxt, seq, ctx, ROW_TILE, HALO)
        scale = ps_ref[...]
        for gi, w in enumerate(POOL_WINDOWS):
            sl = slice(gi * GROUP_CH, (gi + 1) * GROUP_CH)
            left = w // 2
            right = w - 1 - left
            ssum = jnp.zeros((ROW_TILE, GROUP_CH), F32)
            for o in range(-left, right + 1):
                ssum = ssum + pad_ref[HALO + o:HALO + o + ROW_TILE, sl]
            cnt = _pool_count(pos, length, w)
            m = ssum / cnt - p_ref[:, sl]
            wg = pw_ref[gi].astype(BF16)
            pre = _dot(m.astype(BF16), wg)
            dpo = dm_ref[:, sl]
            dv_ref[0:1, sl] += _colsum(dpo * pre)
            dpre = (dpo * scale[:, sl]).astype(BF16)
            dpw_ref[gi] += _dot_tn(m.astype(BF16), dpre)
            dmc = _dot_nt(dpre, wg)
            halo_dm = lambda ref, ps_: _dot_nt((ref[:, sl] * scale[:, sl]).astype(BF16), wg) / _pool_count(ps_, length, w)
            dpad_ref[0:HALO, sl] = jnp.where(first, 0.0, halo_dm(dmp_ref, pos_p))
            dpad_ref[HALO:HALO + ROW_TILE, sl] = dmc / cnt
            dpad_ref[HALO + ROW_TILE:2 * HALO + ROW_TILE, sl] = jnp.where(last, 0.0, halo_dm(dmn_ref, pos_n))
            atd = jnp.zeros((ROW_TILE, GROUP_CH), F32)
            for o in range(-right, left + 1):
                atd = atd + dpad_ref[HALO + o:HALO + o + ROW_TILE, sl]
            dp_ref[:, sl] = (atd - dmc).astype(BF16)

        pu = p_ref[:, POOL_CH:POOL_CH + SG_CH]
        pv = p_ref[:, POOL_CH + SG_CH:ODD_IN]
        u = _gelu(pu)
        xh, rs_ = _ln_stats(_gelu(pv))
        lw = lw_ref[...]
        vln = xh * lw + lb_ref[...]
        dgelu = lambda x: 0.5 * (1.0 + lax.erf(x * INV_SQRT2)) + x * jnp.exp(-0.5 * x * x) * INV_SQRT_2PI
        for n in range(ROW_TILE // SG_CHUNK):
            rs = slice(n * SG_CHUNK, (n + 1) * SG_CHUNK)
            dvl = []
            for gi in range(4):
                sl = slice(gi * GROUP_CH, (gi + 1) * GROUP_CH)
                wq = sw_ref[gi].astype(BF16)
                vb = vln[rs, sl].astype(BF16)
                s = _dot(wq, vb) + sb_ref[:, sl]
                dsg = dm_ref[rs, POOL_CH + gi * GROUP_CH:POOL_CH + (gi + 1) * GROUP_CH]
                ds = dsg * u[rs, sl]
                dsb_ref[:, sl] += ds
                dsw_ref[gi] += _dot_nt(ds.astype(BF16), vb)
                dvl.append(_dot_tn(wq, ds.astype(BF16)))
                dp_ref[rs, POOL_CH + gi * GROUP_CH:POOL_CH + (gi + 1) * GROUP_CH] = (
                    dsg * s * dgelu(pu[rs, sl])).astype(BF16)
            dvln = jnp.concatenate(dvl, axis=1)
            xhc = xh[rs, :]
            dv_ref[1:2, :] += _colsum(dvln * xhc)
            dv_ref[2:3, :] += _colsum(dvln)
            dxh = dvln * lw
            dvv = rs_[rs, :] * (dxh - _rowmean(dxh) - xhc * _rowmean(dxh * xhc))
            dp_ref[rs, POOL_CH + SG_CH:ODD_IN] = (dvv * dgelu(pv[rs, :])).astype(BF16)

    pp, pn = _halo_specs(T, POOL_CH, 0)
    dmp, dmn = _halo_specs(T, POOL_CH, 0)
    gsd = jax.ShapeDtypeStruct((4, GROUP_CH, GROUP_CH), F32)
    return _call(
        body, grid=(nt,), name=name,
        in_specs=[_row(ODD_IN), pp, pn, _row(D), dmp, dmn, _const((4, GROUP_CH, GROUP_CH)), _const((1, POOL_CH)),
                  _const((1, SG_CH)), _const((1, SG_CH)), _const((4, SG_CHUNK, SG_CHUNK)), _const((SG_CHUNK, SG_CH))],
        out_specs=[_row(ODD_IN), _const((4, GROUP_CH, GROUP_CH)), _const((4, SG_CHUNK, SG_CHUNK)), _const((8, POOL_CH)),
                   _const((SG_CHUNK, SG_CH))],
        out_shape=[jax.ShapeDtypeStruct((T, ODD_IN), BF16), gsd, gsd, jax.ShapeDtypeStruct((8, POOL_CH), F32),
                   jax.ShapeDtypeStruct((SG_CHUNK, SG_CH), F32)],
        scratch=[pltpu.VMEM((ROW_TILE + 2 * HALO, POOL_CH), F32), pltpu.VMEM((ROW_TILE + 2 * HALO, POOL_CH), F32)],
        sem=("arbitrary",), args=(p, p, p, dmix, dmix, dmix, pw, ps, slw, slb, sw, sbf), comm=comm)


def _ada_fwd(cs, aw, ab, name):
    cols = aw.shape[2]

    def body(c_ref, w_ref, b_ref, o_ref):
        c = c_ref[...]
        s = (c * _sigmoid(c)).astype(BF16)
        o_ref[0] = _dot(s, w_ref[0].astype(BF16)) + b_ref[0]

    return pl.pallas_call(
        body, grid=(DEPTH,), name=name,
        in_specs=[pl.BlockSpec((16, D), lambda i: (0, 0)), pl.BlockSpec((1, D, cols), lambda i: (i, 0, 0)),
                  pl.BlockSpec((1, 1, cols), lambda i: (i, 0, 0))],
        out_specs=pl.BlockSpec((1, 16, cols), lambda i: (i, 0, 0)),
        out_shape=jax.ShapeDtypeStruct((DEPTH, 16, cols), F32),
        compiler_params=_params(("parallel",)),
    )(cs, aw, ab)


def _ada_bwd(cs, G, aw, name):
    cols = aw.shape[2]

    def body(c_ref, g_ref, w_ref, gw_ref, cc_ref):
        i = pl.program_id(0)

        @pl.when(i == 0)
        def _():
            cc_ref[...] = jnp.zeros_like(cc_ref)

        c = c_ref[...]
        s = (c * _sigmoid(c)).astype(BF16)
        g = g_ref[0]
        dc = g[8:9, :]
        for d in range(9, 16):
            dc = dc + g[d:d + 1, :]
        row = lax.broadcasted_iota(jnp.int32, (8, cols), 0)
        dcrows = jnp.where(row == 0, dc, 0.0)
        dm = jnp.concatenate([g[0:8, :], dcrows], axis=0).astype(BF16)
        gw_ref[0] = _dot_tn(s, dm)
        cc_ref[...] += _dot_nt(dcrows.astype(BF16), w_ref[0].astype(BF16))

    return pl.pallas_call(
        body, grid=(DEPTH,), name=name,
        in_specs=[pl.BlockSpec((16, D), lambda i: (0, 0)), pl.BlockSpec((1, 16, cols), lambda i: (i, 0, 0)),
                  pl.BlockSpec((1, D, cols), lambda i: (i, 0, 0))],
        out_specs=[pl.BlockSpec((1, D, cols), lambda i: (i, 0, 0)), pl.BlockSpec((8, D), lambda i: (0, 0))],
        out_shape=[jax.ShapeDtypeStruct((DEPTH, D, cols), F32), jax.ShapeDtypeStruct((8, D), F32)],
        compiler_params=_params(("arbitrary",)),
    )(cs, G, aw)


def _row_tile_for(rows):
    for t in (512, 256, 128, 64, 32, 16, 8):
        if rows % t == 0:
            return t
    raise ValueError(f"rows={rows} is not a multiple of 8")


def _sum_devices(x, name):
    _, rows, cols = x.shape
    tr = _row_tile_for(rows)

    def body(x_ref, o_ref):
        acc = x_ref[0]
        for d in range(1, N_DEV):
            acc = acc + x_ref[d]
        o_ref[...] = acc

    return pl.pallas_call(
        body, grid=(rows // tr,), name=name,
        in_specs=[pl.BlockSpec((N_DEV, tr, cols), lambda i: (0, i, 0))],
        out_specs=pl.BlockSpec((tr, cols), lambda i: (i, 0)),
        out_shape=jax.ShapeDtypeStruct((rows, cols), F32),
        compiler_params=_params(("parallel",)),
    )(x)


def _adamw(w, g, m, v, name):
    rows, cols = w.shape
    tr = _row_tile_for(rows)

    def body(w_ref, g_ref, m_ref, v_ref, go_ref, d_ref, mo_ref, vo_ref):
        gv = g_ref[...]
        mn = ADAM_B1 * m_ref[...] + (1.0 - ADAM_B1) * gv
        vn = ADAM_B2 * v_ref[...] + (1.0 - ADAM_B2) * (gv * gv)
        go_ref[...] = gv
        mo_ref[...] = mn
        vo_ref[...] = vn
        d_ref[...] = -ADAM_LR * ((mn / ADAM_BC1) / (jnp.sqrt(vn / ADAM_BC2) + ADAM_EPS) + ADAM_WD * w_ref[...])

    blk = pl.BlockSpec((tr, cols), lambda i: (i, 0))
    sd = jax.ShapeDtypeStruct((rows, cols), F32)
    return pl.pallas_call(
        body, grid=(rows // tr,), name=name,
        in_specs=[blk, blk, blk, blk], out_specs=[blk, blk, blk, blk], out_shape=[sd, sd, sd, sd],
        compiler_params=_params(("parallel",)),
    )(w, g, m, v)


def _adamw_layers(w, pieces, m, v, name):
    L, a, b = w.shape
    bp = pieces[0].shape[2]
    tr = next(t for t in (256, 128, 64, 32, 16) if a % t == 0)

    def body(w_ref, m_ref, v_ref, *rest):
        p_refs, (go_ref, d_ref, mo_ref, vo_ref) = rest[:L], rest[L:]
        layer = pl.program_id(0)
        for k in range(L):
            @pl.when(layer == k)
            def _(k=k):
                gv = p_refs[k][0].astype(F32)
                for d in range(1, N_DEV):
                    gv = gv + p_refs[k][d].astype(F32)
                gv = gv[:, :b]
                mn = ADAM_B1 * m_ref[0] + (1.0 - ADAM_B1) * gv
                vn = ADAM_B2 * v_ref[0] + (1.0 - ADAM_B2) * (gv * gv)
                go_ref[0] = gv
                mo_ref[0] = mn
                vo_ref[0] = vn
                d_ref[0] = -ADAM_LR * ((mn / ADAM_BC1) / (jnp.sqrt(vn / ADAM_BC2) + ADAM_EPS) + ADAM_WD * w_ref[0])

    blk = pl.BlockSpec((1, tr, b), lambda l, i: (l, i, 0))
    pspecs = [pl.BlockSpec((N_DEV, tr, bp), lambda l, i, k=k: (0, jnp.where(l == k, i, 0), 0)) for k in range(L)]
    sd = jax.ShapeDtypeStruct((L, a, b), F32)
    return pl.pallas_call(
        body, grid=(L, a // tr), name=name,
        in_specs=[blk, blk, blk] + pspecs, out_specs=[blk, blk, blk, blk], out_shape=[sd, sd, sd, sd],
        compiler_params=_params(("arbitrary", "arbitrary")),
    )(w, m, v, *pieces)


def _pack_rows(shape):
    return -(-math.prod(shape) // (8 * LANES)) * 8


def _pack(arrs, row_mult):
    parts = []
    for a in arrs:
        n, rows = math.prod(a.shape), _pack_rows(a.shape)
        parts.append(jnp.pad(a.reshape(-1).astype(F32), (0, rows * LANES - n)).reshape(rows, LANES))
    total = sum(p.shape[0] for p in parts)
    if total % row_mult:
        parts.append(jnp.zeros((-total % row_mult, LANES), F32))
    return jnp.concatenate(parts, axis=0)


def _unpack(packed, shapes):
    out, r0 = [], 0
    lead = packed.shape[:-2]
    for s in shapes:
        n, rows = math.prod(s), _pack_rows(s)
        piece = packed[..., r0:r0 + rows, :].reshape(lead + (rows * LANES,))
        out.append(piece[..., :n].reshape(lead + tuple(s)))
        r0 += rows
    return out


def _rope_tables(seq, ctx):
    def angles(ps):
        parts = []
        for pvec, n in zip(ps, ROPE_PAIRS):
            freq = ROPE_BASE ** (-jnp.arange(n, dtype=F32) / n)
            parts.append(pvec[:, None] * freq[None, :])
        return jnp.concatenate(parts, axis=-1)

    rows = seq // GRID_W
    grid_r = jnp.broadcast_to(jnp.arange(rows, dtype=F32)[:, None], (rows, GRID_W)).reshape(-1)
    grid_c = jnp.broadcast_to(jnp.arange(GRID_W, dtype=F32)[None, :], (rows, GRID_W)).reshape(-1)
    zc = jnp.zeros((ctx,), F32)
    ang = jnp.concatenate([angles((jnp.full((seq,), ctx, F32), grid_r, grid_c)),
                           angles((jnp.arange(ctx, dtype=F32), zc, zc))], axis=0)
    cos, sin = jnp.cos(ang), jnp.sin(ang)
    return jnp.concatenate([cos, cos], axis=1), jnp.concatenate([-sin, sin], axis=1)


def _layer_gather(sh, even):
    return [("ag_cols" if even else "ag_rows", sh["in"]), ("ag_rows", sh["out"]), ("ag_rows", sh["gate"]),
            ("ag_rows", sh["up"]), ("ag_rows", sh["down"])]


def _sample_step(X0, tgt, mods, shards, gather0, S, seq, ctx):
    nxt = seq // ROW_TILE
    cosf, sgn = _rope_tables(seq, ctx)
    X = X0
    saved = []
    gathers = {(0, "mix"): gather0}

    def start_gather(i, half, dep):
        comm = _layer_gather(shards[i], i % 2 == 0)
        comm = comm[:2] if half == "mix" else comm[2:]
        comm[0] = (comm[0][0], comm[0][1] + dep.astype(BF16))
        gathers[(i, half)], token = _xfer_start(comm, f"gather_{half}{i}_start")
        return token[0, 0]

    dep = mods[0, 0, 0, 0] * 0.0
    for i, half in ((0, "ffn"), (1, "mix"), (1, "ffn")):
        dep = start_gather(i, half, dep)
    w_in, w_out = _xfer_wait(gathers[(0, "mix")], mods, "gather_mix0_wait")
    for i in range(DEPTH):
        j, even = i // 2, i % 2 == 0
        t = f"l{i}_"
        last = i == DEPTH - 1
        if i in (1, 2):
            dep = w_out[0, 0].astype(F32) * 0.0
            for half in ("mix", "ffn"):
                dep = start_gather(i + 1, half, dep)
        md = mods[i] + dep
        h1 = _norm_mod_fwd(X, S["norm_w"][i, 0][None], md, 0, nxt, t + "norm1")
        if even:
            (p,), _ = _mm_nn(h1, w_in, F32, t + "proj_in")
            (yret, ro), _ = _ret_fwd(p, S["lg"][j], cosf, sgn, seq, t + "ret_fwd")
            (cv, co), _ = _conv_fwd(p, S["conv_w"][j], S["conv_ln_w"][j][None], S["conv_ln_b"][j][None], nxt,
                                    t + "conv_fwd")
            mix = jnp.concatenate([ro, co], axis=1)
            extra = (yret, cv)
        else:
            (p,), _ = _mm_nt(h1, w_in, t + "proj_in")
            sbf = jnp.repeat(S["sg_b"][j].T, GROUP_CH, axis=1)
            (mix,), _ = _odd_fwd(p, S["pool_w"][j], S["pool_scale"][j][None], S["sg_ln_w"][j][None],
                                 S["sg_ln_b"][j][None], S["sg_w"][j], sbf, nxt, seq, ctx, t + "odd_fwd")
            extra = (sbf,)
        (X1, y1), _ = _mm_nn_resid(mix, w_out, X, md, 2, nxt, t + "proj_out")
        h2 = _norm_mod_fwd(X1, S["norm_w"][i, 1][None], md, 3, nxt, t + "norm2")
        wg, wu, wd = _xfer_wait(gathers[(i, "ffn")], h2, f"gather_ffn{i}_wait")
        (gp, up, act), _ = _ffn_up(h2, wg, wu, t + "ffn_up")
        (X2, y2), _ = _mm_nn_resid(act, wd, X1, md, 5, nxt, t + "ffn_down")
        saved.append((X, h1, p, mix, extra, y1, X1, h2, gp, up, act, y2, (w_in, w_out, wg, wu, wd)))
        X = X2
        if not last:
            w_in, w_out = _xfer_wait(gathers[(i + 1, "mix")], X2, f"gather_mix{i + 1}_wait")

    dX, hst = _loss_head(X, tgt, S["final_norm_w"][None], nxt, "loss_head")
    loss = jnp.sum(hst[1])
    pieces = {}
    gS = {"final_norm_w": hst[0], "norm_w": [None] * DEPTH, "lg": [None] * 2, "conv_w": [None] * 2,
          "conv_ln_w": [None] * 2, "conv_ln_b": [None] * 2, "pool_w": [None] * 2, "pool_scale": [None] * 2,
          "sg_ln_w": [None] * 2, "sg_ln_b": [None] * 2, "sg_w": [None] * 2, "sg_b": [None] * 2}
    dmods = [None] * DEPTH
    in_flight = []

    def land(flights, after, tag):
        for handle, keys, name in flights:
            for key, got in zip(keys, _xfer_wait(handle, after, name + tag)):
                pieces[key] = got

    for i in reversed(range(DEPTH)):
        j, even = i // 2, i % 2 == 0
        md = mods[i]
        t = f"l{i}_"
        X_in, h1, p, mix, extra, y1, X1, h2, gp, up, act, y2, (w_in, w_out, wg, wu, wd) = saved[i]
        dyf, g2 = _resid_bwd(dX, y2, md, 5, nxt, t + "ffn_resid_bwd")
        (dgp, dup), _ = _ffn_dact(dyf, wd, gp, up, t + "ffn_dact")
        g_down = _mm_tn(act, dyf, t + "dw_down")
        g_gate = _mm_tn(dgp, h2, t + "dw_gate")
        g_up = _mm_tn(dup, h2, t + "dw_up")
        ffn_flight, token = _xfer_start([("a2a_rows", g_down), ("a2a_rows", g_gate), ("a2a_rows", g_up)],
                                        t + "scatter_ffn_start")
        (dh2,), _ = _mm_nn(dgp, wg, F32, t + "dh2_gate")
        (dh2,), _ = _mm_nn(dup, wu, F32, t + "dh2_up", add=dh2)
        dX1, s2 = _norm_mod_bwd(dh2, X1, S["norm_w"][i, 1][None], md + token[0, 0], dX, 3, nxt, t + "norm2_bwd")
        dym, g1 = _resid_bwd(dX1, y1, md, 2, nxt, t + "mix_resid_bwd")
        (dmix,), _ = _mm_nt(dym, w_out, t + "dmix")
        g_out = _mm_tn(mix, dym, t + "dw_out")
        if even:
            yret, cv = extra
            dyr, dg = _ret_gate_bwd(yret, p, dmix, t + "ret_gate_bwd")
            (dq, dk, dv, dlg), _ = _ret_bwd(p, S["lg"][j], dyr, cosf, sgn, seq, t + "ret_bwd")
            early = [("ag_blk", _pack([jnp.stack(gS["pool_w"]), jnp.stack(gS["sg_w"])], 8))] if i == 0 else []
            (dpc, dcw, dln), got = _conv_bwd(
                p, cv, dmix, S["conv_w"][j], S["conv_ln_w"][j][None], S["conv_ln_b"][j][None], nxt,
                t + "conv_bwd", comm=early)
            if i == 0:
                early_all = got[0]
            dp = jnp.concatenate([dq, dk, dv, dg, dpc], axis=1)
            gS["lg"][j] = dlg[:, 0:2, 0].T
            gS["conv_w"][j], gS["conv_ln_w"][j], gS["conv_ln_b"][j] = dcw, dln[0], dln[1]
            g_in = _mm_tn(h1, dp, t + "dw_in")
            mix_flight, token = _xfer_start([("a2a_rows", g_out), ("a2a_cols", g_in)], t + "scatter_mix_start")
            (dh1,), _ = _mm_nt(dp, w_in, t + "dh1")
        else:
            (sbf,) = extra
            (dp, dpw, dsw, dvec, dsb), _ = _odd_bwd(
                p, dmix, S["pool_w"][j], S["pool_scale"][j][None], S["sg_ln_w"][j][None], S["sg_ln_b"][j][None],
                S["sg_w"][j], sbf, nxt, seq, ctx, t + "odd_bwd")
            gS["pool_w"][j], gS["sg_w"][j] = dpw, dsw
            gS["pool_scale"][j], gS["sg_ln_w"][j], gS["sg_ln_b"][j] = dvec[0], dvec[1], dvec[2]
            gS["sg_b"][j] = jnp.sum(dsb.reshape(SG_CHUNK, 4, GROUP_CH), axis=2).T
            g_in = _mm_tn(dp, h1, t + "dw_in")
            mix_flight, token = _xfer_start([("a2a_rows", g_out), ("a2a_rows", g_in)], t + "scatter_mix_start")
            (dh1,), _ = _mm_nn(dp, w_in, F32, t + "dh1")
        dX, s1 = _norm_mod_bwd(dh1, X_in, S["norm_w"][i, 0][None], md + token[0, 0], dX1, 0, nxt, t + "norm1_bwd")
        gS["norm_w"][i] = jnp.stack([s1[0, 2] + s1[1, 2], s2[0, 2] + s2[1, 2]])
        dmods[i] = jnp.stack([s1[:, 0], s1[:, 1], g1[:, 0], s2[:, 0], s2[:, 1], g2[:, 0]], axis=1)
        land(in_flight, dX, "_wait")
        in_flight = [(ffn_flight, [("down", i), ("gate", i), ("up", i)], t + "scatter_ffn"),
                     (mix_flight, [("out", i), ("in", i)], t + "scatter_mix")]
    land(in_flight[:1], dX, "_wait")
    gS = {k: (jnp.stack(v) if isinstance(v, list) else v) for k, v in gS.items()}
    return loss, dX, pieces, gS, jnp.stack(dmods), early_all, in_flight[1]


FF_SHARD = D_FF // N_DEV
FF_SHARD_PAD = 384


def kernel(x, c, ctx, c_ctx, ada_w, ada_b, norm_w, even_w_in, even_w_out, ret_decay_logit, conv_dw_w, conv_ln_w, conv_ln_b, odd_w_in, odd_w_out, pool_w, pool_scale, sg_ln_w, sg_ln_b, sg_w, sg_b, ffn_w_gate, ffn_w_up, ffn_w_down, final_norm_w, loss_target, m_c_ctx, m_ada_w, m_ada_b, m_norm_w, m_even_w_in, m_even_w_out, m_ret_decay_logit, m_conv_dw_w, m_conv_ln_w, m_conv_ln_b, m_odd_w_in, m_odd_w_out, m_pool_w, m_pool_scale, m_sg_ln_w, m_sg_ln_b, m_sg_w, m_sg_b, m_ffn_w_gate, m_ffn_w_up, m_ffn_w_down, m_final_norm_w, v_c_ctx, v_ada_w, v_ada_b, v_norm_w, v_even_w_in, v_even_w_out, v_ret_decay_logit, v_conv_dw_w, v_conv_ln_w, v_conv_ln_b, v_odd_w_in, v_odd_w_out, v_pool_w, v_pool_scale, v_sg_ln_w, v_sg_ln_b, v_sg_w, v_sg_b, v_ffn_w_gate, v_ffn_w_up, v_ffn_w_down, v_final_norm_w):
    seq, n_ctx = x.shape[1], ctx.shape[1]
    me = 4 * lax.axis_index("x") + 2 * lax.axis_index("y") + lax.axis_index("c")
    mcols = ada_w.shape[2]

    tr = lambda a: jnp.swapaxes(a, 1, 2)
    pad_rows = lambda a: jnp.pad(a, ((0, FF_SHARD_PAD - FF_SHARD), (0, 0))).astype(BF16)
    gate_t, up_t, odd_in_t = tr(ffn_w_gate), tr(ffn_w_up), tr(odd_w_in)
    shards = []
    for i in range(DEPTH):
        j, even = i // 2, i % 2 == 0
        w_in, w_out = (even_w_in[j], even_w_out[j]) if even else (odd_in_t[j], odd_w_out[j])
        shards.append({"in": w_in.astype(BF16), "out": w_out.astype(BF16), "gate": pad_rows(gate_t[i]),
                       "up": pad_rows(up_t[i]), "down": pad_rows(ffn_w_down[i])})
    gather0, token = _xfer_start(_layer_gather(shards[0], True)[:2], "gather_mix0_start")

    small_shapes = [(D,), norm_w.shape, conv_dw_w.shape, pool_scale.shape, sg_ln_w.shape, sg_ln_b.shape]
    (sm,) = _exchange([("ag_blk", _pack([c + token[0, 0], norm_w, conv_dw_w, pool_scale, sg_ln_w, sg_ln_b], 8))],
                      "gather_small")
    c_all, nw_s, cw_s, ps_s, slw_s, slb_s = _unpack(sm, small_shapes)
    cat_last = lambda a: jnp.moveaxis(a, 0, -2).reshape(a.shape[1:-1] + (-1,))
    conv_w_full = cat_last(cw_s)
    S = {"norm_w": cat_last(nw_s), "lg": jax.nn.log_sigmoid(ret_decay_logit),
         "conv_w": jnp.pad(conv_w_full, ((0, 0), (0, 32 - CONV_K), (0, 0))),
         "conv_ln_w": conv_ln_w, "conv_ln_b": conv_ln_b, "pool_w": pool_w, "pool_scale": cat_last(ps_s),
         "sg_ln_w": cat_last(slw_s), "sg_ln_b": cat_last(slb_s), "sg_w": sg_w, "sg_b": sg_b,
         "final_norm_w": final_norm_w}

    cs = jnp.concatenate([c_all, c_ctx[None], jnp.zeros((7, D), F32)], axis=0)
    ab_loc = lax.dynamic_slice_in_dim(ada_b, me * mcols, mcols, axis=1)
    mod_loc = _ada_fwd(cs, ada_w, ab_loc[:, None, :], "ada_fwd")
    (mod_all,) = _exchange([("ag_blk", mod_loc.reshape(DEPTH * 16, mcols))], "gather_mod")
    mod_all = mod_all.reshape(N_DEV, DEPTH, 16, mcols).transpose(1, 2, 0, 3).reshape(DEPTH, 16, 6, D)
    mod_x = lax.dynamic_index_in_dim(mod_all, me, axis=1, keepdims=False)
    mods = jnp.stack([mod_x, mod_all[:, 8]], axis=1)

    X0 = jnp.concatenate([x[0], ctx[0]], axis=0)
    loss, dX, big, gS, dmods, early_all, late = _sample_step(X0, loss_target[0], mods, shards, gather0, S, seq,
                                                             n_ctx)
    loss = lax.psum(loss, ("x", "y", "c"))
    grad_x = dX[:seq][None]

    (dm_all,) = _exchange([("ag_blk", dmods.reshape(DEPTH * 2, 6 * D))], "gather_dmod")
    dm_all = dm_all.reshape(N_DEV, DEPTH, 2, 6 * D)
    dm_sum = _sum_devices(dm_all.reshape(N_DEV, DEPTH * 2, 6 * D), "sum_dmod").reshape(DEPTH, 2, 6 * D)
    g_ada_b = dm_sum[:, 0] + dm_sum[:, 1]
    G = lax.dynamic_slice_in_dim(dm_all, me * mcols, mcols, axis=3).transpose(1, 2, 0, 3).reshape(DEPTH, 16, mcols)
    g_ada_w, ccp = _ada_bwd(cs, G, ada_w, "ada_bwd")
    sg_cc = _sigmoid(c_ctx)
    g_cctx_part = ccp[0] * sg_cc * (1.0 + c_ctx * (1.0 - sg_cc))

    dsig = _sigmoid(-ret_decay_logit)
    part = [g_cctx_part, gS["norm_w"], gS["lg"] * dsig, gS["conv_w"][:, :CONV_K], gS["conv_ln_w"], gS["conv_ln_b"],
            gS["pool_scale"], gS["sg_ln_w"], gS["sg_ln_b"], gS["sg_b"], gS["final_norm_w"]]
    part_shapes = [a.shape for a in part]
    (part_all,) = _exchange([("ag_blk", _pack(part, 8))], "gather_small_grads")
    red = _sum_devices(part_all, "sum_small_grads")
    g_cctx, g_nw, g_rdl, g_cw, g_clw, g_clb, g_ps, g_slw, g_slb, g_sb, g_fnw = _unpack(red, part_shapes)
    g_pw, g_sw = _unpack(_sum_devices(early_all, "sum_early_grads"), [pool_w.shape, sg_w.shape])
    mine_last = lambda a, n: lax.dynamic_slice_in_dim(a, me * n, n, axis=a.ndim - 1)
    g_nw, g_cw = mine_last(g_nw, norm_w.shape[2]), mine_last(g_cw, conv_dw_w.shape[2])
    g_ps, g_slw, g_slb = (mine_last(a, pool_scale.shape[1]) for a in (g_ps, g_slw, g_slb))

    def adam_big(key, layers, w, m, v, name):
        return _adamw_layers(w, [big[(key, l)] for l in layers], m, v, name)

    res = {}
    adam_t = lambda key, layers, w, m, v, name: [tr(o) for o in adam_big(key, layers, tr(w), tr(m), tr(v), name)]
    res["odd_w_in"] = adam_t("in", (1, 3), odd_w_in, m_odd_w_in, v_odd_w_in, "adam_odd_in")
    res["odd_w_out"] = adam_big("out", (1, 3), odd_w_out, m_odd_w_out, v_odd_w_out, "adam_odd_out")
    res["ffn_w_gate"] = adam_t("gate", (0, 1, 2, 3), ffn_w_gate, m_ffn_w_gate, v_ffn_w_gate, "adam_gate")
    res["ffn_w_up"] = adam_t("up", (0, 1, 2, 3), ffn_w_up, m_ffn_w_up, v_ffn_w_up, "adam_up")
    res["ffn_w_down"] = adam_big("down", (0, 1, 2, 3), ffn_w_down, m_ffn_w_down, v_ffn_w_down, "adam_down")
    late_handle, late_keys, late_name = late
    for key, got in zip(late_keys, _xfer_wait(late_handle, res["ffn_w_down"][1], late_name + "_wait")):
        big[key] = got
    res["even_w_in"] = adam_big("in", (0, 2), even_w_in, m_even_w_in, v_even_w_in, "adam_even_in")
    res["even_w_out"] = adam_big("out", (0, 2), even_w_out, m_even_w_out, v_even_w_out, "adam_even_out")
    flat2 = lambda a: a.reshape(-1, a.shape[-1])
    res["ada_w"] = [o.reshape(ada_w.shape) for o in
                    _adamw(flat2(ada_w), flat2(g_ada_w), flat2(m_ada_w), flat2(v_ada_w), "adam_ada_w")]

    names = ["c_ctx", "ada_b", "norm_w", "ret_decay_logit", "conv_dw_w", "conv_ln_w", "conv_ln_b", "pool_w",
             "pool_scale", "sg_ln_w", "sg_ln_b", "sg_w", "sg_b", "final_norm_w"]
    ws = [c_ctx, ada_b, norm_w, ret_decay_logit, conv_dw_w, conv_ln_w, conv_ln_b, pool_w, pool_scale, sg_ln_w,
          sg_ln_b, sg_w, sg_b, final_norm_w]
    gs = [g_cctx, g_ada_b, g_nw, g_rdl, g_cw, g_clw, g_clb, g_pw, g_ps, g_slw, g_slb, g_sw, g_sb, g_fnw]
    ms = [m_c_ctx, m_ada_b, m_norm_w, m_ret_decay_logit, m_conv_dw_w, m_conv_ln_w, m_conv_ln_b, m_pool_w,
          m_pool_scale, m_sg_ln_w, m_sg_ln_b, m_sg_w, m_sg_b, m_final_norm_w]
    vs = [v_c_ctx, v_ada_b, v_norm_w, v_ret_decay_logit, v_conv_dw_w, v_conv_ln_w, v_conv_ln_b, v_pool_w,
          v_pool_scale, v_sg_ln_w, v_sg_ln_b, v_sg_w, v_sg_b, v_final_norm_w]
    shapes = [a.shape for a in ws]
    gs = [g.reshape(s) for g, s in zip(gs, shapes)]
    pk = lambda arrs: _pack(arrs, 512)
    outs = _adamw(pk(ws), pk(gs), pk(ms), pk(vs), "adam_small")
    for k, o in enumerate(outs):
        for nm, arr in zip(names, _unpack(o, shapes)):
            res.setdefault(nm, [None] * 4)[k] = arr

    order = ["c_ctx", "ada_w", "ada_b", "norm_w", "even_w_in", "even_w_out", "ret_decay_logit", "conv_dw_w",
             "conv_ln_w", "conv_ln_b", "odd_w_in", "odd_w_out", "pool_w", "pool_scale", "sg_ln_w", "sg_ln_b",
             "sg_w", "sg_b", "ffn_w_gate", "ffn_w_up", "ffn_w_down", "final_norm_w"]
    return (loss, grad_x, *[res[n][0] for n in order], *[res[n][1] for n in order],
            *[res[n][2] for n in order], *[res[n][3] for n in order])
```

```python
import math

import jax
import jax.numpy as jnp
from jax import lax
from jax.experimental import pallas as pl
from jax.experimental.pallas import tpu as pltpu

F32 = jnp.float32
BF16 = jnp.bfloat16

N_DEV = 8
D = 1024
DEPTH = 4
ROW_TILE = 256
HALO = 16
LANES = 128
EPS = 1e-6
RET_HEADS = 4
HEAD_DIM = 128
RET_W = RET_HEADS * HEAD_DIM
CHUNK = 128
Q_SCALE = HEAD_DIM ** -0.5
ROPE_BASE = 10000.0
ROPE_PAIRS = (HEAD_DIM // 8, 3 * HEAD_DIM // 16, 3 * HEAD_DIM // 16)
GRID_W = 64
CONV_CH = 512
CONV_K = 31
EVEN_IN = 4 * RET_W + 2 * CONV_CH
POOL_CH = 512
POOL_WINDOWS = (2, 4, 8, 16)
GROUP_CH = 128
SG_CH = 512
SG_CHUNK = 128
ODD_IN = POOL_CH + 2 * SG_CH
D_FF = 2816
INV_SQRT2 = 1.0 / math.sqrt(2.0)
INV_SQRT_2PI = 1.0 / math.sqrt(2.0 * math.pi)
ADAM_LR, ADAM_B1, ADAM_B2, ADAM_EPS, ADAM_WD, ADAM_STEP = 0.001, 0.9, 0.999, 1e-08, 0.01, 10
ADAM_BC1 = 1.0 - ADAM_B1 ** ADAM_STEP
ADAM_BC2 = 1.0 - ADAM_B2 ** ADAM_STEP
VMEM_LIMIT = 56 * 1024 * 1024
MESH = pl.DeviceIdType.MESH


def _params(sem=None):
    return pltpu.CompilerParams(dimension_semantics=sem, vmem_limit_bytes=VMEM_LIMIT)


def _dot(a, b):
    return jnp.dot(a, b, preferred_element_type=F32)


def _dot_nt(a, b):
    return lax.dot_general(a, b, (((1,), (1,)), ((), ())), preferred_element_type=F32)


def _dot_tn(a, b):
    return lax.dot_general(a, b, (((0,), (0,)), ((), ())), preferred_element_type=F32)


def _sigmoid(x):
    return 1.0 / (1.0 + jnp.exp(-x))


def _colsum(x):
    return jnp.sum(x, axis=0, keepdims=True)


def _rowmean(x):
    return jnp.mean(x, axis=-1, keepdims=True)


def _row(shape_cols, tile=ROW_TILE):
    return pl.BlockSpec((tile, shape_cols), lambda i: (i, 0))


def _mm_tile(T, cands):
    return next(t for t in cands if T % t == 0)


MM_ROWS = (544, 512, ROW_TILE)
TN_ROWS = (1088, 1024, 512, ROW_TILE)


def _const(shape):
    nd = len(shape)
    return pl.BlockSpec(shape, lambda i: (0,) * nd)


def _mod_spec(nxt):
    return pl.BlockSpec((1, 6, D), lambda i: (i // nxt, 0, 0))


def _stat_spec(nxt):
    return pl.BlockSpec((1, 8, D), lambda i: (i // nxt, 0, 0))


def _xfer_out_shape(kind, a):
    r, c = a.shape
    shape = {"ag_blk": (N_DEV, r, c), "ag_cols": (r, N_DEV * c), "ag_rows": (N_DEV * r, c),
             "a2a_cols": (N_DEV, r, c // N_DEV), "a2a_rows": (N_DEV, r // N_DEV, c)}[kind]
    return jax.ShapeDtypeStruct(shape, a.dtype)


def _src_view(kind, ref, who):
    if kind == "a2a_cols":
        n = ref.shape[1] // N_DEV
        return ref.at[:, pl.ds(pl.multiple_of(who * n, LANES), n)]
    if kind == "a2a_rows":
        r = ref.shape[0] // N_DEV
        return ref.at[pl.ds(pl.multiple_of(who * r, 16), r), :]
    return ref


def _dst_view(kind, ref, who):
    if kind == "ag_cols":
        n = ref.shape[1] // N_DEV
        return ref.at[:, pl.ds(pl.multiple_of(who * n, LANES), n)]
    if kind == "ag_rows":
        r = ref.shape[0] // N_DEV
        return ref.at[pl.ds(pl.multiple_of(who * r, 16), r), :]
    return ref.at[who]


def _xfers(kinds, srcs, dsts, send_sems, recv_sems, local_sems, start):
    mx, my, mc = lax.axis_index("x"), lax.axis_index("y"), lax.axis_index("c")
    me = 4 * mx + 2 * my + mc
    peers = []
    for k in range(1, N_DEV):
        px, py, pc = mx ^ ((k >> 2) & 1), my ^ ((k >> 1) & 1), mc ^ (k & 1)
        peers.append(((px, py, pc), 4 * px + 2 * py + pc))
    for j, (kind, src, dst) in enumerate(zip(kinds, srcs, dsts)):
        own = pltpu.make_async_copy(_src_view(kind, src, me), _dst_view(kind, dst, me), local_sems.at[j])
        if start:
            own.start()
        for k, (pid, pidx) in enumerate(peers):
            sem = (N_DEV - 1) * j + k
            cp = pltpu.make_async_remote_copy(
                src_ref=_src_view(kind, src, pidx), dst_ref=_dst_view(kind, dst, me if start else pidx),
                send_sem=send_sems.at[sem], recv_sem=recv_sems.at[sem], device_id=pid, device_id_type=MESH)
            if start:
                cp.start()
            else:
                cp.wait_recv()
                cp.wait_send()
        if not start:
            own.wait()


def _xfer_sems(n):
    return [pltpu.SemaphoreType.DMA(((N_DEV - 1) * n,)), pltpu.SemaphoreType.DMA(((N_DEV - 1) * n,)),
            pltpu.SemaphoreType.DMA((n,))]


def _exchange(comm, name):
    kinds = [k for k, _ in comm]
    n = len(comm)

    def body(*refs):
        sems = refs[2 * n:]
        _xfers(kinds, refs[:n], refs[n:2 * n], *sems, start=True)
        _xfers(kinds, refs[:n], refs[n:2 * n], *sems, start=False)

    hbm = pl.BlockSpec(memory_space=pl.ANY)
    return pl.pallas_call(
        body, name=name, in_specs=[hbm] * n, out_specs=[hbm] * n,
        out_shape=[_xfer_out_shape(k, a) for k, a in comm], scratch_shapes=_xfer_sems(n),
    )(*[a for _, a in comm])


_HBM = pl.BlockSpec(memory_space=pltpu.HBM)
_SEM = pl.BlockSpec(memory_space=pltpu.SEMAPHORE)
_EFFECT = pltpu.SideEffectType.DATAFLOW_SIDE_EFFECTING


def _mesh_peers():
    mx, my, mc = lax.axis_index("x"), lax.axis_index("y"), lax.axis_index("c")
    peers = []
    for k in range(1, N_DEV):
        px, py, pc = mx ^ ((k >> 2) & 1), my ^ ((k >> 1) & 1), mc ^ (k & 1)
        peers.append(((px, py, pc), 4 * px + 2 * py + pc))
    return 4 * mx + 2 * my + mc, peers


def _xfer_start(comm, name):
    kinds = [k for k, _ in comm]
    n = len(comm)
    srcs = [pltpu.with_memory_space_constraint(a, pltpu.HBM) for _, a in comm]
    lands = []
    for k, a in comm:
        sd = _xfer_out_shape(k, a)
        lands.append(pltpu.with_memory_space_constraint(lax.empty(sd.shape, sd.dtype), pltpu.HBM))

    def body(*refs):
        src_refs, land_refs = refs[:n], refs[n:2 * n]
        send_sems, recv_sems, local_sems, token = refs[2 * n], refs[2 * n + 1], refs[2 * n + 2], refs[-1]
        my, peers = _mesh_peers()
        for j, kind in enumerate(kinds):
            pltpu.make_async_copy(_src_view(kind, src_refs[j], my), _dst_view(kind, land_refs[j], my),
                                  local_sems.at[j]).start()
            for k, (pid, pidx) in enumerate(peers):
                sem = (N_DEV - 1) * j + k
                pltpu.make_async_remote_copy(
                    src_ref=_src_view(kind, src_refs[j], pidx), dst_ref=_dst_view(kind, land_refs[j], my),
                    send_sem=send_sems.at[sem], recv_sem=recv_sems.at[sem], device_id=pid,
                    device_id_type=MESH).start()
        token[...] = jnp.zeros_like(token)

    sems = pltpu.SemaphoreType.DMA(((N_DEV - 1) * n,))
    outs = pl.pallas_call(
        body, name=name,
        out_shape=(sems, sems, pltpu.SemaphoreType.DMA((n,)), *[pltpu.HBM(a.shape, a.dtype) for a in srcs + lands],
                   jax.ShapeDtypeStruct((8, LANES), F32)),
        in_specs=[_HBM] * (2 * n),
        out_specs=(_SEM, _SEM, _SEM, *[_HBM] * (2 * n), pl.BlockSpec(memory_space=pltpu.VMEM)),
        input_output_aliases={k: 3 + k for k in range(2 * n)},
        compiler_params=pltpu.CompilerParams(has_side_effects=_EFFECT),
    )(*srcs, *lands)
    return (kinds, outs[0], outs[1], outs[2], list(outs[3:3 + 2 * n])), outs[-1]


def _xfer_wait(handle, after, name):
    kinds, send_sems, recv_sems, local_sems, bufs = handle
    n = len(kinds)

    def body(*refs):
        src_refs, land_refs = refs[:n], refs[n:2 * n]
        send_ref, recv_ref, local_ref = refs[2 * n], refs[2 * n + 1], refs[2 * n + 2]
        my, peers = _mesh_peers()
        for j, kind in enumerate(kinds):
            pltpu.make_async_copy(_src_view(kind, src_refs[j], my), _dst_view(kind, land_refs[j], my),
                                  local_ref.at[j]).wait()
            for k, (pid, pidx) in enumerate(peers):
                sem = (N_DEV - 1) * j + k
                cp = pltpu.make_async_remote_copy(
                    src_ref=_src_view(kind, src_refs[j], pidx), dst_ref=_dst_view(kind, land_refs[j], pidx),
                    send_sem=send_ref.at[sem], recv_sem=recv_ref.at[sem], device_id=pid, device_id_type=MESH)
                cp.wait_send()
                cp.wait_recv()

    outs = pl.pallas_call(
        body, name=name, out_shape=[pltpu.HBM(a.shape, a.dtype) for a in bufs],
        in_specs=[_HBM] * (2 * n) + [_SEM, _SEM, _SEM, pl.BlockSpec(memory_space=pl.ANY)],
        out_specs=[_HBM] * (2 * n), input_output_aliases={k: k for k in range(2 * n)},
        compiler_params=pltpu.CompilerParams(has_side_effects=_EFFECT),
    )(*bufs, send_sems, recv_sems, local_sems, after)
    return list(outs[n:])


def _call(body, *, grid, in_specs, out_specs, out_shape, args, name, sem, scratch=(), comm=()):
    n_in, n_out, n_scr, n_c = len(in_specs), len(out_specs), len(scratch), len(comm)
    if not comm:
        outs = pl.pallas_call(body, grid=grid, in_specs=list(in_specs), out_specs=list(out_specs),
                              out_shape=list(out_shape), scratch_shapes=list(scratch), name=name,
                              compiler_params=_params(sem))(*args)
        return list(outs), []
    kinds = [k for k, _ in comm]
    n_steps = grid[0]

    def wrapped(*refs):
        ins, csrc = refs[:n_in], refs[n_in:n_in + n_c]
        o0 = n_in + n_c
        outs, cdst = refs[o0:o0 + n_out], refs[o0 + n_out:o0 + n_out + n_c]
        s0 = o0 + n_out + n_c
        scr, sems = refs[s0:s0 + n_scr], refs[s0 + n_scr:]
        i = pl.program_id(0)

        @pl.when(i == 0)
        def _():
            _xfers(kinds, csrc, cdst, *sems, start=True)

        body(*ins, *outs, *scr)

        @pl.when(i == n_steps - 1)
        def _():
            _xfers(kinds, csrc, cdst, *sems, start=False)

    hbm = pl.BlockSpec(memory_space=pl.ANY)
    outs = pl.pallas_call(
        wrapped, grid=grid, in_specs=list(in_specs) + [hbm] * n_c, out_specs=list(out_specs) + [hbm] * n_c,
        out_shape=list(out_shape) + [_xfer_out_shape(k, a) for k, a in comm],
        scratch_shapes=list(scratch) + _xfer_sems(n_c), name=name, compiler_params=_params(("arbitrary",)),
    )(*args, *[a for _, a in comm])
    return list(outs[:n_out]), list(outs[n_out:])


def _norm_mod_fwd(X, nw, mods, si, nxt, name):
    T = X.shape[0]

    def body(x_ref, w_ref, m_ref, h_ref):
        x = x_ref[...]
        r = lax.rsqrt(_rowmean(x * x) + EPS)
        m = m_ref[0]
        y = x * r * w_ref[...]
        h_ref[...] = (y * (1.0 + m[si + 1:si + 2, :]) + m[si:si + 1, :]).astype(BF16)

    return pl.pallas_call(
        body, grid=(T // ROW_TILE,), name=name,
        in_specs=[_row(D), _const((1, D)), _mod_spec(nxt)],
        out_specs=_row(D), out_shape=jax.ShapeDtypeStruct((T, D), BF16),
        compiler_params=_params(("parallel",)),
    )(X, nw, mods)


def _norm_mod_bwd(dh, X, nw, mods, dres, si, nxt, name):
    T = X.shape[0]

    def body(dh_ref, x_ref, w_ref, m_ref, dres_ref, dx_ref, st_ref):
        i = pl.program_id(0)

        @pl.when((i == 0) | (i == nxt))
        def _():
            st_ref[...] = jnp.zeros_like(st_ref)

        x = x_ref[...]
        r = lax.rsqrt(_rowmean(x * x) + EPS)
        xn = x * r
        w = w_ref[...]
        scale = m_ref[0][si + 1:si + 2, :]
        dhv = dh_ref[...]
        dy = dhv * (1.0 + scale)
        dxn = dy * w
        dx_ref[...] = dres_ref[...] + r * (dxn - xn * _rowmean(dxn * xn))
        st_ref[0, 0:1, :] += _colsum(dhv)
        st_ref[0, 1:2, :] += _colsum(dhv * xn * w)
        st_ref[0, 2:3, :] += _colsum(dy * xn)

    return pl.pallas_call(
        body, grid=(T // ROW_TILE,), name=name,
        in_specs=[_row(D), _row(D), _const((1, D)), _mod_spec(nxt), _row(D)],
        out_specs=[_row(D), _stat_spec(nxt)],
        out_shape=[jax.ShapeDtypeStruct((T, D), F32), jax.ShapeDtypeStruct((2, 8, D), F32)],
        compiler_params=_params(("arbitrary",)),
    )(dh, X, nw, mods, dres)


def _resid_bwd(dX, y, mods, gi, nxt, name):
    T = dX.shape[0]

    def body(dx_ref, y_ref, m_ref, dy_ref, st_ref):
        i = pl.program_id(0)

        @pl.when((i == 0) | (i == nxt))
        def _():
            st_ref[...] = jnp.zeros_like(st_ref)

        dx = dx_ref[...]
        dy_ref[...] = (dx * m_ref[0][gi:gi + 1, :]).astype(BF16)
        st_ref[0, 0:1, :] += _colsum(dx * y_ref[...].astype(F32))

    return pl.pallas_call(
        body, grid=(T // ROW_TILE,), name=name,
        in_specs=[_row(D), _row(D), _mod_spec(nxt)],
        out_specs=[_row(D), _stat_spec(nxt)],
        out_shape=[jax.ShapeDtypeStruct((T, D), BF16), jax.ShapeDtypeStruct((2, 8, D), F32)],
        compiler_params=_params(("arbitrary",)),
    )(dX, y, mods)


def _loss_head(X, tgt, fw, nxt, name):
    T = X.shape[0]

    def body(x_ref, t_ref, w_ref, dx_ref, st_ref):
        i = pl.program_id(0)

        @pl.when(i == 0)
        def _():
            st_ref[...] = jnp.zeros_like(st_ref)

        @pl.when(i < nxt)
        def _():
            x = x_ref[...]
            r = lax.rsqrt(_rowmean(x * x) + EPS)
            xn = x * r
            w = w_ref[...]
            err = xn * w - t_ref[...]
            dy = err * (1.0 / D)
            dxn = dy * w
            dx_ref[...] = r * (dxn - xn * _rowmean(dxn * xn))
            st_ref[0:1, :] += _colsum(dy * xn)
            st_ref[1:2, :] += (0.5 / D) * _colsum(err * err)

        @pl.when(i >= nxt)
        def _():
            dx_ref[...] = jnp.zeros_like(dx_ref)

    return pl.pallas_call(
        body, grid=(T // ROW_TILE,), name=name,
        in_specs=[_row(D), pl.BlockSpec((ROW_TILE, D), lambda i: (jnp.minimum(i, nxt - 1), 0)), _const((1, D))],
        out_specs=[_row(D), _const((8, D))],
        out_shape=[jax.ShapeDtypeStruct((T, D), F32), jax.ShapeDtypeStruct((8, D), F32)],
        compiler_params=_params(("arbitrary",)),
    )(X, tgt, fw)


def _mm_nn(A, B, out_dtype, name, add=None, comm=()):
    T, K = A.shape
    N = B.shape[1]

    tm = _mm_tile(T, MM_ROWS)
    if add is None:
        def body(a_ref, b_ref, o_ref):
            o_ref[...] = _dot(a_ref[...], b_ref[...]).astype(out_dtype)
        ins, specs = (A, B), [_row(K, tm), _const((K, N))]
    else:
        def body(a_ref, b_ref, c_ref, o_ref):
            o_ref[...] = (c_ref[...] + _dot(a_ref[...], b_ref[...])).astype(out_dtype)
        ins, specs = (A, B, add), [_row(K, tm), _const((K, N)), _row(N, tm)]

    return _call(body, grid=(T // tm,), name=name, in_specs=specs,
                 out_specs=[_row(N, tm)], out_shape=[jax.ShapeDtypeStruct((T, N), out_dtype)],
                 sem=("parallel",), args=ins, comm=comm)


def _mm_nn_resid(A, B, X, mods, gi, seq, name, comm=()):
    T, K = A.shape
    N = B.shape[1]
    tm = _mm_tile(T, MM_ROWS)

    def body(a_ref, b_ref, x_ref, m_ref, xo_ref, y_ref):
        acc = _dot(a_ref[...], b_ref[...])
        y_ref[...] = acc.astype(BF16)
        row = pl.program_id(0) * tm + lax.broadcasted_iota(jnp.int32, (tm, 1), 0)
        gate = jnp.where(row >= seq, m_ref[1][gi:gi + 1, :], m_ref[0][gi:gi + 1, :])
        xo_ref[...] = x_ref[...] + gate * acc

    return _call(body, grid=(T // tm,), name=name,
                 in_specs=[_row(K, tm), _const((K, N)), _row(N, tm), _const((2, 6, D))],
                 out_specs=[_row(N, tm), _row(N, tm)],
                 out_shape=[jax.ShapeDtypeStruct((T, N), F32), jax.ShapeDtypeStruct((T, N), BF16)],
                 sem=("parallel",), args=(A, B, X, mods), comm=comm)


def _mm_nt(A, B, name, comm=()):
    T, N = A.shape
    K = B.shape[0]
    tm = _mm_tile(T, MM_ROWS)

    def body(a_ref, b_ref, o_ref):
        o_ref[...] = _dot_nt(a_ref[...], b_ref[...])

    return _call(body, grid=(T // tm,), name=name, in_specs=[_row(N, tm), _const((K, N))], out_specs=[_row(K, tm)],
                 out_shape=[jax.ShapeDtypeStruct((T, K), F32)], sem=("parallel",), args=(A, B), comm=comm)


def _mm_tn(A, G, name):
    T, K = A.shape
    N = G.shape[1]
    tm = _mm_tile(T, TN_ROWS)
    nt = T // tm

    def body(a_ref, g_ref, o_ref, acc_ref):
        i = pl.program_id(0)

        @pl.when(i == 0)
        def _():
            acc_ref[...] = jnp.zeros_like(acc_ref)

        acc_ref[...] += _dot_tn(a_ref[...], g_ref[...])

        @pl.when(i == nt - 1)
        def _():
            o_ref[...] = acc_ref[...].astype(BF16)

    return pl.pallas_call(
        body, grid=(nt,), name=name,
        in_specs=[_row(K, tm), _row(N, tm)],
        out_specs=_const((K, N)), out_shape=jax.ShapeDtypeStruct((K, N), BF16),
        scratch_shapes=[pltpu.VMEM((K, N), F32)],
        compiler_params=_params(("arbitrary",)),
    )(A, G)


def _ffn_up(h, WgT, WuT, name, comm=()):
    T = h.shape[0]
    F = WgT.shape[0]

    def body(h_ref, wg_ref, wu_ref, gp_ref, up_ref, act_ref):
        hv = h_ref[...]
        gp = _dot_nt(hv, wg_ref[...])
        up = _dot_nt(hv, wu_ref[...])
        gp_ref[...] = gp.astype(BF16)
        up_ref[...] = up.astype(BF16)
        act_ref[...] = (gp * _sigmoid(gp) * up).astype(BF16)

    sd = jax.ShapeDtypeStruct((T, F), BF16)
    return _call(body, grid=(T // ROW_TILE,), name=name, in_specs=[_row(D), _const((F, D)), _const((F, D))],
                 out_specs=[_row(F), _row(F), _row(F)], out_shape=[sd, sd, sd], sem=("parallel",),
                 args=(h, WgT, WuT), comm=comm)


def _ffn_dact(dy, Wd, gp, up, name, comm=()):
    T = dy.shape[0]
    F = Wd.shape[0]

    def body(dy_ref, wd_ref, gp_ref, up_ref, dgp_ref, dup_ref):
        dact = _dot_nt(dy_ref[...], wd_ref[...])
        gpv = gp_ref[...].astype(F32)
        upv = up_ref[...].astype(F32)
        sg = _sigmoid(gpv)
        dup_ref[...] = (dact * gpv * sg).astype(BF16)
        dgp_ref[...] = (dact * upv * sg * (1.0 + gpv * (1.0 - sg))).astype(BF16)

    sd = jax.ShapeDtypeStruct((T, F), BF16)
    return _call(body, grid=(T // ROW_TILE,), name=name, in_specs=[_row(D), _const((F, D)), _row(F), _row(F)],
                 out_specs=[_row(F), _row(F)], out_shape=[sd, sd], sem=("parallel",),
                 args=(dy, Wd, gp, up), comm=comm)


N_TAB = 7


def _ret_tables(tab_ref, lgf, lgb):
    r = lax.broadcasted_iota(jnp.int32, (CHUNK, CHUNK), 0).astype(F32)
    c = lax.broadcasted_iota(jnp.int32, (CHUNK, CHUNK), 1).astype(F32)
    for d, lg in ((0, lgf), (1, lgb)):
        if d == 0:
            mask, expo, xe, ze = r >= c, r - c, r + 1.0, (CHUNK - 1.0) - r
        else:
            mask, expo, xe, ze = c > r, c - r - 1.0, (CHUNK - 1.0) - r, r
        e = jnp.where(mask, expo, 0.0)
        tab_ref[N_TAB * d + 0] = jnp.where(mask, jnp.exp(lg * e), 0.0)
        tab_ref[N_TAB * d + 1] = jnp.exp(lg * xe)
        tab_ref[N_TAB * d + 2] = jnp.exp(lg * ze)
        tab_ref[N_TAB * d + 3] = jnp.exp(jnp.full((CHUNK, CHUNK), lg * float(CHUNK), F32))
        tab_ref[N_TAB * d + 4] = e
        tab_ref[N_TAB * d + 5] = xe
        tab_ref[N_TAB * d + 6] = ze


def _rope(t, cosf, sgn):
    return t * cosf + pltpu.roll(t, HEAD_DIM // 2, 1) * sgn


def _rope_t(d, cosf, sgn):
    return d * cosf + pltpu.roll(d * sgn, HEAD_DIM // 2, 1)


def _chunk_of(d, j, nc, ncx):
    return lax.rem(j + ncx, nc) if d == 0 else nc - 1 - j


def _head_spec(T, col0):
    return pl.BlockSpec((T, HEAD_DIM), lambda h: (0, col0 + h))


def _ret_fwd(p, lg, cosf, sgn, seq, name, comm=()):
    T = p.shape[0]
    nc, ncx = T // CHUNK, seq // CHUNK

    def body(lg_ref, q_ref, k_ref, v_ref, g_ref, cos_ref, sgn_ref, y_ref, ro_ref, tab_ref, of_ref):
        h = pl.program_id(0)
        _ret_tables(tab_ref, lg_ref[0, h], lg_ref[1, h])

        def load(cidx):
            rows = pl.ds(pl.multiple_of(cidx * CHUNK, CHUNK), CHUNK)
            cs, sn = cos_ref[rows, :], sgn_ref[rows, :]
            return rows, _rope(q_ref[rows, :], cs, sn) * Q_SCALE, _rope(k_ref[rows, :], cs, sn), v_ref[rows, :]

        def chunk(d, q, k, v, S):
            b = N_TAB * d
            kb, vb = k.astype(BF16), v.astype(BF16)
            pm = _dot_nt(q.astype(BF16), kb) * tab_ref[b]
            o = _dot(pm.astype(BF16), vb) + _dot((q * tab_ref[b + 1]).astype(BF16), S.astype(BF16))
            return o, S * tab_ref[b + 3] + _dot_tn((k * tab_ref[b + 2]).astype(BF16), vb)

        def step(j, carry):
            Sf, Sb = carry
            rows, q, k, v = load(_chunk_of(0, j, nc, ncx))
            o, Sf = chunk(0, q, k, v, Sf)
            y_ref[rows, :] = o
            rows, q, k, v = load(_chunk_of(1, j, nc, ncx))
            o, Sb = chunk(1, q, k, v, Sb)
            of_ref[rows, :] = o
            return Sf, Sb

        zero = jnp.zeros((CHUNK, CHUNK), F32)
        lax.fori_loop(0, nc, step, (zero, zero))

        def finish(cidx, carry):
            rows = pl.ds(pl.multiple_of(cidx * CHUNK, CHUNK), CHUNK)
            y = y_ref[rows, :] + of_ref[rows, :]
            y_ref[rows, :] = y
            g = g_ref[rows, :]
            ro_ref[rows, :] = (g * _sigmoid(g) * y * lax.rsqrt(_rowmean(y * y) + EPS)).astype(BF16)
            return carry

        lax.fori_loop(0, nc, finish, 0)

    tbl = pl.BlockSpec((T, HEAD_DIM), lambda h: (0, 0))
    return _call(
        body, grid=(RET_HEADS,), name=name,
        in_specs=[pl.BlockSpec(memory_space=pltpu.SMEM), _head_spec(T, 0), _head_spec(T, 4), _head_spec(T, 8),
                  _head_spec(T, 12), tbl, tbl],
        out_specs=[_head_spec(T, 0), _head_spec(T, 0)],
        out_shape=[jax.ShapeDtypeStruct((T, RET_W), F32), jax.ShapeDtypeStruct((T, RET_W), BF16)],
        scratch=[pltpu.VMEM((2 * N_TAB, CHUNK, CHUNK), F32), pltpu.VMEM((T, HEAD_DIM), F32)],
        sem=("arbitrary",), args=(lg, p, p, p, p, cosf, sgn), comm=comm)


def _ret_gate_bwd(yret, p, dmix, name):
    T = yret.shape[0]

    def body(y_ref, g_ref, dm_ref, dy_ref, dg_ref):
        for hh in range(RET_HEADS):
            sl = slice(hh * HEAD_DIM, (hh + 1) * HEAD_DIM)
            y = y_ref[:, sl]
            r = lax.rsqrt(_rowmean(y * y) + EPS)
            yn = y * r
            g = g_ref[:, sl]
            sg = _sigmoid(g)
            dro = dm_ref[:, sl]
            dg_ref[:, sl] = (dro * yn * sg * (1.0 + g * (1.0 - sg))).astype(BF16)
            dyn = dro * g * sg
            dy_ref[:, sl] = r * (dyn - yn * _rowmean(dyn * yn))

    return pl.pallas_call(
        body, grid=(T // ROW_TILE,), name=name,
        in_specs=[_row(RET_W), pl.BlockSpec((ROW_TILE, RET_W), lambda i: (i, 3)), _row(RET_W)],
        out_specs=[_row(RET_W), _row(RET_W)],
        out_shape=[jax.ShapeDtypeStruct((T, RET_W), F32), jax.ShapeDtypeStruct((T, RET_W), BF16)],
        compiler_params=_params(("parallel",)),
    )(yret, p, dmix)


def _ret_bwd(p, lg, dy, cosf, sgn, seq, name, comm=()):
    T = p.shape[0]
    nc, ncx = T // CHUNK, seq // CHUNK

    def body(lg_ref, q_ref, k_ref, v_ref, dy_ref, cos_ref, sgn_ref, dq_ref, dk_ref, dv_ref, dlg_ref,
             tab_ref, st_ref, dqs, dks, dvs):
        h = pl.program_id(0)
        _ret_tables(tab_ref, lg_ref[0, h], lg_ref[1, h])
        dlg_ref[...] = jnp.zeros_like(dlg_ref)

        def load(cidx):
            rows = pl.ds(pl.multiple_of(cidx * CHUNK, CHUNK), CHUNK)
            cs, sn = cos_ref[rows, :], sgn_ref[rows, :]
            return rows, _rope(q_ref[rows, :], cs, sn) * Q_SCALE, _rope(k_ref[rows, :], cs, sn), v_ref[rows, :]

        def states(j, carry):
            out = []
            for d, S in enumerate(carry):
                b = N_TAB * d
                _, _, k, v = load(_chunk_of(d, j, nc, ncx))
                st_ref[d, j] = S
                out.append(S * tab_ref[b + 3] + _dot_tn((k * tab_ref[b + 2]).astype(BF16), v.astype(BF16)))
            return tuple(out)

        zero = jnp.zeros((CHUNK, CHUNK), F32)
        lax.fori_loop(0, nc, states, (zero, zero))

        def sweep_one(d, j, dS, acc):
            b = N_TAB * d
            rows, q, k, v = load(_chunk_of(d, j, nc, ncx))
            dO = dy_ref[rows, :]
            Sp = st_ref[d, j]
            qb, kb, vb, dOb = q.astype(BF16), k.astype(BF16), v.astype(BF16), dO.astype(BF16)
            Spb, dSb = Sp.astype(BF16), dS.astype(BF16)
            dmat, xi, ze, cd = tab_ref[b], tab_ref[b + 1], tab_ref[b + 2], tab_ref[b + 3]
            pm = _dot_nt(qb, kb) * dmat
            dpm = _dot_nt(dOb, vb)
            acc = acc + _colsum(dpm * pm * tab_ref[b + 4])
            dsc = (dpm * dmat).astype(BF16)
            qx = (q * xi).astype(BF16)
            kz = (k * ze).astype(BF16)
            dq = _dot(dsc, kb) + _dot_nt(dOb, Spb) * xi
            dk = _dot_tn(dsc, qb) + _dot_nt(vb, dSb) * ze
            dvst = _dot(kz, dSb)
            dv = _dot_tn(pm.astype(BF16), dOb) + dvst
            inter = _dot(qx, Spb)
            acc = acc + _colsum(dO * inter * tab_ref[b + 5])
            acc = acc + float(CHUNK) * _colsum(dS * cd * Sp) + _colsum(v * dvst * tab_ref[b + 6])
            dqs[d, rows, :] = dq.astype(BF16)
            dks[d, rows, :] = dk.astype(BF16)
            dvs[d, rows, :] = dv.astype(BF16)
            return dS * cd + _dot_tn(qx, dOb), acc

        def sweep(jj, carry):
            dSf, accf, dSb, accb = carry
            j = nc - 1 - jj
            dSf, accf = sweep_one(0, j, dSf, accf)
            dSb, accb = sweep_one(1, j, dSb, accb)
            return dSf, accf, dSb, accb

        zacc = jnp.zeros((1, CHUNK), F32)
        _, accf, _, accb = lax.fori_loop(0, nc, sweep, (zero, zacc, zero, zacc))
        dlg_ref[0, 0:1, :] = jnp.zeros((1, LANES), F32) + jnp.sum(accf)
        dlg_ref[0, 1:2, :] = jnp.zeros((1, LANES), F32) + jnp.sum(accb)

        def finish(cidx, carry):
            rows = pl.ds(pl.multiple_of(cidx * CHUNK, CHUNK), CHUNK)
            cs, sn = cos_ref[rows, :], sgn_ref[rows, :]
            both = lambda ref: ref[0, rows, :].astype(F32) + ref[1, rows, :].astype(F32)
            dq_ref[rows, :] = (_rope_t(both(dqs), cs, sn) * Q_SCALE).astype(BF16)
            dk_ref[rows, :] = _rope_t(both(dks), cs, sn).astype(BF16)
            dv_ref[rows, :] = both(dvs).astype(BF16)
            return carry

        lax.fori_loop(0, nc, finish, 0)

    tbl = pl.BlockSpec((T, HEAD_DIM), lambda h: (0, 0))
    sd = jax.ShapeDtypeStruct((T, RET_W), BF16)
    return _call(
        body, grid=(RET_HEADS,), name=name,
        in_specs=[pl.BlockSpec(memory_space=pltpu.SMEM), _head_spec(T, 0), _head_spec(T, 4), _head_spec(T, 8),
                  _head_spec(T, 0), tbl, tbl],
        out_specs=[_head_spec(T, 0), _head_spec(T, 0), _head_spec(T, 0),
                   pl.BlockSpec((1, 8, LANES), lambda h: (h, 0, 0))],
        out_shape=[sd, sd, sd, jax.ShapeDtypeStruct((RET_HEADS, 8, LANES), F32)],
        scratch=[pltpu.VMEM((2 * N_TAB, CHUNK, CHUNK), F32), pltpu.VMEM((2, nc, CHUNK, CHUNK), F32),
                 pltpu.VMEM((2, T, HEAD_DIM), BF16), pltpu.VMEM((2, T, HEAD_DIM), BF16),
                 pltpu.VMEM((2, T, HEAD_DIM), BF16)],
        sem=("arbitrary",), args=(lg, p, p, p, dy, cosf, sgn), comm=comm)


def _halo_specs(T, cols, colblock):
    per = ROW_TILE // HALO
    last = T // HALO - 1
    prv = pl.BlockSpec((HALO, cols), lambda i: (jnp.maximum(i * per - 1, 0), colblock))
    nxt = pl.BlockSpec((HALO, cols), lambda i: (jnp.minimum((i + 1) * per, last), colblock))
    return prv, nxt


def _seq_edges(i, nxt, nt):
    first = (i == 0) | (i == nxt)
    last = (i == nxt - 1) | (i == nt - 1)
    return first, last


def _fill_pad(pad_ref, prv, cur, nxt, first, last):
    pad_ref[0:HALO, :] = jnp.where(first, 0.0, prv)
    pad_ref[HALO:HALO + ROW_TILE, :] = cur
    pad_ref[HALO + ROW_TILE:2 * HALO + ROW_TILE, :] = jnp.where(last, 0.0, nxt)


def _ln_stats(x):
    mu = _rowmean(x)
    xc = x - mu
    rs = lax.rsqrt(_rowmean(xc * xc) + EPS)
    return xc * rs, rs


def _conv_fwd(p, cw, lnw, lnb, nxt, name, comm=()):
    T = p.shape[0]
    nt = T // ROW_TILE
    a_col, g_col = 4, 5

    def body(a_ref, g_ref, ap_ref, gp_ref, an_ref, gn_ref, w_ref, lw_ref, lb_ref, cv_ref, co_ref, pad_ref):
        i = pl.program_id(0)
        first, last = _seq_edges(i, nxt, nt)
        glu = lambda a, g: a * _sigmoid(g)
        _fill_pad(pad_ref, glu(ap_ref[...], gp_ref[...]), glu(a_ref[...], g_ref[...]),
                  glu(an_ref[...], gn_ref[...]), first, last)
        acc = jnp.zeros((ROW_TILE, CONV_CH), F32)
        for k in range(CONV_K):
            acc = acc + w_ref[k:k + 1, :] * pad_ref[k + 1:k + 1 + ROW_TILE, :]
        cv_ref[...] = acc
        xh, _ = _ln_stats(acc)
        z = xh * lw_ref[...] + lb_ref[...]
        co_ref[...] = (z * _sigmoid(z)).astype(BF16)

    cur = lambda cb: pl.BlockSpec((ROW_TILE, CONV_CH), lambda i: (i, cb))
    ap, an = _halo_specs(T, CONV_CH, a_col)
    gp, gn = _halo_specs(T, CONV_CH, g_col)
    return _call(
        body, grid=(nt,), name=name,
        in_specs=[cur(a_col), cur(g_col), ap, gp, an, gn, _const((32, CONV_CH)), _const((1, CONV_CH)),
                  _const((1, CONV_CH))],
        out_specs=[_row(CONV_CH), _row(CONV_CH)],
        out_shape=[jax.ShapeDtypeStruct((T, CONV_CH), F32), jax.ShapeDtypeStruct((T, CONV_CH), BF16)],
        scratch=[pltpu.VMEM((ROW_TILE + 2 * HALO, CONV_CH), F32)],
        sem=("parallel",), args=(p, p, p, p, p, p, cw, lnw, lnb), comm=comm)


def _conv_bwd(p, cv, dmix, cw, lnw, lnb, nxt, name, comm=()):
    T = p.shape[0]
    nt = T // ROW_TILE
    a_col, g_col = 4, 5

    def body(a_ref, g_ref, ap_ref, gp_ref, an_ref, gn_ref, cv_ref, cvp_ref, cvn_ref, dc_ref, dcp_ref, dcn_ref,
             w_ref, lw_ref, lb_ref, dp_ref, dw_ref, dl_ref, upad_ref, dpad_ref):
        i = pl.program_id(0)
        first, last = _seq_edges(i, nxt, nt)

        @pl.when(i == 0)
        def _():
            dw_ref[...] = jnp.zeros_like(dw_ref)
            dl_ref[...] = jnp.zeros_like(dl_ref)

        lw, lb = lw_ref[...], lb_ref[...]

        def ln_bwd(cvv, dco):
            xh, rs = _ln_stats(cvv)
            z = xh * lw + lb
            sg = _sigmoid(z)
            dz = dco * sg * (1.0 + z * (1.0 - sg))
            dxh = dz * lw
            return rs * (dxh - _rowmean(dxh) - xh * _rowmean(dxh * xh)), dz, xh

        dcv, dz, xh = ln_bwd(cv_ref[...], dc_ref[...])
        _fill_pad(dpad_ref, ln_bwd(cvp_ref[...], dcp_ref[...])[0], dcv, ln_bwd(cvn_ref[...], dcn_ref[...])[0],
                  first, last)
        a, g = a_ref[...], g_ref[...]
        sg = _sigmoid(g)
        glu = lambda av, gv: av * _sigmoid(gv)
        _fill_pad(upad_ref, glu(ap_ref[...], gp_ref[...]), a * sg, glu(an_ref[...], gn_ref[...]), first, last)
        du = jnp.zeros((ROW_TILE, CONV_CH), F32)
        for k in range(CONV_K):
            du = du + w_ref[k:k + 1, :] * dpad_ref[CONV_K - k:CONV_K - k + ROW_TILE, :]
            dw_ref[k:k + 1, :] += _colsum(dcv * upad_ref[k + 1:k + 1 + ROW_TILE, :])
        dl_ref[0:1, :] += _colsum(dz * xh)
        dl_ref[1:2, :] += _colsum(dz)
        dp_ref[:, 0:CONV_CH] = (du * sg).astype(BF16)
        dp_ref[:, CONV_CH:2 * CONV_CH] = (du * a * sg * (1.0 - sg)).astype(BF16)

    cur = lambda cb: pl.BlockSpec((ROW_TILE, CONV_CH), lambda i: (i, cb))
    ap, an = _halo_specs(T, CONV_CH, a_col)
    gp, gn = _halo_specs(T, CONV_CH, g_col)
    cvp, cvn = _halo_specs(T, CONV_CH, 0)
    dcp, dcn = _halo_specs(T, CONV_CH, 1)
    return _call(
        body, grid=(nt,), name=name,
        in_specs=[cur(a_col), cur(g_col), ap, gp, an, gn, cur(0), cvp, cvn, cur(1), dcp, dcn,
                  _const((32, CONV_CH)), _const((1, CONV_CH)), _const((1, CONV_CH))],
        out_specs=[_row(2 * CONV_CH), _const((32, CONV_CH)), _const((8, CONV_CH))],
        out_shape=[jax.ShapeDtypeStruct((T, 2 * CONV_CH), BF16), jax.ShapeDtypeStruct((32, CONV_CH), F32),
                   jax.ShapeDtypeStruct((8, CONV_CH), F32)],
        scratch=[pltpu.VMEM((ROW_TILE + 2 * HALO, CONV_CH), F32), pltpu.VMEM((ROW_TILE + 2 * HALO, CONV_CH), F32)],
        sem=("arbitrary",), args=(p, p, p, p, p, p, cv, cv, cv, dmix, dmix, dmix, cw, lnw, lnb), comm=comm)


def _tile_positions(i, nxt, seq, ctx, offset, rows):
    is_ctx = i >= nxt
    pos0 = (i - jnp.where(is_ctx, nxt, 0)) * ROW_TILE + offset
    length = jnp.where(is_ctx, ctx, seq).astype(F32)
    pos = (pos0 + lax.broadcasted_iota(jnp.int32, (rows, 1), 0)).astype(F32)
    return pos, length


def _pool_count(pos, length, w):
    left = w // 2
    right = w - 1 - left
    return jnp.minimum(pos + right, length - 1.0) - jnp.maximum(pos - left, 0.0) + 1.0


def _gelu(x):
    return 0.5 * x * (1.0 + lax.erf(x * INV_SQRT2))


def _odd_fwd(p, pw, ps, slw, slb, sw, sbf, nxt, seq, ctx, name, comm=()):
    T = p.shape[0]
    nt = T // ROW_TILE

    def body(p_ref, pp_ref, pn_ref, pw_ref, ps_ref, lw_ref, lb_ref, sw_ref, sb_ref, o_ref, pad_ref):
        i = pl.program_id(0)
        first, last = _seq_edges(i, nxt, nt)
        _fill_pad(pad_ref, pp_ref[...], p_ref[:, 0:POOL_CH], pn_ref[...], first, last)
        pos, length = _tile_positions(i, nxt, seq, ctx, 0, ROW_TILE)
        for gi, w in enumerate(POOL_WINDOWS):
            sl = slice(gi * GROUP_CH, (gi + 1) * GROUP_CH)
            left = w // 2
            ssum = jnp.zeros((ROW_TILE, GROUP_CH), F32)
            for o in range(-left, w - left):
                ssum = ssum + pad_ref[HALO + o:HALO + o + ROW_TILE, sl]
            m = ssum / _pool_count(pos, length, w) - p_ref[:, sl]
            pre = _dot(m.astype(BF16), pw_ref[gi].astype(BF16))
            o_ref[:, sl] = (pre * ps_ref[:, sl]).astype(BF16)
        u = _gelu(p_ref[:, POOL_CH:POOL_CH + SG_CH])
        xh, _ = _ln_stats(_gelu(p_ref[:, POOL_CH + SG_CH:ODD_IN]))
        vln = xh * lw_ref[...] + lb_ref[...]
        for n in range(ROW_TILE // SG_CHUNK):
            rs = slice(n * SG_CHUNK, (n + 1) * SG_CHUNK)
            for gi in range(4):
                sl = slice(gi * GROUP_CH, (gi + 1) * GROUP_CH)
                s = _dot(sw_ref[gi].astype(BF16), vln[rs, sl].astype(BF16)) + sb_ref[:, sl]
                o_ref[rs, POOL_CH + gi * GROUP_CH:POOL_CH + (gi + 1) * GROUP_CH] = (u[rs, sl] * s).astype(BF16)

    pp, pn = _halo_specs(T, POOL_CH, 0)
    return _call(
        body, grid=(nt,), name=name,
        in_specs=[_row(ODD_IN), pp, pn, _const((4, GROUP_CH, GROUP_CH)), _const((1, POOL_CH)), _const((1, SG_CH)),
                  _const((1, SG_CH)), _const((4, SG_CHUNK, SG_CHUNK)), _const((SG_CHUNK, SG_CH))],
        out_specs=[_row(D)], out_shape=[jax.ShapeDtypeStruct((T, D), BF16)],
        scratch=[pltpu.VMEM((ROW_TILE + 2 * HALO, POOL_CH), F32)],
        sem=("parallel",), args=(p, p, p, pw, ps, slw, slb, sw, sbf), comm=comm)


def _odd_bwd(p, dmix, pw, ps, slw, slb, sw, sbf, nxt, seq, ctx, name, comm=()):
    T = p.shape[0]
    nt = T // ROW_TILE

    def body(p_ref, pp_ref, pn_ref, dm_ref, dmp_ref, dmn_ref, pw_ref, ps_ref, lw_ref, lb_ref, sw_ref, sb_ref,
             dp_ref, dpw_ref, dsw_ref, dv_ref, dsb_ref, pad_ref, dpad_ref):
        i = pl.program_id(0)
        first, last = _seq_edges(i, nxt, nt)

        @pl.when(i == 0)
        def _():
            dpw_ref[...] = jnp.zeros_like(dpw_ref)
            dsw_ref[...] = jnp.zeros_like(dsw_ref)
            dv_ref[...] = jnp.zeros_like(dv_ref)
            dsb_ref[...] = jnp.zeros_like(dsb_ref)

        _fill_pad(pad_ref, pp_ref[...], p_ref[:, 0:POOL_CH], pn_ref[...], first, last)
        pos, length = _tile_positions(i, nxt, seq, ctx, 0, ROW_TILE)
        pos_p, _ = _tile_positions(i, nxt, seq, ctx, -HALO, HALO)
        pos_n, _ = _tile_positions(i, nxt, seq, ctx, ROW_TILE, HALO)
        scale = ps_ref[...]
        for gi, w in enumerate(POOL_WINDOWS):
            sl = slice(gi * GROUP_CH, (gi + 1) * GROUP_CH)
            left = w // 2
            right = w - 1 - left
            ssum = jnp.zeros((ROW_TILE, GROUP_CH), F32)
            for o in range(-left, right + 1):
                ssum = ssum + pad_ref[HALO + o:HALO + o + ROW_TILE, sl]
            cnt = _pool_count(pos, length, w)
            m = ssum / cnt - p_ref[:, sl]
            wg = pw_ref[gi].astype(BF16)
            pre = _dot(m.astype(BF16), wg)
            dpo = dm_ref[:, sl]
            dv_ref[0:1, sl] += _colsum(dpo * pre)
            dpre = (dpo * scale[:, sl]).astype(BF16)
            dpw_ref[gi] += _dot_tn(m.astype(BF16), dpre)
            dmc = _dot_nt(dpre, wg)
            halo_dm = lambda ref, ps_: _dot_nt((ref[:, sl] * scale[:, sl]).astype(BF16), wg) / _pool_count(ps_, length, w)
            dpad_ref[0:HALO, sl] = jnp.where(first, 0.0, halo_dm(dmp_ref, pos_p))
            dpad_ref[HALO:HALO + ROW_TILE, sl] = dmc / cnt
            dpad_ref[HALO + ROW_TILE:2 * HALO + ROW_TILE, sl] = jnp.where(last, 0.0, halo_dm(dmn_ref, pos_n))
            atd = jnp.zeros((ROW_TILE, GROUP_CH), F32)
            for o in range(-right, left + 1):
                atd = atd + dpad_ref[HALO + o:HALO + o + ROW_TILE, sl]
            dp_ref[:, sl] = (atd - dmc).astype(BF16)

        pu = p_ref[:, POOL_CH:POOL_CH + SG_CH]
        pv = p_ref[:, POOL_CH + SG_CH:ODD_IN]
        u = _gelu(pu)
        xh, rs_ = _ln_stats(_gelu(pv))
        lw = lw_ref[...]
        vln = xh * lw + lb_ref[...]
        dgelu = lambda x: 0.5 * (1.0 + lax.erf(x * INV_SQRT2)) + x * jnp.exp(-0.5 * x * x) * INV_SQRT_2PI
        for n in range(ROW_TILE // SG_CHUNK):
            rs = slice(n * SG_CHUNK, (n + 1) * SG_CHUNK)
            dvl = []
            for gi in range(4):
                sl = slice(gi * GROUP_CH, (gi + 1) * GROUP_CH)
                wq = sw_ref[gi].astype(BF16)
                vb = vln[rs, sl].astype(BF16)
                s = _dot(wq, vb) + sb_ref[:, sl]
                dsg = dm_ref[rs, POOL_CH + gi * GROUP_CH:POOL_CH + (gi + 1) * GROUP_CH]
                ds = dsg * u[rs, sl]
                dsb_ref[:, sl] += ds
                dsw_ref[gi] += _dot_nt(ds.astype(BF16), vb)
                dvl.append(_dot_tn(wq, ds.astype(BF16)))
                dp_ref[rs, POOL_CH + gi * GROUP_CH:POOL_CH + (gi + 1) * GROUP_CH] = (
                    dsg * s * dgelu(pu[rs, sl])).astype(BF16)
            dvln = jnp.concatenate(dvl, axis=1)
            xhc = xh[rs, :]
            dv_ref[1:2, :] += _colsum(dvln * xhc)
            dv_ref[2:3, :] += _colsum(dvln)
            dxh = dvln * lw
            dvv = rs_[rs, :] * (dxh - _rowmean(dxh) - xhc * _rowmean(dxh * xhc))
            dp_ref[rs, POOL_CH + SG_CH:ODD_IN] = (dvv * dgelu(pv[rs, :])).astype(BF16)

    pp, pn = _halo_specs(T, POOL_CH, 0)
    dmp, dmn = _halo_specs(T, POOL_CH, 0)
    gsd = jax.ShapeDtypeStruct((4, GROUP_CH, GROUP_CH), F32)
    return _call(
        body, grid=(nt,), name=name,
        in_specs=[_row(ODD_IN), pp, pn, _row(D), dmp, dmn, _const((4, GROUP_CH, GROUP_CH)), _const((1, POOL_CH)),
                  _const((1, SG_CH)), _const((1, SG_CH)), _const((4, SG_CHUNK, SG_CHUNK)), _const((SG_CHUNK, SG_CH))],
        out_specs=[_row(ODD_IN), _const((4, GROUP_CH, GROUP_CH)), _const((4, SG_CHUNK, SG_CHUNK)), _const((8, POOL_CH)),
                   _const((SG_CHUNK, SG_CH))],
        out_shape=[jax.ShapeDtypeStruct((T, ODD_IN), BF16), gsd, gsd, jax.ShapeDtypeStruct((8, POOL_CH), F32),
                   jax.ShapeDtypeStruct((SG_CHUNK, SG_CH), F32)],
        scratch=[pltpu.VMEM((ROW_TILE + 2 * HALO, POOL_CH), F32), pltpu.VMEM((ROW_TILE + 2 * HALO, POOL_CH), F32)],
        sem=("arbitrary",), args=(p, p, p, dmix, dmix, dmix, pw, ps, slw, slb, sw, sbf), comm=comm)


def _ada_fwd(cs, aw, ab, name):
    cols = aw.shape[2]

    def body(c_ref, w_ref, b_ref, o_ref):
        c = c_ref[...]
        s = (c * _sigmoid(c)).astype(BF16)
        o_ref[0] = _dot(s, w_ref[0].astype(BF16)) + b_ref[0]

    return pl.pallas_call(
        body, grid=(DEPTH,), name=name,
        in_specs=[pl.BlockSpec((16, D), lambda i: (0, 0)), pl.BlockSpec((1, D, cols), lambda i: (i, 0, 0)),
                  pl.BlockSpec((1, 1, cols), lambda i: (i, 0, 0))],
        out_specs=pl.BlockSpec((1, 16, cols), lambda i: (i, 0, 0)),
        out_shape=jax.ShapeDtypeStruct((DEPTH, 16, cols), F32),
        compiler_params=_params(("parallel",)),
    )(cs, aw, ab)


def _ada_bwd(cs, G, aw, name):
    cols = aw.shape[2]

    def body(c_ref, g_ref, w_ref, gw_ref, cc_ref):
        i = pl.program_id(0)

        @pl.when(i == 0)
        def _():
            cc_ref[...] = jnp.zeros_like(cc_ref)

        c = c_ref[...]
        s = (c * _sigmoid(c)).astype(BF16)
        g = g_ref[0]
        dc = g[8:9, :]
        for d in range(9, 16):
            dc = dc + g[d:d + 1, :]
        row = lax.broadcasted_iota(jnp.int32, (8, cols), 0)
        dcrows = jnp.where(row == 0, dc, 0.0)
        dm = jnp.concatenate([g[0:8, :], dcrows], axis=0).astype(BF16)
        gw_ref[0] = _dot_tn(s, dm)
        cc_ref[...] += _dot_nt(dcrows.astype(BF16), w_ref[0].astype(BF16))

    return pl.pallas_call(
        body, grid=(DEPTH,), name=name,
        in_specs=[pl.BlockSpec((16, D), lambda i: (0, 0)), pl.BlockSpec((1, 16, cols), lambda i: (i, 0, 0)),
                  pl.BlockSpec((1, D, cols), lambda i: (i, 0, 0))],
        out_specs=[pl.BlockSpec((1, D, cols), lambda i: (i, 0, 0)), pl.BlockSpec((8, D), lambda i: (0, 0))],
        out_shape=[jax.ShapeDtypeStruct((DEPTH, D, cols), F32), jax.ShapeDtypeStruct((8, D), F32)],
        compiler_params=_params(("arbitrary",)),
    )(cs, G, aw)


def _row_tile_for(rows):
    for t in (512, 256, 128, 64, 32, 16, 8):
        if rows % t == 0:
            return t
    raise ValueError(f"rows={rows} is not a multiple of 8")


def _sum_devices(x, name):
    _, rows, cols = x.shape
    tr = _row_tile_for(rows)

    def body(x_ref, o_ref):
        acc = x_ref[0]
        for d in range(1, N_DEV):
            acc = acc + x_ref[d]
        o_ref[...] = acc

    return pl.pallas_call(
        body, grid=(rows // tr,), name=name,
        in_specs=[pl.BlockSpec((N_DEV, tr, cols), lambda i: (0, i, 0))],
        out_specs=pl.BlockSpec((tr, cols), lambda i: (i, 0)),
        out_shape=jax.ShapeDtypeStruct((rows, cols), F32),
        compiler_params=_params(("parallel",)),
    )(x)


def _adamw(w, g, m, v, name):
    rows, cols = w.shape
    tr = _row_tile_for(rows)

    def body(w_ref, g_ref, m_ref, v_ref, go_ref, d_ref, mo_ref, vo_ref):
        gv = g_ref[...]
        mn = ADAM_B1 * m_ref[...] + (1.0 - ADAM_B1) * gv
        vn = ADAM_B2 * v_ref[...] + (1.0 - ADAM_B2) * (gv * gv)
        go_ref[...] = gv
        mo_ref[...] = mn
        vo_ref[...] = vn
        d_ref[...] = -ADAM_LR * ((mn / ADAM_BC1) / (jnp.sqrt(vn / ADAM_BC2) + ADAM_EPS) + ADAM_WD * w_ref[...])

    blk = pl.BlockSpec((tr, cols), lambda i: (i, 0))
    sd = jax.ShapeDtypeStruct((rows, cols), F32)
    return pl.pallas_call(
        body, grid=(rows // tr,), name=name,
        in_specs=[blk, blk, blk, blk], out_specs=[blk, blk, blk, blk], out_shape=[sd, sd, sd, sd],
        compiler_params=_params(("parallel",)),
    )(w, g, m, v)


def _adamw_layers(w, pieces, m, v, name):
    L, a, b = w.shape
    bp = pieces[0].shape[2]
    tr = next(t for t in (256, 128, 64, 32, 16) if a % t == 0)

    def body(w_ref, m_ref, v_ref, *rest):
        p_refs, (go_ref, d_ref, mo_ref, vo_ref) = rest[:L], rest[L:]
        layer = pl.program_id(0)
        for k in range(L):
            @pl.when(layer == k)
            def _(k=k):
                gv = p_refs[k][0].astype(F32)
                for d in range(1, N_DEV):
                    gv = gv + p_refs[k][d].astype(F32)
                gv = gv[:, :b]
                mn = ADAM_B1 * m_ref[0] + (1.0 - ADAM_B1) * gv
                vn = ADAM_B2 * v_ref[0] + (1.0 - ADAM_B2) * (gv * gv)
                go_ref[0] = gv
                mo_ref[0] = mn
                vo_ref[0] = vn
                d_ref[0] = -ADAM_LR * ((mn / ADAM_BC1) / (jnp.sqrt(vn / ADAM_BC2) + ADAM_EPS) + ADAM_WD * w_ref[0])

    blk = pl.BlockSpec((1, tr, b), lambda l, i: (l, i, 0))
    pspecs = [pl.BlockSpec((N_DEV, tr, bp), lambda l, i, k=k: (0, jnp.where(l == k, i, 0), 0)) for k in range(L)]
    sd = jax.ShapeDtypeStruct((L, a, b), F32)
    return pl.pallas_call(
        body, grid=(L, a // tr), name=name,
        in_specs=[blk, blk, blk] + pspecs, out_specs=[blk, blk, blk, blk], out_shape=[sd, sd, sd, sd],
        compiler_params=_params(("arbitrary", "arbitrary")),
    )(w, m, v, *pieces)


def _pack_rows(shape):
    return -(-math.prod(shape) // (8 * LANES)) * 8


def _pack(arrs, row_mult):
    parts = []
    for a in arrs:
        n, rows = math.prod(a.shape), _pack_rows(a.shape)
        parts.append(jnp.pad(a.reshape(-1).astype(F32), (0, rows * LANES - n)).reshape(rows, LANES))
    total = sum(p.shape[0] for p in parts)
    if total % row_mult:
        parts.append(jnp.zeros((-total % row_mult, LANES), F32))
    return jnp.concatenate(parts, axis=0)


def _unpack(packed, shapes):
    out, r0 = [], 0
    lead = packed.shape[:-2]
    for s in shapes:
        n, rows = math.prod(s), _pack_rows(s)
        piece = packed[..., r0:r0 + rows, :].reshape(lead + (rows * LANES,))
        out.append(piece[..., :n].reshape(lead + tuple(s)))
        r0 += rows
    return out


def _rope_tables(seq, ctx):
    def angles(ps):
        parts = []
        for pvec, n in zip(ps, ROPE_PAIRS):
            freq = ROPE_BASE ** (-jnp.arange(n, dtype=F32) / n)
            parts.append(pvec[:, None] * freq[None, :])
        return jnp.concatenate(parts, axis=-1)

    rows = seq // GRID_W
    grid_r = jnp.broadcast_to(jnp.arange(rows, dtype=F32)[:, None], (rows, GRID_W)).reshape(-1)
    grid_c = jnp.broadcast_to(jnp.arange(GRID_W, dtype=F32)[None, :], (rows, GRID_W)).reshape(-1)
    zc = jnp.zeros((ctx,), F32)
    ang = jnp.concatenate([angles((jnp.full((seq,), ctx, F32), grid_r, grid_c)),
                           angles((jnp.arange(ctx, dtype=F32), zc, zc))], axis=0)
    cos, sin = jnp.cos(ang), jnp.sin(ang)
    return jnp.concatenate([cos, cos], axis=1), jnp.concatenate([-sin, sin], axis=1)


def _layer_gather(sh, even):
    return [("ag_cols" if even else "ag_rows", sh["in"]), ("ag_rows", sh["out"]), ("ag_rows", sh["gate"]),
            ("ag_rows", sh["up"]), ("ag_rows", sh["down"])]


def _sample_step(X0, tgt, mods, shards, gather0, S, seq, ctx):
    nxt = seq // ROW_TILE
    cosf, sgn = _rope_tables(seq, ctx)
    X = X0
    saved = []
    gathers = {(0, "mix"): gather0}

    def start_gather(i, half, dep):
        comm = _layer_gather(shards[i], i % 2 == 0)
        comm = comm[:2] if half == "mix" else comm[2:]
        comm[0] = (comm[0][0], comm[0][1] + dep.astype(BF16))
        gathers[(i, half)], token = _xfer_start(comm, f"gather_{half}{i}_start")
        return token[0, 0]

    dep = mods[0, 0, 0, 0] * 0.0
    for i, half in ((0, "ffn"), (1, "mix"), (1, "ffn")):
        dep = start_gather(i, half, dep)
    w_in, w_out = _xfer_wait(gathers[(0, "mix")], mods, "gather_mix0_wait")
    for i in range(DEPTH):
        j, even = i // 2, i % 2 == 0
        t = f"l{i}_"
        last = i == DEPTH - 1
        if i in (1, 2):
            dep = w_out[0, 0].astype(F32) * 0.0
            for half in ("mix", "ffn"):
                dep = start_gather(i + 1, half, dep)
        md = mods[i] + dep
        h1 = _norm_mod_fwd(X, S["norm_w"][i, 0][None], md, 0, nxt, t + "norm1")
        if even:
            (p,), _ = _mm_nn(h1, w_in, F32, t + "proj_in")
            (yret, ro), _ = _ret_fwd(p, S["lg"][j], cosf, sgn, seq, t + "ret_fwd")
            (cv, co), _ = _conv_fwd(p, S["conv_w"][j], S["conv_ln_w"][j][None], S["conv_ln_b"][j][None], nxt,
                                    t + "conv_fwd")
            mix = jnp.concatenate([ro, co], axis=1)
            extra = (yret, cv)
        else:
            (p,), _ = _mm_nt(h1, w_in, t + "proj_in")
            sbf = jnp.repeat(S["sg_b"][j].T, GROUP_CH, axis=1)
            (mix,), _ = _odd_fwd(p, S["pool_w"][j], S["pool_scale"][j][None], S["sg_ln_w"][j][None],
                                 S["sg_ln_b"][j][None], S["sg_w"][j], sbf, nxt, seq, ctx, t + "odd_fwd")
            extra = (sbf,)
        (X1, y1), _ = _mm_nn_resid(mix, w_out, X, md, 2, seq, t + "proj_out")
        h2 = _norm_mod_fwd(X1, S["norm_w"][i, 1][None], md, 3, nxt, t + "norm2")
        wg, wu, wd = _xfer_wait(gathers[(i, "ffn")], h2, f"gather_ffn{i}_wait")
        (gp, up, act), _ = _ffn_up(h2, wg, wu, t + "ffn_up")
        (X2, y2), _ = _mm_nn_resid(act, wd, X1, md, 5, seq, t + "ffn_down")
        saved.append((X, h1, p, mix, extra, y1, X1, h2, gp, up, act, y2, (w_in, w_out, wg, wu, wd)))
        X = X2
        if not last:
            w_in, w_out = _xfer_wait(gathers[(i + 1, "mix")], X2, f"gather_mix{i + 1}_wait")

    dX, hst = _loss_head(X, tgt, S["final_norm_w"][None], nxt, "loss_head")
    loss = jnp.sum(hst[1])
    pieces = {}
    gS = {"final_norm_w": hst[0], "norm_w": [None] * DEPTH, "lg": [None] * 2, "conv_w": [None] * 2,
          "conv_ln_w": [None] * 2, "conv_ln_b": [None] * 2, "pool_w": [None] * 2, "pool_scale": [None] * 2,
          "sg_ln_w": [None] * 2, "sg_ln_b": [None] * 2, "sg_w": [None] * 2, "sg_b": [None] * 2}
    dmods = [None] * DEPTH
    in_flight = []

    def land(flights, after, tag):
        for handle, keys, name in flights:
            for key, got in zip(keys, _xfer_wait(handle, after, name + tag)):
                pieces[key] = got

    for i in reversed(range(DEPTH)):
        j, even = i // 2, i % 2 == 0
        md = mods[i]
        t = f"l{i}_"
        X_in, h1, p, mix, extra, y1, X1, h2, gp, up, act, y2, (w_in, w_out, wg, wu, wd) = saved[i]
        dyf, g2 = _resid_bwd(dX, y2, md, 5, nxt, t + "ffn_resid_bwd")
        (dgp, dup), _ = _ffn_dact(dyf, wd, gp, up, t + "ffn_dact")
        g_down = _mm_tn(act, dyf, t + "dw_down")
        g_gate = _mm_tn(dgp, h2, t + "dw_gate")
        g_up = _mm_tn(dup, h2, t + "dw_up")
        ffn_flight, token = _xfer_start([("a2a_rows", g_down), ("a2a_rows", g_gate), ("a2a_rows", g_up)],
                                        t + "scatter_ffn_start")
        (dh2,), _ = _mm_nn(dgp, wg, F32, t + "dh2_gate")
        (dh2,), _ = _mm_nn(dup, wu, F32, t + "dh2_up", add=dh2)
        dX1, s2 = _norm_mod_bwd(dh2, X1, S["norm_w"][i, 1][None], md + token[0, 0], dX, 3, nxt, t + "norm2_bwd")
        dym, g1 = _resid_bwd(dX1, y1, md, 2, nxt, t + "mix_resid_bwd")
        (dmix,), _ = _mm_nt(dym, w_out, t + "dmix")
        g_out = _mm_tn(mix, dym, t + "dw_out")
        if even:
            yret, cv = extra
            dyr, dg = _ret_gate_bwd(yret, p, dmix, t + "ret_gate_bwd")
            (dq, dk, dv, dlg), _ = _ret_bwd(p, S["lg"][j], dyr, cosf, sgn, seq, t + "ret_bwd")
            early = [("ag_blk", _pack([jnp.stack(gS["pool_w"]), jnp.stack(gS["sg_w"])], 8))] if i == 0 else []
            (dpc, dcw, dln), got = _conv_bwd(
                p, cv, dmix, S["conv_w"][j], S["conv_ln_w"][j][None], S["conv_ln_b"][j][None], nxt,
                t + "conv_bwd", comm=early)
            if i == 0:
                early_all = got[0]
            dp = jnp.concatenate([dq, dk, dv, dg, dpc], axis=1)
            gS["lg"][j] = dlg[:, 0:2, 0].T
            gS["conv_w"][j], gS["conv_ln_w"][j], gS["conv_ln_b"][j] = dcw, dln[0], dln[1]
            g_in = _mm_tn(h1, dp, t + "dw_in")
            mix_flight, token = _xfer_start([("a2a_rows", g_out), ("a2a_cols", g_in)], t + "scatter_mix_start")
            (dh1,), _ = _mm_nt(dp, w_in, t + "dh1")
        else:
            (sbf,) = extra
            (dp, dpw, dsw, dvec, dsb), _ = _odd_bwd(
                p, dmix, S["pool_w"][j], S["pool_scale"][j][None], S["sg_ln_w"][j][None], S["sg_ln_b"][j][None],
                S["sg_w"][j], sbf, nxt, seq, ctx, t + "odd_bwd")
            gS["pool_w"][j], gS["sg_w"][j] = dpw, dsw
            gS["pool_scale"][j], gS["sg_ln_w"][j], gS["sg_ln_b"][j] = dvec[0], dvec[1], dvec[2]
            gS["sg_b"][j] = jnp.sum(dsb.reshape(SG_CHUNK, 4, GROUP_CH), axis=2).T
            g_in = _mm_tn(dp, h1, t + "dw_in")
            mix_flight, token = _xfer_start([("a2a_rows", g_out), ("a2a_rows", g_in)], t + "scatter_mix_start")
            (dh1,), _ = _mm_nn(dp, w_in, F32, t + "dh1")
        dX, s1 = _norm_mod_bwd(dh1, X_in, S["norm_w"][i, 0][None], md + token[0, 0], dX1, 0, nxt, t + "norm1_bwd")
        gS["norm_w"][i] = jnp.stack([s1[0, 2] + s1[1, 2], s2[0, 2] + s2[1, 2]])
        dmods[i] = jnp.stack([s1[:, 0], s1[:, 1], g1[:, 0], s2[:, 0], s2[:, 1], g2[:, 0]], axis=1)
        land(in_flight, dX, "_wait")
        in_flight = [(ffn_flight, [("down", i), ("gate", i), ("up", i)], t + "scatter_ffn"),
                     (mix_flight, [("out", i), ("in", i)], t + "scatter_mix")]
    land(in_flight[:1], dX, "_wait")
    gS = {k: (jnp.stack(v) if isinstance(v, list) else v) for k, v in gS.items()}
    return loss, dX, pieces, gS, jnp.stack(dmods), early_all, in_flight[1]


FF_SHARD = D_FF // N_DEV
FF_SHARD_PAD = 384


def kernel(x, c, ctx, c_ctx, ada_w, ada_b, norm_w, even_w_in, even_w_out, ret_decay_logit, conv_dw_w, conv_ln_w, conv_ln_b, odd_w_in, odd_w_out, pool_w, pool_scale, sg_ln_w, sg_ln_b, sg_w, sg_b, ffn_w_gate, ffn_w_up, ffn_w_down, final_norm_w, loss_target, m_c_ctx, m_ada_w, m_ada_b, m_norm_w, m_even_w_in, m_even_w_out, m_ret_decay_logit, m_conv_dw_w, m_conv_ln_w, m_conv_ln_b, m_odd_w_in, m_odd_w_out, m_pool_w, m_pool_scale, m_sg_ln_w, m_sg_ln_b, m_sg_w, m_sg_b, m_ffn_w_gate, m_ffn_w_up, m_ffn_w_down, m_final_norm_w, v_c_ctx, v_ada_w, v_ada_b, v_norm_w, v_even_w_in, v_even_w_out, v_ret_decay_logit, v_conv_dw_w, v_conv_ln_w, v_conv_ln_b, v_odd_w_in, v_odd_w_out, v_pool_w, v_pool_scale, v_sg_ln_w, v_sg_ln_b, v_sg_w, v_sg_b, v_ffn_w_gate, v_ffn_w_up, v_ffn_w_down, v_final_norm_w):
    seq, n_ctx = x.shape[1], ctx.shape[1]
    me = 4 * lax.axis_index("x") + 2 * lax.axis_index("y") + lax.axis_index("c")
    mcols = ada_w.shape[2]

    tr = lambda a: jnp.swapaxes(a, 1, 2)
    pad_rows = lambda a: jnp.pad(a, ((0, FF_SHARD_PAD - FF_SHARD), (0, 0))).astype(BF16)
    gate_t, up_t, odd_in_t = tr(ffn_w_gate), tr(ffn_w_up), tr(odd_w_in)
    shards = []
    for i in range(DEPTH):
        j, even = i // 2, i % 2 == 0
        w_in, w_out = (even_w_in[j], even_w_out[j]) if even else (odd_in_t[j], odd_w_out[j])
        shards.append({"in": w_in.astype(BF16), "out": w_out.astype(BF16), "gate": pad_rows(gate_t[i]),
                       "up": pad_rows(up_t[i]), "down": pad_rows(ffn_w_down[i])})
    gather0, token = _xfer_start(_layer_gather(shards[0], True)[:2], "gather_mix0_start")

    small_shapes = [(D,), norm_w.shape, conv_dw_w.shape, pool_scale.shape, sg_ln_w.shape, sg_ln_b.shape]
    (sm,) = _exchange([("ag_blk", _pack([c + token[0, 0], norm_w, conv_dw_w, pool_scale, sg_ln_w, sg_ln_b], 8))],
                      "gather_small")
    c_all, nw_s, cw_s, ps_s, slw_s, slb_s = _unpack(sm, small_shapes)
    cat_last = lambda a: jnp.moveaxis(a, 0, -2).reshape(a.shape[1:-1] + (-1,))
    conv_w_full = cat_last(cw_s)
    S = {"norm_w": cat_last(nw_s), "lg": jax.nn.log_sigmoid(ret_decay_logit),
         "conv_w": jnp.pad(conv_w_full, ((0, 0), (0, 32 - CONV_K), (0, 0))),
         "conv_ln_w": conv_ln_w, "conv_ln_b": conv_ln_b, "pool_w": pool_w, "pool_scale": cat_last(ps_s),
         "sg_ln_w": cat_last(slw_s), "sg_ln_b": cat_last(slb_s), "sg_w": sg_w, "sg_b": sg_b,
         "final_norm_w": final_norm_w}

    cs = jnp.concatenate([c_all, c_ctx[None], jnp.zeros((7, D), F32)], axis=0)
    ab_loc = lax.dynamic_slice_in_dim(ada_b, me * mcols, mcols, axis=1)
    mod_loc = _ada_fwd(cs, ada_w, ab_loc[:, None, :], "ada_fwd")
    (mod_all,) = _exchange([("ag_blk", mod_loc.reshape(DEPTH * 16, mcols))], "gather_mod")
    mod_all = mod_all.reshape(N_DEV, DEPTH, 16, mcols).transpose(1, 2, 0, 3).reshape(DEPTH, 16, 6, D)
    mod_x = lax.dynamic_index_in_dim(mod_all, me, axis=1, keepdims=False)
    mods = jnp.stack([mod_x, mod_all[:, 8]], axis=1)

    X0 = jnp.concatenate([x[0], ctx[0]], axis=0)
    loss, dX, big, gS, dmods, early_all, late = _sample_step(X0, loss_target[0], mods, shards, gather0, S, seq,
                                                             n_ctx)
    loss = lax.psum(loss, ("x", "y", "c"))
    grad_x = dX[:seq][None]

    (dm_all,) = _exchange([("ag_blk", dmods.reshape(DEPTH * 2, 6 * D))], "gather_dmod")
    dm_all = dm_all.reshape(N_DEV, DEPTH, 2, 6 * D)
    dm_sum = _sum_devices(dm_all.reshape(N_DEV, DEPTH * 2, 6 * D), "sum_dmod").reshape(DEPTH, 2, 6 * D)
    g_ada_b = dm_sum[:, 0] + dm_sum[:, 1]
    G = lax.dynamic_slice_in_dim(dm_all, me * mcols, mcols, axis=3).transpose(1, 2, 0, 3).reshape(DEPTH, 16, mcols)
    g_ada_w, ccp = _ada_bwd(cs, G, ada_w, "ada_bwd")
    sg_cc = _sigmoid(c_ctx)
    g_cctx_part = ccp[0] * sg_cc * (1.0 + c_ctx * (1.0 - sg_cc))

    dsig = _sigmoid(-ret_decay_logit)
    part = [g_cctx_part, gS["norm_w"], gS["lg"] * dsig, gS["conv_w"][:, :CONV_K], gS["conv_ln_w"], gS["conv_ln_b"],
            gS["pool_scale"], gS["sg_ln_w"], gS["sg_ln_b"], gS["sg_b"], gS["final_norm_w"]]
    part_shapes = [a.shape for a in part]
    (part_all,) = _exchange([("ag_blk", _pack(part, 8))], "gather_small_grads")
    red = _sum_devices(part_all, "sum_small_grads")
    g_cctx, g_nw, g_rdl, g_cw, g_clw, g_clb, g_ps, g_slw, g_slb, g_sb, g_fnw = _unpack(red, part_shapes)
    g_pw, g_sw = _unpack(_sum_devices(early_all, "sum_early_grads"), [pool_w.shape, sg_w.shape])
    mine_last = lambda a, n: lax.dynamic_slice_in_dim(a, me * n, n, axis=a.ndim - 1)
    g_nw, g_cw = mine_last(g_nw, norm_w.shape[2]), mine_last(g_cw, conv_dw_w.shape[2])
    g_ps, g_slw, g_slb = (mine_last(a, pool_scale.shape[1]) for a in (g_ps, g_slw, g_slb))

    def adam_big(key, layers, w, m, v, name):
        return _adamw_layers(w, [big[(key, l)] for l in layers], m, v, name)

    res = {}
    adam_t = lambda key, layers, w, m, v, name: [tr(o) for o in adam_big(key, layers, tr(w), tr(m), tr(v), name)]
    res["odd_w_in"] = adam_t("in", (1, 3), odd_w_in, m_odd_w_in, v_odd_w_in, "adam_odd_in")
    res["odd_w_out"] = adam_big("out", (1, 3), odd_w_out, m_odd_w_out, v_odd_w_out, "adam_odd_out")
    res["ffn_w_gate"] = adam_t("gate", (0, 1, 2, 3), ffn_w_gate, m_ffn_w_gate, v_ffn_w_gate, "adam_gate")
    res["ffn_w_up"] = adam_t("up", (0, 1, 2, 3), ffn_w_up, m_ffn_w_up, v_ffn_w_up, "adam_up")
    res["ffn_w_down"] = adam_big("down", (0, 1, 2, 3), ffn_w_down, m_ffn_w_down, v_ffn_w_down, "adam_down")
    late_handle, late_keys, late_name = late
    for key, got in zip(late_keys, _xfer_wait(late_handle, res["ffn_w_down"][1], late_name + "_wait")):
        big[key] = got
    res["even_w_in"] = adam_big("in", (0, 2), even_w_in, m_even_w_in, v_even_w_in, "adam_even_in")
    res["even_w_out"] = adam_big("out", (0, 2), even_w_out, m_even_w_out, v_even_w_out, "adam_even_out")
    flat2 = lambda a: a.reshape(-1, a.shape[-1])
    res["ada_w"] = [o.reshape(ada_w.shape) for o in
                    _adamw(flat2(ada_w), flat2(g_ada_w), flat2(m_ada_w), flat2(v_ada_w), "adam_ada_w")]

    names = ["c_ctx", "ada_b", "norm_w", "ret_decay_logit", "conv_dw_w", "conv_ln_w", "conv_ln_b", "pool_w",
             "pool_scale", "sg_ln_w", "sg_ln_b", "sg_w", "sg_b", "final_norm_w"]
    ws = [c_ctx, ada_b, norm_w, ret_decay_logit, conv_dw_w, conv_ln_w, conv_ln_b, pool_w, pool_scale, sg_ln_w,
          sg_ln_b, sg_w, sg_b, final_norm_w]
    gs = [g_cctx, g_ada_b, g_nw, g_rdl, g_cw, g_clw, g_clb, g_pw, g_ps, g_slw, g_slb, g_sw, g_sb, g_fnw]
    ms = [m_c_ctx, m_ada_b, m_norm_w, m_ret_decay_logit, m_conv_dw_w, m_conv_ln_w, m_conv_ln_b, m_pool_w,
          m_pool_scale, m_sg_ln_w, m_sg_ln_b, m_sg_w, m_sg_b, m_final_norm_w]
    vs = [v_c_ctx, v_ada_b, v_norm_w, v_ret_decay_logit, v_conv_dw_w, v_conv_ln_w, v_conv_ln_b, v_pool_w,
          v_pool_scale, v_sg_ln_w, v_sg_ln_b, v_sg_w, v_sg_b, v_final_norm_w]
    shapes = [a.shape for a in ws]
    gs = [g.reshape(s) for g, s in zip(gs, shapes)]
    pk = lambda arrs: _pack(arrs, 512)
    outs = _adamw(pk(ws), pk(gs), pk(ms), pk(vs), "adam_small")
    for k, o in enumerate(outs):
        for nm, arr in zip(names, _unpack(o, shapes)):
            res.setdefault(nm, [None] * 4)[k] = arr

    order = ["c_ctx", "ada_w", "ada_b", "norm_w", "even_w_in", "even_w_out", "ret_decay_logit", "conv_dw_w",
             "conv_ln_w", "conv_ln_b", "odd_w_in", "odd_w_out", "pool_w", "pool_scale", "sg_ln_w", "sg_ln_b",
             "sg_w", "sg_b", "ffn_w_gate", "ffn_w_up", "ffn_w_down", "final_norm_w"]
    return (loss, grad_x, *[res[n][0] for n in order], *[res[n][1] for n in order],
            *[res[n][2] for n in order], *[res[n][3] for n in order])
```

```python
import math

import jax
import jax.numpy as jnp
from jax import lax
from jax.experimental import pallas as pl
from jax.experimental.pallas import tpu as pltpu

F32 = jnp.float32
BF16 = jnp.bfloat16

N_DEV = 8
D = 1024
DEPTH = 4
ROW_TILE = 256
HALO = 16
LANES = 128
EPS = 1e-6
RET_HEADS = 4
HEAD_DIM = 128
RET_W = RET_HEADS * HEAD_DIM
CHUNK = 128
Q_SCALE = HEAD_DIM ** -0.5
ROPE_BASE = 10000.0
ROPE_PAIRS = (HEAD_DIM // 8, 3 * HEAD_DIM // 16, 3 * HEAD_DIM // 16)
GRID_W = 64
CONV_CH = 512
CONV_K = 31
EVEN_IN = 4 * RET_W + 2 * CONV_CH
POOL_CH = 512
POOL_WINDOWS = (2, 4, 8, 16)
GROUP_CH = 128
SG_CH = 512
SG_CHUNK = 128
ODD_IN = POOL_CH + 2 * SG_CH
D_FF = 2816
INV_SQRT2 = 1.0 / math.sqrt(2.0)
INV_SQRT_2PI = 1.0 / math.sqrt(2.0 * math.pi)
ADAM_LR, ADAM_B1, ADAM_B2, ADAM_EPS, ADAM_WD, ADAM_STEP = 0.001, 0.9, 0.999, 1e-08, 0.01, 10
ADAM_BC1 = 1.0 - ADAM_B1 ** ADAM_STEP
ADAM_BC2 = 1.0 - ADAM_B2 ** ADAM_STEP
VMEM_LIMIT = 56 * 1024 * 1024
MESH = pl.DeviceIdType.MESH


def _params(sem=None):
    return pltpu.CompilerParams(dimension_semantics=sem, vmem_limit_bytes=VMEM_LIMIT)


def _dot(a, b):
    return jnp.dot(a, b, preferred_element_type=F32)


def _dot_nt(a, b):
    return lax.dot_general(a, b, (((1,), (1,)), ((), ())), preferred_element_type=F32)


def _dot_tn(a, b):
    return lax.dot_general(a, b, (((0,), (0,)), ((), ())), preferred_element_type=F32)


def _sigmoid(x):
    return 1.0 / (1.0 + jnp.exp(-x))


def _colsum(x):
    return jnp.sum(x, axis=0, keepdims=True)


def _rowmean(x):
    return jnp.mean(x, axis=-1, keepdims=True)


def _row(shape_cols, tile=ROW_TILE):
    return pl.BlockSpec((tile, shape_cols), lambda i: (i, 0))


def _mm_tile(T, cands):
    return next(t for t in cands if T % t == 0)


MM_ROWS = (544, 512, ROW_TILE)


def _const(shape):
    nd = len(shape)
    return pl.BlockSpec(shape, lambda i: (0,) * nd)


def _mod_spec(nxt):
    return pl.BlockSpec((1, 6, D), lambda i: (i // nxt, 0, 0))


def _stat_spec(nxt):
    return pl.BlockSpec((1, 8, D), lambda i: (i // nxt, 0, 0))


def _xfer_out_shape(kind, a):
    r, c = a.shape
    shape = {"ag_blk": (N_DEV, r, c), "ag_cols": (r, N_DEV * c), "ag_rows": (N_DEV * r, c),
             "a2a_cols": (N_DEV, r, c // N_DEV), "a2a_rows": (N_DEV, r // N_DEV, c)}[kind]
    return jax.ShapeDtypeStruct(shape, a.dtype)


def _src_view(kind, ref, who):
    if kind == "a2a_cols":
        n = ref.shape[1] // N_DEV
        return ref.at[:, pl.ds(pl.multiple_of(who * n, LANES), n)]
    if kind == "a2a_rows":
        r = ref.shape[0] // N_DEV
        return ref.at[pl.ds(pl.multiple_of(who * r, 16), r), :]
    return ref


def _dst_view(kind, ref, who):
    if kind == "ag_cols":
        n = ref.shape[1] // N_DEV
        return ref.at[:, pl.ds(pl.multiple_of(who * n, LANES), n)]
    if kind == "ag_rows":
        r = ref.shape[0] // N_DEV
        return ref.at[pl.ds(pl.multiple_of(who * r, 16), r), :]
    return ref.at[who]


def _xfers(kinds, srcs, dsts, send_sems, recv_sems, local_sems, start):
    mx, my, mc = lax.axis_index("x"), lax.axis_index("y"), lax.axis_index("c")
    me = 4 * mx + 2 * my + mc
    peers = []
    for k in range(1, N_DEV):
        px, py, pc = mx ^ ((k >> 2) & 1), my ^ ((k >> 1) & 1), mc ^ (k & 1)
        peers.append(((px, py, pc), 4 * px + 2 * py + pc))
    for j, (kind, src, dst) in enumerate(zip(kinds, srcs, dsts)):
        own = pltpu.make_async_copy(_src_view(kind, src, me), _dst_view(kind, dst, me), local_sems.at[j])
        if start:
            own.start()
        for k, (pid, pidx) in enumerate(peers):
            sem = (N_DEV - 1) * j + k
            cp = pltpu.make_async_remote_copy(
                src_ref=_src_view(kind, src, pidx), dst_ref=_dst_view(kind, dst, me if start else pidx),
                send_sem=send_sems.at[sem], recv_sem=recv_sems.at[sem], device_id=pid, device_id_type=MESH)
            if start:
                cp.start()
            else:
                cp.wait_recv()
                cp.wait_send()
        if not start:
            own.wait()


def _xfer_sems(n):
    return [pltpu.SemaphoreType.DMA(((N_DEV - 1) * n,)), pltpu.SemaphoreType.DMA(((N_DEV - 1) * n,)),
            pltpu.SemaphoreType.DMA((n,))]


def _exchange(comm, name):
    kinds = [k for k, _ in comm]
    n = len(comm)

    def body(*refs):
        sems = refs[2 * n:]
        _xfers(kinds, refs[:n], refs[n:2 * n], *sems, start=True)
        _xfers(kinds, refs[:n], refs[n:2 * n], *sems, start=False)

    hbm = pl.BlockSpec(memory_space=pl.ANY)
    return pl.pallas_call(
        body, name=name, in_specs=[hbm] * n, out_specs=[hbm] * n,
        out_shape=[_xfer_out_shape(k, a) for k, a in comm], scratch_shapes=_xfer_sems(n),
    )(*[a for _, a in comm])


_HBM = pl.BlockSpec(memory_space=pltpu.HBM)
_SEM = pl.BlockSpec(memory_space=pltpu.SEMAPHORE)
_EFFECT = pltpu.SideEffectType.DATAFLOW_SIDE_EFFECTING


def _mesh_peers():
    mx, my, mc = lax.axis_index("x"), lax.axis_index("y"), lax.axis_index("c")
    peers = []
    for k in range(1, N_DEV):
        px, py, pc = mx ^ ((k >> 2) & 1), my ^ ((k >> 1) & 1), mc ^ (k & 1)
        peers.append(((px, py, pc), 4 * px + 2 * py + pc))
    return 4 * mx + 2 * my + mc, peers


def _xfer_start(comm, name):
    kinds = [k for k, _ in comm]
    n = len(comm)
    srcs = [pltpu.with_memory_space_constraint(a, pltpu.HBM) for _, a in comm]
    lands = []
    for k, a in comm:
        sd = _xfer_out_shape(k, a)
        lands.append(pltpu.with_memory_space_constraint(lax.empty(sd.shape, sd.dtype), pltpu.HBM))

    def body(*refs):
        src_refs, land_refs = refs[:n], refs[n:2 * n]
        send_sems, recv_sems, local_sems, token = refs[2 * n], refs[2 * n + 1], refs[2 * n + 2], refs[-1]
        my, peers = _mesh_peers()
        for j, kind in enumerate(kinds):
            pltpu.make_async_copy(_src_view(kind, src_refs[j], my), _dst_view(kind, land_refs[j], my),
                                  local_sems.at[j]).start()
            for k, (pid, pidx) in enumerate(peers):
                sem = (N_DEV - 1) * j + k
                pltpu.make_async_remote_copy(
                    src_ref=_src_view(kind, src_refs[j], pidx), dst_ref=_dst_view(kind, land_refs[j], my),
                    send_sem=send_sems.at[sem], recv_sem=recv_sems.at[sem], device_id=pid,
                    device_id_type=MESH).start()
        token[...] = jnp.zeros_like(token)

    sems = pltpu.SemaphoreType.DMA(((N_DEV - 1) * n,))
    outs = pl.pallas_call(
        body, name=name,
        out_shape=(sems, sems, pltpu.SemaphoreType.DMA((n,)), *[pltpu.HBM(a.shape, a.dtype) for a in srcs + lands],
                   jax.ShapeDtypeStruct((8, LANES), F32)),
        in_specs=[_HBM] * (2 * n),
        out_specs=(_SEM, _SEM, _SEM, *[_HBM] * (2 * n), pl.BlockSpec(memory_space=pltpu.VMEM)),
        input_output_aliases={k: 3 + k for k in range(2 * n)},
        compiler_params=pltpu.CompilerParams(has_side_effects=_EFFECT),
    )(*srcs, *lands)
    return (kinds, outs[0], outs[1], outs[2], list(outs[3:3 + 2 * n])), outs[-1]


def _xfer_wait(handle, after, name):
    kinds, send_sems, recv_sems, local_sems, bufs = handle
    n = len(kinds)

    def body(*refs):
        src_refs, land_refs = refs[:n], refs[n:2 * n]
        send_ref, recv_ref, local_ref = refs[2 * n], refs[2 * n + 1], refs[2 * n + 2]
        my, peers = _mesh_peers()
        for j, kind in enumerate(kinds):
            pltpu.make_async_copy(_src_view(kind, src_refs[j], my), _dst_view(kind, land_refs[j], my),
                                  local_ref.at[j]).wait()
            for k, (pid, pidx) in enumerate(peers):
                sem = (N_DEV - 1) * j + k
                cp = pltpu.make_async_remote_copy(
                    src_ref=_src_view(kind, src_refs[j], pidx), dst_ref=_dst_view(kind, land_refs[j], pidx),
                    send_sem=send_ref.at[sem], recv_sem=recv_ref.at[sem], device_id=pid, device_id_type=MESH)
                cp.wait_send()
                cp.wait_recv()

    outs = pl.pallas_call(
        body, name=name, out_shape=[pltpu.HBM(a.shape, a.dtype) for a in bufs],
        in_specs=[_HBM] * (2 * n) + [_SEM, _SEM, _SEM, pl.BlockSpec(memory_space=pl.ANY)],
        out_specs=[_HBM] * (2 * n), input_output_aliases={k: k for k in range(2 * n)},
        compiler_params=pltpu.CompilerParams(has_side_effects=_EFFECT),
    )(*bufs, send_sems, recv_sems, local_sems, after)
    return list(outs[n:])


def _call(body, *, grid, in_specs, out_specs, out_shape, args, name, sem, scratch=(), comm=()):
    n_in, n_out, n_scr, n_c = len(in_specs), len(out_specs), len(scratch), len(comm)
    if not comm:
        outs = pl.pallas_call(body, grid=grid, in_specs=list(in_specs), out_specs=list(out_specs),
                              out_shape=list(out_shape), scratch_shapes=list(scratch), name=name,
                              compiler_params=_params(sem))(*args)
        return list(outs), []
    kinds = [k for k, _ in comm]
    n_steps = grid[0]

    def wrapped(*refs):
        ins, csrc = refs[:n_in], refs[n_in:n_in + n_c]
        o0 = n_in + n_c
        outs, cdst = refs[o0:o0 + n_out], refs[o0 + n_out:o0 + n_out + n_c]
        s0 = o0 + n_out + n_c
        scr, sems = refs[s0:s0 + n_scr], refs[s0 + n_scr:]
        i = pl.program_id(0)

        @pl.when(i == 0)
        def _():
            _xfers(kinds, csrc, cdst, *sems, start=True)

        body(*ins, *outs, *scr)

        @pl.when(i == n_steps - 1)
        def _():
            _xfers(kinds, csrc, cdst, *sems, start=False)

    hbm = pl.BlockSpec(memory_space=pl.ANY)
    outs = pl.pallas_call(
        wrapped, grid=grid, in_specs=list(in_specs) + [hbm] * n_c, out_specs=list(out_specs) + [hbm] * n_c,
        out_shape=list(out_shape) + [_xfer_out_shape(k, a) for k, a in comm],
        scratch_shapes=list(scratch) + _xfer_sems(n_c), name=name, compiler_params=_params(("arbitrary",)),
    )(*args, *[a for _, a in comm])
    return list(outs[:n_out]), list(outs[n_out:])


def _norm_mod_fwd(X, nw, mods, si, nxt, name):
    T = X.shape[0]

    def body(x_ref, w_ref, m_ref, h_ref):
        x = x_ref[...]
        r = lax.rsqrt(_rowmean(x * x) + EPS)
        m = m_ref[0]
        y = x * r * w_ref[...]
        h_ref[...] = (y * (1.0 + m[si + 1:si + 2, :]) + m[si:si + 1, :]).astype(BF16)

    return pl.pallas_call(
        body, grid=(T // ROW_TILE,), name=name,
        in_specs=[_row(D), _const((1, D)), _mod_spec(nxt)],
        out_specs=_row(D), out_shape=jax.ShapeDtypeStruct((T, D), BF16),
        compiler_params=_params(("parallel",)),
    )(X, nw, mods)


def _norm_mod_bwd(dh, X, nw, mods, dres, si, nxt, name):
    T = X.shape[0]

    def body(dh_ref, x_ref, w_ref, m_ref, dres_ref, dx_ref, st_ref):
        i = pl.program_id(0)

        @pl.when((i == 0) | (i == nxt))
        def _():
            st_ref[...] = jnp.zeros_like(st_ref)

        x = x_ref[...]
        r = lax.rsqrt(_rowmean(x * x) + EPS)
        xn = x * r
        w = w_ref[...]
        scale = m_ref[0][si + 1:si + 2, :]
        dhv = dh_ref[...]
        dy = dhv * (1.0 + scale)
        dxn = dy * w
        dx_ref[...] = dres_ref[...] + r * (dxn - xn * _rowmean(dxn * xn))
        st_ref[0, 0:1, :] += _colsum(dhv)
        st_ref[0, 1:2, :] += _colsum(dhv * xn * w)
        st_ref[0, 2:3, :] += _colsum(dy * xn)

    return pl.pallas_call(
        body, grid=(T // ROW_TILE,), name=name,
        in_specs=[_row(D), _row(D), _const((1, D)), _mod_spec(nxt), _row(D)],
        out_specs=[_row(D), _stat_spec(nxt)],
        out_shape=[jax.ShapeDtypeStruct((T, D), F32), jax.ShapeDtypeStruct((2, 8, D), F32)],
        compiler_params=_params(("arbitrary",)),
    )(dh, X, nw, mods, dres)


def _resid_bwd(dX, y, mods, gi, nxt, name):
    T = dX.shape[0]

    def body(dx_ref, y_ref, m_ref, dy_ref, st_ref):
        i = pl.program_id(0)

        @pl.when((i == 0) | (i == nxt))
        def _():
            st_ref[...] = jnp.zeros_like(st_ref)

        dx = dx_ref[...]
        dy_ref[...] = (dx * m_ref[0][gi:gi + 1, :]).astype(BF16)
        st_ref[0, 0:1, :] += _colsum(dx * y_ref[...].astype(F32))

    return pl.pallas_call(
        body, grid=(T // ROW_TILE,), name=name,
        in_specs=[_row(D), _row(D), _mod_spec(nxt)],
        out_specs=[_row(D), _stat_spec(nxt)],
        out_shape=[jax.ShapeDtypeStruct((T, D), BF16), jax.ShapeDtypeStruct((2, 8, D), F32)],
        compiler_params=_params(("arbitrary",)),
    )(dX, y, mods)


def _loss_head(X, tgt, fw, nxt, name):
    T = X.shape[0]

    def body(x_ref, t_ref, w_ref, dx_ref, st_ref):
        i = pl.program_id(0)

        @pl.when(i == 0)
        def _():
            st_ref[...] = jnp.zeros_like(st_ref)

        @pl.when(i < nxt)
        def _():
            x = x_ref[...]
            r = lax.rsqrt(_rowmean(x * x) + EPS)
            xn = x * r
            w = w_ref[...]
            err = xn * w - t_ref[...]
            dy = err * (1.0 / D)
            dxn = dy * w
            dx_ref[...] = r * (dxn - xn * _rowmean(dxn * xn))
            st_ref[0:1, :] += _colsum(dy * xn)
            st_ref[1:2, :] += (0.5 / D) * _colsum(err * err)

        @pl.when(i >= nxt)
        def _():
            dx_ref[...] = jnp.zeros_like(dx_ref)

    return pl.pallas_call(
        body, grid=(T // ROW_TILE,), name=name,
        in_specs=[_row(D), pl.BlockSpec((ROW_TILE, D), lambda i: (jnp.minimum(i, nxt - 1), 0)), _const((1, D))],
        out_specs=[_row(D), _const((8, D))],
        out_shape=[jax.ShapeDtypeStruct((T, D), F32), jax.ShapeDtypeStruct((8, D), F32)],
        compiler_params=_params(("arbitrary",)),
    )(X, tgt, fw)


def _mm_nn(A, B, out_dtype, name, add=None, comm=()):
    T, K = A.shape
    N = B.shape[1]

    tm = _mm_tile(T, MM_ROWS)
    if add is None:
        def body(a_ref, b_ref, o_ref):
            o_ref[...] = _dot(a_ref[...], b_ref[...]).astype(out_dtype)
        ins, specs = (A, B), [_row(K, tm), _const((K, N))]
    else:
        def body(a_ref, b_ref, c_ref, o_ref):
            o_ref[...] = (c_ref[...] + _dot(a_ref[...], b_ref[...])).astype(out_dtype)
        ins, specs = (A, B, add), [_row(K, tm), _const((K, N)), _row(N, tm)]

    return _call(body, grid=(T // tm,), name=name, in_specs=specs,
                 out_specs=[_row(N, tm)], out_shape=[jax.ShapeDtypeStruct((T, N), out_dtype)],
                 sem=("parallel",), args=ins, comm=comm)


def _mm_nn_resid(A, B, X, mods, gi, seq, name, comm=()):
    T, K = A.shape
    N = B.shape[1]
    tm = _mm_tile(T, MM_ROWS)

    def body(a_ref, b_ref, x_ref, m_ref, xo_ref, y_ref):
        acc = _dot(a_ref[...], b_ref[...])
        y_ref[...] = acc.astype(BF16)
        row = pl.program_id(0) * tm + lax.broadcasted_iota(jnp.int32, (tm, 1), 0)
        gate = jnp.where(row >= seq, m_ref[1][gi:gi + 1, :], m_ref[0][gi:gi + 1, :])
        xo_ref[...] = x_ref[...] + gate * acc

    return _call(body, grid=(T // tm,), name=name,
                 in_specs=[_row(K, tm), _const((K, N)), _row(N, tm), _const((2, 6, D))],
                 out_specs=[_row(N, tm), _row(N, tm)],
                 out_shape=[jax.ShapeDtypeStruct((T, N), F32), jax.ShapeDtypeStruct((T, N), BF16)],
                 sem=("parallel",), args=(A, B, X, mods), comm=comm)


def _mm_nt(A, B, name, comm=()):
    T, N = A.shape
    K = B.shape[0]
    tm = _mm_tile(T, MM_ROWS)

    def body(a_ref, b_ref, o_ref):
        o_ref[...] = _dot_nt(a_ref[...], b_ref[...])

    return _call(body, grid=(T // tm,), name=name, in_specs=[_row(N, tm), _const((K, N))], out_specs=[_row(K, tm)],
                 out_shape=[jax.ShapeDtypeStruct((T, K), F32)], sem=("parallel",), args=(A, B), comm=comm)


def _mm_tn(A, G, name):
    T, K = A.shape
    N = G.shape[1]
    tm = ROW_TILE
    nt = T // tm

    def body(a_ref, g_ref, o_ref, acc_ref):
        i = pl.program_id(0)

        @pl.when(i == 0)
        def _():
            acc_ref[...] = jnp.zeros_like(acc_ref)

        acc_ref[...] += _dot_tn(a_ref[...], g_ref[...])

        @pl.when(i == nt - 1)
        def _():
            o_ref[...] = acc_ref[...].astype(BF16)

    return pl.pallas_call(
        body, grid=(nt,), name=name,
        in_specs=[_row(K, tm), _row(N, tm)],
        out_specs=_const((K, N)), out_shape=jax.ShapeDtypeStruct((K, N), BF16),
        scratch_shapes=[pltpu.VMEM((K, N), F32)],
        compiler_params=_params(("arbitrary",)),
    )(A, G)


def _ffn_up(h, WgT, WuT, name, comm=()):
    T = h.shape[0]
    F = WgT.shape[0]

    def body(h_ref, wg_ref, wu_ref, gp_ref, up_ref, act_ref):
        hv = h_ref[...]
        gp = _dot_nt(hv, wg_ref[...])
        up = _dot_nt(hv, wu_ref[...])
        gp_ref[...] = gp.astype(BF16)
        up_ref[...] = up.astype(BF16)
        act_ref[...] = (gp * _sigmoid(gp) * up).astype(BF16)

    sd = jax.ShapeDtypeStruct((T, F), BF16)
    return _call(body, grid=(T // ROW_TILE,), name=name, in_specs=[_row(D), _const((F, D)), _const((F, D))],
                 out_specs=[_row(F), _row(F), _row(F)], out_shape=[sd, sd, sd], sem=("parallel",),
                 args=(h, WgT, WuT), comm=comm)


def _ffn_dact(dy, Wd, gp, up, name, comm=()):
    T = dy.shape[0]
    F = Wd.shape[0]

    def body(dy_ref, wd_ref, gp_ref, up_ref, dgp_ref, dup_ref):
        dact = _dot_nt(dy_ref[...], wd_ref[...])
        gpv = gp_ref[...].astype(F32)
        upv = up_ref[...].astype(F32)
        sg = _sigmoid(gpv)
        dup_ref[...] = (dact * gpv * sg).astype(BF16)
        dgp_ref[...] = (dact * upv * sg * (1.0 + gpv * (1.0 - sg))).astype(BF16)

    sd = jax.ShapeDtypeStruct((T, F), BF16)
    return _call(body, grid=(T // ROW_TILE,), name=name, in_specs=[_row(D), _const((F, D)), _row(F), _row(F)],
                 out_specs=[_row(F), _row(F)], out_shape=[sd, sd], sem=("parallel",),
                 args=(dy, Wd, gp, up), comm=comm)


N_TAB = 7


def _ret_tables(tab_ref, lgf, lgb):
    r = lax.broadcasted_iota(jnp.int32, (CHUNK, CHUNK), 0).astype(F32)
    c = lax.broadcasted_iota(jnp.int32, (CHUNK, CHUNK), 1).astype(F32)
    for d, lg in ((0, lgf), (1, lgb)):
        if d == 0:
            mask, expo, xe, ze = r >= c, r - c, r + 1.0, (CHUNK - 1.0) - r
        else:
            mask, expo, xe, ze = c > r, c - r - 1.0, (CHUNK - 1.0) - r, r
        e = jnp.where(mask, expo, 0.0)
        tab_ref[N_TAB * d + 0] = jnp.where(mask, jnp.exp(lg * e), 0.0)
        tab_ref[N_TAB * d + 1] = jnp.exp(lg * xe)
        tab_ref[N_TAB * d + 2] = jnp.exp(lg * ze)
        tab_ref[N_TAB * d + 3] = jnp.exp(jnp.full((CHUNK, CHUNK), lg * float(CHUNK), F32))
        tab_ref[N_TAB * d + 4] = e
        tab_ref[N_TAB * d + 5] = xe
        tab_ref[N_TAB * d + 6] = ze


def _rope(t, cosf, sgn):
    return t * cosf + pltpu.roll(t, HEAD_DIM // 2, 1) * sgn


def _rope_t(d, cosf, sgn):
    return d * cosf + pltpu.roll(d * sgn, HEAD_DIM // 2, 1)


def _chunk_of(d, j, nc, ncx):
    return lax.rem(j + ncx, nc) if d == 0 else nc - 1 - j


def _head_spec(T, col0):
    return pl.BlockSpec((T, HEAD_DIM), lambda h: (0, col0 + h))


def _ret_fwd(p, lg, cosf, sgn, seq, name, comm=()):
    T = p.shape[0]
    nc, ncx = T // CHUNK, seq // CHUNK

    def body(lg_ref, q_ref, k_ref, v_ref, g_ref, cos_ref, sgn_ref, y_ref, ro_ref, tab_ref, of_ref):
        h = pl.program_id(0)
        _ret_tables(tab_ref, lg_ref[0, h], lg_ref[1, h])

        def load(cidx):
            rows = pl.ds(pl.multiple_of(cidx * CHUNK, CHUNK), CHUNK)
            cs, sn = cos_ref[rows, :], sgn_ref[rows, :]
            return rows, _rope(q_ref[rows, :], cs, sn) * Q_SCALE, _rope(k_ref[rows, :], cs, sn), v_ref[rows, :]

        def chunk(d, q, k, v, S):
            b = N_TAB * d
            kb, vb = k.astype(BF16), v.astype(BF16)
            pm = _dot_nt(q.astype(BF16), kb) * tab_ref[b]
            o = _dot(pm.astype(BF16), vb) + _dot((q * tab_ref[b + 1]).astype(BF16), S.astype(BF16))
            return o, S * tab_ref[b + 3] + _dot_tn((k * tab_ref[b + 2]).astype(BF16), vb)

        def step(j, carry):
            Sf, Sb = carry
            rows, q, k, v = load(_chunk_of(0, j, nc, ncx))
            o, Sf = chunk(0, q, k, v, Sf)
            y_ref[rows, :] = o
            rows, q, k, v = load(_chunk_of(1, j, nc, ncx))
            o, Sb = chunk(1, q, k, v, Sb)
            of_ref[rows, :] = o
            return Sf, Sb

        zero = jnp.zeros((CHUNK, CHUNK), F32)
        lax.fori_loop(0, nc, step, (zero, zero))

        def finish(cidx, carry):
            rows = pl.ds(pl.multiple_of(cidx * CHUNK, CHUNK), CHUNK)
            y = y_ref[rows, :] + of_ref[rows, :]
            y_ref[rows, :] = y
            g = g_ref[rows, :]
            ro_ref[rows, :] = (g * _sigmoid(g) * y * lax.rsqrt(_rowmean(y * y) + EPS)).astype(BF16)
            return carry

        lax.fori_loop(0, nc, finish, 0)

    tbl = pl.BlockSpec((T, HEAD_DIM), lambda h: (0, 0))
    return _call(
        body, grid=(RET_HEADS,), name=name,
        in_specs=[pl.BlockSpec(memory_space=pltpu.SMEM), _head_spec(T, 0), _head_spec(T, 4), _head_spec(T, 8),
                  _head_spec(T, 12), tbl, tbl],
        out_specs=[_head_spec(T, 0), _head_spec(T, 0)],
        out_shape=[jax.ShapeDtypeStruct((T, RET_W), F32), jax.ShapeDtypeStruct((T, RET_W), BF16)],
        scratch=[pltpu.VMEM((2 * N_TAB, CHUNK, CHUNK), F32), pltpu.VMEM((T, HEAD_DIM), F32)],
        sem=("arbitrary",), args=(lg, p, p, p, p, cosf, sgn), comm=comm)


def _ret_gate_bwd(yret, p, dmix, name):
    T = yret.shape[0]

    def body(y_ref, g_ref, dm_ref, dy_ref, dg_ref):
        for hh in range(RET_HEADS):
            sl = slice(hh * HEAD_DIM, (hh + 1) * HEAD_DIM)
            y = y_ref[:, sl]
            r = lax.rsqrt(_rowmean(y * y) + EPS)
            yn = y * r
            g = g_ref[:, sl]
            sg = _sigmoid(g)
            dro = dm_ref[:, sl]
            dg_ref[:, sl] = (dro * yn * sg * (1.0 + g * (1.0 - sg))).astype(BF16)
            dyn = dro * g * sg
            dy_ref[:, sl] = r * (dyn - yn * _rowmean(dyn * yn))

    return pl.pallas_call(
        body, grid=(T // ROW_TILE,), name=name,
        in_specs=[_row(RET_W), pl.BlockSpec((ROW_TILE, RET_W), lambda i: (i, 3)), _row(RET_W)],
        out_specs=[_row(RET_W), _row(RET_W)],
        out_shape=[jax.ShapeDtypeStruct((T, RET_W), F32), jax.ShapeDtypeStruct((T, RET_W), BF16)],
        compiler_params=_params(("parallel",)),
    )(yret, p, dmix)


def _ret_bwd(p, lg, dy, cosf, sgn, seq, name, comm=()):
    T = p.shape[0]
    nc, ncx = T // CHUNK, seq // CHUNK

    def body(lg_ref, q_ref, k_ref, v_ref, dy_ref, cos_ref, sgn_ref, dq_ref, dk_ref, dv_ref, dlg_ref,
             tab_ref, st_ref, dqs, dks, dvs):
        h = pl.program_id(0)
        _ret_tables(tab_ref, lg_ref[0, h], lg_ref[1, h])
        dlg_ref[...] = jnp.zeros_like(dlg_ref)

        def load(cidx):
            rows = pl.ds(pl.multiple_of(cidx * CHUNK, CHUNK), CHUNK)
            cs, sn = cos_ref[rows, :], sgn_ref[rows, :]
            return rows, _rope(q_ref[rows, :], cs, sn) * Q_SCALE, _rope(k_ref[rows, :], cs, sn), v_ref[rows, :]

        def states(j, carry):
            out = []
            for d, S in enumerate(carry):
                b = N_TAB * d
                _, _, k, v = load(_chunk_of(d, j, nc, ncx))
                st_ref[d, j] = S
                out.append(S * tab_ref[b + 3] + _dot_tn((k * tab_ref[b + 2]).astype(BF16), v.astype(BF16)))
            return tuple(out)

        zero = jnp.zeros((CHUNK, CHUNK), F32)
        lax.fori_loop(0, nc, states, (zero, zero))

        def sweep_one(d, j, dS, acc):
            b = N_TAB * d
            rows, q, k, v = load(_chunk_of(d, j, nc, ncx))
            dO = dy_ref[rows, :]
            Sp = st_ref[d, j]
            qb, kb, vb, dOb = q.astype(BF16), k.astype(BF16), v.astype(BF16), dO.astype(BF16)
            Spb, dSb = Sp.astype(BF16), dS.astype(BF16)
            dmat, xi, ze, cd = tab_ref[b], tab_ref[b + 1], tab_ref[b + 2], tab_ref[b + 3]
            pm = _dot_nt(qb, kb) * dmat
            dpm = _dot_nt(dOb, vb)
            acc = acc + _colsum(dpm * pm * tab_ref[b + 4])
            dsc = (dpm * dmat).astype(BF16)
            qx = (q * xi).astype(BF16)
            kz = (k * ze).astype(BF16)
            dq = _dot(dsc, kb) + _dot_nt(dOb, Spb) * xi
            dk = _dot_tn(dsc, qb) + _dot_nt(vb, dSb) * ze
            dvst = _dot(kz, dSb)
            dv = _dot_tn(pm.astype(BF16), dOb) + dvst
            inter = _dot(qx, Spb)
            acc = acc + _colsum(dO * inter * tab_ref[b + 5])
            acc = acc + float(CHUNK) * _colsum(dS * cd * Sp) + _colsum(v * dvst * tab_ref[b + 6])
            dqs[d, rows, :] = dq.astype(BF16)
            dks[d, rows, :] = dk.astype(BF16)
            dvs[d, rows, :] = dv.astype(BF16)
            return dS * cd + _dot_tn(qx, dOb), acc

        def sweep(jj, carry):
            dSf, accf, dSb, accb = carry
            j = nc - 1 - jj
            dSf, accf = sweep_one(0, j, dSf, accf)
            dSb, accb = sweep_one(1, j, dSb, accb)
            return dSf, accf, dSb, accb

        zacc = jnp.zeros((1, CHUNK), F32)
        _, accf, _, accb = lax.fori_loop(0, nc, sweep, (zero, zacc, zero, zacc))
        dlg_ref[0, 0:1, :] = jnp.zeros((1, LANES), F32) + jnp.sum(accf)
        dlg_ref[0, 1:2, :] = jnp.zeros((1, LANES), F32) + jnp.sum(accb)

        def finish(cidx, carry):
            rows = pl.ds(pl.multiple_of(cidx * CHUNK, CHUNK), CHUNK)
            cs, sn = cos_ref[rows, :], sgn_ref[rows, :]
            both = lambda ref: ref[0, rows, :].astype(F32) + ref[1, rows, :].astype(F32)
            dq_ref[rows, :] = (_rope_t(both(dqs), cs, sn) * Q_SCALE).astype(BF16)
            dk_ref[rows, :] = _rope_t(both(dks), cs, sn).astype(BF16)
            dv_ref[rows, :] = both(dvs).astype(BF16)
            return carry

        lax.fori_loop(0, nc, finish, 0)

    tbl = pl.BlockSpec((T, HEAD_DIM), lambda h: (0, 0))
    sd = jax.ShapeDtypeStruct((T, RET_W), BF16)
    return _call(
        body, grid=(RET_HEADS,), name=name,
        in_specs=[pl.BlockSpec(memory_space=pltpu.SMEM), _head_spec(T, 0), _head_spec(T, 4), _head_spec(T, 8),
                  _head_spec(T, 0), tbl, tbl],
        out_specs=[_head_spec(T, 0), _head_spec(T, 0), _head_spec(T, 0),
                   pl.BlockSpec((1, 8, LANES), lambda h: (h, 0, 0))],
        out_shape=[sd, sd, sd, jax.ShapeDtypeStruct((RET_HEADS, 8, LANES), F32)],
        scratch=[pltpu.VMEM((2 * N_TAB, CHUNK, CHUNK), F32), pltpu.VMEM((2, nc, CHUNK, CHUNK), F32),
                 pltpu.VMEM((2, T, HEAD_DIM), BF16), pltpu.VMEM((2, T, HEAD_DIM), BF16),
                 pltpu.VMEM((2, T, HEAD_DIM), BF16)],
        sem=("arbitrary",), args=(lg, p, p, p, dy, cosf, sgn), comm=comm)


def _halo_specs(T, cols, colblock):
    per = ROW_TILE // HALO
    last = T // HALO - 1
    prv = pl.BlockSpec((HALO, cols), lambda i: (jnp.maximum(i * per - 1, 0), colblock))
    nxt = pl.BlockSpec((HALO, cols), lambda i: (jnp.minimum((i + 1) * per, last), colblock))
    return prv, nxt


def _seq_edges(i, nxt, nt):
    first = (i == 0) | (i == nxt)
    last = (i == nxt - 1) | (i == nt - 1)
    return first, last


def _fill_pad(pad_ref, prv, cur, nxt, first, last):
    pad_ref[0:HALO, :] = jnp.where(first, 0.0, prv)
    pad_ref[HALO:HALO + ROW_TILE, :] = cur
    pad_ref[HALO + ROW_TILE:2 * HALO + ROW_TILE, :] = jnp.where(last, 0.0, nxt)


SUBLANES = 8
SHIFT_ROWS = ROW_TILE + 2 * HALO - SUBLANES


def _fill_shifts(sh_ref, pad_ref):
    for b in range(SUBLANES):
        sh_ref[b] = pad_ref[b:b + SHIFT_ROWS, :]


TAP_ROWS = 32


def _window(sh_ref, o, r0=0, rows=ROW_TILE):
    b = o % SUBLANES
    return sh_ref[b, o - b + r0:o - b + r0 + rows, :]


def _tap_sum(w_ref, sh_ref, offset_of_tap, r0):
    acc = jnp.zeros((TAP_ROWS, CONV_CH), F32)
    for k in range(CONV_K):
        acc = acc + w_ref[k:k + 1, :] * _window(sh_ref, offset_of_tap(k), r0, TAP_ROWS)
    return acc


def _ln_stats(x):
    mu = _rowmean(x)
    xc = x - mu
    rs = lax.rsqrt(_rowmean(xc * xc) + EPS)
    return xc * rs, rs


def _conv_fwd(p, cw, lnw, lnb, nxt, name, comm=()):
    T = p.shape[0]
    nt = T // ROW_TILE
    a_col, g_col = 4, 5

    def body(a_ref, g_ref, ap_ref, gp_ref, an_ref, gn_ref, w_ref, lw_ref, lb_ref, cv_ref, co_ref, pad_ref, sh_ref):
        i = pl.program_id(0)
        first, last = _seq_edges(i, nxt, nt)
        glu = lambda a, g: a * _sigmoid(g)
        _fill_pad(pad_ref, glu(ap_ref[...], gp_ref[...]), glu(a_ref[...], g_ref[...]),
                  glu(an_ref[...], gn_ref[...]), first, last)
        _fill_shifts(sh_ref, pad_ref)
        for r0 in range(0, ROW_TILE, TAP_ROWS):
            cv_ref[r0:r0 + TAP_ROWS, :] = _tap_sum(w_ref, sh_ref, lambda k: k + 1, r0)
        xh, _ = _ln_stats(cv_ref[...])
        z = xh * lw_ref[...] + lb_ref[...]
        co_ref[...] = (z * _sigmoid(z)).astype(BF16)

    cur = lambda cb: pl.BlockSpec((ROW_TILE, CONV_CH), lambda i: (i, cb))
    ap, an = _halo_specs(T, CONV_CH, a_col)
    gp, gn = _halo_specs(T, CONV_CH, g_col)
    return _call(
        body, grid=(nt,), name=name,
        in_specs=[cur(a_col), cur(g_col), ap, gp, an, gn, _const((32, CONV_CH)), _const((1, CONV_CH)),
                  _const((1, CONV_CH))],
        out_specs=[_row(CONV_CH), _row(CONV_CH)],
        out_shape=[jax.ShapeDtypeStruct((T, CONV_CH), F32), jax.ShapeDtypeStruct((T, CONV_CH), BF16)],
        scratch=[pltpu.VMEM((ROW_TILE + 2 * HALO, CONV_CH), F32), pltpu.VMEM((SUBLANES, SHIFT_ROWS, CONV_CH), F32)],
        sem=("parallel",), args=(p, p, p, p, p, p, cw, lnw, lnb), comm=comm)


def _conv_bwd(p, cv, dmix, cw, lnw, lnb, nxt, name, comm=()):
    T = p.shape[0]
    nt = T // ROW_TILE
    a_col, g_col = 4, 5

    def body(a_ref, g_ref, ap_ref, gp_ref, an_ref, gn_ref, cv_ref, cvp_ref, cvn_ref, dc_ref, dcp_ref, dcn_ref,
             w_ref, lw_ref, lb_ref, dp_ref, dw_ref, dl_ref, upad_ref, dpad_ref, ush_ref, dsh_ref):
        i = pl.program_id(0)
        first, last = _seq_edges(i, nxt, nt)

        @pl.when(i == 0)
        def _():
            dw_ref[...] = jnp.zeros_like(dw_ref)
            dl_ref[...] = jnp.zeros_like(dl_ref)

        lw, lb = lw_ref[...], lb_ref[...]

        def ln_bwd(cvv, dco):
            xh, rs = _ln_stats(cvv)
            z = xh * lw + lb
            sg = _sigmoid(z)
            dz = dco * sg * (1.0 + z * (1.0 - sg))
            dxh = dz * lw
            return rs * (dxh - _rowmean(dxh) - xh * _rowmean(dxh * xh)), dz, xh

        dcv, dz, xh = ln_bwd(cv_ref[...], dc_ref[...])
        _fill_pad(dpad_ref, ln_bwd(cvp_ref[...], dcp_ref[...])[0], dcv, ln_bwd(cvn_ref[...], dcn_ref[...])[0],
                  first, last)
        a, g = a_ref[...], g_ref[...]
        sg = _sigmoid(g)
        glu = lambda av, gv: av * _sigmoid(gv)
        _fill_pad(upad_ref, glu(ap_ref[...], gp_ref[...]), a * sg, glu(an_ref[...], gn_ref[...]), first, last)
        _fill_shifts(dsh_ref, dpad_ref)
        _fill_shifts(ush_ref, upad_ref)
        for k in range(CONV_K):
            dw_ref[k:k + 1, :] += _colsum(dcv * _window(ush_ref, k + 1))
        dl_ref[0:1, :] += _colsum(dz * xh)
        dl_ref[1:2, :] += _colsum(dz)
        for r0 in range(0, ROW_TILE, TAP_ROWS):
            rs = slice(r0, r0 + TAP_ROWS)
            du = _tap_sum(w_ref, dsh_ref, lambda k: CONV_K - k, r0)
            ar, sr = a_ref[rs, :], _sigmoid(g_ref[rs, :])
            dp_ref[rs, 0:CONV_CH] = (du * sr).astype(BF16)
            dp_ref[rs, CONV_CH:2 * CONV_CH] = (du * ar * sr * (1.0 - sr)).astype(BF16)

    cur = lambda cb: pl.BlockSpec((ROW_TILE, CONV_CH), lambda i: (i, cb))
    ap, an = _halo_specs(T, CONV_CH, a_col)
    gp, gn = _halo_specs(T, CONV_CH, g_col)
    cvp, cvn = _halo_specs(T, CONV_CH, 0)
    dcp, dcn = _halo_specs(T, CONV_CH, 1)
    return _call(
        body, grid=(nt,), name=name,
        in_specs=[cur(a_col), cur(g_col), ap, gp, an, gn, cur(0), cvp, cvn, cur(1), dcp, dcn,
                  _const((32, CONV_CH)), _const((1, CONV_CH)), _const((1, CONV_CH))],
        out_specs=[_row(2 * CONV_CH), _const((32, CONV_CH)), _const((8, CONV_CH))],
        out_shape=[jax.ShapeDtypeStruct((T, 2 * CONV_CH), BF16), jax.ShapeDtypeStruct((32, CONV_CH), F32),
                   jax.ShapeDtypeStruct((8, CONV_CH), F32)],
        scratch=[pltpu.VMEM((ROW_TILE + 2 * HALO, CONV_CH), F32), pltpu.VMEM((ROW_TILE + 2 * HALO, CONV_CH), F32),
                 pltpu.VMEM((SUBLANES, SHIFT_ROWS, CONV_CH), F32), pltpu.VMEM((SUBLANES, SHIFT_ROWS, CONV_CH), F32)],
        sem=("arbitrary",), args=(p, p, p, p, p, p, cv, cv, cv, dmix, dmix, dmix, cw, lnw, lnb), comm=comm)


def _tile_positions(i, nxt, seq, ctx, offset, rows):
    is_ctx = i >= nxt
    pos0 = (i - jnp.where(is_ctx, nxt, 0)) * ROW_TILE + offset
    length = jnp.where(is_ctx, ctx, seq).astype(F32)
    pos = (pos0 + lax.broadcasted_iota(jnp.int32, (rows, 1), 0)).astype(F32)
    return pos, length


def _pool_count(pos, length, w):
    left = w // 2
    right = w - 1 - left
    return jnp.minimum(pos + right, length - 1.0) - jnp.maximum(pos - left, 0.0) + 1.0


def _gelu(x):
    return 0.5 * x * (1.0 + lax.erf(x * INV_SQRT2))


def _odd_fwd(p, pw, ps, slw, slb, sw, sbf, nxt, seq, ctx, name, comm=()):
    T = p.shape[0]
    nt = T // ROW_TILE

    def body(p_ref, pp_ref, pn_ref, pw_ref, ps_ref, lw_ref, lb_ref, sw_ref, sb_ref, o_ref, pad_ref):
        i = pl.program_id(0)
        first, last = _seq_edges(i, nxt, nt)
        _fill_pad(pad_ref, pp_ref[...], p_ref[:, 0:POOL_CH], pn_ref[...], first, last)
        pos, length = _tile_positions(i, nxt, seq, ctx, 0, ROW_TILE)
        for gi, w in enumerate(POOL_WINDOWS):
            sl = slice(gi * GROUP_CH, (gi + 1) * GROUP_CH)
            left = w // 2
            ssum = jnp.zeros((ROW_TILE, GROUP_CH), F32)
            for o in range(-left, w - left):
                ssum = ssum + pad_ref[HALO + o:HALO + o + ROW_TILE, sl]
            m = ssum / _pool_count(pos, length, w) - p_ref[:, sl]
            pre = _dot(m.astype(BF16), pw_ref[gi].astype(BF16))
            o_ref[:, sl] = (pre * ps_ref[:, sl]).astype(BF16)
        u = _gelu(p_ref[:, POOL_CH:POOL_CH + SG_CH])
        xh, _ = _ln_stats(_gelu(p_ref[:, POOL_CH + SG_CH:ODD_IN]))
        vln = xh * lw_ref[...] + lb_ref[...]
        for n in range(ROW_TILE // SG_CHUNK):
            rs = slice(n * SG_CHUNK, (n + 1) * SG_CHUNK)
            for gi in range(4):
                sl = slice(gi * GROUP_CH, (gi + 1) * GROUP_CH)
                s = _dot(sw_ref[gi].astype(BF16), vln[rs, sl].astype(BF16)) + sb_ref[:, sl]
                o_ref[rs, POOL_CH + gi * GROUP_CH:POOL_CH + (gi + 1) * GROUP_CH] = (u[rs, sl] * s).astype(BF16)

    pp, pn = _halo_specs(T, POOL_CH, 0)
    return _call(
        body, grid=(nt,), name=name,
        in_specs=[_row(ODD_IN), pp, pn, _const((4, GROUP_CH, GROUP_CH)), _const((1, POOL_CH)), _const((1, SG_CH)),
                  _const((1, SG_CH)), _const((4, SG_CHUNK, SG_CHUNK)), _const((SG_CHUNK, SG_CH))],
        out_specs=[_row(D)], out_shape=[jax.ShapeDtypeStruct((T, D), BF16)],
        scratch=[pltpu.VMEM((ROW_TILE + 2 * HALO, POOL_CH), F32)],
        sem=("parallel",), args=(p, p, p, pw, ps, slw, slb, sw, sbf), comm=comm)


def _odd_bwd(p, dmix, pw, ps, slw, slb, sw, sbf, nxt, seq, ctx, name, comm=()):
    T = p.shape[0]
    nt = T // ROW_TILE

    def body(p_ref, pp_ref, pn_ref, dm_ref, dmp_ref, dmn_ref, pw_ref, ps_ref, lw_ref, lb_ref, sw_ref, sb_ref,
             dp_ref, dpw_ref, dsw_ref, dv_ref, dsb_ref, pad_ref, dpad_ref):
        i = pl.program_id(0)
        first, last = _seq_edges(i, nxt, nt)

        @pl.when(i == 0)
        def _():
            dpw_ref[...] = jnp.zeros_like(dpw_ref)
            dsw_ref[...] = jnp.zeros_like(dsw_ref)
            dv_ref[...] = jnp.zeros_like(dv_ref)
            dsb_ref[...] = jnp.zeros_like(dsb_ref)

        _fill_pad(pad_ref, pp_ref[...], p_ref[:, 0:POOL_CH], pn_ref[...], first, last)
        pos, length = _tile_positions(i, nxt, seq, ctx, 0, ROW_TILE)
        pos_p, _ = _tile_positions(i, nxt, seq, ctx, -HALO, HALO)
        pos_n, _ = _tile_positions(i, nxt, seq, ctx, ROW_TILE, HALO)
        scale = ps_ref[...]
        for gi, w in enumerate(POOL_WINDOWS):
            sl = slice(gi * GROUP_CH, (gi + 1) * GROUP_CH)
            left = w // 2
            right = w - 1 - left
            ssum = jnp.zeros((ROW_TILE, GROUP_CH), F32)
            for o in range(-left, right + 1):
                ssum = ssum + pad_ref[HALO + o:HALO + o + ROW_TILE, sl]
            cnt = _pool_count(pos, length, w)
            m = ssum / cnt - p_ref[:, sl]
            wg = pw_ref[gi].astype(BF16)
            pre = _dot(m.astype(BF16), wg)
            dpo = dm_ref[:, sl]
            dv_ref[0:1, sl] += _colsum(dpo * pre)
            dpre = (dpo * scale[:, sl]).astype(BF16)
            dpw_ref[gi] += _dot_tn(m.astype(BF16), dpre)
            dmc = _dot_nt(dpre, wg)
            halo_dm = lambda ref, ps_: _dot_nt((ref[:, sl] * scale[:, sl]).astype(BF16), wg) / _pool_count(ps_, length, w)
            dpad_ref[0:HALO, sl] = jnp.where(first, 0.0, halo_dm(dmp_ref, pos_p))
            dpad_ref[HALO:HALO + ROW_TILE, sl] = dmc / cnt
            dpad_ref[HALO + ROW_TILE:2 * HALO + ROW_TILE, sl] = jnp.where(last, 0.0, halo_dm(dmn_ref, pos_n))
            atd = jnp.zeros((ROW_TILE, GROUP_CH), F32)
            for o in range(-right, left + 1):
                atd = atd + dpad_ref[HALO + o:HALO + o + ROW_TILE, sl]
            dp_ref[:, sl] = (atd - dmc).astype(BF16)

        pu = p_ref[:, POOL_CH:POOL_CH + SG_CH]
        pv = p_ref[:, POOL_CH + SG_CH:ODD_IN]
        u = _gelu(pu)
        xh, rs_ = _ln_stats(_gelu(pv))
        lw = lw_ref[...]
        vln = xh * lw + lb_ref[...]
        dgelu = lambda x: 0.5 * (1.0 + lax.erf(x * INV_SQRT2)) + x * jnp.exp(-0.5 * x * x) * INV_SQRT_2PI
        for n in range(ROW_TILE // SG_CHUNK):
            rs = slice(n * SG_CHUNK, (n + 1) * SG_CHUNK)
            dvl = []
            for gi in range(4):
                sl = slice(gi * GROUP_CH, (gi + 1) * GROUP_CH)
                wq = sw_ref[gi].astype(BF16)
                vb = vln[rs, sl].astype(BF16)
                s = _dot(wq, vb) + sb_ref[:, sl]
                dsg = dm_ref[rs, POOL_CH + gi * GROUP_CH:POOL_CH + (gi + 1) * GROUP_CH]
                ds = dsg * u[rs, sl]
                dsb_ref[:, sl] += ds
                dsw_ref[gi] += _dot_nt(ds.astype(BF16), vb)
                dvl.append(_dot_tn(wq, ds.astype(BF16)))
                dp_ref[rs, POOL_CH + gi * GROUP_CH:POOL_CH + (gi + 1) * GROUP_CH] = (
                    dsg * s * dgelu(pu[rs, sl])).astype(BF16)
            dvln = jnp.concatenate(dvl, axis=1)
            xhc = xh[rs, :]
            dv_ref[1:2, :] += _colsum(dvln * xhc)
            dv_ref[2:3, :] += _colsum(dvln)
            dxh = dvln * lw
            dvv = rs_[rs, :] * (dxh - _rowmean(dxh) - xhc * _rowmean(dxh * xhc))
            dp_ref[rs, POOL_CH + SG_CH:ODD_IN] = (dvv * dgelu(pv[rs, :])).astype(BF16)

    pp, pn = _halo_specs(T, POOL_CH, 0)
    dmp, dmn = _halo_specs(T, POOL_CH, 0)
    gsd = jax.ShapeDtypeStruct((4, GROUP_CH, GROUP_CH), F32)
    return _call(
        body, grid=(nt,), name=name,
        in_specs=[_row(ODD_IN), pp, pn, _row(D), dmp, dmn, _const((4, GROUP_CH, GROUP_CH)), _const((1, POOL_CH)),
                  _const((1, SG_CH)), _const((1, SG_CH)), _const((4, SG_CHUNK, SG_CHUNK)), _const((SG_CHUNK, SG_CH))],
        out_specs=[_row(ODD_IN), _const((4, GROUP_CH, GROUP_CH)), _const((4, SG_CHUNK, SG_CHUNK)), _const((8, POOL_CH)),
                   _const((SG_CHUNK, SG_CH))],
        out_shape=[jax.ShapeDtypeStruct((T, ODD_IN), BF16), gsd, gsd, jax.ShapeDtypeStruct((8, POOL_CH), F32),
                   jax.ShapeDtypeStruct((SG_CHUNK, SG_CH), F32)],
        scratch=[pltpu.VMEM((ROW_TILE + 2 * HALO, POOL_CH), F32), pltpu.VMEM((ROW_TILE + 2 * HALO, POOL_CH), F32)],
        sem=("arbitrary",), args=(p, p, p, dmix, dmix, dmix, pw, ps, slw, slb, sw, sbf), comm=comm)


def _ada_fwd(cs, aw, ab, name):
    cols = aw.shape[2]

    def body(c_ref, w_ref, b_ref, o_ref):
        c = c_ref[...]
        s = (c * _sigmoid(c)).astype(BF16)
        o_ref[0] = _dot(s, w_ref[0].astype(BF16)) + b_ref[0]

    return pl.pallas_call(
        body, grid=(DEPTH,), name=name,
        in_specs=[pl.BlockSpec((16, D), lambda i: (0, 0)), pl.BlockSpec((1, D, cols), lambda i: (i, 0, 0)),
                  pl.BlockSpec((1, 1, cols), lambda i: (i, 0, 0))],
        out_specs=pl.BlockSpec((1, 16, cols), lambda i: (i, 0, 0)),
        out_shape=jax.ShapeDtypeStruct((DEPTH, 16, cols), F32),
        compiler_params=_params(("parallel",)),
    )(cs, aw, ab)


def _ada_bwd(cs, G, aw, name):
    cols = aw.shape[2]

    def body(c_ref, g_ref, w_ref, gw_ref, cc_ref):
        i = pl.program_id(0)

        @pl.when(i == 0)
        def _():
            cc_ref[...] = jnp.zeros_like(cc_ref)

        c = c_ref[...]
        s = (c * _sigmoid(c)).astype(BF16)
        g = g_ref[0]
        dc = g[8:9, :]
        for d in range(9, 16):
            dc = dc + g[d:d + 1, :]
        row = lax.broadcasted_iota(jnp.int32, (8, cols), 0)
        dcrows = jnp.where(row == 0, dc, 0.0)
        dm = jnp.concatenate([g[0:8, :], dcrows], axis=0).astype(BF16)
        gw_ref[0] = _dot_tn(s, dm)
        cc_ref[...] += _dot_nt(dcrows.astype(BF16), w_ref[0].astype(BF16))

    return pl.pallas_call(
        body, grid=(DEPTH,), name=name,
        in_specs=[pl.BlockSpec((16, D), lambda i: (0, 0)), pl.BlockSpec((1, 16, cols), lambda i: (i, 0, 0)),
                  pl.BlockSpec((1, D, cols), lambda i: (i, 0, 0))],
        out_specs=[pl.BlockSpec((1, D, cols), lambda i: (i, 0, 0)), pl.BlockSpec((8, D), lambda i: (0, 0))],
        out_shape=[jax.ShapeDtypeStruct((DEPTH, D, cols), F32), jax.ShapeDtypeStruct((8, D), F32)],
        compiler_params=_params(("arbitrary",)),
    )(cs, G, aw)


def _row_tile_for(rows):
    for t in (512, 256, 128, 64, 32, 16, 8):
        if rows % t == 0:
            return t
    raise ValueError(f"rows={rows} is not a multiple of 8")


def _sum_devices(x, name):
    _, rows, cols = x.shape
    tr = _row_tile_for(rows)

    def body(x_ref, o_ref):
        acc = x_ref[0]
        for d in range(1, N_DEV):
            acc = acc + x_ref[d]
        o_ref[...] = acc

    return pl.pallas_call(
        body, grid=(rows // tr,), name=name,
        in_specs=[pl.BlockSpec((N_DEV, tr, cols), lambda i: (0, i, 0))],
        out_specs=pl.BlockSpec((tr, cols), lambda i: (i, 0)),
        out_shape=jax.ShapeDtypeStruct((rows, cols), F32),
        compiler_params=_params(("parallel",)),
    )(x)


def _adamw(w, g, m, v, name):
    rows, cols = w.shape
    tr = _row_tile_for(rows)

    def body(w_ref, g_ref, m_ref, v_ref, go_ref, d_ref, mo_ref, vo_ref):
        gv = g_ref[...]
        mn = ADAM_B1 * m_ref[...] + (1.0 - ADAM_B1) * gv
        vn = ADAM_B2 * v_ref[...] + (1.0 - ADAM_B2) * (gv * gv)
        go_ref[...] = gv
        mo_ref[...] = mn
        vo_ref[...] = vn
        d_ref[...] = -ADAM_LR * ((mn / ADAM_BC1) / (jnp.sqrt(vn / ADAM_BC2) + ADAM_EPS) + ADAM_WD * w_ref[...])

    blk = pl.BlockSpec((tr, cols), lambda i: (i, 0))
    sd = jax.ShapeDtypeStruct((rows, cols), F32)
    return pl.pallas_call(
        body, grid=(rows // tr,), name=name,
        in_specs=[blk, blk, blk, blk], out_specs=[blk, blk, blk, blk], out_shape=[sd, sd, sd, sd],
        compiler_params=_params(("parallel",)),
    )(w, g, m, v)


def _adamw_layers(w, pieces, m, v, name):
    L, a, b = w.shape
    bp = pieces[0].shape[2]
    tr = next(t for t in (256, 128, 64, 32, 16) if a % t == 0)

    def body(w_ref, m_ref, v_ref, *rest):
        p_refs, (go_ref, d_ref, mo_ref, vo_ref) = rest[:L], rest[L:]
        layer = pl.program_id(0)
        for k in range(L):
            @pl.when(layer == k)
            def _(k=k):
                gv = p_refs[k][0].astype(F32)
                for d in range(1, N_DEV):
                    gv = gv + p_refs[k][d].astype(F32)
                gv = gv[:, :b]
                mn = ADAM_B1 * m_ref[0] + (1.0 - ADAM_B1) * gv
                vn = ADAM_B2 * v_ref[0] + (1.0 - ADAM_B2) * (gv * gv)
                go_ref[0] = gv
                mo_ref[0] = mn
                vo_ref[0] = vn
                d_ref[0] = -ADAM_LR * ((mn / ADAM_BC1) / (jnp.sqrt(vn / ADAM_BC2) + ADAM_EPS) + ADAM_WD * w_ref[0])

    blk = pl.BlockSpec((1, tr, b), lambda l, i: (l, i, 0))
    pspecs = [pl.BlockSpec((N_DEV, tr, bp), lambda l, i, k=k: (0, jnp.where(l == k, i, 0), 0)) for k in range(L)]
    sd = jax.ShapeDtypeStruct((L, a, b), F32)
    return pl.pallas_call(
        body, grid=(L, a // tr), name=name,
        in_specs=[blk, blk, blk] + pspecs, out_specs=[blk, blk, blk, blk], out_shape=[sd, sd, sd, sd],
        compiler_params=_params(("arbitrary", "arbitrary")),
    )(w, m, v, *pieces)


def _pack_rows(shape):
    return -(-math.prod(shape) // (8 * LANES)) * 8


def _pack(arrs, row_mult):
    parts = []
    for a in arrs:
        n, rows = math.prod(a.shape), _pack_rows(a.shape)
        parts.append(jnp.pad(a.reshape(-1).astype(F32), (0, rows * LANES - n)).reshape(rows, LANES))
    total = sum(p.shape[0] for p in parts)
    if total % row_mult:
        parts.append(jnp.zeros((-total % row_mult, LANES), F32))
    return jnp.concatenate(parts, axis=0)


def _unpack(packed, shapes):
    out, r0 = [], 0
    lead = packed.shape[:-2]
    for s in shapes:
        n, rows = math.prod(s), _pack_rows(s)
        piece = packed[..., r0:r0 + rows, :].reshape(lead + (rows * LANES,))
        out.append(piece[..., :n].reshape(lead + tuple(s)))
        r0 += rows
    return out


def _rope_tables(seq, ctx):
    def angles(ps):
        parts = []
        for pvec, n in zip(ps, ROPE_PAIRS):
            freq = ROPE_BASE ** (-jnp.arange(n, dtype=F32) / n)
            parts.append(pvec[:, None] * freq[None, :])
        return jnp.concatenate(parts, axis=-1)

    rows = seq // GRID_W
    grid_r = jnp.broadcast_to(jnp.arange(rows, dtype=F32)[:, None], (rows, GRID_W)).reshape(-1)
    grid_c = jnp.broadcast_to(jnp.arange(GRID_W, dtype=F32)[None, :], (rows, GRID_W)).reshape(-1)
    zc = jnp.zeros((ctx,), F32)
    ang = jnp.concatenate([angles((jnp.full((seq,), ctx, F32), grid_r, grid_c)),
                           angles((jnp.arange(ctx, dtype=F32), zc, zc))], axis=0)
    cos, sin = jnp.cos(ang), jnp.sin(ang)
    return jnp.concatenate([cos, cos], axis=1), jnp.concatenate([-sin, sin], axis=1)


def _layer_gather(sh, even):
    return [("ag_cols" if even else "ag_rows", sh["in"]), ("ag_rows", sh["out"]), ("ag_rows", sh["gate"]),
            ("ag_rows", sh["up"]), ("ag_rows", sh["down"])]


def _sample_step(X0, tgt, mods, shards, gather0, S, seq, ctx):
    nxt = seq // ROW_TILE
    cosf, sgn = _rope_tables(seq, ctx)
    X = X0
    saved = []
    gathers = {(0, "mix"): gather0}

    def start_gather(i, half, dep):
        comm = _layer_gather(shards[i], i % 2 == 0)
        comm = comm[:2] if half == "mix" else comm[2:]
        comm[0] = (comm[0][0], comm[0][1] + dep.astype(BF16))
        gathers[(i, half)], token = _xfer_start(comm, f"gather_{half}{i}_start")
        return token[0, 0]

    dep = mods[0, 0, 0, 0] * 0.0
    for i, half in ((0, "ffn"), (1, "mix"), (1, "ffn")):
        dep = start_gather(i, half, dep)
    w_in, w_out = _xfer_wait(gathers[(0, "mix")], mods, "gather_mix0_wait")
    for i in range(DEPTH):
        j, even = i // 2, i % 2 == 0
        t = f"l{i}_"
        last = i == DEPTH - 1
        if i in (1, 2):
            dep = w_out[0, 0].astype(F32) * 0.0
            for half in ("mix", "ffn"):
                dep = start_gather(i + 1, half, dep)
        md = mods[i] + dep
        h1 = _norm_mod_fwd(X, S["norm_w"][i, 0][None], md, 0, nxt, t + "norm1")
        if even:
            (p,), _ = _mm_nn(h1, w_in, F32, t + "proj_in")
            (yret, ro), _ = _ret_fwd(p, S["lg"][j], cosf, sgn, seq, t + "ret_fwd")
            (cv, co), _ = _conv_fwd(p, S["conv_w"][j], S["conv_ln_w"][j][None], S["conv_ln_b"][j][None], nxt,
                                    t + "conv_fwd")
            mix = jnp.concatenate([ro, co], axis=1)
            extra = (yret, cv)
        else:
            (p,), _ = _mm_nt(h1, w_in, t + "proj_in")
            sbf = jnp.repeat(S["sg_b"][j].T, GROUP_CH, axis=1)
            (mix,), _ = _odd_fwd(p, S["pool_w"][j], S["pool_scale"][j][None], S["sg_ln_w"][j][None],
                                 S["sg_ln_b"][j][None], S["sg_w"][j], sbf, nxt, seq, ctx, t + "odd_fwd")
            extra = (sbf,)
        (X1, y1), _ = _mm_nn_resid(mix, w_out, X, md, 2, seq, t + "proj_out")
        h2 = _norm_mod_fwd(X1, S["norm_w"][i, 1][None], md, 3, nxt, t + "norm2")
        wg, wu, wd = _xfer_wait(gathers[(i, "ffn")], h2, f"gather_ffn{i}_wait")
        (gp, up, act), _ = _ffn_up(h2, wg, wu, t + "ffn_up")
        (X2, y2), _ = _mm_nn_resid(act, wd, X1, md, 5, seq, t + "ffn_down")
        saved.append((X, h1, p, mix, extra, y1, X1, h2, gp, up, act, y2, (w_in, w_out, wg, wu, wd)))
        X = X2
        if not last:
            w_in, w_out = _xfer_wait(gathers[(i + 1, "mix")], X2, f"gather_mix{i + 1}_wait")

    dX, hst = _loss_head(X, tgt, S["final_norm_w"][None], nxt, "loss_head")
    loss = jnp.sum(hst[1])
    pieces = {}
    gS = {"final_norm_w": hst[0], "norm_w": [None] * DEPTH, "lg": [None] * 2, "conv_w": [None] * 2,
          "conv_ln_w": [None] * 2, "conv_ln_b": [None] * 2, "pool_w": [None] * 2, "pool_scale": [None] * 2,
          "sg_ln_w": [None] * 2, "sg_ln_b": [None] * 2, "sg_w": [None] * 2, "sg_b": [None] * 2}
    dmods = [None] * DEPTH
    in_flight = []

    def land(flights, after, tag):
        for handle, keys, name in flights:
            for key, got in zip(keys, _xfer_wait(handle, after, name + tag)):
                pieces[key] = got

    for i in reversed(range(DEPTH)):
        j, even = i // 2, i % 2 == 0
        md = mods[i]
        t = f"l{i}_"
        X_in, h1, p, mix, extra, y1, X1, h2, gp, up, act, y2, (w_in, w_out, wg, wu, wd) = saved[i]
        dyf, g2 = _resid_bwd(dX, y2, md, 5, nxt, t + "ffn_resid_bwd")
        (dgp, dup), _ = _ffn_dact(dyf, wd, gp, up, t + "ffn_dact")
        g_down = _mm_tn(act, dyf, t + "dw_down")
        g_gate = _mm_tn(dgp, h2, t + "dw_gate")
        g_up = _mm_tn(dup, h2, t + "dw_up")
        ffn_flight, token = _xfer_start([("a2a_rows", g_down), ("a2a_rows", g_gate), ("a2a_rows", g_up)],
                                        t + "scatter_ffn_start")
        (dh2,), _ = _mm_nn(dgp, wg, F32, t + "dh2_gate")
        (dh2,), _ = _mm_nn(dup, wu, F32, t + "dh2_up", add=dh2)
        dX1, s2 = _norm_mod_bwd(dh2, X1, S["norm_w"][i, 1][None], md + token[0, 0], dX, 3, nxt, t + "norm2_bwd")
        dym, g1 = _resid_bwd(dX1, y1, md, 2, nxt, t + "mix_resid_bwd")
        (dmix,), _ = _mm_nt(dym, w_out, t + "dmix")
        g_out = _mm_tn(mix, dym, t + "dw_out")
        if even:
            yret, cv = extra
            dyr, dg = _ret_gate_bwd(yret, p, dmix, t + "ret_gate_bwd")
            (dq, dk, dv, dlg), _ = _ret_bwd(p, S["lg"][j], dyr, cosf, sgn, seq, t + "ret_bwd")
            early = [("ag_blk", _pack([jnp.stack(gS["pool_w"]), jnp.stack(gS["sg_w"])], 8))] if i == 0 else []
            (dpc, dcw, dln), got = _conv_bwd(
                p, cv, dmix, S["conv_w"][j], S["conv_ln_w"][j][None], S["conv_ln_b"][j][None], nxt,
                t + "conv_bwd", comm=early)
            if i == 0:
                early_all = got[0]
            dp = jnp.concatenate([dq, dk, dv, dg, dpc], axis=1)
            gS["lg"][j] = dlg[:, 0:2, 0].T
            gS["conv_w"][j], gS["conv_ln_w"][j], gS["conv_ln_b"][j] = dcw, dln[0], dln[1]
            g_in = _mm_tn(h1, dp, t + "dw_in")
            mix_flight, token = _xfer_start([("a2a_rows", g_out), ("a2a_cols", g_in)], t + "scatter_mix_start")
            (dh1,), _ = _mm_nt(dp, w_in, t + "dh1")
        else:
            (sbf,) = extra
            (dp, dpw, dsw, dvec, dsb), _ = _odd_bwd(
                p, dmix, S["pool_w"][j], S["pool_scale"][j][None], S["sg_ln_w"][j][None], S["sg_ln_b"][j][None],
                S["sg_w"][j], sbf, nxt, seq, ctx, t + "odd_bwd")
            gS["pool_w"][j], gS["sg_w"][j] = dpw, dsw
            gS["pool_scale"][j], gS["sg_ln_w"][j], gS["sg_ln_b"][j] = dvec[0], dvec[1], dvec[2]
            gS["sg_b"][j] = jnp.sum(dsb.reshape(SG_CHUNK, 4, GROUP_CH), axis=2).T
            g_in = _mm_tn(dp, h1, t + "dw_in")
            mix_flight, token = _xfer_start([("a2a_rows", g_out), ("a2a_rows", g_in)], t + "scatter_mix_start")
            (dh1,), _ = _mm_nn(dp, w_in, F32, t + "dh1")
        dX, s1 = _norm_mod_bwd(dh1, X_in, S["norm_w"][i, 0][None], md + token[0, 0], dX1, 0, nxt, t + "norm1_bwd")
        gS["norm_w"][i] = jnp.stack([s1[0, 2] + s1[1, 2], s2[0, 2] + s2[1, 2]])
        dmods[i] = jnp.stack([s1[:, 0], s1[:, 1], g1[:, 0], s2[:, 0], s2[:, 1], g2[:, 0]], axis=1)
        land(in_flight, dX, "_wait")
        in_flight = [(ffn_flight, [("down", i), ("gate", i), ("up", i)], t + "scatter_ffn"),
                     (mix_flight, [("out", i), ("in", i)], t + "scatter_mix")]
    land(in_flight[:1], dX, "_wait")
    gS = {k: (jnp.stack(v) if isinstance(v, list) else v) for k, v in gS.items()}
    return loss, dX, pieces, gS, jnp.stack(dmods), early_all, in_flight[1]


FF_SHARD = D_FF // N_DEV
FF_SHARD_PAD = 384


def kernel(x, c, ctx, c_ctx, ada_w, ada_b, norm_w, even_w_in, even_w_out, ret_decay_logit, conv_dw_w, conv_ln_w, conv_ln_b, odd_w_in, odd_w_out, pool_w, pool_scale, sg_ln_w, sg_ln_b, sg_w, sg_b, ffn_w_gate, ffn_w_up, ffn_w_down, final_norm_w, loss_target, m_c_ctx, m_ada_w, m_ada_b, m_norm_w, m_even_w_in, m_even_w_out, m_ret_decay_logit, m_conv_dw_w, m_conv_ln_w, m_conv_ln_b, m_odd_w_in, m_odd_w_out, m_pool_w, m_pool_scale, m_sg_ln_w, m_sg_ln_b, m_sg_w, m_sg_b, m_ffn_w_gate, m_ffn_w_up, m_ffn_w_down, m_final_norm_w, v_c_ctx, v_ada_w, v_ada_b, v_norm_w, v_even_w_in, v_even_w_out, v_ret_decay_logit, v_conv_dw_w, v_conv_ln_w, v_conv_ln_b, v_odd_w_in, v_odd_w_out, v_pool_w, v_pool_scale, v_sg_ln_w, v_sg_ln_b, v_sg_w, v_sg_b, v_ffn_w_gate, v_ffn_w_up, v_ffn_w_down, v_final_norm_w):
    seq, n_ctx = x.shape[1], ctx.shape[1]
    me = 4 * lax.axis_index("x") + 2 * lax.axis_index("y") + lax.axis_index("c")
    mcols = ada_w.shape[2]

    tr = lambda a: jnp.swapaxes(a, 1, 2)
    pad_rows = lambda a: jnp.pad(a, ((0, FF_SHARD_PAD - FF_SHARD), (0, 0))).astype(BF16)
    gate_t, up_t, odd_in_t = tr(ffn_w_gate), tr(ffn_w_up), tr(odd_w_in)
    shards = []
    for i in range(DEPTH):
        j, even = i // 2, i % 2 == 0
        w_in, w_out = (even_w_in[j], even_w_out[j]) if even else (odd_in_t[j], odd_w_out[j])
        shards.append({"in": w_in.astype(BF16), "out": w_out.astype(BF16), "gate": pad_rows(gate_t[i]),
                       "up": pad_rows(up_t[i]), "down": pad_rows(ffn_w_down[i])})
    gather0, token = _xfer_start(_layer_gather(shards[0], True)[:2], "gather_mix0_start")

    small_shapes = [(D,), norm_w.shape, conv_dw_w.shape, pool_scale.shape, sg_ln_w.shape, sg_ln_b.shape]
    (sm,) = _exchange([("ag_blk", _pack([c + token[0, 0], norm_w, conv_dw_w, pool_scale, sg_ln_w, sg_ln_b], 8))],
                      "gather_small")
    c_all, nw_s, cw_s, ps_s, slw_s, slb_s = _unpack(sm, small_shapes)
    cat_last = lambda a: jnp.moveaxis(a, 0, -2).reshape(a.shape[1:-1] + (-1,))
    conv_w_full = cat_last(cw_s)
    S = {"norm_w": cat_last(nw_s), "lg": jax.nn.log_sigmoid(ret_decay_logit),
         "conv_w": jnp.pad(conv_w_full, ((0, 0), (0, 32 - CONV_K), (0, 0))),
         "conv_ln_w": conv_ln_w, "conv_ln_b": conv_ln_b, "pool_w": pool_w, "pool_scale": cat_last(ps_s),
         "sg_ln_w": cat_last(slw_s), "sg_ln_b": cat_last(slb_s), "sg_w": sg_w, "sg_b": sg_b,
         "final_norm_w": final_norm_w}

    cs = jnp.concatenate([c_all, c_ctx[None], jnp.zeros((7, D), F32)], axis=0)
    ab_loc = lax.dynamic_slice_in_dim(ada_b, me * mcols, mcols, axis=1)
    mod_loc = _ada_fwd(cs, ada_w, ab_loc[:, None, :], "ada_fwd")
    (mod_all,) = _exchange([("ag_blk", mod_loc.reshape(DEPTH * 16, mcols))], "gather_mod")
    mod_all = mod_all.reshape(N_DEV, DEPTH, 16, mcols).transpose(1, 2, 0, 3).reshape(DEPTH, 16, 6, D)
    mod_x = lax.dynamic_index_in_dim(mod_all, me, axis=1, keepdims=False)
    mods = jnp.stack([mod_x, mod_all[:, 8]], axis=1)

    X0 = jnp.concatenate([x[0], ctx[0]], axis=0)
    loss, dX, big, gS, dmods, early_all, late = _sample_step(X0, loss_target[0], mods, shards, gather0, S, seq,
                                                             n_ctx)
    loss = lax.psum(loss, ("x", "y", "c"))
    grad_x = dX[:seq][None]

    (dm_all,) = _exchange([("ag_blk", dmods.reshape(DEPTH * 2, 6 * D))], "gather_dmod")
    dm_all = dm_all.reshape(N_DEV, DEPTH, 2, 6 * D)
    dm_sum = _sum_devices(dm_all.reshape(N_DEV, DEPTH * 2, 6 * D), "sum_dmod").reshape(DEPTH, 2, 6 * D)
    g_ada_b = dm_sum[:, 0] + dm_sum[:, 1]
    G = lax.dynamic_slice_in_dim(dm_all, me * mcols, mcols, axis=3).transpose(1, 2, 0, 3).reshape(DEPTH, 16, mcols)
    g_ada_w, ccp = _ada_bwd(cs, G, ada_w, "ada_bwd")
    sg_cc = _sigmoid(c_ctx)
    g_cctx_part = ccp[0] * sg_cc * (1.0 + c_ctx * (1.0 - sg_cc))

    dsig = _sigmoid(-ret_decay_logit)
    part = [g_cctx_part, gS["norm_w"], gS["lg"] * dsig, gS["conv_w"][:, :CONV_K], gS["conv_ln_w"], gS["conv_ln_b"],
            gS["pool_scale"], gS["sg_ln_w"], gS["sg_ln_b"], gS["sg_b"], gS["final_norm_w"]]
    part_shapes = [a.shape for a in part]
    (part_all,) = _exchange([("ag_blk", _pack(part, 8))], "gather_small_grads")
    red = _sum_devices(part_all, "sum_small_grads")
    g_cctx, g_nw, g_rdl, g_cw, g_clw, g_clb, g_ps, g_slw, g_slb, g_sb, g_fnw = _unpack(red, part_shapes)
    g_pw, g_sw = _unpack(_sum_devices(early_all, "sum_early_grads"), [pool_w.shape, sg_w.shape])
    mine_last = lambda a, n: lax.dynamic_slice_in_dim(a, me * n, n, axis=a.ndim - 1)
    g_nw, g_cw = mine_last(g_nw, norm_w.shape[2]), mine_last(g_cw, conv_dw_w.shape[2])
    g_ps, g_slw, g_slb = (mine_last(a, pool_scale.shape[1]) for a in (g_ps, g_slw, g_slb))

    def adam_big(key, layers, w, m, v, name):
        return _adamw_layers(w, [big[(key, l)] for l in layers], m, v, name)

    res = {}
    adam_t = lambda key, layers, w, m, v, name: [tr(o) for o in adam_big(key, layers, tr(w), tr(m), tr(v), name)]
    res["odd_w_in"] = adam_t("in", (1, 3), odd_w_in, m_odd_w_in, v_odd_w_in, "adam_odd_in")
    res["odd_w_out"] = adam_big("out", (1, 3), odd_w_out, m_odd_w_out, v_odd_w_out, "adam_odd_out")
    res["ffn_w_gate"] = adam_t("gate", (0, 1, 2, 3), ffn_w_gate, m_ffn_w_gate, v_ffn_w_gate, "adam_gate")
    res["ffn_w_up"] = adam_t("up", (0, 1, 2, 3), ffn_w_up, m_ffn_w_up, v_ffn_w_up, "adam_up")
    res["ffn_w_down"] = adam_big("down", (0, 1, 2, 3), ffn_w_down, m_ffn_w_down, v_ffn_w_down, "adam_down")
    late_handle, late_keys, late_name = late
    for key, got in zip(late_keys, _xfer_wait(late_handle, res["ffn_w_down"][1], late_name + "_wait")):
        big[key] = got
    res["even_w_in"] = adam_big("in", (0, 2), even_w_in, m_even_w_in, v_even_w_in, "adam_even_in")
    res["even_w_out"] = adam_big("out", (0, 2), even_w_out, m_even_w_out, v_even_w_out, "adam_even_out")
    flat2 = lambda a: a.reshape(-1, a.shape[-1])
    res["ada_w"] = [o.reshape(ada_w.shape) for o in
                    _adamw(flat2(ada_w), flat2(g_ada_w), flat2(m_ada_w), flat2(v_ada_w), "adam_ada_w")]

    names = ["c_ctx", "ada_b", "norm_w", "ret_decay_logit", "conv_dw_w", "conv_ln_w", "conv_ln_b", "pool_w",
             "pool_scale", "sg_ln_w", "sg_ln_b", "sg_w", "sg_b", "final_norm_w"]
    ws = [c_ctx, ada_b, norm_w, ret_decay_logit, conv_dw_w, conv_ln_w, conv_ln_b, pool_w, pool_scale, sg_ln_w,
          sg_ln_b, sg_w, sg_b, final_norm_w]
    gs = [g_cctx, g_ada_b, g_nw, g_rdl, g_cw, g_clw, g_clb, g_pw, g_ps, g_slw, g_slb, g_sw, g_sb, g_fnw]
    ms = [m_c_ctx, m_ada_b, m_norm_w, m_ret_decay_logit, m_conv_dw_w, m_conv_ln_w, m_conv_ln_b, m_pool_w,
          m_pool_scale, m_sg_ln_w, m_sg_ln_b, m_sg_w, m_sg_b, m_final_norm_w]
    vs = [v_c_ctx, v_ada_b, v_norm_w, v_ret_decay_logit, v_conv_dw_w, v_conv_ln_w, v_conv_ln_b, v_pool_w,
          v_pool_scale, v_sg_ln_w, v_sg_ln_b, v_sg_w, v_sg_b, v_final_norm_w]
    shapes = [a.shape for a in ws]
    gs = [g.reshape(s) for g, s in zip(gs, shapes)]
    pk = lambda arrs: _pack(arrs, 512)
    outs = _adamw(pk(ws), pk(gs), pk(ms), pk(vs), "adam_small")
    for k, o in enumerate(outs):
        for nm, arr in zip(names, _unpack(o, shapes)):
            res.setdefault(nm, [None] * 4)[k] = arr

    order = ["c_ctx", "ada_w", "ada_b", "norm_w", "even_w_in", "even_w_out", "ret_decay_logit", "conv_dw_w",
             "conv_ln_w", "conv_ln_b", "odd_w_in", "odd_w_out", "pool_w", "pool_scale", "sg_ln_w", "sg_ln_b",
             "sg_w", "sg_b", "ffn_w_gate", "ffn_w_up", "ffn_w_down", "final_norm_w"]
    return (loss, grad_x, *[res[n][0] for n in order], *[res[n][1] for n in order],
            *[res[n][2] for n in order], *[res[n][3] for n in order])
```

```python
import math

import jax
import jax.numpy as jnp
from jax import lax
from jax.experimental import pallas as pl
from jax.experimental.pallas import tpu as pltpu

F32 = jnp.float32
BF16 = jnp.bfloat16

N_DEV = 8
D = 1024
DEPTH = 4
ROW_TILE = 256
HALO = 16
LANES = 128
EPS = 1e-6
RET_HEADS = 4
HEAD_DIM = 128
RET_W = RET_HEADS * HEAD_DIM
CHUNK = 128
Q_SCALE = HEAD_DIM ** -0.5
ROPE_BASE = 10000.0
ROPE_PAIRS = (HEAD_DIM // 8, 3 * HEAD_DIM // 16, 3 * HEAD_DIM // 16)
GRID_W = 64
CONV_CH = 512
CONV_K = 31
EVEN_IN = 4 * RET_W + 2 * CONV_CH
POOL_CH = 512
POOL_WINDOWS = (2, 4, 8, 16)
GROUP_CH = 128
SG_CH = 512
SG_CHUNK = 128
ODD_IN = POOL_CH + 2 * SG_CH
D_FF = 2816
INV_SQRT2 = 1.0 / math.sqrt(2.0)
INV_SQRT_2PI = 1.0 / math.sqrt(2.0 * math.pi)
ADAM_LR, ADAM_B1, ADAM_B2, ADAM_EPS, ADAM_WD, ADAM_STEP = 0.001, 0.9, 0.999, 1e-08, 0.01, 10
ADAM_BC1 = 1.0 - ADAM_B1 ** ADAM_STEP
ADAM_BC2 = 1.0 - ADAM_B2 ** ADAM_STEP
VMEM_LIMIT = 56 * 1024 * 1024
MESH = pl.DeviceIdType.MESH


def _params(sem=None):
    return pltpu.CompilerParams(dimension_semantics=sem, vmem_limit_bytes=VMEM_LIMIT)


def _dot(a, b):
    return jnp.dot(a, b, preferred_element_type=F32)


def _dot_nt(a, b):
    return lax.dot_general(a, b, (((1,), (1,)), ((), ())), preferred_element_type=F32)


def _dot_tn(a, b):
    return lax.dot_general(a, b, (((0,), (0,)), ((), ())), preferred_element_type=F32)


def _sigmoid(x):
    return 1.0 / (1.0 + jnp.exp(-x))


def _colsum(x):
    return jnp.sum(x, axis=0, keepdims=True)


def _rowmean(x):
    return jnp.mean(x, axis=-1, keepdims=True)


def _row(shape_cols, tile=ROW_TILE):
    return pl.BlockSpec((tile, shape_cols), lambda i: (i, 0))


def _mm_tile(T, cands):
    return next(t for t in cands if T % t == 0)


MM_ROWS = (544, 512, ROW_TILE)


def _const(shape):
    nd = len(shape)
    return pl.BlockSpec(shape, lambda i: (0,) * nd)


def _mod_spec(nxt):
    return pl.BlockSpec((1, 6, D), lambda i: (i // nxt, 0, 0))


def _stat_spec(nxt):
    return pl.BlockSpec((1, 8, D), lambda i: (i // nxt, 0, 0))


def _xfer_out_shape(kind, a):
    r, c = a.shape
    shape = {"ag_blk": (N_DEV, r, c), "ag_cols": (r, N_DEV * c), "ag_rows": (N_DEV * r, c),
             "a2a_cols": (N_DEV, r, c // N_DEV), "a2a_rows": (N_DEV, r // N_DEV, c)}[kind]
    return jax.ShapeDtypeStruct(shape, a.dtype)


def _src_view(kind, ref, who):
    if kind == "a2a_cols":
        n = ref.shape[1] // N_DEV
        return ref.at[:, pl.ds(pl.multiple_of(who * n, LANES), n)]
    if kind == "a2a_rows":
        r = ref.shape[0] // N_DEV
        return ref.at[pl.ds(pl.multiple_of(who * r, 16), r), :]
    return ref


def _dst_view(kind, ref, who):
    if kind == "ag_cols":
        n = ref.shape[1] // N_DEV
        return ref.at[:, pl.ds(pl.multiple_of(who * n, LANES), n)]
    if kind == "ag_rows":
        r = ref.shape[0] // N_DEV
        return ref.at[pl.ds(pl.multiple_of(who * r, 16), r), :]
    return ref.at[who]


def _xfers(kinds, srcs, dsts, send_sems, recv_sems, local_sems, start):
    mx, my, mc = lax.axis_index("x"), lax.axis_index("y"), lax.axis_index("c")
    me = 4 * mx + 2 * my + mc
    peers = []
    for k in range(1, N_DEV):
        px, py, pc = mx ^ ((k >> 2) & 1), my ^ ((k >> 1) & 1), mc ^ (k & 1)
        peers.append(((px, py, pc), 4 * px + 2 * py + pc))
    for j, (kind, src, dst) in enumerate(zip(kinds, srcs, dsts)):
        own = pltpu.make_async_copy(_src_view(kind, src, me), _dst_view(kind, dst, me), local_sems.at[j])
        if start:
            own.start()
        for k, (pid, pidx) in enumerate(peers):
            sem = (N_DEV - 1) * j + k
            cp = pltpu.make_async_remote_copy(
                src_ref=_src_view(kind, src, pidx), dst_ref=_dst_view(kind, dst, me if start else pidx),
                send_sem=send_sems.at[sem], recv_sem=recv_sems.at[sem], device_id=pid, device_id_type=MESH)
            if start:
                cp.start()
            else:
                cp.wait_recv()
                cp.wait_send()
        if not start:
            own.wait()


def _xfer_sems(n):
    return [pltpu.SemaphoreType.DMA(((N_DEV - 1) * n,)), pltpu.SemaphoreType.DMA(((N_DEV - 1) * n,)),
            pltpu.SemaphoreType.DMA((n,))]


def _exchange(comm, name):
    kinds = [k for k, _ in comm]
    n = len(comm)

    def body(*refs):
        sems = refs[2 * n:]
        _xfers(kinds, refs[:n], refs[n:2 * n], *sems, start=True)
        _xfers(kinds, refs[:n], refs[n:2 * n], *sems, start=False)

    hbm = pl.BlockSpec(memory_space=pl.ANY)
    return pl.pallas_call(
        body, name=name, in_specs=[hbm] * n, out_specs=[hbm] * n,
        out_shape=[_xfer_out_shape(k, a) for k, a in comm], scratch_shapes=_xfer_sems(n),
    )(*[a for _, a in comm])


_HBM = pl.BlockSpec(memory_space=pltpu.HBM)
_SEM = pl.BlockSpec(memory_space=pltpu.SEMAPHORE)
_EFFECT = pltpu.SideEffectType.DATAFLOW_SIDE_EFFECTING


def _mesh_peers():
    mx, my, mc = lax.axis_index("x"), lax.axis_index("y"), lax.axis_index("c")
    peers = []
    for k in range(1, N_DEV):
        px, py, pc = mx ^ ((k >> 2) & 1), my ^ ((k >> 1) & 1), mc ^ (k & 1)
        peers.append(((px, py, pc), 4 * px + 2 * py + pc))
    return 4 * mx + 2 * my + mc, peers


def _xfer_start(comm, name):
    kinds = [k for k, _ in comm]
    n = len(comm)
    srcs = [pltpu.with_memory_space_constraint(a, pltpu.HBM) for _, a in comm]
    lands = []
    for k, a in comm:
        sd = _xfer_out_shape(k, a)
        lands.append(pltpu.with_memory_space_constraint(lax.empty(sd.shape, sd.dtype), pltpu.HBM))

    def body(*refs):
        src_refs, land_refs = refs[:n], refs[n:2 * n]
        send_sems, recv_sems, local_sems, token = refs[2 * n], refs[2 * n + 1], refs[2 * n + 2], refs[-1]
        my, peers = _mesh_peers()
        for j, kind in enumerate(kinds):
            pltpu.make_async_copy(_src_view(kind, src_refs[j], my), _dst_view(kind, land_refs[j], my),
                                  local_sems.at[j]).start()
            for k, (pid, pidx) in enumerate(peers):
                sem = (N_DEV - 1) * j + k
                pltpu.make_async_remote_copy(
                    src_ref=_src_view(kind, src_refs[j], pidx), dst_ref=_dst_view(kind, land_refs[j], my),
                    send_sem=send_sems.at[sem], recv_sem=recv_sems.at[sem], device_id=pid,
                    device_id_type=MESH).start()
        token[...] = jnp.zeros_like(token)

    sems = pltpu.SemaphoreType.DMA(((N_DEV - 1) * n,))
    outs = pl.pallas_call(
        body, name=name,
        out_shape=(sems, sems, pltpu.SemaphoreType.DMA((n,)), *[pltpu.HBM(a.shape, a.dtype) for a in srcs + lands],
                   jax.ShapeDtypeStruct((8, LANES), F32)),
        in_specs=[_HBM] * (2 * n),
        out_specs=(_SEM, _SEM, _SEM, *[_HBM] * (2 * n), pl.BlockSpec(memory_space=pltpu.VMEM)),
        input_output_aliases={k: 3 + k for k in range(2 * n)},
        compiler_params=pltpu.CompilerParams(has_side_effects=_EFFECT),
    )(*srcs, *lands)
    return (kinds, outs[0], outs[1], outs[2], list(outs[3:3 + 2 * n])), outs[-1]


def _xfer_wait(handle, after, name):
    kinds, send_sems, recv_sems, local_sems, bufs = handle
    n = len(kinds)

    def body(*refs):
        src_refs, land_refs = refs[:n], refs[n:2 * n]
        send_ref, recv_ref, local_ref = refs[2 * n], refs[2 * n + 1], refs[2 * n + 2]
        my, peers = _mesh_peers()
        for j, kind in enumerate(kinds):
            pltpu.make_async_copy(_src_view(kind, src_refs[j], my), _dst_view(kind, land_refs[j], my),
                                  local_ref.at[j]).wait()
            for k, (pid, pidx) in enumerate(peers):
                sem = (N_DEV - 1) * j + k
                cp = pltpu.make_async_remote_copy(
                    src_ref=_src_view(kind, src_refs[j], pidx), dst_ref=_dst_view(kind, land_refs[j], pidx),
                    send_sem=send_ref.at[sem], recv_sem=recv_ref.at[sem], device_id=pid, device_id_type=MESH)
                cp.wait_send()
                cp.wait_recv()

    outs = pl.pallas_call(
        body, name=name, out_shape=[pltpu.HBM(a.shape, a.dtype) for a in bufs],
        in_specs=[_HBM] * (2 * n) + [_SEM, _SEM, _SEM, pl.BlockSpec(memory_space=pl.ANY)],
        out_specs=[_HBM] * (2 * n), input_output_aliases={k: k for k in range(2 * n)},
        compiler_params=pltpu.CompilerParams(has_side_effects=_EFFECT),
    )(*bufs, send_sems, recv_sems, local_sems, after)
    return list(outs[n:])


def _call(body, *, grid, in_specs, out_specs, out_shape, args, name, sem, scratch=(), comm=()):
    n_in, n_out, n_scr, n_c = len(in_specs), len(out_specs), len(scratch), len(comm)
    if not comm:
        outs = pl.pallas_call(body, grid=grid, in_specs=list(in_specs), out_specs=list(out_specs),
                              out_shape=list(out_shape), scratch_shapes=list(scratch), name=name,
                              compiler_params=_params(sem))(*args)
        return list(outs), []
    kinds = [k for k, _ in comm]
    n_steps = grid[0]

    def wrapped(*refs):
        ins, csrc = refs[:n_in], refs[n_in:n_in + n_c]
        o0 = n_in + n_c
        outs, cdst = refs[o0:o0 + n_out], refs[o0 + n_out:o0 + n_out + n_c]
        s0 = o0 + n_out + n_c
        scr, sems = refs[s0:s0 + n_scr], refs[s0 + n_scr:]
        i = pl.program_id(0)

        @pl.when(i == 0)
        def _():
            _xfers(kinds, csrc, cdst, *sems, start=True)

        body(*ins, *outs, *scr)

        @pl.when(i == n_steps - 1)
        def _():
            _xfers(kinds, csrc, cdst, *sems, start=False)

    hbm = pl.BlockSpec(memory_space=pl.ANY)
    outs = pl.pallas_call(
        wrapped, grid=grid, in_specs=list(in_specs) + [hbm] * n_c, out_specs=list(out_specs) + [hbm] * n_c,
        out_shape=list(out_shape) + [_xfer_out_shape(k, a) for k, a in comm],
        scratch_shapes=list(scratch) + _xfer_sems(n_c), name=name, compiler_params=_params(("arbitrary",)),
    )(*args, *[a for _, a in comm])
    return list(outs[:n_out]), list(outs[n_out:])


def _norm_mod_fwd(X, nw, mods, si, nxt, name):
    T = X.shape[0]

    def body(x_ref, w_ref, m_ref, h_ref):
        x = x_ref[...]
        r = lax.rsqrt(_rowmean(x * x) + EPS)
        m = m_ref[0]
        y = x * r * w_ref[...]
        h_ref[...] = (y * (1.0 + m[si + 1:si + 2, :]) + m[si:si + 1, :]).astype(BF16)

    return pl.pallas_call(
        body, grid=(T // ROW_TILE,), name=name,
        in_specs=[_row(D), _const((1, D)), _mod_spec(nxt)],
        out_specs=_row(D), out_shape=jax.ShapeDtypeStruct((T, D), BF16),
        compiler_params=_params(("parallel",)),
    )(X, nw, mods)


def _norm_mod_bwd(dh, X, nw, mods, dres, si, nxt, name):
    T = X.shape[0]

    def body(dh_ref, x_ref, w_ref, m_ref, dres_ref, dx_ref, st_ref):
        i = pl.program_id(0)

        @pl.when((i == 0) | (i == nxt))
        def _():
            st_ref[...] = jnp.zeros_like(st_ref)

        x = x_ref[...]
        r = lax.rsqrt(_rowmean(x * x) + EPS)
        xn = x * r
        w = w_ref[...]
        scale = m_ref[0][si + 1:si + 2, :]
        dhv = dh_ref[...]
        dy = dhv * (1.0 + scale)
        dxn = dy * w
        dx_ref[...] = dres_ref[...] + r * (dxn - xn * _rowmean(dxn * xn))
        st_ref[0, 0:1, :] += _colsum(dhv)
        st_ref[0, 1:2, :] += _colsum(dhv * xn * w)
        st_ref[0, 2:3, :] += _colsum(dy * xn)

    return pl.pallas_call(
        body, grid=(T // ROW_TILE,), name=name,
        in_specs=[_row(D), _row(D), _const((1, D)), _mod_spec(nxt), _row(D)],
        out_specs=[_row(D), _stat_spec(nxt)],
        out_shape=[jax.ShapeDtypeStruct((T, D), F32), jax.ShapeDtypeStruct((2, 8, D), F32)],
        compiler_params=_params(("arbitrary",)),
    )(dh, X, nw, mods, dres)


def _resid_bwd(dX, y, mods, gi, nxt, name):
    T = dX.shape[0]

    def body(dx_ref, y_ref, m_ref, dy_ref, st_ref):
        i = pl.program_id(0)

        @pl.when((i == 0) | (i == nxt))
        def _():
            st_ref[...] = jnp.zeros_like(st_ref)

        dx = dx_ref[...]
        dy_ref[...] = (dx * m_ref[0][gi:gi + 1, :]).astype(BF16)
        st_ref[0, 0:1, :] += _colsum(dx * y_ref[...].astype(F32))

    return pl.pallas_call(
        body, grid=(T // ROW_TILE,), name=name,
        in_specs=[_row(D), _row(D), _mod_spec(nxt)],
        out_specs=[_row(D), _stat_spec(nxt)],
        out_shape=[jax.ShapeDtypeStruct((T, D), BF16), jax.ShapeDtypeStruct((2, 8, D), F32)],
        compiler_params=_params(("arbitrary",)),
    )(dX, y, mods)


def _loss_head(X, tgt, fw, nxt, name):
    T = X.shape[0]

    def body(x_ref, t_ref, w_ref, dx_ref, st_ref):
        i = pl.program_id(0)

        @pl.when(i == 0)
        def _():
            st_ref[...] = jnp.zeros_like(st_ref)

        @pl.when(i < nxt)
        def _():
            x = x_ref[...]
            r = lax.rsqrt(_rowmean(x * x) + EPS)
            xn = x * r
            w = w_ref[...]
            err = xn * w - t_ref[...]
            dy = err * (1.0 / D)
            dxn = dy * w
            dx_ref[...] = r * (dxn - xn * _rowmean(dxn * xn))
            st_ref[0:1, :] += _colsum(dy * xn)
            st_ref[1:2, :] += (0.5 / D) * _colsum(err * err)

        @pl.when(i >= nxt)
        def _():
            dx_ref[...] = jnp.zeros_like(dx_ref)

    return pl.pallas_call(
        body, grid=(T // ROW_TILE,), name=name,
        in_specs=[_row(D), pl.BlockSpec((ROW_TILE, D), lambda i: (jnp.minimum(i, nxt - 1), 0)), _const((1, D))],
        out_specs=[_row(D), _const((8, D))],
        out_shape=[jax.ShapeDtypeStruct((T, D), F32), jax.ShapeDtypeStruct((8, D), F32)],
        compiler_params=_params(("arbitrary",)),
    )(X, tgt, fw)


def _mm_nn(A, B, out_dtype, name, add=None, comm=()):
    T, K = A.shape
    N = B.shape[1]

    tm = _mm_tile(T, MM_ROWS)
    if add is None:
        def body(a_ref, b_ref, o_ref):
            o_ref[...] = _dot(a_ref[...], b_ref[...]).astype(out_dtype)
        ins, specs = (A, B), [_row(K, tm), _const((K, N))]
    else:
        def body(a_ref, b_ref, c_ref, o_ref):
            o_ref[...] = (c_ref[...] + _dot(a_ref[...], b_ref[...])).astype(out_dtype)
        ins, specs = (A, B, add), [_row(K, tm), _const((K, N)), _row(N, tm)]

    return _call(body, grid=(T // tm,), name=name, in_specs=specs,
                 out_specs=[_row(N, tm)], out_shape=[jax.ShapeDtypeStruct((T, N), out_dtype)],
                 sem=("parallel",), args=ins, comm=comm)


def _mm_nn_resid(A, B, X, mods, gi, seq, name, comm=()):
    T, K = A.shape
    N = B.shape[1]
    tm = _mm_tile(T, MM_ROWS)

    def body(a_ref, b_ref, x_ref, m_ref, xo_ref, y_ref):
        acc = _dot(a_ref[...], b_ref[...])
        y_ref[...] = acc.astype(BF16)
        row = pl.program_id(0) * tm + lax.broadcasted_iota(jnp.int32, (tm, 1), 0)
        gate = jnp.where(row >= seq, m_ref[1][gi:gi + 1, :], m_ref[0][gi:gi + 1, :])
        xo_ref[...] = x_ref[...] + gate * acc

    return _call(body, grid=(T // tm,), name=name,
                 in_specs=[_row(K, tm), _const((K, N)), _row(N, tm), _const((2, 6, D))],
                 out_specs=[_row(N, tm), _row(N, tm)],
                 out_shape=[jax.ShapeDtypeStruct((T, N), F32), jax.ShapeDtypeStruct((T, N), BF16)],
                 sem=("parallel",), args=(A, B, X, mods), comm=comm)


def _mm_nt(A, B, name, comm=()):
    T, N = A.shape
    K = B.shape[0]
    tm = _mm_tile(T, MM_ROWS)

    def body(a_ref, b_ref, o_ref):
        o_ref[...] = _dot_nt(a_ref[...], b_ref[...])

    return _call(body, grid=(T // tm,), name=name, in_specs=[_row(N, tm), _const((K, N))], out_specs=[_row(K, tm)],
                 out_shape=[jax.ShapeDtypeStruct((T, K), F32)], sem=("parallel",), args=(A, B), comm=comm)


def _mm_tn(A, G, name):
    T, K = A.shape
    N = G.shape[1]
    wide = max(K, N)
    blk = next(b for b in (768, 512) if wide % b == 0)

    def body(a_ref, g_ref, o_ref):
        o_ref[...] = _dot_tn(a_ref[...], g_ref[...]).astype(BF16)

    whole = lambda cols: pl.BlockSpec((T, cols), lambda j: (0, 0))
    cols = lambda: pl.BlockSpec((T, blk), lambda j: (0, j))
    if N >= K:
        in_specs, out_spec = [whole(K), cols()], pl.BlockSpec((K, blk), lambda j: (0, j))
    else:
        in_specs, out_spec = [cols(), whole(N)], pl.BlockSpec((blk, N), lambda j: (j, 0))
    return pl.pallas_call(
        body, grid=(wide // blk,), name=name, in_specs=in_specs, out_specs=out_spec,
        out_shape=jax.ShapeDtypeStruct((K, N), BF16), compiler_params=_params(("parallel",)),
    )(A, G)


def _ffn_up(h, WgT, WuT, name, comm=()):
    T = h.shape[0]
    F = WgT.shape[0]

    def body(h_ref, wg_ref, wu_ref, gp_ref, up_ref, act_ref):
        hv = h_ref[...]
        gp = _dot_nt(hv, wg_ref[...])
        up = _dot_nt(hv, wu_ref[...])
        gp_ref[...] = gp.astype(BF16)
        up_ref[...] = up.astype(BF16)
        act_ref[...] = (gp * _sigmoid(gp) * up).astype(BF16)

    sd = jax.ShapeDtypeStruct((T, F), BF16)
    return _call(body, grid=(T // ROW_TILE,), name=name, in_specs=[_row(D), _const((F, D)), _const((F, D))],
                 out_specs=[_row(F), _row(F), _row(F)], out_shape=[sd, sd, sd], sem=("parallel",),
                 args=(h, WgT, WuT), comm=comm)


def _ffn_dact(dy, Wd, gp, up, name, comm=()):
    T = dy.shape[0]
    F = Wd.shape[0]

    def body(dy_ref, wd_ref, gp_ref, up_ref, dgp_ref, dup_ref):
        dact = _dot_nt(dy_ref[...], wd_ref[...])
        gpv = gp_ref[...].astype(F32)
        upv = up_ref[...].astype(F32)
        sg = _sigmoid(gpv)
        dup_ref[...] = (dact * gpv * sg).astype(BF16)
        dgp_ref[...] = (dact * upv * sg * (1.0 + gpv * (1.0 - sg))).astype(BF16)

    sd = jax.ShapeDtypeStruct((T, F), BF16)
    return _call(body, grid=(T // ROW_TILE,), name=name, in_specs=[_row(D), _const((F, D)), _row(F), _row(F)],
                 out_specs=[_row(F), _row(F)], out_shape=[sd, sd], sem=("parallel",),
                 args=(dy, Wd, gp, up), comm=comm)


N_TAB = 7


def _ret_tables(tab_ref, lgf, lgb):
    r = lax.broadcasted_iota(jnp.int32, (CHUNK, CHUNK), 0).astype(F32)
    c = lax.broadcasted_iota(jnp.int32, (CHUNK, CHUNK), 1).astype(F32)
    for d, lg in ((0, lgf), (1, lgb)):
        if d == 0:
            mask, expo, xe, ze = r >= c, r - c, r + 1.0, (CHUNK - 1.0) - r
        else:
            mask, expo, xe, ze = c > r, c - r - 1.0, (CHUNK - 1.0) - r, r
        e = jnp.where(mask, expo, 0.0)
        tab_ref[N_TAB * d + 0] = jnp.where(mask, jnp.exp(lg * e), 0.0)
        tab_ref[N_TAB * d + 1] = jnp.exp(lg * xe)
        tab_ref[N_TAB * d + 2] = jnp.exp(lg * ze)
        tab_ref[N_TAB * d + 3] = jnp.exp(jnp.full((CHUNK, CHUNK), lg * float(CHUNK), F32))
        tab_ref[N_TAB * d + 4] = e
        tab_ref[N_TAB * d + 5] = xe
        tab_ref[N_TAB * d + 6] = ze


def _rope(t, cosf, sgn):
    return t * cosf + pltpu.roll(t, HEAD_DIM // 2, 1) * sgn


def _rope_t(d, cosf, sgn):
    return d * cosf + pltpu.roll(d * sgn, HEAD_DIM // 2, 1)


def _chunk_of(d, j, nc, ncx):
    return lax.rem(j + ncx, nc) if d == 0 else nc - 1 - j


def _head_spec(T, col0):
    return pl.BlockSpec((T, HEAD_DIM), lambda h: (0, col0 + h))


def _ret_fwd(p, lg, cosf, sgn, seq, name, comm=()):
    T = p.shape[0]
    nc, ncx = T // CHUNK, seq // CHUNK

    def body(lg_ref, q_ref, k_ref, v_ref, g_ref, cos_ref, sgn_ref, y_ref, ro_ref, tab_ref, of_ref):
        h = pl.program_id(0)
        _ret_tables(tab_ref, lg_ref[0, h], lg_ref[1, h])

        def load(cidx):
            rows = pl.ds(pl.multiple_of(cidx * CHUNK, CHUNK), CHUNK)
            cs, sn = cos_ref[rows, :], sgn_ref[rows, :]
            return rows, _rope(q_ref[rows, :], cs, sn) * Q_SCALE, _rope(k_ref[rows, :], cs, sn), v_ref[rows, :]

        def chunk(d, q, k, v, S):
            b = N_TAB * d
            kb, vb = k.astype(BF16), v.astype(BF16)
            pm = _dot_nt(q.astype(BF16), kb) * tab_ref[b]
            o = _dot(pm.astype(BF16), vb) + _dot((q * tab_ref[b + 1]).astype(BF16), S.astype(BF16))
            return o, S * tab_ref[b + 3] + _dot_tn((k * tab_ref[b + 2]).astype(BF16), vb)

        def step(j, carry):
            Sf, Sb = carry
            rows, q, k, v = load(_chunk_of(0, j, nc, ncx))
            o, Sf = chunk(0, q, k, v, Sf)
            y_ref[rows, :] = o
            rows, q, k, v = load(_chunk_of(1, j, nc, ncx))
            o, Sb = chunk(1, q, k, v, Sb)
            of_ref[rows, :] = o
            return Sf, Sb

        zero = jnp.zeros((CHUNK, CHUNK), F32)
        lax.fori_loop(0, nc, step, (zero, zero))

        def finish(cidx, carry):
            rows = pl.ds(pl.multiple_of(cidx * CHUNK, CHUNK), CHUNK)
            y = y_ref[rows, :] + of_ref[rows, :]
            y_ref[rows, :] = y
            g = g_ref[rows, :]
            ro_ref[rows, :] = (g * _sigmoid(g) * y * lax.rsqrt(_rowmean(y * y) + EPS)).astype(BF16)
            return carry

        lax.fori_loop(0, nc, finish, 0)

    tbl = pl.BlockSpec((T, HEAD_DIM), lambda h: (0, 0))
    return _call(
        body, grid=(RET_HEADS,), name=name,
        in_specs=[pl.BlockSpec(memory_space=pltpu.SMEM), _head_spec(T, 0), _head_spec(T, 4), _head_spec(T, 8),
                  _head_spec(T, 12), tbl, tbl],
        out_specs=[_head_spec(T, 0), _head_spec(T, 0)],
        out_shape=[jax.ShapeDtypeStruct((T, RET_W), F32), jax.ShapeDtypeStruct((T, RET_W), BF16)],
        scratch=[pltpu.VMEM((2 * N_TAB, CHUNK, CHUNK), F32), pltpu.VMEM((T, HEAD_DIM), F32)],
        sem=("arbitrary",), args=(lg, p, p, p, p, cosf, sgn), comm=comm)


def _ret_gate_bwd(yret, p, dmix, name):
    T = yret.shape[0]

    def body(y_ref, g_ref, dm_ref, dy_ref, dg_ref):
        for hh in range(RET_HEADS):
            sl = slice(hh * HEAD_DIM, (hh + 1) * HEAD_DIM)
            y = y_ref[:, sl]
            r = lax.rsqrt(_rowmean(y * y) + EPS)
            yn = y * r
            g = g_ref[:, sl]
            sg = _sigmoid(g)
            dro = dm_ref[:, sl]
            dg_ref[:, sl] = (dro * yn * sg * (1.0 + g * (1.0 - sg))).astype(BF16)
            dyn = dro * g * sg
            dy_ref[:, sl] = r * (dyn - yn * _rowmean(dyn * yn))

    return pl.pallas_call(
        body, grid=(T // ROW_TILE,), name=name,
        in_specs=[_row(RET_W), pl.BlockSpec((ROW_TILE, RET_W), lambda i: (i, 3)), _row(RET_W)],
        out_specs=[_row(RET_W), _row(RET_W)],
        out_shape=[jax.ShapeDtypeStruct((T, RET_W), F32), jax.ShapeDtypeStruct((T, RET_W), BF16)],
        compiler_params=_params(("parallel",)),
    )(yret, p, dmix)


def _ret_bwd(p, lg, dy, cosf, sgn, seq, name, comm=()):
    T = p.shape[0]
    nc, ncx = T // CHUNK, seq // CHUNK

    def body(lg_ref, q_ref, k_ref, v_ref, dy_ref, cos_ref, sgn_ref, dq_ref, dk_ref, dv_ref, dlg_ref,
             tab_ref, st_ref, dqs, dks, dvs):
        h = pl.program_id(0)
        _ret_tables(tab_ref, lg_ref[0, h], lg_ref[1, h])
        dlg_ref[...] = jnp.zeros_like(dlg_ref)

        def load(cidx):
            rows = pl.ds(pl.multiple_of(cidx * CHUNK, CHUNK), CHUNK)
            cs, sn = cos_ref[rows, :], sgn_ref[rows, :]
            return rows, _rope(q_ref[rows, :], cs, sn) * Q_SCALE, _rope(k_ref[rows, :], cs, sn), v_ref[rows, :]

        def states(j, carry):
            out = []
            for d, S in enumerate(carry):
                b = N_TAB * d
                _, _, k, v = load(_chunk_of(d, j, nc, ncx))
                st_ref[d, j] = S
                out.append(S * tab_ref[b + 3] + _dot_tn((k * tab_ref[b + 2]).astype(BF16), v.astype(BF16)))
            return tuple(out)

        zero = jnp.zeros((CHUNK, CHUNK), F32)
        lax.fori_loop(0, nc, states, (zero, zero))

        def sweep_one(d, j, dS, acc):
            b = N_TAB * d
            rows, q, k, v = load(_chunk_of(d, j, nc, ncx))
            dO = dy_ref[rows, :]
            Sp = st_ref[d, j]
            qb, kb, vb, dOb = q.astype(BF16), k.astype(BF16), v.astype(BF16), dO.astype(BF16)
            Spb, dSb = Sp.astype(BF16), dS.astype(BF16)
            dmat, xi, ze, cd = tab_ref[b], tab_ref[b + 1], tab_ref[b + 2], tab_ref[b + 3]
            pm = _dot_nt(qb, kb) * dmat
            dpm = _dot_nt(dOb, vb)
            acc = acc + _colsum(dpm * pm * tab_ref[b + 4])
            dsc = (dpm * dmat).astype(BF16)
            qx = (q * xi).astype(BF16)
            kz = (k * ze).astype(BF16)
            dq = _dot(dsc, kb) + _dot_nt(dOb, Spb) * xi
            dk = _dot_tn(dsc, qb) + _dot_nt(vb, dSb) * ze
            dvst = _dot(kz, dSb)
            dv = _dot_tn(pm.astype(BF16), dOb) + dvst
            inter = _dot(qx, Spb)
            acc = acc + _colsum(dO * inter * tab_ref[b + 5])
            acc = acc + float(CHUNK) * _colsum(dS * cd * Sp) + _colsum(v * dvst * tab_ref[b + 6])
            dqs[d, rows, :] = dq.astype(BF16)
            dks[d, rows, :] = dk.astype(BF16)
            dvs[d, rows, :] = dv.astype(BF16)
            return dS * cd + _dot_tn(qx, dOb), acc

        def sweep(jj, carry):
            dSf, accf, dSb, accb = carry
            j = nc - 1 - jj
            dSf, accf = sweep_one(0, j, dSf, accf)
            dSb, accb = sweep_one(1, j, dSb, accb)
            return dSf, accf, dSb, accb

        zacc = jnp.zeros((1, CHUNK), F32)
        _, accf, _, accb = lax.fori_loop(0, nc, sweep, (zero, zacc, zero, zacc))
        dlg_ref[0, 0:1, :] = jnp.zeros((1, LANES), F32) + jnp.sum(accf)
        dlg_ref[0, 1:2, :] = jnp.zeros((1, LANES), F32) + jnp.sum(accb)

        def finish(cidx, carry):
            rows = pl.ds(pl.multiple_of(cidx * CHUNK, CHUNK), CHUNK)
            cs, sn = cos_ref[rows, :], sgn_ref[rows, :]
            both = lambda ref: ref[0, rows, :].astype(F32) + ref[1, rows, :].astype(F32)
            dq_ref[rows, :] = (_rope_t(both(dqs), cs, sn) * Q_SCALE).astype(BF16)
            dk_ref[rows, :] = _rope_t(both(dks), cs, sn).astype(BF16)
            dv_ref[rows, :] = both(dvs).astype(BF16)
            return carry

        lax.fori_loop(0, nc, finish, 0)

    tbl = pl.BlockSpec((T, HEAD_DIM), lambda h: (0, 0))
    sd = jax.ShapeDtypeStruct((T, RET_W), BF16)
    return _call(
        body, grid=(RET_HEADS,), name=name,
        in_specs=[pl.BlockSpec(memory_space=pltpu.SMEM), _head_spec(T, 0), _head_spec(T, 4), _head_spec(T, 8),
                  _head_spec(T, 0), tbl, tbl],
        out_specs=[_head_spec(T, 0), _head_spec(T, 0), _head_spec(T, 0),
                   pl.BlockSpec((1, 8, LANES), lambda h: (h, 0, 0))],
        out_shape=[sd, sd, sd, jax.ShapeDtypeStruct((RET_HEADS, 8, LANES), F32)],
        scratch=[pltpu.VMEM((2 * N_TAB, CHUNK, CHUNK), F32), pltpu.VMEM((2, nc, CHUNK, CHUNK), F32),
                 pltpu.VMEM((2, T, HEAD_DIM), BF16), pltpu.VMEM((2, T, HEAD_DIM), BF16),
                 pltpu.VMEM((2, T, HEAD_DIM), BF16)],
        sem=("arbitrary",), args=(lg, p, p, p, dy, cosf, sgn), comm=comm)


def _halo_specs(T, cols, colblock):
    per = ROW_TILE // HALO
    last = T // HALO - 1
    prv = pl.BlockSpec((HALO, cols), lambda i: (jnp.maximum(i * per - 1, 0), colblock))
    nxt = pl.BlockSpec((HALO, cols), lambda i: (jnp.minimum((i + 1) * per, last), colblock))
    return prv, nxt


def _seq_edges(i, nxt, nt):
    first = (i == 0) | (i == nxt)
    last = (i == nxt - 1) | (i == nt - 1)
    return first, last


def _fill_pad(pad_ref, prv, cur, nxt, first, last):
    pad_ref[0:HALO, :] = jnp.where(first, 0.0, prv)
    pad_ref[HALO:HALO + ROW_TILE, :] = cur
    pad_ref[HALO + ROW_TILE:2 * HALO + ROW_TILE, :] = jnp.where(last, 0.0, nxt)


SUBLANES = 8
SHIFT_ROWS = ROW_TILE + 2 * HALO - SUBLANES


def _fill_shifts(sh_ref, pad_ref):
    for b in range(SUBLANES):
        sh_ref[b] = pad_ref[b:b + SHIFT_ROWS, :]


TAP_ROWS = 32


def _window(sh_ref, o, r0=0, rows=ROW_TILE):
    b = o % SUBLANES
    return sh_ref[b, o - b + r0:o - b + r0 + rows, :]


def _tap_sum(w_ref, sh_ref, offset_of_tap, r0):
    acc = jnp.zeros((TAP_ROWS, CONV_CH), F32)
    for k in range(CONV_K):
        acc = acc + w_ref[k:k + 1, :] * _window(sh_ref, offset_of_tap(k), r0, TAP_ROWS)
    return acc


def _ln_stats(x):
    mu = _rowmean(x)
    xc = x - mu
    rs = lax.rsqrt(_rowmean(xc * xc) + EPS)
    return xc * rs, rs


def _conv_fwd(p, cw, lnw, lnb, nxt, name, comm=()):
    T = p.shape[0]
    nt = T // ROW_TILE
    a_col, g_col = 4, 5

    def body(a_ref, g_ref, ap_ref, gp_ref, an_ref, gn_ref, w_ref, lw_ref, lb_ref, cv_ref, co_ref, pad_ref, sh_ref):
        i = pl.program_id(0)
        first, last = _seq_edges(i, nxt, nt)
        glu = lambda a, g: a * _sigmoid(g)
        _fill_pad(pad_ref, glu(ap_ref[...], gp_ref[...]), glu(a_ref[...], g_ref[...]),
                  glu(an_ref[...], gn_ref[...]), first, last)
        _fill_shifts(sh_ref, pad_ref)
        for r0 in range(0, ROW_TILE, TAP_ROWS):
            cv_ref[r0:r0 + TAP_ROWS, :] = _tap_sum(w_ref, sh_ref, lambda k: k + 1, r0)
        xh, _ = _ln_stats(cv_ref[...])
        z = xh * lw_ref[...] + lb_ref[...]
        co_ref[...] = (z * _sigmoid(z)).astype(BF16)

    cur = lambda cb: pl.BlockSpec((ROW_TILE, CONV_CH), lambda i: (i, cb))
    ap, an = _halo_specs(T, CONV_CH, a_col)
    gp, gn = _halo_specs(T, CONV_CH, g_col)
    return _call(
        body, grid=(nt,), name=name,
        in_specs=[cur(a_col), cur(g_col), ap, gp, an, gn, _const((32, CONV_CH)), _const((1, CONV_CH)),
                  _const((1, CONV_CH))],
        out_specs=[_row(CONV_CH), _row(CONV_CH)],
        out_shape=[jax.ShapeDtypeStruct((T, CONV_CH), F32), jax.ShapeDtypeStruct((T, CONV_CH), BF16)],
        scratch=[pltpu.VMEM((ROW_TILE + 2 * HALO, CONV_CH), F32), pltpu.VMEM((SUBLANES, SHIFT_ROWS, CONV_CH), F32)],
        sem=("parallel",), args=(p, p, p, p, p, p, cw, lnw, lnb), comm=comm)


def _conv_bwd(p, cv, dmix, cw, lnw, lnb, nxt, name, comm=()):
    T = p.shape[0]
    nt = T // ROW_TILE
    a_col, g_col = 4, 5

    def body(a_ref, g_ref, ap_ref, gp_ref, an_ref, gn_ref, cv_ref, cvp_ref, cvn_ref, dc_ref, dcp_ref, dcn_ref,
             w_ref, lw_ref, lb_ref, dp_ref, dw_ref, dl_ref, upad_ref, dpad_ref, ush_ref, dsh_ref):
        i = pl.program_id(0)
        first, last = _seq_edges(i, nxt, nt)

        @pl.when(i == 0)
        def _():
            dw_ref[...] = jnp.zeros_like(dw_ref)
            dl_ref[...] = jnp.zeros_like(dl_ref)

        lw, lb = lw_ref[...], lb_ref[...]

        def ln_bwd(cvv, dco):
            xh, rs = _ln_stats(cvv)
            z = xh * lw + lb
            sg = _sigmoid(z)
            dz = dco * sg * (1.0 + z * (1.0 - sg))
            dxh = dz * lw
            return rs * (dxh - _rowmean(dxh) - xh * _rowmean(dxh * xh)), dz, xh

        dcv, dz, xh = ln_bwd(cv_ref[...], dc_ref[...])
        _fill_pad(dpad_ref, ln_bwd(cvp_ref[...], dcp_ref[...])[0], dcv, ln_bwd(cvn_ref[...], dcn_ref[...])[0],
                  first, last)
        a, g = a_ref[...], g_ref[...]
        sg = _sigmoid(g)
        glu = lambda av, gv: av * _sigmoid(gv)
        _fill_pad(upad_ref, glu(ap_ref[...], gp_ref[...]), a * sg, glu(an_ref[...], gn_ref[...]), first, last)
        _fill_shifts(dsh_ref, dpad_ref)
        _fill_shifts(ush_ref, upad_ref)
        for k in range(CONV_K):
            dw_ref[k:k + 1, :] += _colsum(dcv * _window(ush_ref, k + 1))
        dl_ref[0:1, :] += _colsum(dz * xh)
        dl_ref[1:2, :] += _colsum(dz)
        for r0 in range(0, ROW_TILE, TAP_ROWS):
            rs = slice(r0, r0 + TAP_ROWS)
            du = _tap_sum(w_ref, dsh_ref, lambda k: CONV_K - k, r0)
            ar, sr = a_ref[rs, :], _sigmoid(g_ref[rs, :])
            dp_ref[rs, 0:CONV_CH] = (du * sr).astype(BF16)
            dp_ref[rs, CONV_CH:2 * CONV_CH] = (du * ar * sr * (1.0 - sr)).astype(BF16)

    cur = lambda cb: pl.BlockSpec((ROW_TILE, CONV_CH), lambda i: (i, cb))
    ap, an = _halo_specs(T, CONV_CH, a_col)
    gp, gn = _halo_specs(T, CONV_CH, g_col)
    cvp, cvn = _halo_specs(T, CONV_CH, 0)
    dcp, dcn = _halo_specs(T, CONV_CH, 1)
    return _call(
        body, grid=(nt,), name=name,
        in_specs=[cur(a_col), cur(g_col), ap, gp, an, gn, cur(0), cvp, cvn, cur(1), dcp, dcn,
                  _const((32, CONV_CH)), _const((1, CONV_CH)), _const((1, CONV_CH))],
        out_specs=[_row(2 * CONV_CH), _const((32, CONV_CH)), _const((8, CONV_CH))],
        out_shape=[jax.ShapeDtypeStruct((T, 2 * CONV_CH), BF16), jax.ShapeDtypeStruct((32, CONV_CH), F32),
                   jax.ShapeDtypeStruct((8, CONV_CH), F32)],
        scratch=[pltpu.VMEM((ROW_TILE + 2 * HALO, CONV_CH), F32), pltpu.VMEM((ROW_TILE + 2 * HALO, CONV_CH), F32),
                 pltpu.VMEM((SUBLANES, SHIFT_ROWS, CONV_CH), F32), pltpu.VMEM((SUBLANES, SHIFT_ROWS, CONV_CH), F32)],
        sem=("arbitrary",), args=(p, p, p, p, p, p, cv, cv, cv, dmix, dmix, dmix, cw, lnw, lnb), comm=comm)


def _tile_positions(i, nxt, seq, ctx, offset, rows):
    is_ctx = i >= nxt
    pos0 = (i - jnp.where(is_ctx, nxt, 0)) * ROW_TILE + offset
    length = jnp.where(is_ctx, ctx, seq).astype(F32)
    pos = (pos0 + lax.broadcasted_iota(jnp.int32, (rows, 1), 0)).astype(F32)
    return pos, length


def _pool_count(pos, length, w):
    left = w // 2
    right = w - 1 - left
    return jnp.minimum(pos + right, length - 1.0) - jnp.maximum(pos - left, 0.0) + 1.0


def _gelu(x):
    return 0.5 * x * (1.0 + lax.erf(x * INV_SQRT2))


def _odd_fwd(p, pw, ps, slw, slb, sw, sbf, nxt, seq, ctx, name, comm=()):
    T = p.shape[0]
    nt = T // ROW_TILE

    def body(p_ref, pp_ref, pn_ref, pw_ref, ps_ref, lw_ref, lb_ref, sw_ref, sb_ref, o_ref, pad_ref):
        i = pl.program_id(0)
        first, last = _seq_edges(i, nxt, nt)
        _fill_pad(pad_ref, pp_ref[...], p_ref[:, 0:POOL_CH], pn_ref[...], first, last)
        pos, length = _tile_positions(i, nxt, seq, ctx, 0, ROW_TILE)
        for gi, w in enumerate(POOL_WINDOWS):
            sl = slice(gi * GROUP_CH, (gi + 1) * GROUP_CH)
            left = w // 2
            ssum = jnp.zeros((ROW_TILE, GROUP_CH), F32)
            for o in range(-left, w - left):
                ssum = ssum + pad_ref[HALO + o:HALO + o + ROW_TILE, sl]
            m = ssum / _pool_count(pos, length, w) - p_ref[:, sl]
            pre = _dot(m.astype(BF16), pw_ref[gi].astype(BF16))
            o_ref[:, sl] = (pre * ps_ref[:, sl]).astype(BF16)
        u = _gelu(p_ref[:, POOL_CH:POOL_CH + SG_CH])
        xh, _ = _ln_stats(_gelu(p_ref[:, POOL_CH + SG_CH:ODD_IN]))
        vln = xh * lw_ref[...] + lb_ref[...]
        for n in range(ROW_TILE // SG_CHUNK):
            rs = slice(n * SG_CHUNK, (n + 1) * SG_CHUNK)
            for gi in range(4):
                sl = slice(gi * GROUP_CH, (gi + 1) * GROUP_CH)
                s = _dot(sw_ref[gi].astype(BF16), vln[rs, sl].astype(BF16)) + sb_ref[:, sl]
                o_ref[rs, POOL_CH + gi * GROUP_CH:POOL_CH + (gi + 1) * GROUP_CH] = (u[rs, sl] * s).astype(BF16)

    pp, pn = _halo_specs(T, POOL_CH, 0)
    return _call(
        body, grid=(nt,), name=name,
        in_specs=[_row(ODD_IN), pp, pn, _const((4, GROUP_CH, GROUP_CH)), _const((1, POOL_CH)), _const((1, SG_CH)),
                  _const((1, SG_CH)), _const((4, SG_CHUNK, SG_CHUNK)), _const((SG_CHUNK, SG_CH))],
        out_specs=[_row(D)], out_shape=[jax.ShapeDtypeStruct((T, D), BF16)],
        scratch=[pltpu.VMEM((ROW_TILE + 2 * HALO, POOL_CH), F32)],
        sem=("parallel",), args=(p, p, p, pw, ps, slw, slb, sw, sbf), comm=comm)


def _odd_bwd(p, dmix, pw, ps, slw, slb, sw, sbf, nxt, seq, ctx, name, comm=()):
    T = p.shape[0]
    nt = T // ROW_TILE

    def body(p_ref, pp_ref, pn_ref, dm_ref, dmp_ref, dmn_ref, pw_ref, ps_ref, lw_ref, lb_ref, sw_ref, sb_ref,
             dp_ref, dpw_ref, dsw_ref, dv_ref, dsb_ref, pad_ref, dpad_ref):
        i = pl.program_id(0)
        first, last = _seq_edges(i, nxt, nt)

        @pl.when(i == 0)
        def _():
            dpw_ref[...] = jnp.zeros_like(dpw_ref)
            dsw_ref[...] = jnp.zeros_like(dsw_ref)
            dv_ref[...] = jnp.zeros_like(dv_ref)
            dsb_ref[...] = jnp.zeros_like(dsb_ref)

        _fill_pad(pad_ref, pp_ref[...], p_ref[:, 0:POOL_CH], pn_ref[...], first, last)
        pos, length = _tile_positions(i, nxt, seq, ctx, 0, ROW_TILE)
        pos_p, _ = _tile_positions(i, nxt, seq, ctx, -HALO, HALO)
        pos_n, _ = _tile_positions(i, nxt, seq, ctx, ROW_TILE, HALO)
        scale = ps_ref[...]
        for gi, w in enumerate(POOL_WINDOWS):
            sl = slice(gi * GROUP_CH, (gi + 1) * GROUP_CH)
            left = w // 2
            right = w - 1 - left
            ssum = jnp.zeros((ROW_TILE, GROUP_CH), F32)
            for o in range(-left, right + 1):
                ssum = ssum + pad_ref[HALO + o:HALO + o + ROW_TILE, sl]
            cnt = _pool_count(pos, length, w)
            m = ssum / cnt - p_ref[:, sl]
            wg = pw_ref[gi].astype(BF16)
            pre = _dot(m.astype(BF16), wg)
            dpo = dm_ref[:, sl]
            dv_ref[0:1, sl] += _colsum(dpo * pre)
            dpre = (dpo * scale[:, sl]).astype(BF16)
            dpw_ref[gi] += _dot_tn(m.astype(BF16), dpre)
            dmc = _dot_nt(dpre, wg)
            halo_dm = lambda ref, ps_: _dot_nt((ref[:, sl] * scale[:, sl]).astype(BF16), wg) / _pool_count(ps_, length, w)
            dpad_ref[0:HALO, sl] = jnp.where(first, 0.0, halo_dm(dmp_ref, pos_p))
            dpad_ref[HALO:HALO + ROW_TILE, sl] = dmc / cnt
            dpad_ref[HALO + ROW_TILE:2 * HALO + ROW_TILE, sl] = jnp.where(last, 0.0, halo_dm(dmn_ref, pos_n))
            atd = jnp.zeros((ROW_TILE, GROUP_CH), F32)
            for o in range(-right, left + 1):
                atd = atd + dpad_ref[HALO + o:HALO + o + ROW_TILE, sl]
            dp_ref[:, sl] = (atd - dmc).astype(BF16)

        pu = p_ref[:, POOL_CH:POOL_CH + SG_CH]
        pv = p_ref[:, POOL_CH + SG_CH:ODD_IN]
        u = _gelu(pu)
        xh, rs_ = _ln_stats(_gelu(pv))
        lw = lw_ref[...]
        vln = xh * lw + lb_ref[...]
        dgelu = lambda x: 0.5 * (1.0 + lax.erf(x * INV_SQRT2)) + x * jnp.exp(-0.5 * x * x) * INV_SQRT_2PI
        for n in range(ROW_TILE // SG_CHUNK):
            rs = slice(n * SG_CHUNK, (n + 1) * SG_CHUNK)
            dvl = []
            for gi in range(4):
                sl = slice(gi * GROUP_CH, (gi + 1) * GROUP_CH)
                wq = sw_ref[gi].astype(BF16)
                vb = vln[rs, sl].astype(BF16)
                s = _dot(wq, vb) + sb_ref[:, sl]
                dsg = dm_ref[rs, POOL_CH + gi * GROUP_CH:POOL_CH + (gi + 1) * GROUP_CH]
                ds = dsg * u[rs, sl]
                dsb_ref[:, sl] += ds
                dsw_ref[gi] += _dot_nt(ds.astype(BF16), vb)
                dvl.append(_dot_tn(wq, ds.astype(BF16)))
                dp_ref[rs, POOL_CH + gi * GROUP_CH:POOL_CH + (gi + 1) * GROUP_CH] = (
                    dsg * s * dgelu(pu[rs, sl])).astype(BF16)
            dvln = jnp.concatenate(dvl, axis=1)
            xhc = xh[rs, :]
            dv_ref[1:2, :] += _colsum(dvln * xhc)
            dv_ref[2:3, :] += _colsum(dvln)
            dxh = dvln * lw
            dvv = rs_[rs, :] * (dxh - _rowmean(dxh) - xhc * _rowmean(dxh * xhc))
            dp_ref[rs, POOL_CH + SG_CH:ODD_IN] = (dvv * dgelu(pv[rs, :])).astype(BF16)

    pp, pn = _halo_specs(T, POOL_CH, 0)
    dmp, dmn = _halo_specs(T, POOL_CH, 0)
    gsd = jax.ShapeDtypeStruct((4, GROUP_CH, GROUP_CH), F32)
    return _call(
        body, grid=(nt,), name=name,
        in_specs=[_row(ODD_IN), pp, pn, _row(D), dmp, dmn, _const((4, GROUP_CH, GROUP_CH)), _const((1, POOL_CH)),
                  _const((1, SG_CH)), _const((1, SG_CH)), _const((4, SG_CHUNK, SG_CHUNK)), _const((SG_CHUNK, SG_CH))],
        out_specs=[_row(ODD_IN), _const((4, GROUP_CH, GROUP_CH)), _const((4, SG_CHUNK, SG_CHUNK)), _const((8, POOL_CH)),
                   _const((SG_CHUNK, SG_CH))],
        out_shape=[jax.ShapeDtypeStruct((T, ODD_IN), BF16), gsd, gsd, jax.ShapeDtypeStruct((8, POOL_CH), F32),
                   jax.ShapeDtypeStruct((SG_CHUNK, SG_CH), F32)],
        scratch=[pltpu.VMEM((ROW_TILE + 2 * HALO, POOL_CH), F32), pltpu.VMEM((ROW_TILE + 2 * HALO, POOL_CH), F32)],
        sem=("arbitrary",), args=(p, p, p, dmix, dmix, dmix, pw, ps, slw, slb, sw, sbf), comm=comm)


def _ada_fwd(cs, aw, ab, name):
    cols = aw.shape[2]

    def body(c_ref, w_ref, b_ref, o_ref):
        c = c_ref[...]
        s = (c * _sigmoid(c)).astype(BF16)
        o_ref[0] = _dot(s, w_ref[0].astype(BF16)) + b_ref[0]

    return pl.pallas_call(
        body, grid=(DEPTH,), name=name,
        in_specs=[pl.BlockSpec((16, D), lambda i: (0, 0)), pl.BlockSpec((1, D, cols), lambda i: (i, 0, 0)),
                  pl.BlockSpec((1, 1, cols), lambda i: (i, 0, 0))],
        out_specs=pl.BlockSpec((1, 16, cols), lambda i: (i, 0, 0)),
        out_shape=jax.ShapeDtypeStruct((DEPTH, 16, cols), F32),
        compiler_params=_params(("parallel",)),
    )(cs, aw, ab)


def _ada_bwd(cs, G, aw, name):
    cols = aw.shape[2]

    def body(c_ref, g_ref, w_ref, gw_ref, cc_ref):
        i = pl.program_id(0)

        @pl.when(i == 0)
        def _():
            cc_ref[...] = jnp.zeros_like(cc_ref)

        c = c_ref[...]
        s = (c * _sigmoid(c)).astype(BF16)
        g = g_ref[0]
        dc = g[8:9, :]
        for d in range(9, 16):
            dc = dc + g[d:d + 1, :]
        row = lax.broadcasted_iota(jnp.int32, (8, cols), 0)
        dcrows = jnp.where(row == 0, dc, 0.0)
        dm = jnp.concatenate([g[0:8, :], dcrows], axis=0).astype(BF16)
        gw_ref[0] = _dot_tn(s, dm)
        cc_ref[...] += _dot_nt(dcrows.astype(BF16), w_ref[0].astype(BF16))

    return pl.pallas_call(
        body, grid=(DEPTH,), name=name,
        in_specs=[pl.BlockSpec((16, D), lambda i: (0, 0)), pl.BlockSpec((1, 16, cols), lambda i: (i, 0, 0)),
                  pl.BlockSpec((1, D, cols), lambda i: (i, 0, 0))],
        out_specs=[pl.BlockSpec((1, D, cols), lambda i: (i, 0, 0)), pl.BlockSpec((8, D), lambda i: (0, 0))],
        out_shape=[jax.ShapeDtypeStruct((DEPTH, D, cols), F32), jax.ShapeDtypeStruct((8, D), F32)],
        compiler_params=_params(("arbitrary",)),
    )(cs, G, aw)


def _row_tile_for(rows):
    for t in (512, 256, 128, 64, 32, 16, 8):
        if rows % t == 0:
            return t
    raise ValueError(f"rows={rows} is not a multiple of 8")


def _sum_devices(x, name):
    _, rows, cols = x.shape
    tr = _row_tile_for(rows)

    def body(x_ref, o_ref):
        acc = x_ref[0]
        for d in range(1, N_DEV):
            acc = acc + x_ref[d]
        o_ref[...] = acc

    return pl.pallas_call(
        body, grid=(rows // tr,), name=name,
        in_specs=[pl.BlockSpec((N_DEV, tr, cols), lambda i: (0, i, 0))],
        out_specs=pl.BlockSpec((tr, cols), lambda i: (i, 0)),
        out_shape=jax.ShapeDtypeStruct((rows, cols), F32),
        compiler_params=_params(("parallel",)),
    )(x)


def _adamw(w, g, m, v, name):
    rows, cols = w.shape
    tr = _row_tile_for(rows)

    def body(w_ref, g_ref, m_ref, v_ref, go_ref, d_ref, mo_ref, vo_ref):
        gv = g_ref[...]
        mn = ADAM_B1 * m_ref[...] + (1.0 - ADAM_B1) * gv
        vn = ADAM_B2 * v_ref[...] + (1.0 - ADAM_B2) * (gv * gv)
        go_ref[...] = gv
        mo_ref[...] = mn
        vo_ref[...] = vn
        d_ref[...] = -ADAM_LR * ((mn / ADAM_BC1) / (jnp.sqrt(vn / ADAM_BC2) + ADAM_EPS) + ADAM_WD * w_ref[...])

    blk = pl.BlockSpec((tr, cols), lambda i: (i, 0))
    sd = jax.ShapeDtypeStruct((rows, cols), F32)
    return pl.pallas_call(
        body, grid=(rows // tr,), name=name,
        in_specs=[blk, blk, blk, blk], out_specs=[blk, blk, blk, blk], out_shape=[sd, sd, sd, sd],
        compiler_params=_params(("parallel",)),
    )(w, g, m, v)


def _adamw_layers(w, pieces, m, v, name):
    L, a, b = w.shape
    bp = pieces[0].shape[2]
    tr = next(t for t in (256, 128, 64, 32, 16) if a % t == 0)

    def body(w_ref, m_ref, v_ref, *rest):
        p_refs, (go_ref, d_ref, mo_ref, vo_ref) = rest[:L], rest[L:]
        layer = pl.program_id(0)
        for k in range(L):
            @pl.when(layer == k)
            def _(k=k):
                gv = p_refs[k][0].astype(F32)
                for d in range(1, N_DEV):
                    gv = gv + p_refs[k][d].astype(F32)
                gv = gv[:, :b]
                mn = ADAM_B1 * m_ref[0] + (1.0 - ADAM_B1) * gv
                vn = ADAM_B2 * v_ref[0] + (1.0 - ADAM_B2) * (gv * gv)
                go_ref[0] = gv
                mo_ref[0] = mn
                vo_ref[0] = vn
                d_ref[0] = -ADAM_LR * ((mn / ADAM_BC1) / (jnp.sqrt(vn / ADAM_BC2) + ADAM_EPS) + ADAM_WD * w_ref[0])

    blk = pl.BlockSpec((1, tr, b), lambda l, i: (l, i, 0))
    pspecs = [pl.BlockSpec((N_DEV, tr, bp), lambda l, i, k=k: (0, jnp.where(l == k, i, 0), 0)) for k in range(L)]
    sd = jax.ShapeDtypeStruct((L, a, b), F32)
    return pl.pallas_call(
        body, grid=(L, a // tr), name=name,
        in_specs=[blk, blk, blk] + pspecs, out_specs=[blk, blk, blk, blk], out_shape=[sd, sd, sd, sd],
        compiler_params=_params(("arbitrary", "arbitrary")),
    )(w, m, v, *pieces)


def _pack_rows(shape):
    return -(-math.prod(shape) // (8 * LANES)) * 8


def _pack(arrs, row_mult):
    parts = []
    for a in arrs:
        n, rows = math.prod(a.shape), _pack_rows(a.shape)
        parts.append(jnp.pad(a.reshape(-1).astype(F32), (0, rows * LANES - n)).reshape(rows, LANES))
    total = sum(p.shape[0] for p in parts)
    if total % row_mult:
        parts.append(jnp.zeros((-total % row_mult, LANES), F32))
    return jnp.concatenate(parts, axis=0)


def _unpack(packed, shapes):
    out, r0 = [], 0
    lead = packed.shape[:-2]
    for s in shapes:
        n, rows = math.prod(s), _pack_rows(s)
        piece = packed[..., r0:r0 + rows, :].reshape(lead + (rows * LANES,))
        out.append(piece[..., :n].reshape(lead + tuple(s)))
        r0 += rows
    return out


def _rope_tables(seq, ctx):
    def angles(ps):
        parts = []
        for pvec, n in zip(ps, ROPE_PAIRS):
            freq = ROPE_BASE ** (-jnp.arange(n, dtype=F32) / n)
            parts.append(pvec[:, None] * freq[None, :])
        return jnp.concatenate(parts, axis=-1)

    rows = seq // GRID_W
    grid_r = jnp.broadcast_to(jnp.arange(rows, dtype=F32)[:, None], (rows, GRID_W)).reshape(-1)
    grid_c = jnp.broadcast_to(jnp.arange(GRID_W, dtype=F32)[None, :], (rows, GRID_W)).reshape(-1)
    zc = jnp.zeros((ctx,), F32)
    ang = jnp.concatenate([angles((jnp.full((seq,), ctx, F32), grid_r, grid_c)),
                           angles((jnp.arange(ctx, dtype=F32), zc, zc))], axis=0)
    cos, sin = jnp.cos(ang), jnp.sin(ang)
    return jnp.concatenate([cos, cos], axis=1), jnp.concatenate([-sin, sin], axis=1)


def _layer_gather(sh, even):
    return [("ag_cols" if even else "ag_rows", sh["in"]), ("ag_rows", sh["out"]), ("ag_rows", sh["gate"]),
            ("ag_rows", sh["up"]), ("ag_rows", sh["down"])]


def _sample_step(X0, tgt, mods, shards, S, seq, ctx):
    nxt = seq // ROW_TILE
    cosf, sgn = _rope_tables(seq, ctx)
    X = X0
    saved = []
    gathers = {}

    def start_gather(i, half, dep):
        comm = _layer_gather(shards[i], i % 2 == 0)
        comm = comm[:2] if half == "mix" else comm[2:]
        comm[0] = (comm[0][0], comm[0][1] + dep.astype(BF16))
        gathers[(i, half)], token = _xfer_start(comm, f"gather_{half}{i}_start")
        return token[0, 0]

    dep = mods[0, 0, 0, 0] * 0.0
    for i, half in ((0, "mix"), (0, "ffn"), (1, "mix"), (1, "ffn")):
        dep = start_gather(i, half, dep)
    w_in, w_out = _xfer_wait(gathers[(0, "mix")], mods, "gather_mix0_wait")
    for i in range(DEPTH):
        j, even = i // 2, i % 2 == 0
        t = f"l{i}_"
        last = i == DEPTH - 1
        if i in (1, 2):
            dep = w_out[0, 0].astype(F32) * 0.0
            for half in ("mix", "ffn"):
                dep = start_gather(i + 1, half, dep)
        md = mods[i] + dep
        h1 = _norm_mod_fwd(X, S["norm_w"][i, 0][None], md, 0, nxt, t + "norm1")
        if even:
            (p,), _ = _mm_nn(h1, w_in, F32, t + "proj_in")
            (yret, ro), _ = _ret_fwd(p, S["lg"][j], cosf, sgn, seq, t + "ret_fwd")
            (cv, co), _ = _conv_fwd(p, S["conv_w"][j], S["conv_ln_w"][j][None], S["conv_ln_b"][j][None], nxt,
                                    t + "conv_fwd")
            mix = jnp.concatenate([ro, co], axis=1)
            extra = (yret, cv)
        else:
            (p,), _ = _mm_nt(h1, w_in, t + "proj_in")
            sbf = jnp.repeat(S["sg_b"][j].T, GROUP_CH, axis=1)
            (mix,), _ = _odd_fwd(p, S["pool_w"][j], S["pool_scale"][j][None], S["sg_ln_w"][j][None],
                                 S["sg_ln_b"][j][None], S["sg_w"][j], sbf, nxt, seq, ctx, t + "odd_fwd")
            extra = (sbf,)
        (X1, y1), _ = _mm_nn_resid(mix, w_out, X, md, 2, seq, t + "proj_out")
        h2 = _norm_mod_fwd(X1, S["norm_w"][i, 1][None], md, 3, nxt, t + "norm2")
        wg, wu, wd = _xfer_wait(gathers[(i, "ffn")], h2, f"gather_ffn{i}_wait")
        (gp, up, act), _ = _ffn_up(h2, wg, wu, t + "ffn_up")
        (X2, y2), _ = _mm_nn_resid(act, wd, X1, md, 5, seq, t + "ffn_down")
        saved.append((X, h1, p, mix, extra, y1, X1, h2, gp, up, act, y2, (w_in, w_out, wg, wu, wd)))
        X = X2
        if not last:
            w_in, w_out = _xfer_wait(gathers[(i + 1, "mix")], X2, f"gather_mix{i + 1}_wait")

    dX, hst = _loss_head(X, tgt, S["final_norm_w"][None], nxt, "loss_head")
    loss = jnp.sum(hst[1])
    pieces = {}
    gS = {"final_norm_w": hst[0], "norm_w": [None] * DEPTH, "lg": [None] * 2, "conv_w": [None] * 2,
          "conv_ln_w": [None] * 2, "conv_ln_b": [None] * 2, "pool_w": [None] * 2, "pool_scale": [None] * 2,
          "sg_ln_w": [None] * 2, "sg_ln_b": [None] * 2, "sg_w": [None] * 2, "sg_b": [None] * 2}
    dmods = [None] * DEPTH
    in_flight = []

    def land(flights, after, tag):
        for handle, keys, name in flights:
            for key, got in zip(keys, _xfer_wait(handle, after, name + tag)):
                pieces[key] = got

    for i in reversed(range(DEPTH)):
        j, even = i // 2, i % 2 == 0
        md = mods[i]
        t = f"l{i}_"
        X_in, h1, p, mix, extra, y1, X1, h2, gp, up, act, y2, (w_in, w_out, wg, wu, wd) = saved[i]
        dyf, g2 = _resid_bwd(dX, y2, md, 5, nxt, t + "ffn_resid_bwd")
        (dgp, dup), _ = _ffn_dact(dyf, wd, gp, up, t + "ffn_dact")
        g_down = _mm_tn(act, dyf, t + "dw_down")
        g_gate = _mm_tn(dgp, h2, t + "dw_gate")
        g_up = _mm_tn(dup, h2, t + "dw_up")
        ffn_flight, token = _xfer_start([("a2a_rows", g_down), ("a2a_rows", g_gate), ("a2a_rows", g_up)],
                                        t + "scatter_ffn_start")
        (dh2,), _ = _mm_nn(dgp, wg, F32, t + "dh2_gate")
        (dh2,), _ = _mm_nn(dup, wu, F32, t + "dh2_up", add=dh2)
        dX1, s2 = _norm_mod_bwd(dh2, X1, S["norm_w"][i, 1][None], md + token[0, 0], dX, 3, nxt, t + "norm2_bwd")
        dym, g1 = _resid_bwd(dX1, y1, md, 2, nxt, t + "mix_resid_bwd")
        (dmix,), _ = _mm_nt(dym, w_out, t + "dmix")
        g_out = _mm_tn(mix, dym, t + "dw_out")
        if even:
            yret, cv = extra
            dyr, dg = _ret_gate_bwd(yret, p, dmix, t + "ret_gate_bwd")
            (dq, dk, dv, dlg), _ = _ret_bwd(p, S["lg"][j], dyr, cosf, sgn, seq, t + "ret_bwd")
            early = [("ag_blk", _pack([jnp.stack(gS["pool_w"]), jnp.stack(gS["sg_w"])], 8))] if i == 0 else []
            (dpc, dcw, dln), got = _conv_bwd(
                p, cv, dmix, S["conv_w"][j], S["conv_ln_w"][j][None], S["conv_ln_b"][j][None], nxt,
                t + "conv_bwd", comm=early)
            if i == 0:
                early_all = got[0]
            dp = jnp.concatenate([dq, dk, dv, dg, dpc], axis=1)
            gS["lg"][j] = dlg[:, 0:2, 0].T
            gS["conv_w"][j], gS["conv_ln_w"][j], gS["conv_ln_b"][j] = dcw, dln[0], dln[1]
            g_in = _mm_tn(h1, dp, t + "dw_in")
            mix_flight, token = _xfer_start([("a2a_rows", g_out), ("a2a_cols", g_in)], t + "scatter_mix_start")
            (dh1,), _ = _mm_nt(dp, w_in, t + "dh1")
        else:
            (sbf,) = extra
            (dp, dpw, dsw, dvec, dsb), _ = _odd_bwd(
                p, dmix, S["pool_w"][j], S["pool_scale"][j][None], S["sg_ln_w"][j][None], S["sg_ln_b"][j][None],
                S["sg_w"][j], sbf, nxt, seq, ctx, t + "odd_bwd")
            gS["pool_w"][j], gS["sg_w"][j] = dpw, dsw
            gS["pool_scale"][j], gS["sg_ln_w"][j], gS["sg_ln_b"][j] = dvec[0], dvec[1], dvec[2]
            gS["sg_b"][j] = jnp.sum(dsb.reshape(SG_CHUNK, 4, GROUP_CH), axis=2).T
            g_in = _mm_tn(dp, h1, t + "dw_in")
            mix_flight, token = _xfer_start([("a2a_rows", g_out), ("a2a_rows", g_in)], t + "scatter_mix_start")
            (dh1,), _ = _mm_nn(dp, w_in, F32, t + "dh1")
        dX, s1 = _norm_mod_bwd(dh1, X_in, S["norm_w"][i, 0][None], md + token[0, 0], dX1, 0, nxt, t + "norm1_bwd")
        gS["norm_w"][i] = jnp.stack([s1[0, 2] + s1[1, 2], s2[0, 2] + s2[1, 2]])
        dmods[i] = jnp.stack([s1[:, 0], s1[:, 1], g1[:, 0], s2[:, 0], s2[:, 1], g2[:, 0]], axis=1)
        land(in_flight, dX, "_wait")
        in_flight = [(ffn_flight, [("down", i), ("gate", i), ("up", i)], t + "scatter_ffn"),
                     (mix_flight, [("out", i), ("in", i)], t + "scatter_mix")]
    land(in_flight[:1], dX, "_wait")
    gS = {k: (jnp.stack(v) if isinstance(v, list) else v) for k, v in gS.items()}
    return loss, dX, pieces, gS, jnp.stack(dmods), early_all, in_flight[1]


FF_SHARD = D_FF // N_DEV
FF_SHARD_PAD = 384


def kernel(x, c, ctx, c_ctx, ada_w, ada_b, norm_w, even_w_in, even_w_out, ret_decay_logit, conv_dw_w, conv_ln_w, conv_ln_b, odd_w_in, odd_w_out, pool_w, pool_scale, sg_ln_w, sg_ln_b, sg_w, sg_b, ffn_w_gate, ffn_w_up, ffn_w_down, final_norm_w, loss_target, m_c_ctx, m_ada_w, m_ada_b, m_norm_w, m_even_w_in, m_even_w_out, m_ret_decay_logit, m_conv_dw_w, m_conv_ln_w, m_conv_ln_b, m_odd_w_in, m_odd_w_out, m_pool_w, m_pool_scale, m_sg_ln_w, m_sg_ln_b, m_sg_w, m_sg_b, m_ffn_w_gate, m_ffn_w_up, m_ffn_w_down, m_final_norm_w, v_c_ctx, v_ada_w, v_ada_b, v_norm_w, v_even_w_in, v_even_w_out, v_ret_decay_logit, v_conv_dw_w, v_conv_ln_w, v_conv_ln_b, v_odd_w_in, v_odd_w_out, v_pool_w, v_pool_scale, v_sg_ln_w, v_sg_ln_b, v_sg_w, v_sg_b, v_ffn_w_gate, v_ffn_w_up, v_ffn_w_down, v_final_norm_w):
    seq, n_ctx = x.shape[1], ctx.shape[1]
    me = 4 * lax.axis_index("x") + 2 * lax.axis_index("y") + lax.axis_index("c")
    mcols = ada_w.shape[2]

    tr = lambda a: jnp.swapaxes(a, 1, 2)
    pad_rows = lambda a: jnp.pad(a, ((0, FF_SHARD_PAD - FF_SHARD), (0, 0))).astype(BF16)
    gate_t, up_t, odd_in_t = tr(ffn_w_gate), tr(ffn_w_up), tr(odd_w_in)
    shards = []
    for i in range(DEPTH):
        j, even = i // 2, i % 2 == 0
        w_in, w_out = (even_w_in[j], even_w_out[j]) if even else (odd_in_t[j], odd_w_out[j])
        shards.append({"in": w_in.astype(BF16), "out": w_out.astype(BF16), "gate": pad_rows(gate_t[i]),
                       "up": pad_rows(up_t[i]), "down": pad_rows(ffn_w_down[i])})

    small_shapes = [(D,), norm_w.shape, conv_dw_w.shape, pool_scale.shape, sg_ln_w.shape, sg_ln_b.shape]
    (sm,) = _exchange([("ag_blk", _pack([c, norm_w, conv_dw_w, pool_scale, sg_ln_w, sg_ln_b], 8))], "gather_small")
    c_all, nw_s, cw_s, ps_s, slw_s, slb_s = _unpack(sm, small_shapes)
    cat_last = lambda a: jnp.moveaxis(a, 0, -2).reshape(a.shape[1:-1] + (-1,))
    conv_w_full = cat_last(cw_s)
    S = {"norm_w": cat_last(nw_s), "lg": jax.nn.log_sigmoid(ret_decay_logit),
         "conv_w": jnp.pad(conv_w_full, ((0, 0), (0, 32 - CONV_K), (0, 0))),
         "conv_ln_w": conv_ln_w, "conv_ln_b": conv_ln_b, "pool_w": pool_w, "pool_scale": cat_last(ps_s),
         "sg_ln_w": cat_last(slw_s), "sg_ln_b": cat_last(slb_s), "sg_w": sg_w, "sg_b": sg_b,
         "final_norm_w": final_norm_w}

    cs = jnp.concatenate([c_all, c_ctx[None], jnp.zeros((7, D), F32)], axis=0)
    ab_loc = lax.dynamic_slice_in_dim(ada_b, me * mcols, mcols, axis=1)
    mod_loc = _ada_fwd(cs, ada_w, ab_loc[:, None, :], "ada_fwd")
    (mod_all,) = _exchange([("ag_blk", mod_loc.reshape(DEPTH * 16, mcols))], "gather_mod")
    mod_all = mod_all.reshape(N_DEV, DEPTH, 16, mcols).transpose(1, 2, 0, 3).reshape(DEPTH, 16, 6, D)
    mod_x = lax.dynamic_index_in_dim(mod_all, me, axis=1, keepdims=False)
    mods = jnp.stack([mod_x, mod_all[:, 8]], axis=1)

    X0 = jnp.concatenate([x[0], ctx[0]], axis=0)
    loss, dX, big, gS, dmods, early_all, late = _sample_step(X0, loss_target[0], mods, shards, S, seq, n_ctx)
    loss = lax.psum(loss, ("x", "y", "c"))
    grad_x = dX[:seq][None]

    (dm_all,) = _exchange([("ag_blk", dmods.reshape(DEPTH * 2, 6 * D))], "gather_dmod")
    dm_all = dm_all.reshape(N_DEV, DEPTH, 2, 6 * D)
    dm_sum = _sum_devices(dm_all.reshape(N_DEV, DEPTH * 2, 6 * D), "sum_dmod").reshape(DEPTH, 2, 6 * D)
    g_ada_b = dm_sum[:, 0] + dm_sum[:, 1]
    G = lax.dynamic_slice_in_dim(dm_all, me * mcols, mcols, axis=3).transpose(1, 2, 0, 3).reshape(DEPTH, 16, mcols)
    g_ada_w, ccp = _ada_bwd(cs, G, ada_w, "ada_bwd")
    sg_cc = _sigmoid(c_ctx)
    g_cctx_part = ccp[0] * sg_cc * (1.0 + c_ctx * (1.0 - sg_cc))

    dsig = _sigmoid(-ret_decay_logit)
    part = [g_cctx_part, gS["norm_w"], gS["lg"] * dsig, gS["conv_w"][:, :CONV_K], gS["conv_ln_w"], gS["conv_ln_b"],
            gS["pool_scale"], gS["sg_ln_w"], gS["sg_ln_b"], gS["sg_b"], gS["final_norm_w"]]
    part_shapes = [a.shape for a in part]
    (part_all,) = _exchange([("ag_blk", _pack(part, 8))], "gather_small_grads")
    red = _sum_devices(part_all, "sum_small_grads")
    g_cctx, g_nw, g_rdl, g_cw, g_clw, g_clb, g_ps, g_slw, g_slb, g_sb, g_fnw = _unpack(red, part_shapes)
    g_pw, g_sw = _unpack(_sum_devices(early_all, "sum_early_grads"), [pool_w.shape, sg_w.shape])
    mine_last = lambda a, n: lax.dynamic_slice_in_dim(a, me * n, n, axis=a.ndim - 1)
    g_nw, g_cw = mine_last(g_nw, norm_w.shape[2]), mine_last(g_cw, conv_dw_w.shape[2])
    g_ps, g_slw, g_slb = (mine_last(a, pool_scale.shape[1]) for a in (g_ps, g_slw, g_slb))

    def adam_big(key, layers, w, m, v, name):
        return _adamw_layers(w, [big[(key, l)] for l in layers], m, v, name)

    res = {}
    adam_t = lambda key, layers, w, m, v, name: [tr(o) for o in adam_big(key, layers, tr(w), tr(m), tr(v), name)]
    res["odd_w_in"] = adam_t("in", (1, 3), odd_w_in, m_odd_w_in, v_odd_w_in, "adam_odd_in")
    res["odd_w_out"] = adam_big("out", (1, 3), odd_w_out, m_odd_w_out, v_odd_w_out, "adam_odd_out")
    res["ffn_w_gate"] = adam_t("gate", (0, 1, 2, 3), ffn_w_gate, m_ffn_w_gate, v_ffn_w_gate, "adam_gate")
    res["ffn_w_up"] = adam_t("up", (0, 1, 2, 3), ffn_w_up, m_ffn_w_up, v_ffn_w_up, "adam_up")
    res["ffn_w_down"] = adam_big("down", (0, 1, 2, 3), ffn_w_down, m_ffn_w_down, v_ffn_w_down, "adam_down")
    late_handle, late_keys, late_name = late
    for key, got in zip(late_keys, _xfer_wait(late_handle, res["ffn_w_down"][1], late_name + "_wait")):
        big[key] = got
    res["even_w_in"] = adam_big("in", (0, 2), even_w_in, m_even_w_in, v_even_w_in, "adam_even_in")
    res["even_w_out"] = adam_big("out", (0, 2), even_w_out, m_even_w_out, v_even_w_out, "adam_even_out")
    flat2 = lambda a: a.reshape(-1, a.shape[-1])
    res["ada_w"] = [o.reshape(ada_w.shape) for o in
                    _adamw(flat2(ada_w), flat2(g_ada_w), flat2(m_ada_w), flat2(v_ada_w), "adam_ada_w")]

    names = ["c_ctx", "ada_b", "norm_w", "ret_decay_logit", "conv_dw_w", "conv_ln_w", "conv_ln_b", "pool_w",
             "pool_scale", "sg_ln_w", "sg_ln_b", "sg_w", "sg_b", "final_norm_w"]
    ws = [c_ctx, ada_b, norm_w, ret_decay_logit, conv_dw_w, conv_ln_w, conv_ln_b, pool_w, pool_scale, sg_ln_w,
          sg_ln_b, sg_w, sg_b, final_norm_w]
    gs = [g_cctx, g_ada_b, g_nw, g_rdl, g_cw, g_clw, g_clb, g_pw, g_ps, g_slw, g_slb, g_sw, g_sb, g_fnw]
    ms = [m_c_ctx, m_ada_b, m_norm_w, m_ret_decay_logit, m_conv_dw_w, m_conv_ln_w, m_conv_ln_b, m_pool_w,
          m_pool_scale, m_sg_ln_w, m_sg_ln_b, m_sg_w, m_sg_b, m_final_norm_w]
    vs = [v_c_ctx, v_ada_b, v_norm_w, v_ret_decay_logit, v_conv_dw_w, v_conv_ln_w, v_conv_ln_b, v_pool_w,
          v_pool_scale, v_sg_ln_w, v_sg_ln_b, v_sg_w, v_sg_b, v_final_norm_w]
    shapes = [a.shape for a in ws]
    gs = [g.reshape(s) for g, s in zip(gs, shapes)]
    pk = lambda arrs: _pack(arrs, 512)
    outs = _adamw(pk(ws), pk(gs), pk(ms), pk(vs), "adam_small")
    for k, o in enumerate(outs):
        for nm, arr in zip(names, _unpack(o, shapes)):
            res.setdefault(nm, [None] * 4)[k] = arr

    order = ["c_ctx", "ada_w", "ada_b", "norm_w", "even_w_in", "even_w_out", "ret_decay_logit", "conv_dw_w",
             "conv_ln_w", "conv_ln_b", "odd_w_in", "odd_w_out", "pool_w", "pool_scale", "sg_ln_w", "sg_ln_b",
             "sg_w", "sg_b", "ffn_w_gate", "ffn_w_up", "ffn_w_down", "final_norm_w"]
    return (loss, grad_x, *[res[n][0] for n in order], *[res[n][1] for n in order],
            *[res[n][2] for n in order], *[res[n][3] for n in order])
```

```python
import math

import jax
import jax.numpy as jnp
from jax import lax
from jax.experimental import pallas as pl
from jax.experimental.pallas import tpu as pltpu

F32 = jnp.float32
BF16 = jnp.bfloat16

N_DEV = 8
D = 1024
DEPTH = 4
ROW_TILE = 256
HALO = 16
LANES = 128
EPS = 1e-6
RET_HEADS = 4
HEAD_DIM = 128
RET_W = RET_HEADS * HEAD_DIM
CHUNK = 128
Q_SCALE = HEAD_DIM ** -0.5
ROPE_BASE = 10000.0
ROPE_PAIRS = (HEAD_DIM // 8, 3 * HEAD_DIM // 16, 3 * HEAD_DIM // 16)
GRID_W = 64
CONV_CH = 512
CONV_K = 31
EVEN_IN = 4 * RET_W + 2 * CONV_CH
POOL_CH = 512
POOL_WINDOWS = (2, 4, 8, 16)
GROUP_CH = 128
SG_CH = 512
SG_CHUNK = 128
ODD_IN = POOL_CH + 2 * SG_CH
D_FF = 2816
INV_SQRT2 = 1.0 / math.sqrt(2.0)
INV_SQRT_2PI = 1.0 / math.sqrt(2.0 * math.pi)
ADAM_LR, ADAM_B1, ADAM_B2, ADAM_EPS, ADAM_WD, ADAM_STEP = 0.001, 0.9, 0.999, 1e-08, 0.01, 10
ADAM_BC1 = 1.0 - ADAM_B1 ** ADAM_STEP
ADAM_BC2 = 1.0 - ADAM_B2 ** ADAM_STEP
VMEM_LIMIT = 56 * 1024 * 1024
MESH = pl.DeviceIdType.MESH


def _params(sem=None):
    return pltpu.CompilerParams(dimension_semantics=sem, vmem_limit_bytes=VMEM_LIMIT)


def _dot(a, b):
    return jnp.dot(a, b, preferred_element_type=F32)


def _dot_nt(a, b):
    return lax.dot_general(a, b, (((1,), (1,)), ((), ())), preferred_element_type=F32)


def _dot_tn(a, b):
    return lax.dot_general(a, b, (((0,), (0,)), ((), ())), preferred_element_type=F32)


def _sigmoid(x):
    return 1.0 / (1.0 + jnp.exp(-x))


def _colsum(x):
    return jnp.sum(x, axis=0, keepdims=True)


def _rowmean(x):
    return jnp.mean(x, axis=-1, keepdims=True)


def _row(shape_cols, tile=ROW_TILE):
    return pl.BlockSpec((tile, shape_cols), lambda i: (i, 0))


def _mm_tile(T, cands):
    return next(t for t in cands if T % t == 0)


MM_ROWS = (544, 512, ROW_TILE)


def _const(shape):
    nd = len(shape)
    return pl.BlockSpec(shape, lambda i: (0,) * nd)


def _mod_spec(nxt):
    return pl.BlockSpec((1, 6, D), lambda i: (i // nxt, 0, 0))


def _stat_spec(nxt):
    return pl.BlockSpec((1, 8, D), lambda i: (i // nxt, 0, 0))


def _xfer_out_shape(kind, a):
    r, c = a.shape
    shape = {"ag_blk": (N_DEV, r, c), "ag_cols": (r, N_DEV * c), "ag_rows": (N_DEV * r, c),
             "a2a_cols": (N_DEV, r, c // N_DEV), "a2a_rows": (N_DEV, r // N_DEV, c)}[kind]
    return jax.ShapeDtypeStruct(shape, a.dtype)


def _src_view(kind, ref, who):
    if kind == "a2a_cols":
        n = ref.shape[1] // N_DEV
        return ref.at[:, pl.ds(pl.multiple_of(who * n, LANES), n)]
    if kind == "a2a_rows":
        r = ref.shape[0] // N_DEV
        return ref.at[pl.ds(pl.multiple_of(who * r, 16), r), :]
    return ref


def _dst_view(kind, ref, who):
    if kind == "ag_cols":
        n = ref.shape[1] // N_DEV
        return ref.at[:, pl.ds(pl.multiple_of(who * n, LANES), n)]
    if kind == "ag_rows":
        r = ref.shape[0] // N_DEV
        return ref.at[pl.ds(pl.multiple_of(who * r, 16), r), :]
    return ref.at[who]


def _xfers(kinds, srcs, dsts, send_sems, recv_sems, local_sems, start):
    mx, my, mc = lax.axis_index("x"), lax.axis_index("y"), lax.axis_index("c")
    me = 4 * mx + 2 * my + mc
    peers = []
    for k in range(1, N_DEV):
        px, py, pc = mx ^ ((k >> 2) & 1), my ^ ((k >> 1) & 1), mc ^ (k & 1)
        peers.append(((px, py, pc), 4 * px + 2 * py + pc))
    for j, (kind, src, dst) in enumerate(zip(kinds, srcs, dsts)):
        own = pltpu.make_async_copy(_src_view(kind, src, me), _dst_view(kind, dst, me), local_sems.at[j])
        if start:
            own.start()
        for k, (pid, pidx) in enumerate(peers):
            sem = (N_DEV - 1) * j + k
            cp = pltpu.make_async_remote_copy(
                src_ref=_src_view(kind, src, pidx), dst_ref=_dst_view(kind, dst, me if start else pidx),
                send_sem=send_sems.at[sem], recv_sem=recv_sems.at[sem], device_id=pid, device_id_type=MESH)
            if start:
                cp.start()
            else:
                cp.wait_recv()
                cp.wait_send()
        if not start:
            own.wait()


def _xfer_sems(n):
    return [pltpu.SemaphoreType.DMA(((N_DEV - 1) * n,)), pltpu.SemaphoreType.DMA(((N_DEV - 1) * n,)),
            pltpu.SemaphoreType.DMA((n,))]


def _exchange(comm, name):
    kinds = [k for k, _ in comm]
    n = len(comm)

    def body(*refs):
        sems = refs[2 * n:]
        _xfers(kinds, refs[:n], refs[n:2 * n], *sems, start=True)
        _xfers(kinds, refs[:n], refs[n:2 * n], *sems, start=False)

    hbm = pl.BlockSpec(memory_space=pl.ANY)
    return pl.pallas_call(
        body, name=name, in_specs=[hbm] * n, out_specs=[hbm] * n,
        out_shape=[_xfer_out_shape(k, a) for k, a in comm], scratch_shapes=_xfer_sems(n),
    )(*[a for _, a in comm])


_HBM = pl.BlockSpec(memory_space=pltpu.HBM)
_SEM = pl.BlockSpec(memory_space=pltpu.SEMAPHORE)
_EFFECT = pltpu.SideEffectType.DATAFLOW_SIDE_EFFECTING


def _mesh_peers():
    mx, my, mc = lax.axis_index("x"), lax.axis_index("y"), lax.axis_index("c")
    peers = []
    for k in range(1, N_DEV):
        px, py, pc = mx ^ ((k >> 2) & 1), my ^ ((k >> 1) & 1), mc ^ (k & 1)
        peers.append(((px, py, pc), 4 * px + 2 * py + pc))
    return 4 * mx + 2 * my + mc, peers


def _xfer_start(comm, name):
    kinds = [k for k, _ in comm]
    n = len(comm)
    srcs = [pltpu.with_memory_space_constraint(a, pltpu.HBM) for _, a in comm]
    lands = []
    for k, a in comm:
        sd = _xfer_out_shape(k, a)
        lands.append(pltpu.with_memory_space_constraint(lax.empty(sd.shape, sd.dtype), pltpu.HBM))

    def body(*refs):
        src_refs, land_refs = refs[:n], refs[n:2 * n]
        send_sems, recv_sems, local_sems, token = refs[2 * n], refs[2 * n + 1], refs[2 * n + 2], refs[-1]
        my, peers = _mesh_peers()
        for j, kind in enumerate(kinds):
            pltpu.make_async_copy(_src_view(kind, src_refs[j], my), _dst_view(kind, land_refs[j], my),
                                  local_sems.at[j]).start()
            for k, (pid, pidx) in enumerate(peers):
                sem = (N_DEV - 1) * j + k
                pltpu.make_async_remote_copy(
                    src_ref=_src_view(kind, src_refs[j], pidx), dst_ref=_dst_view(kind, land_refs[j], my),
                    send_sem=send_sems.at[sem], recv_sem=recv_sems.at[sem], device_id=pid,
                    device_id_type=MESH).start()
        token[...] = jnp.zeros_like(token)

    sems = pltpu.SemaphoreType.DMA(((N_DEV - 1) * n,))
    outs = pl.pallas_call(
        body, name=name,
        out_shape=(sems, sems, pltpu.SemaphoreType.DMA((n,)), *[pltpu.HBM(a.shape, a.dtype) for a in srcs + lands],
                   jax.ShapeDtypeStruct((8, LANES), F32)),
        in_specs=[_HBM] * (2 * n),
        out_specs=(_SEM, _SEM, _SEM, *[_HBM] * (2 * n), pl.BlockSpec(memory_space=pltpu.VMEM)),
        input_output_aliases={k: 3 + k for k in range(2 * n)},
        compiler_params=pltpu.CompilerParams(has_side_effects=_EFFECT),
    )(*srcs, *lands)
    return (kinds, outs[0], outs[1], outs[2], list(outs[3:3 + 2 * n])), outs[-1]


def _xfer_wait(handle, after, name):
    kinds, send_sems, recv_sems, local_sems, bufs = handle
    n = len(kinds)

    def body(*refs):
        src_refs, land_refs = refs[:n], refs[n:2 * n]
        send_ref, recv_ref, local_ref = refs[2 * n], refs[2 * n + 1], refs[2 * n + 2]
        my, peers = _mesh_peers()
        for j, kind in enumerate(kinds):
            pltpu.make_async_copy(_src_view(kind, src_refs[j], my), _dst_view(kind, land_refs[j], my),
                                  local_ref.at[j]).wait()
            for k, (pid, pidx) in enumerate(peers):
                sem = (N_DEV - 1) * j + k
                cp = pltpu.make_async_remote_copy(
                    src_ref=_src_view(kind, src_refs[j], pidx), dst_ref=_dst_view(kind, land_refs[j], pidx),
                    send_sem=send_ref.at[sem], recv_sem=recv_ref.at[sem], device_id=pid, device_id_type=MESH)
                cp.wait_send()
                cp.wait_recv()

    outs = pl.pallas_call(
        body, name=name, out_shape=[pltpu.HBM(a.shape, a.dtype) for a in bufs],
        in_specs=[_HBM] * (2 * n) + [_SEM, _SEM, _SEM, pl.BlockSpec(memory_space=pl.ANY)],
        out_specs=[_HBM] * (2 * n), input_output_aliases={k: k for k in range(2 * n)},
        compiler_params=pltpu.CompilerParams(has_side_effects=_EFFECT),
    )(*bufs, send_sems, recv_sems, local_sems, after)
    return list(outs[n:])


def _call(body, *, grid, in_specs, out_specs, out_shape, args, name, sem, scratch=(), comm=()):
    n_in, n_out, n_scr, n_c = len(in_specs), len(out_specs), len(scratch), len(comm)
    if not comm:
        outs = pl.pallas_call(body, grid=grid, in_specs=list(in_specs), out_specs=list(out_specs),
                              out_shape=list(out_shape), scratch_shapes=list(scratch), name=name,
                              compiler_params=_params(sem))(*args)
        return list(outs), []
    kinds = [k for k, _ in comm]
    n_steps = grid[0]

    def wrapped(*refs):
        ins, csrc = refs[:n_in], refs[n_in:n_in + n_c]
        o0 = n_in + n_c
        outs, cdst = refs[o0:o0 + n_out], refs[o0 + n_out:o0 + n_out + n_c]
        s0 = o0 + n_out + n_c
        scr, sems = refs[s0:s0 + n_scr], refs[s0 + n_scr:]
        i = pl.program_id(0)

        @pl.when(i == 0)
        def _():
            _xfers(kinds, csrc, cdst, *sems, start=True)

        body(*ins, *outs, *scr)

        @pl.when(i == n_steps - 1)
        def _():
            _xfers(kinds, csrc, cdst, *sems, start=False)

    hbm = pl.BlockSpec(memory_space=pl.ANY)
    outs = pl.pallas_call(
        wrapped, grid=grid, in_specs=list(in_specs) + [hbm] * n_c, out_specs=list(out_specs) + [hbm] * n_c,
        out_shape=list(out_shape) + [_xfer_out_shape(k, a) for k, a in comm],
        scratch_shapes=list(scratch) + _xfer_sems(n_c), name=name, compiler_params=_params(("arbitrary",)),
    )(*args, *[a for _, a in comm])
    return list(outs[:n_out]), list(outs[n_out:])


def _norm_mod_fwd(X, nw, mods, si, nxt, name):
    T = X.shape[0]

    def body(x_ref, w_ref, m_ref, h_ref):
        x = x_ref[...]
        r = lax.rsqrt(_rowmean(x * x) + EPS)
        m = m_ref[0]
        y = x * r * w_ref[...]
        h_ref[...] = (y * (1.0 + m[si + 1:si + 2, :]) + m[si:si + 1, :]).astype(BF16)

    return pl.pallas_call(
        body, grid=(T // ROW_TILE,), name=name,
        in_specs=[_row(D), _const((1, D)), _mod_spec(nxt)],
        out_specs=_row(D), out_shape=jax.ShapeDtypeStruct((T, D), BF16),
        compiler_params=_params(("parallel",)),
    )(X, nw, mods)


def _norm_mod_bwd(dh, X, nw, mods, dres, si, nxt, name):
    T = X.shape[0]

    def body(dh_ref, x_ref, w_ref, m_ref, dres_ref, dx_ref, st_ref):
        i = pl.program_id(0)

        @pl.when((i == 0) | (i == nxt))
        def _():
            st_ref[...] = jnp.zeros_like(st_ref)

        x = x_ref[...]
        r = lax.rsqrt(_rowmean(x * x) + EPS)
        xn = x * r
        w = w_ref[...]
        scale = m_ref[0][si + 1:si + 2, :]
        dhv = dh_ref[...]
        dy = dhv * (1.0 + scale)
        dxn = dy * w
        dx_ref[...] = dres_ref[...] + r * (dxn - xn * _rowmean(dxn * xn))
        st_ref[0, 0:1, :] += _colsum(dhv)
        st_ref[0, 1:2, :] += _colsum(dhv * xn * w)
        st_ref[0, 2:3, :] += _colsum(dy * xn)

    return pl.pallas_call(
        body, grid=(T // ROW_TILE,), name=name,
        in_specs=[_row(D), _row(D), _const((1, D)), _mod_spec(nxt), _row(D)],
        out_specs=[_row(D), _stat_spec(nxt)],
        out_shape=[jax.ShapeDtypeStruct((T, D), F32), jax.ShapeDtypeStruct((2, 8, D), F32)],
        compiler_params=_params(("arbitrary",)),
    )(dh, X, nw, mods, dres)


def _norm_mod_resid_bwd(dh, X, nw, mods, dres, si, y, gmods, gi, nxt, name):
    T = X.shape[0]

    def body(dh_ref, x_ref, w_ref, m_ref, dres_ref, y_ref, gm_ref, dx_ref, st_ref, dy_ref):
        i = pl.program_id(0)

        @pl.when((i == 0) | (i == nxt))
        def _():
            st_ref[...] = jnp.zeros_like(st_ref)

        x = x_ref[...]
        r = lax.rsqrt(_rowmean(x * x) + EPS)
        xn = x * r
        w = w_ref[...]
        scale = m_ref[0][si + 1:si + 2, :]
        dhv = dh_ref[...]
        dyn = dhv * (1.0 + scale)
        dxn = dyn * w
        dx = dres_ref[...] + r * (dxn - xn * _rowmean(dxn * xn))
        dx_ref[...] = dx
        dy_ref[...] = (dx * gm_ref[0][gi:gi + 1, :]).astype(BF16)
        st_ref[0, 0:1, :] += _colsum(dhv)
        st_ref[0, 1:2, :] += _colsum(dhv * xn * w)
        st_ref[0, 2:3, :] += _colsum(dyn * xn)
        st_ref[0, 3:4, :] += _colsum(dx * y_ref[...].astype(F32))

    return pl.pallas_call(
        body, grid=(T // ROW_TILE,), name=name,
        in_specs=[_row(D), _row(D), _const((1, D)), _mod_spec(nxt), _row(D), _row(D), _mod_spec(nxt)],
        out_specs=[_row(D), _stat_spec(nxt), _row(D)],
        out_shape=[jax.ShapeDtypeStruct((T, D), F32), jax.ShapeDtypeStruct((2, 8, D), F32),
                   jax.ShapeDtypeStruct((T, D), BF16)],
        compiler_params=_params(("arbitrary",)),
    )(dh, X, nw, mods, dres, y, gmods)


def _resid_bwd(dX, y, mods, gi, nxt, name):
    T = dX.shape[0]

    def body(dx_ref, y_ref, m_ref, dy_ref, st_ref):
        i = pl.program_id(0)

        @pl.when((i == 0) | (i == nxt))
        def _():
            st_ref[...] = jnp.zeros_like(st_ref)

        dx = dx_ref[...]
        dy_ref[...] = (dx * m_ref[0][gi:gi + 1, :]).astype(BF16)
        st_ref[0, 0:1, :] += _colsum(dx * y_ref[...].astype(F32))

    return pl.pallas_call(
        body, grid=(T // ROW_TILE,), name=name,
        in_specs=[_row(D), _row(D), _mod_spec(nxt)],
        out_specs=[_row(D), _stat_spec(nxt)],
        out_shape=[jax.ShapeDtypeStruct((T, D), BF16), jax.ShapeDtypeStruct((2, 8, D), F32)],
        compiler_params=_params(("arbitrary",)),
    )(dX, y, mods)


def _loss_head(X, tgt, fw, nxt, name):
    T = X.shape[0]

    def body(x_ref, t_ref, w_ref, dx_ref, st_ref):
        i = pl.program_id(0)

        @pl.when(i == 0)
        def _():
            st_ref[...] = jnp.zeros_like(st_ref)

        @pl.when(i < nxt)
        def _():
            x = x_ref[...]
            r = lax.rsqrt(_rowmean(x * x) + EPS)
            xn = x * r
            w = w_ref[...]
            err = xn * w - t_ref[...]
            dy = err * (1.0 / D)
            dxn = dy * w
            dx_ref[...] = r * (dxn - xn * _rowmean(dxn * xn))
            st_ref[0:1, :] += _colsum(dy * xn)
            st_ref[1:2, :] += (0.5 / D) * _colsum(err * err)

        @pl.when(i >= nxt)
        def _():
            dx_ref[...] = jnp.zeros_like(dx_ref)

    return pl.pallas_call(
        body, grid=(T // ROW_TILE,), name=name,
        in_specs=[_row(D), pl.BlockSpec((ROW_TILE, D), lambda i: (jnp.minimum(i, nxt - 1), 0)), _const((1, D))],
        out_specs=[_row(D), _const((8, D))],
        out_shape=[jax.ShapeDtypeStruct((T, D), F32), jax.ShapeDtypeStruct((8, D), F32)],
        compiler_params=_params(("arbitrary",)),
    )(X, tgt, fw)


def _mm_nn(A, B, out_dtype, name, add=None, comm=()):
    T, K = A.shape
    N = B.shape[1]

    tm = _mm_tile(T, MM_ROWS)
    if add is None:
        def body(a_ref, b_ref, o_ref):
            o_ref[...] = _dot(a_ref[...], b_ref[...]).astype(out_dtype)
        ins, specs = (A, B), [_row(K, tm), _const((K, N))]
    else:
        def body(a_ref, b_ref, c_ref, o_ref):
            o_ref[...] = (c_ref[...] + _dot(a_ref[...], b_ref[...])).astype(out_dtype)
        ins, specs = (A, B, add), [_row(K, tm), _const((K, N)), _row(N, tm)]

    return _call(body, grid=(T // tm,), name=name, in_specs=specs,
                 out_specs=[_row(N, tm)], out_shape=[jax.ShapeDtypeStruct((T, N), out_dtype)],
                 sem=("parallel",), args=ins, comm=comm)


def _mm_nn_resid(A, B, X, mods, gi, seq, name, comm=()):
    T, K = A.shape
    N = B.shape[1]
    tm = _mm_tile(T, MM_ROWS)

    def body(a_ref, b_ref, x_ref, m_ref, xo_ref, y_ref):
        acc = _dot(a_ref[...], b_ref[...])
        y_ref[...] = acc.astype(BF16)
        row = pl.program_id(0) * tm + lax.broadcasted_iota(jnp.int32, (tm, 1), 0)
        gate = jnp.where(row >= seq, m_ref[1][gi:gi + 1, :], m_ref[0][gi:gi + 1, :])
        xo_ref[...] = x_ref[...] + gate * acc

    return _call(body, grid=(T // tm,), name=name,
                 in_specs=[_row(K, tm), _const((K, N)), _row(N, tm), _const((2, 6, D))],
                 out_specs=[_row(N, tm), _row(N, tm)],
                 out_shape=[jax.ShapeDtypeStruct((T, N), F32), jax.ShapeDtypeStruct((T, N), BF16)],
                 sem=("parallel",), args=(A, B, X, mods), comm=comm)


def _mm_nt(A, B, name, comm=()):
    T, N = A.shape
    K = B.shape[0]
    tm = _mm_tile(T, MM_ROWS)

    def body(a_ref, b_ref, o_ref):
        o_ref[...] = _dot_nt(a_ref[...], b_ref[...])

    return _call(body, grid=(T // tm,), name=name, in_specs=[_row(N, tm), _const((K, N))], out_specs=[_row(K, tm)],
                 out_shape=[jax.ShapeDtypeStruct((T, K), F32)], sem=("parallel",), args=(A, B), comm=comm)


def _mm_tn(A, G, name):
    T, K = A.shape
    N = G.shape[1]
    wide = max(K, N)
    blk = next(b for b in (768, 512) if wide % b == 0)

    def body(a_ref, g_ref, o_ref):
        o_ref[...] = _dot_tn(a_ref[...], g_ref[...]).astype(BF16)

    whole = lambda cols: pl.BlockSpec((T, cols), lambda j: (0, 0))
    cols = lambda: pl.BlockSpec((T, blk), lambda j: (0, j))
    if N >= K:
        in_specs, out_spec = [whole(K), cols()], pl.BlockSpec((K, blk), lambda j: (0, j))
    else:
        in_specs, out_spec = [cols(), whole(N)], pl.BlockSpec((blk, N), lambda j: (j, 0))
    return pl.pallas_call(
        body, grid=(wide // blk,), name=name, in_specs=in_specs, out_specs=out_spec,
        out_shape=jax.ShapeDtypeStruct((K, N), BF16), compiler_params=_params(("parallel",)),
    )(A, G)


def _ffn_up(h, WgT, WuT, name, comm=()):
    T = h.shape[0]
    F = WgT.shape[0]

    def body(h_ref, wg_ref, wu_ref, gp_ref, up_ref, act_ref):
        hv = h_ref[...]
        gp = _dot_nt(hv, wg_ref[...])
        up = _dot_nt(hv, wu_ref[...])
        gp_ref[...] = gp.astype(BF16)
        up_ref[...] = up.astype(BF16)
        act_ref[...] = (gp * _sigmoid(gp) * up).astype(BF16)

    sd = jax.ShapeDtypeStruct((T, F), BF16)
    return _call(body, grid=(T // ROW_TILE,), name=name, in_specs=[_row(D), _const((F, D)), _const((F, D))],
                 out_specs=[_row(F), _row(F), _row(F)], out_shape=[sd, sd, sd], sem=("parallel",),
                 args=(h, WgT, WuT), comm=comm)


def _ffn_dact(dy, Wd, gp, up, name, comm=()):
    T = dy.shape[0]
    F = Wd.shape[0]

    def body(dy_ref, wd_ref, gp_ref, up_ref, dgp_ref, dup_ref):
        dact = _dot_nt(dy_ref[...], wd_ref[...])
        gpv = gp_ref[...].astype(F32)
        upv = up_ref[...].astype(F32)
        sg = _sigmoid(gpv)
        dup_ref[...] = (dact * gpv * sg).astype(BF16)
        dgp_ref[...] = (dact * upv * sg * (1.0 + gpv * (1.0 - sg))).astype(BF16)

    sd = jax.ShapeDtypeStruct((T, F), BF16)
    return _call(body, grid=(T // ROW_TILE,), name=name, in_specs=[_row(D), _const((F, D)), _row(F), _row(F)],
                 out_specs=[_row(F), _row(F)], out_shape=[sd, sd], sem=("parallel",),
                 args=(dy, Wd, gp, up), comm=comm)


N_TAB = 7


def _ret_tables(tab_ref, lgf, lgb):
    r = lax.broadcasted_iota(jnp.int32, (CHUNK, CHUNK), 0).astype(F32)
    c = lax.broadcasted_iota(jnp.int32, (CHUNK, CHUNK), 1).astype(F32)
    for d, lg in ((0, lgf), (1, lgb)):
        if d == 0:
            mask, expo, xe, ze = r >= c, r - c, r + 1.0, (CHUNK - 1.0) - r
        else:
            mask, expo, xe, ze = c > r, c - r - 1.0, (CHUNK - 1.0) - r, r
        e = jnp.where(mask, expo, 0.0)
        tab_ref[N_TAB * d + 0] = jnp.where(mask, jnp.exp(lg * e), 0.0)
        tab_ref[N_TAB * d + 1] = jnp.exp(lg * xe)
        tab_ref[N_TAB * d + 2] = jnp.exp(lg * ze)
        tab_ref[N_TAB * d + 3] = jnp.exp(jnp.full((CHUNK, CHUNK), lg * float(CHUNK), F32))
        tab_ref[N_TAB * d + 4] = e
        tab_ref[N_TAB * d + 5] = xe
        tab_ref[N_TAB * d + 6] = ze


def _rope(t, cosf, sgn):
    return t * cosf + pltpu.roll(t, HEAD_DIM // 2, 1) * sgn


def _rope_t(d, cosf, sgn):
    return d * cosf + pltpu.roll(d * sgn, HEAD_DIM // 2, 1)


def _chunk_of(d, j, nc, ncx):
    return lax.rem(j + ncx, nc) if d == 0 else nc - 1 - j


def _head_spec(T, col0):
    return pl.BlockSpec((T, HEAD_DIM), lambda h: (0, col0 + h))


def _ret_fwd(p, lg, cosf, sgn, seq, name, comm=()):
    T = p.shape[0]
    nc, ncx = T // CHUNK, seq // CHUNK

    def body(lg_ref, q_ref, k_ref, v_ref, g_ref, cos_ref, sgn_ref, y_ref, ro_ref, tab_ref, of_ref):
        h = pl.program_id(0)
        _ret_tables(tab_ref, lg_ref[0, h], lg_ref[1, h])

        def load(cidx):
            rows = pl.ds(pl.multiple_of(cidx * CHUNK, CHUNK), CHUNK)
            cs, sn = cos_ref[rows, :], sgn_ref[rows, :]
            return rows, _rope(q_ref[rows, :], cs, sn) * Q_SCALE, _rope(k_ref[rows, :], cs, sn), v_ref[rows, :]

        def chunk(d, q, k, v, S):
            b = N_TAB * d
            kb, vb = k.astype(BF16), v.astype(BF16)
            pm = _dot_nt(q.astype(BF16), kb) * tab_ref[b]
            o = _dot(pm.astype(BF16), vb) + _dot((q * tab_ref[b + 1]).astype(BF16), S.astype(BF16))
            return o, S * tab_ref[b + 3] + _dot_tn((k * tab_ref[b + 2]).astype(BF16), vb)

        def step(j, carry):
            Sf, Sb = carry
            rows, q, k, v = load(_chunk_of(0, j, nc, ncx))
            o, Sf = chunk(0, q, k, v, Sf)
            y_ref[rows, :] = o
            rows, q, k, v = load(_chunk_of(1, j, nc, ncx))
            o, Sb = chunk(1, q, k, v, Sb)
            of_ref[rows, :] = o
            return Sf, Sb

        zero = jnp.zeros((CHUNK, CHUNK), F32)
        lax.fori_loop(0, nc, step, (zero, zero))

        def finish(cidx, carry):
            rows = pl.ds(pl.multiple_of(cidx * CHUNK, CHUNK), CHUNK)
            y = y_ref[rows, :] + of_ref[rows, :]
            y_ref[rows, :] = y
            g = g_ref[rows, :]
            ro_ref[rows, :] = (g * _sigmoid(g) * y * lax.rsqrt(_rowmean(y * y) + EPS)).astype(BF16)
            return carry

        lax.fori_loop(0, nc, finish, 0)

    tbl = pl.BlockSpec((T, HEAD_DIM), lambda h: (0, 0))
    return _call(
        body, grid=(RET_HEADS,), name=name,
        in_specs=[pl.BlockSpec(memory_space=pltpu.SMEM), _head_spec(T, 0), _head_spec(T, 4), _head_spec(T, 8),
                  _head_spec(T, 12), tbl, tbl],
        out_specs=[_head_spec(T, 0), _head_spec(T, 0)],
        out_shape=[jax.ShapeDtypeStruct((T, RET_W), F32), jax.ShapeDtypeStruct((T, RET_W), BF16)],
        scratch=[pltpu.VMEM((2 * N_TAB, CHUNK, CHUNK), F32), pltpu.VMEM((T, HEAD_DIM), F32)],
        sem=("arbitrary",), args=(lg, p, p, p, p, cosf, sgn), comm=comm)


def _ret_gate_bwd(yret, p, dmix, name):
    T = yret.shape[0]

    def body(y_ref, g_ref, dm_ref, dy_ref, dg_ref):
        for hh in range(RET_HEADS):
            sl = slice(hh * HEAD_DIM, (hh + 1) * HEAD_DIM)
            y = y_ref[:, sl]
            r = lax.rsqrt(_rowmean(y * y) + EPS)
            yn = y * r
            g = g_ref[:, sl]
            sg = _sigmoid(g)
            dro = dm_ref[:, sl]
            dg_ref[:, sl] = (dro * yn * sg * (1.0 + g * (1.0 - sg))).astype(BF16)
            dyn = dro * g * sg
            dy_ref[:, sl] = r * (dyn - yn * _rowmean(dyn * yn))

    return pl.pallas_call(
        body, grid=(T // ROW_TILE,), name=name,
        in_specs=[_row(RET_W), pl.BlockSpec((ROW_TILE, RET_W), lambda i: (i, 3)), _row(RET_W)],
        out_specs=[_row(RET_W), _row(RET_W)],
        out_shape=[jax.ShapeDtypeStruct((T, RET_W), F32), jax.ShapeDtypeStruct((T, RET_W), BF16)],
        compiler_params=_params(("parallel",)),
    )(yret, p, dmix)


def _ret_bwd(p, lg, dy, cosf, sgn, seq, name, comm=()):
    T = p.shape[0]
    nc, ncx = T // CHUNK, seq // CHUNK

    def body(lg_ref, q_ref, k_ref, v_ref, dy_ref, cos_ref, sgn_ref, dq_ref, dk_ref, dv_ref, dlg_ref,
             tab_ref, st_ref, dqs, dks, dvs):
        h = pl.program_id(0)
        _ret_tables(tab_ref, lg_ref[0, h], lg_ref[1, h])
        dlg_ref[...] = jnp.zeros_like(dlg_ref)

        def load(cidx):
            rows = pl.ds(pl.multiple_of(cidx * CHUNK, CHUNK), CHUNK)
            cs, sn = cos_ref[rows, :], sgn_ref[rows, :]
            return rows, _rope(q_ref[rows, :], cs, sn) * Q_SCALE, _rope(k_ref[rows, :], cs, sn), v_ref[rows, :]

        def states(j, carry):
            out = []
            for d, S in enumerate(carry):
                b = N_TAB * d
                _, _, k, v = load(_chunk_of(d, j, nc, ncx))
                st_ref[d, j] = S
                out.append(S * tab_ref[b + 3] + _dot_tn((k * tab_ref[b + 2]).astype(BF16), v.astype(BF16)))
            return tuple(out)

        zero = jnp.zeros((CHUNK, CHUNK), F32)
        lax.fori_loop(0, nc, states, (zero, zero))

        def sweep_one(d, j, dS, acc):
            b = N_TAB * d
            rows, q, k, v = load(_chunk_of(d, j, nc, ncx))
            dO = dy_ref[rows, :]
            Sp = st_ref[d, j]
            qb, kb, vb, dOb = q.astype(BF16), k.astype(BF16), v.astype(BF16), dO.astype(BF16)
            Spb, dSb = Sp.astype(BF16), dS.astype(BF16)
            dmat, xi, ze, cd = tab_ref[b], tab_ref[b + 1], tab_ref[b + 2], tab_ref[b + 3]
            pm = _dot_nt(qb, kb) * dmat
            dpm = _dot_nt(dOb, vb)
            acc = acc + _colsum(dpm * pm * tab_ref[b + 4])
            dsc = (dpm * dmat).astype(BF16)
            qx = (q * xi).astype(BF16)
            kz = (k * ze).astype(BF16)
            dq = _dot(dsc, kb) + _dot_nt(dOb, Spb) * xi
            dk = _dot_tn(dsc, qb) + _dot_nt(vb, dSb) * ze
            dvst = _dot(kz, dSb)
            dv = _dot_tn(pm.astype(BF16), dOb) + dvst
            inter = _dot(qx, Spb)
            acc = acc + _colsum(dO * inter * tab_ref[b + 5])
            acc = acc + float(CHUNK) * _colsum(dS * cd * Sp) + _colsum(v * dvst * tab_ref[b + 6])
            dqs[d, rows, :] = dq.astype(BF16)
            dks[d, rows, :] = dk.astype(BF16)
            dvs[d, rows, :] = dv.astype(BF16)
            return dS * cd + _dot_tn(qx, dOb), acc

        def sweep(jj, carry):
            dSf, accf, dSb, accb = carry
            j = nc - 1 - jj
            dSf, accf = sweep_one(0, j, dSf, accf)
            dSb, accb = sweep_one(1, j, dSb, accb)
            return dSf, accf, dSb, accb

        zacc = jnp.zeros((1, CHUNK), F32)
        _, accf, _, accb = lax.fori_loop(0, nc, sweep, (zero, zacc, zero, zacc))
        dlg_ref[0, 0:1, :] = jnp.zeros((1, LANES), F32) + jnp.sum(accf)
        dlg_ref[0, 1:2, :] = jnp.zeros((1, LANES), F32) + jnp.sum(accb)

        def finish(cidx, carry):
            rows = pl.ds(pl.multiple_of(cidx * CHUNK, CHUNK), CHUNK)
            cs, sn = cos_ref[rows, :], sgn_ref[rows, :]
            both = lambda ref: ref[0, rows, :].astype(F32) + ref[1, rows, :].astype(F32)
            dq_ref[rows, :] = (_rope_t(both(dqs), cs, sn) * Q_SCALE).astype(BF16)
            dk_ref[rows, :] = _rope_t(both(dks), cs, sn).astype(BF16)
            dv_ref[rows, :] = both(dvs).astype(BF16)
            return carry

        lax.fori_loop(0, nc, finish, 0)

    tbl = pl.BlockSpec((T, HEAD_DIM), lambda h: (0, 0))
    sd = jax.ShapeDtypeStruct((T, RET_W), BF16)
    return _call(
        body, grid=(RET_HEADS,), name=name,
        in_specs=[pl.BlockSpec(memory_space=pltpu.SMEM), _head_spec(T, 0), _head_spec(T, 4), _head_spec(T, 8),
                  _head_spec(T, 0), tbl, tbl],
        out_specs=[_head_spec(T, 0), _head_spec(T, 0), _head_spec(T, 0),
                   pl.BlockSpec((1, 8, LANES), lambda h: (h, 0, 0))],
        out_shape=[sd, sd, sd, jax.ShapeDtypeStruct((RET_HEADS, 8, LANES), F32)],
        scratch=[pltpu.VMEM((2 * N_TAB, CHUNK, CHUNK), F32), pltpu.VMEM((2, nc, CHUNK, CHUNK), F32),
                 pltpu.VMEM((2, T, HEAD_DIM), BF16), pltpu.VMEM((2, T, HEAD_DIM), BF16),
                 pltpu.VMEM((2, T, HEAD_DIM), BF16)],
        sem=("arbitrary",), args=(lg, p, p, p, dy, cosf, sgn), comm=comm)


def _halo_specs(T, cols, colblock):
    per = ROW_TILE // HALO
    last = T // HALO - 1
    prv = pl.BlockSpec((HALO, cols), lambda i: (jnp.maximum(i * per - 1, 0), colblock))
    nxt = pl.BlockSpec((HALO, cols), lambda i: (jnp.minimum((i + 1) * per, last), colblock))
    return prv, nxt


def _seq_edges(i, nxt, nt):
    first = (i == 0) | (i == nxt)
    last = (i == nxt - 1) | (i == nt - 1)
    return first, last


def _fill_pad(pad_ref, prv, cur, nxt, first, last):
    pad_ref[0:HALO, :] = jnp.where(first, 0.0, prv)
    pad_ref[HALO:HALO + ROW_TILE, :] = cur
    pad_ref[HALO + ROW_TILE:2 * HALO + ROW_TILE, :] = jnp.where(last, 0.0, nxt)


SUBLANES = 8
SHIFT_ROWS = ROW_TILE + 2 * HALO - SUBLANES


def _fill_shifts(sh_ref, pad_ref):
    for b in range(SUBLANES):
        sh_ref[b] = pad_ref[b:b + SHIFT_ROWS, :]


TAP_ROWS = 32


def _window(sh_ref, o, r0=0, rows=ROW_TILE):
    b = o % SUBLANES
    return sh_ref[b, o - b + r0:o - b + r0 + rows, :]


def _tap_sum(w_ref, sh_ref, offset_of_tap, r0):
    acc = jnp.zeros((TAP_ROWS, CONV_CH), F32)
    for k in range(CONV_K):
        acc = acc + w_ref[k:k + 1, :] * _window(sh_ref, offset_of_tap(k), r0, TAP_ROWS)
    return acc


def _ln_stats(x):
    mu = _rowmean(x)
    xc = x - mu
    rs = lax.rsqrt(_rowmean(xc * xc) + EPS)
    return xc * rs, rs


def _conv_fwd(p, cw, lnw, lnb, nxt, name, comm=()):
    T = p.shape[0]
    nt = T // ROW_TILE
    a_col, g_col = 4, 5

    def body(a_ref, g_ref, ap_ref, gp_ref, an_ref, gn_ref, w_ref, lw_ref, lb_ref, cv_ref, co_ref, pad_ref, sh_ref):
        i = pl.program_id(0)
        first, last = _seq_edges(i, nxt, nt)
        glu = lambda a, g: a * _sigmoid(g)
        _fill_pad(pad_ref, glu(ap_ref[...], gp_ref[...]), glu(a_ref[...], g_ref[...]),
                  glu(an_ref[...], gn_ref[...]), first, last)
        _fill_shifts(sh_ref, pad_ref)
        for r0 in range(0, ROW_TILE, TAP_ROWS):
            cv_ref[r0:r0 + TAP_ROWS, :] = _tap_sum(w_ref, sh_ref, lambda k: k + 1, r0)
        xh, _ = _ln_stats(cv_ref[...])
        z = xh * lw_ref[...] + lb_ref[...]
        co_ref[...] = (z * _sigmoid(z)).astype(BF16)

    cur = lambda cb: pl.BlockSpec((ROW_TILE, CONV_CH), lambda i: (i, cb))
    ap, an = _halo_specs(T, CONV_CH, a_col)
    gp, gn = _halo_specs(T, CONV_CH, g_col)
    return _call(
        body, grid=(nt,), name=name,
        in_specs=[cur(a_col), cur(g_col), ap, gp, an, gn, _const((32, CONV_CH)), _const((1, CONV_CH)),
                  _const((1, CONV_CH))],
        out_specs=[_row(CONV_CH), _row(CONV_CH)],
        out_shape=[jax.ShapeDtypeStruct((T, CONV_CH), F32), jax.ShapeDtypeStruct((T, CONV_CH), BF16)],
        scratch=[pltpu.VMEM((ROW_TILE + 2 * HALO, CONV_CH), F32), pltpu.VMEM((SUBLANES, SHIFT_ROWS, CONV_CH), F32)],
        sem=("parallel",), args=(p, p, p, p, p, p, cw, lnw, lnb), comm=comm)


def _conv_bwd(p, cv, dmix, cw, lnw, lnb, nxt, name, comm=()):
    T = p.shape[0]
    nt = T // ROW_TILE
    a_col, g_col = 4, 5

    def body(a_ref, g_ref, ap_ref, gp_ref, an_ref, gn_ref, cv_ref, cvp_ref, cvn_ref, dc_ref, dcp_ref, dcn_ref,
             w_ref, lw_ref, lb_ref, dp_ref, dw_ref, dl_ref, upad_ref, dpad_ref, ush_ref, dsh_ref):
        i = pl.program_id(0)
        first, last = _seq_edges(i, nxt, nt)

        @pl.when(i == 0)
        def _():
            dw_ref[...] = jnp.zeros_like(dw_ref)
            dl_ref[...] = jnp.zeros_like(dl_ref)

        lw, lb = lw_ref[...], lb_ref[...]

        def ln_bwd(cvv, dco):
            xh, rs = _ln_stats(cvv)
            z = xh * lw + lb
            sg = _sigmoid(z)
            dz = dco * sg * (1.0 + z * (1.0 - sg))
            dxh = dz * lw
            return rs * (dxh - _rowmean(dxh) - xh * _rowmean(dxh * xh)), dz, xh

        dcv, dz, xh = ln_bwd(cv_ref[...], dc_ref[...])
        _fill_pad(dpad_ref, ln_bwd(cvp_ref[...], dcp_ref[...])[0], dcv, ln_bwd(cvn_ref[...], dcn_ref[...])[0],
                  first, last)
        a, g = a_ref[...], g_ref[...]
        sg = _sigmoid(g)
        glu = lambda av, gv: av * _sigmoid(gv)
        _fill_pad(upad_ref, glu(ap_ref[...], gp_ref[...]), a * sg, glu(an_ref[...], gn_ref[...]), first, last)
        _fill_shifts(dsh_ref, dpad_ref)
        _fill_shifts(ush_ref, upad_ref)
        for k in range(CONV_K):
            dw_ref[k:k + 1, :] += _colsum(dcv * _window(ush_ref, k + 1))
        dl_ref[0:1, :] += _colsum(dz * xh)
        dl_ref[1:2, :] += _colsum(dz)
        for r0 in range(0, ROW_TILE, TAP_ROWS):
            rs = slice(r0, r0 + TAP_ROWS)
            du = _tap_sum(w_ref, dsh_ref, lambda k: CONV_K - k, r0)
            ar, sr = a_ref[rs, :], _sigmoid(g_ref[rs, :])
            dp_ref[rs, 0:CONV_CH] = (du * sr).astype(BF16)
            dp_ref[rs, CONV_CH:2 * CONV_CH] = (du * ar * sr * (1.0 - sr)).astype(BF16)

    cur = lambda cb: pl.BlockSpec((ROW_TILE, CONV_CH), lambda i: (i, cb))
    ap, an = _halo_specs(T, CONV_CH, a_col)
    gp, gn = _halo_specs(T, CONV_CH, g_col)
    cvp, cvn = _halo_specs(T, CONV_CH, 0)
    dcp, dcn = _halo_specs(T, CONV_CH, 1)
    return _call(
        body, grid=(nt,), name=name,
        in_specs=[cur(a_col), cur(g_col), ap, gp, an, gn, cur(0), cvp, cvn, cur(1), dcp, dcn,
                  _const((32, CONV_CH)), _const((1, CONV_CH)), _const((1, CONV_CH))],
        out_specs=[_row(2 * CONV_CH), _const((32, CONV_CH)), _const((8, CONV_CH))],
        out_shape=[jax.ShapeDtypeStruct((T, 2 * CONV_CH), BF16), jax.ShapeDtypeStruct((32, CONV_CH), F32),
                   jax.ShapeDtypeStruct((8, CONV_CH), F32)],
        scratch=[pltpu.VMEM((ROW_TILE + 2 * HALO, CONV_CH), F32), pltpu.VMEM((ROW_TILE + 2 * HALO, CONV_CH), F32),
                 pltpu.VMEM((SUBLANES, SHIFT_ROWS, CONV_CH), F32), pltpu.VMEM((SUBLANES, SHIFT_ROWS, CONV_CH), F32)],
        sem=("arbitrary",), args=(p, p, p, p, p, p, cv, cv, cv, dmix, dmix, dmix, cw, lnw, lnb), comm=comm)


def _tile_positions(i, nxt, seq, ctx, offset, rows):
    is_ctx = i >= nxt
    pos0 = (i - jnp.where(is_ctx, nxt, 0)) * ROW_TILE + offset
    length = jnp.where(is_ctx, ctx, seq).astype(F32)
    pos = (pos0 + lax.broadcasted_iota(jnp.int32, (rows, 1), 0)).astype(F32)
    return pos, length


def _pool_count(pos, length, w):
    left = w // 2
    right = w - 1 - left
    return jnp.minimum(pos + right, length - 1.0) - jnp.maximum(pos - left, 0.0) + 1.0


def _gelu(x):
    return 0.5 * x * (1.0 + lax.erf(x * INV_SQRT2))


def _odd_fwd(p, pw, ps, slw, slb, sw, sbf, nxt, seq, ctx, name, comm=()):
    T = p.shape[0]
    nt = T // ROW_TILE

    def body(p_ref, pp_ref, pn_ref, pw_ref, ps_ref, lw_ref, lb_ref, sw_ref, sb_ref, o_ref, pad_ref):
        i = pl.program_id(0)
        first, last = _seq_edges(i, nxt, nt)
        _fill_pad(pad_ref, pp_ref[...], p_ref[:, 0:POOL_CH], pn_ref[...], first, last)
        pos, length = _tile_positions(i, nxt, seq, ctx, 0, ROW_TILE)
        for gi, w in enumerate(POOL_WINDOWS):
            sl = slice(gi * GROUP_CH, (gi + 1) * GROUP_CH)
            left = w // 2
            ssum = jnp.zeros((ROW_TILE, GROUP_CH), F32)
            for o in range(-left, w - left):
                ssum = ssum + pad_ref[HALO + o:HALO + o + ROW_TILE, sl]
            m = ssum / _pool_count(pos, length, w) - p_ref[:, sl]
            pre = _dot(m.astype(BF16), pw_ref[gi].astype(BF16))
            o_ref[:, sl] = (pre * ps_ref[:, sl]).astype(BF16)
        u = _gelu(p_ref[:, POOL_CH:POOL_CH + SG_CH])
        xh, _ = _ln_stats(_gelu(p_ref[:, POOL_CH + SG_CH:ODD_IN]))
        vln = xh * lw_ref[...] + lb_ref[...]
        for n in range(ROW_TILE // SG_CHUNK):
            rs = slice(n * SG_CHUNK, (n + 1) * SG_CHUNK)
            for gi in range(4):
                sl = slice(gi * GROUP_CH, (gi + 1) * GROUP_CH)
                s = _dot(sw_ref[gi].astype(BF16), vln[rs, sl].astype(BF16)) + sb_ref[:, sl]
                o_ref[rs, POOL_CH + gi * GROUP_CH:POOL_CH + (gi + 1) * GROUP_CH] = (u[rs, sl] * s).astype(BF16)

    pp, pn = _halo_specs(T, POOL_CH, 0)
    return _call(
        body, grid=(nt,), name=name,
        in_specs=[_row(ODD_IN), pp, pn, _const((4, GROUP_CH, GROUP_CH)), _const((1, POOL_CH)), _const((1, SG_CH)),
                  _const((1, SG_CH)), _const((4, SG_CHUNK, SG_CHUNK)), _const((SG_CHUNK, SG_CH))],
        out_specs=[_row(D)], out_shape=[jax.ShapeDtypeStruct((T, D), BF16)],
        scratch=[pltpu.VMEM((ROW_TILE + 2 * HALO, POOL_CH), F32)],
        sem=("parallel",), args=(p, p, p, pw, ps, slw, slb, sw, sbf), comm=comm)


def _odd_bwd(p, dmix, pw, ps, slw, slb, sw, sbf, nxt, seq, ctx, name, comm=()):
    T = p.shape[0]
    nt = T // ROW_TILE

    def body(p_ref, pp_ref, pn_ref, dm_ref, dmp_ref, dmn_ref, pw_ref, ps_ref, lw_ref, lb_ref, sw_ref, sb_ref,
             dp_ref, dpw_ref, dsw_ref, dv_ref, dsb_ref, pad_ref, dpad_ref):
        i = pl.program_id(0)
        first, last = _seq_edges(i, nxt, nt)

        @pl.when(i == 0)
        def _():
            dpw_ref[...] = jnp.zeros_like(dpw_ref)
            dsw_ref[...] = jnp.zeros_like(dsw_ref)
            dv_ref[...] = jnp.zeros_like(dv_ref)
            dsb_ref[...] = jnp.zeros_like(dsb_ref)

        _fill_pad(pad_ref, pp_ref[...], p_ref[:, 0:POOL_CH], pn_ref[...], first, last)
        pos, length = _tile_positions(i, nxt, seq, ctx, 0, ROW_TILE)
        pos_p, _ = _tile_positions(i, nxt, seq, ctx, -HALO, HALO)
        pos_n, _ = _tile_positions(i, nxt, seq, ctx, ROW_TILE, HALO)
        scale = ps_ref[...]
        for gi, w in enumerate(POOL_WINDOWS):
            sl = slice(gi * GROUP_CH, (gi + 1) * GROUP_CH)
            left = w // 2
            right = w - 1 - left
            ssum = jnp.zeros((ROW_TILE, GROUP_CH), F32)
            for o in range(-left, right + 1):
                ssum = ssum + pad_ref[HALO + o:HALO + o + ROW_TILE, sl]
            cnt = _pool_count(pos, length, w)
            m = ssum / cnt - p_ref[:, sl]
            wg = pw_ref[gi].astype(BF16)
            pre = _dot(m.astype(BF16), wg)
            dpo = dm_ref[:, sl]
            dv_ref[0:1, sl] += _colsum(dpo * pre)
            dpre = (dpo * scale[:, sl]).astype(BF16)
            dpw_ref[gi] += _dot_tn(m.astype(BF16), dpre)
            dmc = _dot_nt(dpre, wg)
            halo_dm = lambda ref, ps_: _dot_nt((ref[:, sl] * scale[:, sl]).astype(BF16), wg) / _pool_count(ps_, length, w)
            dpad_ref[0:HALO, sl] = jnp.where(first, 0.0, halo_dm(dmp_ref, pos_p))
            dpad_ref[HALO:HALO + ROW_TILE, sl] = dmc / cnt
            dpad_ref[HALO + ROW_TILE:2 * HALO + ROW_TILE, sl] = jnp.where(last, 0.0, halo_dm(dmn_ref, pos_n))
            atd = jnp.zeros((ROW_TILE, GROUP_CH), F32)
            for o in range(-right, left + 1):
                atd = atd + dpad_ref[HALO + o:HALO + o + ROW_TILE, sl]
            dp_ref[:, sl] = (atd - dmc).astype(BF16)

        pu = p_ref[:, POOL_CH:POOL_CH + SG_CH]
        pv = p_ref[:, POOL_CH + SG_CH:ODD_IN]
        u = _gelu(pu)
        xh, rs_ = _ln_stats(_gelu(pv))
        lw = lw_ref[...]
        vln = xh * lw + lb_ref[...]
        dgelu = lambda x: 0.5 * (1.0 + lax.erf(x * INV_SQRT2)) + x * jnp.exp(-0.5 * x * x) * INV_SQRT_2PI
        for n in range(ROW_TILE // SG_CHUNK):
            rs = slice(n * SG_CHUNK, (n + 1) * SG_CHUNK)
            dvl = []
            for gi in range(4):
                sl = slice(gi * GROUP_CH, (gi + 1) * GROUP_CH)
                wq = sw_ref[gi].astype(BF16)
                vb = vln[rs, sl].astype(BF16)
                s = _dot(wq, vb) + sb_ref[:, sl]
                dsg = dm_ref[rs, POOL_CH + gi * GROUP_CH:POOL_CH + (gi + 1) * GROUP_CH]
                ds = dsg * u[rs, sl]
                dsb_ref[:, sl] += ds
                dsw_ref[gi] += _dot_nt(ds.astype(BF16), vb)
                dvl.append(_dot_tn(wq, ds.astype(BF16)))
                dp_ref[rs, POOL_CH + gi * GROUP_CH:POOL_CH + (gi + 1) * GROUP_CH] = (
                    dsg * s * dgelu(pu[rs, sl])).astype(BF16)
            dvln = jnp.concatenate(dvl, axis=1)
            xhc = xh[rs, :]
            dv_ref[1:2, :] += _colsum(dvln * xhc)
            dv_ref[2:3, :] += _colsum(dvln)
            dxh = dvln * lw
            dvv = rs_[rs, :] * (dxh - _rowmean(dxh) - xhc * _rowmean(dxh * xhc))
            dp_ref[rs, POOL_CH + SG_CH:ODD_IN] = (dvv * dgelu(pv[rs, :])).astype(BF16)

    pp, pn = _halo_specs(T, POOL_CH, 0)
    dmp, dmn = _halo_specs(T, POOL_CH, 0)
    gsd = jax.ShapeDtypeStruct((4, GROUP_CH, GROUP_CH), F32)
    return _call(
        body, grid=(nt,), name=name,
        in_specs=[_row(ODD_IN), pp, pn, _row(D), dmp, dmn, _const((4, GROUP_CH, GROUP_CH)), _const((1, POOL_CH)),
                  _const((1, SG_CH)), _const((1, SG_CH)), _const((4, SG_CHUNK, SG_CHUNK)), _const((SG_CHUNK, SG_CH))],
        out_specs=[_row(ODD_IN), _const((4, GROUP_CH, GROUP_CH)), _const((4, SG_CHUNK, SG_CHUNK)), _const((8, POOL_CH)),
                   _const((SG_CHUNK, SG_CH))],
        out_shape=[jax.ShapeDtypeStruct((T, ODD_IN), BF16), gsd, gsd, jax.ShapeDtypeStruct((8, POOL_CH), F32),
                   jax.ShapeDtypeStruct((SG_CHUNK, SG_CH), F32)],
        scratch=[pltpu.VMEM((ROW_TILE + 2 * HALO, POOL_CH), F32), pltpu.VMEM((ROW_TILE + 2 * HALO, POOL_CH), F32)],
        sem=("arbitrary",), args=(p, p, p, dmix, dmix, dmix, pw, ps, slw, slb, sw, sbf), comm=comm)


def _ada_fwd(cs, aw, ab, name):
    cols = aw.shape[2]

    def body(c_ref, w_ref, b_ref, o_ref):
        c = c_ref[...]
        s = (c * _sigmoid(c)).astype(BF16)
        o_ref[0] = _dot(s, w_ref[0].astype(BF16)) + b_ref[0]

    return pl.pallas_call(
        body, grid=(DEPTH,), name=name,
        in_specs=[pl.BlockSpec((16, D), lambda i: (0, 0)), pl.BlockSpec((1, D, cols), lambda i: (i, 0, 0)),
                  pl.BlockSpec((1, 1, cols), lambda i: (i, 0, 0))],
        out_specs=pl.BlockSpec((1, 16, cols), lambda i: (i, 0, 0)),
        out_shape=jax.ShapeDtypeStruct((DEPTH, 16, cols), F32),
        compiler_params=_params(("parallel",)),
    )(cs, aw, ab)


def _ada_bwd(cs, G, aw, name):
    cols = aw.shape[2]

    def body(c_ref, g_ref, w_ref, gw_ref, cc_ref):
        i = pl.program_id(0)

        @pl.when(i == 0)
        def _():
            cc_ref[...] = jnp.zeros_like(cc_ref)

        c = c_ref[...]
        s = (c * _sigmoid(c)).astype(BF16)
        g = g_ref[0]
        dc = g[8:9, :]
        for d in range(9, 16):
            dc = dc + g[d:d + 1, :]
        row = lax.broadcasted_iota(jnp.int32, (8, cols), 0)
        dcrows = jnp.where(row == 0, dc, 0.0)
        dm = jnp.concatenate([g[0:8, :], dcrows], axis=0).astype(BF16)
        gw_ref[0] = _dot_tn(s, dm)
        cc_ref[...] += _dot_nt(dcrows.astype(BF16), w_ref[0].astype(BF16))

    return pl.pallas_call(
        body, grid=(DEPTH,), name=name,
        in_specs=[pl.BlockSpec((16, D), lambda i: (0, 0)), pl.BlockSpec((1, 16, cols), lambda i: (i, 0, 0)),
                  pl.BlockSpec((1, D, cols), lambda i: (i, 0, 0))],
        out_specs=[pl.BlockSpec((1, D, cols), lambda i: (i, 0, 0)), pl.BlockSpec((8, D), lambda i: (0, 0))],
        out_shape=[jax.ShapeDtypeStruct((DEPTH, D, cols), F32), jax.ShapeDtypeStruct((8, D), F32)],
        compiler_params=_params(("arbitrary",)),
    )(cs, G, aw)


def _row_tile_for(rows):
    for t in (512, 256, 128, 64, 32, 16, 8):
        if rows % t == 0:
            return t
    raise ValueError(f"rows={rows} is not a multiple of 8")


def _sum_devices(x, name):
    _, rows, cols = x.shape
    tr = _row_tile_for(rows)

    def body(x_ref, o_ref):
        acc = x_ref[0]
        for d in range(1, N_DEV):
            acc = acc + x_ref[d]
        o_ref[...] = acc

    return pl.pallas_call(
        body, grid=(rows // tr,), name=name,
        in_specs=[pl.BlockSpec((N_DEV, tr, cols), lambda i: (0, i, 0))],
        out_specs=pl.BlockSpec((tr, cols), lambda i: (i, 0)),
        out_shape=jax.ShapeDtypeStruct((rows, cols), F32),
        compiler_params=_params(("parallel",)),
    )(x)


def _adamw(w, g, m, v, name):
    rows, cols = w.shape
    tr = _row_tile_for(rows)

    def body(w_ref, g_ref, m_ref, v_ref, go_ref, d_ref, mo_ref, vo_ref):
        gv = g_ref[...]
        mn = ADAM_B1 * m_ref[...] + (1.0 - ADAM_B1) * gv
        vn = ADAM_B2 * v_ref[...] + (1.0 - ADAM_B2) * (gv * gv)
        go_ref[...] = gv
        mo_ref[...] = mn
        vo_ref[...] = vn
        d_ref[...] = -ADAM_LR * ((mn / ADAM_BC1) / (jnp.sqrt(vn / ADAM_BC2) + ADAM_EPS) + ADAM_WD * w_ref[...])

    blk = pl.BlockSpec((tr, cols), lambda i: (i, 0))
    sd = jax.ShapeDtypeStruct((rows, cols), F32)
    return pl.pallas_call(
        body, grid=(rows // tr,), name=name,
        in_specs=[blk, blk, blk, blk], out_specs=[blk, blk, blk, blk], out_shape=[sd, sd, sd, sd],
        compiler_params=_params(("parallel",)),
    )(w, g, m, v)


def _adamw_layers(w, pieces, m, v, name):
    L, a, b = w.shape
    bp = pieces[0].shape[2]
    tr = next(t for t in (256, 128, 64, 32, 16) if a % t == 0)

    def body(w_ref, m_ref, v_ref, *rest):
        p_refs, (go_ref, d_ref, mo_ref, vo_ref) = rest[:L], rest[L:]
        layer = pl.program_id(0)
        for k in range(L):
            @pl.when(layer == k)
            def _(k=k):
                gv = p_refs[k][0].astype(F32)
                for d in range(1, N_DEV):
                    gv = gv + p_refs[k][d].astype(F32)
                gv = gv[:, :b]
                mn = ADAM_B1 * m_ref[0] + (1.0 - ADAM_B1) * gv
                vn = ADAM_B2 * v_ref[0] + (1.0 - ADAM_B2) * (gv * gv)
                go_ref[0] = gv
                mo_ref[0] = mn
                vo_ref[0] = vn
                d_ref[0] = -ADAM_LR * ((mn / ADAM_BC1) / (jnp.sqrt(vn / ADAM_BC2) + ADAM_EPS) + ADAM_WD * w_ref[0])

    blk = pl.BlockSpec((1, tr, b), lambda l, i: (l, i, 0))
    pspecs = [pl.BlockSpec((N_DEV, tr, bp), lambda l, i, k=k: (0, jnp.where(l == k, i, 0), 0)) for k in range(L)]
    sd = jax.ShapeDtypeStruct((L, a, b), F32)
    return pl.pallas_call(
        body, grid=(L, a // tr), name=name,
        in_specs=[blk, blk, blk] + pspecs, out_specs=[blk, blk, blk, blk], out_shape=[sd, sd, sd, sd],
        compiler_params=_params(("arbitrary", "arbitrary")),
    )(w, m, v, *pieces)


def _pack_rows(shape):
    return -(-math.prod(shape) // (8 * LANES)) * 8


def _pack(arrs, row_mult):
    parts = []
    for a in arrs:
        n, rows = math.prod(a.shape), _pack_rows(a.shape)
        parts.append(jnp.pad(a.reshape(-1).astype(F32), (0, rows * LANES - n)).reshape(rows, LANES))
    total = sum(p.shape[0] for p in parts)
    if total % row_mult:
        parts.append(jnp.zeros((-total % row_mult, LANES), F32))
    return jnp.concatenate(parts, axis=0)


def _unpack(packed, shapes):
    out, r0 = [], 0
    lead = packed.shape[:-2]
    for s in shapes:
        n, rows = math.prod(s), _pack_rows(s)
        piece = packed[..., r0:r0 + rows, :].reshape(lead + (rows * LANES,))
        out.append(piece[..., :n].reshape(lead + tuple(s)))
        r0 += rows
    return out


def _rope_tables(seq, ctx):
    def angles(ps):
        parts = []
        for pvec, n in zip(ps, ROPE_PAIRS):
            freq = ROPE_BASE ** (-jnp.arange(n, dtype=F32) / n)
            parts.append(pvec[:, None] * freq[None, :])
        return jnp.concatenate(parts, axis=-1)

    rows = seq // GRID_W
    grid_r = jnp.broadcast_to(jnp.arange(rows, dtype=F32)[:, None], (rows, GRID_W)).reshape(-1)
    grid_c = jnp.broadcast_to(jnp.arange(GRID_W, dtype=F32)[None, :], (rows, GRID_W)).reshape(-1)
    zc = jnp.zeros((ctx,), F32)
    ang = jnp.concatenate([angles((jnp.full((seq,), ctx, F32), grid_r, grid_c)),
                           angles((jnp.arange(ctx, dtype=F32), zc, zc))], axis=0)
    cos, sin = jnp.cos(ang), jnp.sin(ang)
    return jnp.concatenate([cos, cos], axis=1), jnp.concatenate([-sin, sin], axis=1)


def _layer_gather(sh, even):
    return [("ag_cols" if even else "ag_rows", sh["in"]), ("ag_rows", sh["out"]), ("ag_rows", sh["gate"]),
            ("ag_rows", sh["up"]), ("ag_rows", sh["down"])]


def _sample_step(X0, tgt, mods, shards, S, seq, ctx):
    nxt = seq // ROW_TILE
    cosf, sgn = _rope_tables(seq, ctx)
    X = X0
    saved = []
    gathers = {}

    def start_gather(i, half, dep):
        comm = _layer_gather(shards[i], i % 2 == 0)
        comm = comm[:2] if half == "mix" else comm[2:]
        comm[0] = (comm[0][0], comm[0][1] + dep.astype(BF16))
        gathers[(i, half)], token = _xfer_start(comm, f"gather_{half}{i}_start")
        return token[0, 0]

    dep = mods[0, 0, 0, 0] * 0.0
    for i, half in ((0, "mix"), (0, "ffn"), (1, "mix"), (1, "ffn")):
        dep = start_gather(i, half, dep)
    w_in, w_out = _xfer_wait(gathers[(0, "mix")], mods, "gather_mix0_wait")
    for i in range(DEPTH):
        j, even = i // 2, i % 2 == 0
        t = f"l{i}_"
        last = i == DEPTH - 1
        if i in (1, 2):
            dep = w_out[0, 0].astype(F32) * 0.0
            for half in ("mix", "ffn"):
                dep = start_gather(i + 1, half, dep)
        md = mods[i] + dep
        h1 = _norm_mod_fwd(X, S["norm_w"][i, 0][None], md, 0, nxt, t + "norm1")
        if even:
            (p,), _ = _mm_nn(h1, w_in, F32, t + "proj_in")
            (yret, ro), _ = _ret_fwd(p, S["lg"][j], cosf, sgn, seq, t + "ret_fwd")
            (cv, co), _ = _conv_fwd(p, S["conv_w"][j], S["conv_ln_w"][j][None], S["conv_ln_b"][j][None], nxt,
                                    t + "conv_fwd")
            mix = jnp.concatenate([ro, co], axis=1)
            extra = (yret, cv)
        else:
            (p,), _ = _mm_nt(h1, w_in, t + "proj_in")
            sbf = jnp.repeat(S["sg_b"][j].T, GROUP_CH, axis=1)
            (mix,), _ = _odd_fwd(p, S["pool_w"][j], S["pool_scale"][j][None], S["sg_ln_w"][j][None],
                                 S["sg_ln_b"][j][None], S["sg_w"][j], sbf, nxt, seq, ctx, t + "odd_fwd")
            extra = (sbf,)
        (X1, y1), _ = _mm_nn_resid(mix, w_out, X, md, 2, seq, t + "proj_out")
        h2 = _norm_mod_fwd(X1, S["norm_w"][i, 1][None], md, 3, nxt, t + "norm2")
        wg, wu, wd = _xfer_wait(gathers[(i, "ffn")], h2, f"gather_ffn{i}_wait")
        (gp, up, act), _ = _ffn_up(h2, wg, wu, t + "ffn_up")
        (X2, y2), _ = _mm_nn_resid(act, wd, X1, md, 5, seq, t + "ffn_down")
        saved.append((X, h1, p, mix, extra, y1, X1, h2, gp, up, act, y2, (w_in, w_out, wg, wu, wd)))
        X = X2
        if not last:
            w_in, w_out = _xfer_wait(gathers[(i + 1, "mix")], X2, f"gather_mix{i + 1}_wait")

    dX, hst = _loss_head(X, tgt, S["final_norm_w"][None], nxt, "loss_head")
    loss = jnp.sum(hst[1])
    pieces = {}
    gS = {"final_norm_w": hst[0], "norm_w": [None] * DEPTH, "lg": [None] * 2, "conv_w": [None] * 2,
          "conv_ln_w": [None] * 2, "conv_ln_b": [None] * 2, "pool_w": [None] * 2, "pool_scale": [None] * 2,
          "sg_ln_w": [None] * 2, "sg_ln_b": [None] * 2, "sg_w": [None] * 2, "sg_b": [None] * 2}
    dmods = [None] * DEPTH
    in_flight = []

    def land(flights, after, tag):
        for handle, keys, name in flights:
            for key, got in zip(keys, _xfer_wait(handle, after, name + tag)):
                pieces[key] = got

    for i in reversed(range(DEPTH)):
        j, even = i // 2, i % 2 == 0
        md = mods[i]
        t = f"l{i}_"
        X_in, h1, p, mix, extra, y1, X1, h2, gp, up, act, y2, (w_in, w_out, wg, wu, wd) = saved[i]
        if i == DEPTH - 1:
            dyf, g2 = _resid_bwd(dX, y2, md, 5, nxt, t + "ffn_resid_bwd")
            g2 = g2[:, 0]
        (dgp, dup), _ = _ffn_dact(dyf, wd, gp, up, t + "ffn_dact")
        g_down = _mm_tn(act, dyf, t + "dw_down")
        g_gate = _mm_tn(dgp, h2, t + "dw_gate")
        g_up = _mm_tn(dup, h2, t + "dw_up")
        ffn_flight, token = _xfer_start([("a2a_rows", g_down), ("a2a_rows", g_gate), ("a2a_rows", g_up)],
                                        t + "scatter_ffn_start")
        (dh2,), _ = _mm_nn(dgp, wg, F32, t + "dh2_gate")
        (dh2,), _ = _mm_nn(dup, wu, F32, t + "dh2_up", add=dh2)
        dX1, s2, dym = _norm_mod_resid_bwd(dh2, X1, S["norm_w"][i, 1][None], md + token[0, 0], dX, 3, y1, md, 2,
                                           nxt, t + "norm2_bwd")
        (dmix,), _ = _mm_nt(dym, w_out, t + "dmix")
        g_out = _mm_tn(mix, dym, t + "dw_out")
        if even:
            yret, cv = extra
            dyr, dg = _ret_gate_bwd(yret, p, dmix, t + "ret_gate_bwd")
            (dq, dk, dv, dlg), _ = _ret_bwd(p, S["lg"][j], dyr, cosf, sgn, seq, t + "ret_bwd")
            early = [("ag_blk", _pack([jnp.stack(gS["pool_w"]), jnp.stack(gS["sg_w"])], 8))] if i == 0 else []
            (dpc, dcw, dln), got = _conv_bwd(
                p, cv, dmix, S["conv_w"][j], S["conv_ln_w"][j][None], S["conv_ln_b"][j][None], nxt,
                t + "conv_bwd", comm=early)
            if i == 0:
                early_all = got[0]
            dp = jnp.concatenate([dq, dk, dv, dg, dpc], axis=1)
            gS["lg"][j] = dlg[:, 0:2, 0].T
            gS["conv_w"][j], gS["conv_ln_w"][j], gS["conv_ln_b"][j] = dcw, dln[0], dln[1]
            g_in = _mm_tn(h1, dp, t + "dw_in")
            mix_flight, token = _xfer_start([("a2a_rows", g_out), ("a2a_cols", g_in)], t + "scatter_mix_start")
            (dh1,), _ = _mm_nt(dp, w_in, t + "dh1")
        else:
            (sbf,) = extra
            (dp, dpw, dsw, dvec, dsb), _ = _odd_bwd(
                p, dmix, S["pool_w"][j], S["pool_scale"][j][None], S["sg_ln_w"][j][None], S["sg_ln_b"][j][None],
                S["sg_w"][j], sbf, nxt, seq, ctx, t + "odd_bwd")
            gS["pool_w"][j], gS["sg_w"][j] = dpw, dsw
            gS["pool_scale"][j], gS["sg_ln_w"][j], gS["sg_ln_b"][j] = dvec[0], dvec[1], dvec[2]
            gS["sg_b"][j] = jnp.sum(dsb.reshape(SG_CHUNK, 4, GROUP_CH), axis=2).T
            g_in = _mm_tn(dp, h1, t + "dw_in")
            mix_flight, token = _xfer_start([("a2a_rows", g_out), ("a2a_rows", g_in)], t + "scatter_mix_start")
            (dh1,), _ = _mm_nn(dp, w_in, F32, t + "dh1")
        if i == 0:
            dX, s1 = _norm_mod_bwd(dh1, X_in, S["norm_w"][i, 0][None], md + token[0, 0], dX1, 0, nxt,
                                   t + "norm1_bwd")
            g2_below = None
        else:
            dX, s1, dyf = _norm_mod_resid_bwd(dh1, X_in, S["norm_w"][i, 0][None], md + token[0, 0], dX1, 0,
                                              saved[i - 1][11], mods[i - 1], 5, nxt, t + "norm1_bwd")
            g2_below = s1[:, 3]
        gS["norm_w"][i] = jnp.stack([s1[0, 2] + s1[1, 2], s2[0, 2] + s2[1, 2]])
        dmods[i] = jnp.stack([s1[:, 0], s1[:, 1], s2[:, 3], s2[:, 0], s2[:, 1], g2], axis=1)
        g2 = g2_below
        land(in_flight, dX, "_wait")
        in_flight = [(ffn_flight, [("down", i), ("gate", i), ("up", i)], t + "scatter_ffn"),
                     (mix_flight, [("out", i), ("in", i)], t + "scatter_mix")]
    land(in_flight[:1], dX, "_wait")
    gS = {k: (jnp.stack(v) if isinstance(v, list) else v) for k, v in gS.items()}
    return loss, dX, pieces, gS, jnp.stack(dmods), early_all, in_flight[1]


FF_SHARD = D_FF // N_DEV
FF_SHARD_PAD = 384


def kernel(x, c, ctx, c_ctx, ada_w, ada_b, norm_w, even_w_in, even_w_out, ret_decay_logit, conv_dw_w, conv_ln_w, conv_ln_b, odd_w_in, odd_w_out, pool_w, pool_scale, sg_ln_w, sg_ln_b, sg_w, sg_b, ffn_w_gate, ffn_w_up, ffn_w_down, final_norm_w, loss_target, m_c_ctx, m_ada_w, m_ada_b, m_norm_w, m_even_w_in, m_even_w_out, m_ret_decay_logit, m_conv_dw_w, m_conv_ln_w, m_conv_ln_b, m_odd_w_in, m_odd_w_out, m_pool_w, m_pool_scale, m_sg_ln_w, m_sg_ln_b, m_sg_w, m_sg_b, m_ffn_w_gate, m_ffn_w_up, m_ffn_w_down, m_final_norm_w, v_c_ctx, v_ada_w, v_ada_b, v_norm_w, v_even_w_in, v_even_w_out, v_ret_decay_logit, v_conv_dw_w, v_conv_ln_w, v_conv_ln_b, v_odd_w_in, v_odd_w_out, v_pool_w, v_pool_scale, v_sg_ln_w, v_sg_ln_b, v_sg_w, v_sg_b, v_ffn_w_gate, v_ffn_w_up, v_ffn_w_down, v_final_norm_w):
    seq, n_ctx = x.shape[1], ctx.shape[1]
    me = 4 * lax.axis_index("x") + 2 * lax.axis_index("y") + lax.axis_index("c")
    mcols = ada_w.shape[2]

    tr = lambda a: jnp.swapaxes(a, 1, 2)
    pad_rows = lambda a: jnp.pad(a, ((0, FF_SHARD_PAD - FF_SHARD), (0, 0))).astype(BF16)
    gate_t, up_t, odd_in_t = tr(ffn_w_gate), tr(ffn_w_up), tr(odd_w_in)
    shards = []
    for i in range(DEPTH):
        j, even = i // 2, i % 2 == 0
        w_in, w_out = (even_w_in[j], even_w_out[j]) if even else (odd_in_t[j], odd_w_out[j])
        shards.append({"in": w_in.astype(BF16), "out": w_out.astype(BF16), "gate": pad_rows(gate_t[i]),
                       "up": pad_rows(up_t[i]), "down": pad_rows(ffn_w_down[i])})

    small_shapes = [(D,), norm_w.shape, conv_dw_w.shape, pool_scale.shape, sg_ln_w.shape, sg_ln_b.shape]
    (sm,) = _exchange([("ag_blk", _pack([c, norm_w, conv_dw_w, pool_scale, sg_ln_w, sg_ln_b], 8))], "gather_small")
    c_all, nw_s, cw_s, ps_s, slw_s, slb_s = _unpack(sm, small_shapes)
    cat_last = lambda a: jnp.moveaxis(a, 0, -2).reshape(a.shape[1:-1] + (-1,))
    conv_w_full = cat_last(cw_s)
    S = {"norm_w": cat_last(nw_s), "lg": jax.nn.log_sigmoid(ret_decay_logit),
         "conv_w": jnp.pad(conv_w_full, ((0, 0), (0, 32 - CONV_K), (0, 0))),
         "conv_ln_w": conv_ln_w, "conv_ln_b": conv_ln_b, "pool_w": pool_w, "pool_scale": cat_last(ps_s),
         "sg_ln_w": cat_last(slw_s), "sg_ln_b": cat_last(slb_s), "sg_w": sg_w, "sg_b": sg_b,
         "final_norm_w": final_norm_w}

    cs = jnp.concatenate([c_all, c_ctx[None], jnp.zeros((7, D), F32)], axis=0)
    ab_loc = lax.dynamic_slice_in_dim(ada_b, me * mcols, mcols, axis=1)
    mod_loc = _ada_fwd(cs, ada_w, ab_loc[:, None, :], "ada_fwd")
    (mod_all,) = _exchange([("ag_blk", mod_loc.reshape(DEPTH * 16, mcols))], "gather_mod")
    mod_all = mod_all.reshape(N_DEV, DEPTH, 16, mcols).transpose(1, 2, 0, 3).reshape(DEPTH, 16, 6, D)
    mod_x = lax.dynamic_index_in_dim(mod_all, me, axis=1, keepdims=False)
    mods = jnp.stack([mod_x, mod_all[:, 8]], axis=1)

    X0 = jnp.concatenate([x[0], ctx[0]], axis=0)
    loss, dX, big, gS, dmods, early_all, late = _sample_step(X0, loss_target[0], mods, shards, S, seq, n_ctx)
    loss = lax.psum(loss, ("x", "y", "c"))
    grad_x = dX[:seq][None]

    (dm_all,) = _exchange([("ag_blk", dmods.reshape(DEPTH * 2, 6 * D))], "gather_dmod")
    dm_all = dm_all.reshape(N_DEV, DEPTH, 2, 6 * D)
    dm_sum = _sum_devices(dm_all.reshape(N_DEV, DEPTH * 2, 6 * D), "sum_dmod").reshape(DEPTH, 2, 6 * D)
    g_ada_b = dm_sum[:, 0] + dm_sum[:, 1]
    G = lax.dynamic_slice_in_dim(dm_all, me * mcols, mcols, axis=3).transpose(1, 2, 0, 3).reshape(DEPTH, 16, mcols)
    g_ada_w, ccp = _ada_bwd(cs, G, ada_w, "ada_bwd")
    sg_cc = _sigmoid(c_ctx)
    g_cctx_part = ccp[0] * sg_cc * (1.0 + c_ctx * (1.0 - sg_cc))

    dsig = _sigmoid(-ret_decay_logit)
    part = [g_cctx_part, gS["norm_w"], gS["lg"] * dsig, gS["conv_w"][:, :CONV_K], gS["conv_ln_w"], gS["conv_ln_b"],
            gS["pool_scale"], gS["sg_ln_w"], gS["sg_ln_b"], gS["sg_b"], gS["final_norm_w"]]
    part_shapes = [a.shape for a in part]
    (part_all,) = _exchange([("ag_blk", _pack(part, 8))], "gather_small_grads")
    red = _sum_devices(part_all, "sum_small_grads")
    g_cctx, g_nw, g_rdl, g_cw, g_clw, g_clb, g_ps, g_slw, g_slb, g_sb, g_fnw = _unpack(red, part_shapes)
    g_pw, g_sw = _unpack(_sum_devices(early_all, "sum_early_grads"), [pool_w.shape, sg_w.shape])
    mine_last = lambda a, n: lax.dynamic_slice_in_dim(a, me * n, n, axis=a.ndim - 1)
    g_nw, g_cw = mine_last(g_nw, norm_w.shape[2]), mine_last(g_cw, conv_dw_w.shape[2])
    g_ps, g_slw, g_slb = (mine_last(a, pool_scale.shape[1]) for a in (g_ps, g_slw, g_slb))

    def adam_big(key, layers, w, m, v, name):
        return _adamw_layers(w, [big[(key, l)] for l in layers], m, v, name)

    res = {}
    adam_t = lambda key, layers, w, m, v, name: [tr(o) for o in adam_big(key, layers, tr(w), tr(m), tr(v), name)]
    res["odd_w_in"] = adam_t("in", (1, 3), odd_w_in, m_odd_w_in, v_odd_w_in, "adam_odd_in")
    res["odd_w_out"] = adam_big("out", (1, 3), odd_w_out, m_odd_w_out, v_odd_w_out, "adam_odd_out")
    res["ffn_w_gate"] = adam_t("gate", (0, 1, 2, 3), ffn_w_gate, m_ffn_w_gate, v_ffn_w_gate, "adam_gate")
    res["ffn_w_up"] = adam_t("up", (0, 1, 2, 3), ffn_w_up, m_ffn_w_up, v_ffn_w_up, "adam_up")
    res["ffn_w_down"] = adam_big("down", (0, 1, 2, 3), ffn_w_down, m_ffn_w_down, v_ffn_w_down, "adam_down")
    late_handle, late_keys, late_name = late
    for key, got in zip(late_keys, _xfer_wait(late_handle, res["ffn_w_down"][1], late_name + "_wait")):
        big[key] = got
    res["even_w_in"] = adam_big("in", (0, 2), even_w_in, m_even_w_in, v_even_w_in, "adam_even_in")
    res["even_w_out"] = adam_big("out", (0, 2), even_w_out, m_even_w_out, v_even_w_out, "adam_even_out")
    flat2 = lambda a: a.reshape(-1, a.shape[-1])
    res["ada_w"] = [o.reshape(ada_w.shape) for o in
                    _adamw(flat2(ada_w), flat2(g_ada_w), flat2(m_ada_w), flat2(v_ada_w), "adam_ada_w")]

    names = ["c_ctx", "ada_b", "norm_w", "ret_decay_logit", "conv_dw_w", "conv_ln_w", "conv_ln_b", "pool_w",
             "pool_scale", "sg_ln_w", "sg_ln_b", "sg_w", "sg_b", "final_norm_w"]
    ws = [c_ctx, ada_b, norm_w, ret_decay_logit, conv_dw_w, conv_ln_w, conv_ln_b, pool_w, pool_scale, sg_ln_w,
          sg_ln_b, sg_w, sg_b, final_norm_w]
    gs = [g_cctx, g_ada_b, g_nw, g_rdl, g_cw, g_clw, g_clb, g_pw, g_ps, g_slw, g_slb, g_sw, g_sb, g_fnw]
    ms = [m_c_ctx, m_ada_b, m_norm_w, m_ret_decay_logit, m_conv_dw_w, m_conv_ln_w, m_conv_ln_b, m_pool_w,
          m_pool_scale, m_sg_ln_w, m_sg_ln_b, m_sg_w, m_sg_b, m_final_norm_w]
    vs = [v_c_ctx, v_ada_b, v_norm_w, v_ret_decay_logit, v_conv_dw_w, v_conv_ln_w, v_conv_ln_b, v_pool_w,
          v_pool_scale, v_sg_ln_w, v_sg_ln_b, v_sg_w, v_sg_b, v_final_norm_w]
    shapes = [a.shape for a in ws]
    gs = [g.reshape(s) for g, s in zip(gs, shapes)]
    pk = lambda arrs: _pack(arrs, 512)
    outs = _adamw(pk(ws), pk(gs), pk(ms), pk(vs), "adam_small")
    for k, o in enumerate(outs):
        for nm, arr in zip(names, _unpack(o, shapes)):
            res.setdefault(nm, [None] * 4)[k] = arr

    order = ["c_ctx", "ada_w", "ada_b", "norm_w", "even_w_in", "even_w_out", "ret_decay_logit", "conv_dw_w",
             "conv_ln_w", "conv_ln_b", "odd_w_in", "odd_w_out", "pool_w", "pool_scale", "sg_ln_w", "sg_ln_b",
             "sg_w", "sg_b", "ffn_w_gate", "ffn_w_up", "ffn_w_down", "final_norm_w"]
    return (loss, grad_x, *[res[n][0] for n in order], *[res[n][1] for n in order],
            *[res[n][2] for n in order], *[res[n][3] for n in order])
```

```python
import math

import jax
import jax.numpy as jnp
from jax import lax
from jax.experimental import pallas as pl
from jax.experimental.pallas import tpu as pltpu

F32 = jnp.float32
BF16 = jnp.bfloat16

N_DEV = 8
D = 1024
DEPTH = 4
ROW_TILE = 256
HALO = 16
LANES = 128
EPS = 1e-6
RET_HEADS = 4
HEAD_DIM = 128
RET_W = RET_HEADS * HEAD_DIM
CHUNK = 128
Q_SCALE = HEAD_DIM ** -0.5
ROPE_BASE = 10000.0
ROPE_PAIRS = (HEAD_DIM // 8, 3 * HEAD_DIM // 16, 3 * HEAD_DIM // 16)
GRID_W = 64
CONV_CH = 512
CONV_K = 31
EVEN_IN = 4 * RET_W + 2 * CONV_CH
POOL_CH = 512
POOL_WINDOWS = (2, 4, 8, 16)
GROUP_CH = 128
SG_CH = 512
SG_CHUNK = 128
ODD_IN = POOL_CH + 2 * SG_CH
D_FF = 2816
INV_SQRT2 = 1.0 / math.sqrt(2.0)
INV_SQRT_2PI = 1.0 / math.sqrt(2.0 * math.pi)
ADAM_LR, ADAM_B1, ADAM_B2, ADAM_EPS, ADAM_WD, ADAM_STEP = 0.001, 0.9, 0.999, 1e-08, 0.01, 10
ADAM_BC1 = 1.0 - ADAM_B1 ** ADAM_STEP
ADAM_BC2 = 1.0 - ADAM_B2 ** ADAM_STEP
VMEM_LIMIT = 56 * 1024 * 1024
MESH = pl.DeviceIdType.MESH


def _params(sem=None):
    return pltpu.CompilerParams(dimension_semantics=sem, vmem_limit_bytes=VMEM_LIMIT)


def _dot(a, b):
    return jnp.dot(a, b, preferred_element_type=F32)


def _dot_nt(a, b):
    return lax.dot_general(a, b, (((1,), (1,)), ((), ())), preferred_element_type=F32)


def _dot_tn(a, b):
    return lax.dot_general(a, b, (((0,), (0,)), ((), ())), preferred_element_type=F32)


def _sigmoid(x):
    return 1.0 / (1.0 + jnp.exp(-x))


def _colsum(x):
    return jnp.sum(x, axis=0, keepdims=True)


def _rowmean(x):
    return jnp.mean(x, axis=-1, keepdims=True)


def _row(shape_cols, tile=ROW_TILE):
    return pl.BlockSpec((tile, shape_cols), lambda i: (i, 0))


def _mm_tile(T, cands):
    return next(t for t in cands if T % t == 0)


MM_ROWS = (544, 512, ROW_TILE)


def _const(shape):
    nd = len(shape)
    return pl.BlockSpec(shape, lambda i: (0,) * nd)


def _mod_spec(nxt):
    return pl.BlockSpec((1, 6, D), lambda i: (i // nxt, 0, 0))


def _stat_spec(nxt):
    return pl.BlockSpec((1, 8, D), lambda i: (i // nxt, 0, 0))


def _xfer_out_shape(kind, a):
    r, c = a.shape
    shape = {"ag_blk": (N_DEV, r, c), "ag_cols": (r, N_DEV * c), "ag_rows": (N_DEV * r, c),
             "a2a_cols": (N_DEV, r, c // N_DEV), "a2a_rows": (N_DEV, r // N_DEV, c)}[kind]
    return jax.ShapeDtypeStruct(shape, a.dtype)


def _src_view(kind, ref, who):
    if kind == "a2a_cols":
        n = ref.shape[1] // N_DEV
        return ref.at[:, pl.ds(pl.multiple_of(who * n, LANES), n)]
    if kind == "a2a_rows":
        r = ref.shape[0] // N_DEV
        return ref.at[pl.ds(pl.multiple_of(who * r, 16), r), :]
    return ref


def _dst_view(kind, ref, who):
    if kind == "ag_cols":
        n = ref.shape[1] // N_DEV
        return ref.at[:, pl.ds(pl.multiple_of(who * n, LANES), n)]
    if kind == "ag_rows":
        r = ref.shape[0] // N_DEV
        return ref.at[pl.ds(pl.multiple_of(who * r, 16), r), :]
    return ref.at[who]


def _xfers(kinds, srcs, dsts, send_sems, recv_sems, local_sems, start):
    mx, my, mc = lax.axis_index("x"), lax.axis_index("y"), lax.axis_index("c")
    me = 4 * mx + 2 * my + mc
    peers = []
    for k in range(1, N_DEV):
        px, py, pc = mx ^ ((k >> 2) & 1), my ^ ((k >> 1) & 1), mc ^ (k & 1)
        peers.append(((px, py, pc), 4 * px + 2 * py + pc))
    for j, (kind, src, dst) in enumerate(zip(kinds, srcs, dsts)):
        own = pltpu.make_async_copy(_src_view(kind, src, me), _dst_view(kind, dst, me), local_sems.at[j])
        if start:
            own.start()
        for k, (pid, pidx) in enumerate(peers):
            sem = (N_DEV - 1) * j + k
            cp = pltpu.make_async_remote_copy(
                src_ref=_src_view(kind, src, pidx), dst_ref=_dst_view(kind, dst, me if start else pidx),
                send_sem=send_sems.at[sem], recv_sem=recv_sems.at[sem], device_id=pid, device_id_type=MESH)
            if start:
                cp.start()
            else:
                cp.wait_recv()
                cp.wait_send()
        if not start:
            own.wait()


def _xfer_sems(n):
    return [pltpu.SemaphoreType.DMA(((N_DEV - 1) * n,)), pltpu.SemaphoreType.DMA(((N_DEV - 1) * n,)),
            pltpu.SemaphoreType.DMA((n,))]


def _exchange(comm, name):
    kinds = [k for k, _ in comm]
    n = len(comm)

    def body(*refs):
        sems = refs[2 * n:]
        _xfers(kinds, refs[:n], refs[n:2 * n], *sems, start=True)
        _xfers(kinds, refs[:n], refs[n:2 * n], *sems, start=False)

    hbm = pl.BlockSpec(memory_space=pl.ANY)
    return pl.pallas_call(
        body, name=name, in_specs=[hbm] * n, out_specs=[hbm] * n,
        out_shape=[_xfer_out_shape(k, a) for k, a in comm], scratch_shapes=_xfer_sems(n),
    )(*[a for _, a in comm])


_HBM = pl.BlockSpec(memory_space=pltpu.HBM)
_SEM = pl.BlockSpec(memory_space=pltpu.SEMAPHORE)
_EFFECT = pltpu.SideEffectType.DATAFLOW_SIDE_EFFECTING


def _mesh_peers():
    mx, my, mc = lax.axis_index("x"), lax.axis_index("y"), lax.axis_index("c")
    peers = []
    for k in range(1, N_DEV):
        px, py, pc = mx ^ ((k >> 2) & 1), my ^ ((k >> 1) & 1), mc ^ (k & 1)
        peers.append(((px, py, pc), 4 * px + 2 * py + pc))
    return 4 * mx + 2 * my + mc, peers


def _xfer_start(comm, name):
    kinds = [k for k, _ in comm]
    n = len(comm)
    srcs = [pltpu.with_memory_space_constraint(a, pltpu.HBM) for _, a in comm]
    lands = []
    for k, a in comm:
        sd = _xfer_out_shape(k, a)
        lands.append(pltpu.with_memory_space_constraint(lax.empty(sd.shape, sd.dtype), pltpu.HBM))

    def body(*refs):
        src_refs, land_refs = refs[:n], refs[n:2 * n]
        send_sems, recv_sems, local_sems, token = refs[2 * n], refs[2 * n + 1], refs[2 * n + 2], refs[-1]
        my, peers = _mesh_peers()
        for j, kind in enumerate(kinds):
            pltpu.make_async_copy(_src_view(kind, src_refs[j], my), _dst_view(kind, land_refs[j], my),
                                  local_sems.at[j]).start()
            for k, (pid, pidx) in enumerate(peers):
                sem = (N_DEV - 1) * j + k
                pltpu.make_async_remote_copy(
                    src_ref=_src_view(kind, src_refs[j], pidx), dst_ref=_dst_view(kind, land_refs[j], my),
                    send_sem=send_sems.at[sem], recv_sem=recv_sems.at[sem], device_id=pid,
                    device_id_type=MESH).start()
        token[...] = jnp.zeros_like(token)

    sems = pltpu.SemaphoreType.DMA(((N_DEV - 1) * n,))
    outs = pl.pallas_call(
        body, name=name,
        out_shape=(sems, sems, pltpu.SemaphoreType.DMA((n,)), *[pltpu.HBM(a.shape, a.dtype) for a in srcs + lands],
                   jax.ShapeDtypeStruct((8, LANES), F32)),
        in_specs=[_HBM] * (2 * n),
        out_specs=(_SEM, _SEM, _SEM, *[_HBM] * (2 * n), pl.BlockSpec(memory_space=pltpu.VMEM)),
        input_output_aliases={k: 3 + k for k in range(2 * n)},
        compiler_params=pltpu.CompilerParams(has_side_effects=_EFFECT),
    )(*srcs, *lands)
    return (kinds, outs[0], outs[1], outs[2], list(outs[3:3 + 2 * n])), outs[-1]


def _xfer_wait(handle, after, name):
    kinds, send_sems, recv_sems, local_sems, bufs = handle
    n = len(kinds)

    def body(*refs):
        src_refs, land_refs = refs[:n], refs[n:2 * n]
        send_ref, recv_ref, local_ref = refs[2 * n], refs[2 * n + 1], refs[2 * n + 2]
        my, peers = _mesh_peers()
        for j, kind in enumerate(kinds):
            pltpu.make_async_copy(_src_view(kind, src_refs[j], my), _dst_view(kind, land_refs[j], my),
                                  local_ref.at[j]).wait()
            for k, (pid, pidx) in enumerate(peers):
                sem = (N_DEV - 1) * j + k
                cp = pltpu.make_async_remote_copy(
                    src_ref=_src_view(kind, src_refs[j], pidx), dst_ref=_dst_view(kind, land_refs[j], pidx),
                    send_sem=send_ref.at[sem], recv_sem=recv_ref.at[sem], device_id=pid, device_id_type=MESH)
                cp.wait_send()
                cp.wait_recv()

    outs = pl.pallas_call(
        body, name=name, out_shape=[pltpu.HBM(a.shape, a.dtype) for a in bufs],
        in_specs=[_HBM] * (2 * n) + [_SEM, _SEM, _SEM, pl.BlockSpec(memory_space=pl.ANY)],
        out_specs=[_HBM] * (2 * n), input_output_aliases={k: k for k in range(2 * n)},
        compiler_params=pltpu.CompilerParams(has_side_effects=_EFFECT),
    )(*bufs, send_sems, recv_sems, local_sems, after)
    return list(outs[n:])


def _call(body, *, grid, in_specs, out_specs, out_shape, args, name, sem, scratch=(), comm=()):
    n_in, n_out, n_scr, n_c = len(in_specs), len(out_specs), len(scratch), len(comm)
    if not comm:
        outs = pl.pallas_call(body, grid=grid, in_specs=list(in_specs), out_specs=list(out_specs),
                              out_shape=list(out_shape), scratch_shapes=list(scratch), name=name,
                              compiler_params=_params(sem))(*args)
        return list(outs), []
    kinds = [k for k, _ in comm]
    n_steps = grid[0]

    def wrapped(*refs):
        ins, csrc = refs[:n_in], refs[n_in:n_in + n_c]
        o0 = n_in + n_c
        outs, cdst = refs[o0:o0 + n_out], refs[o0 + n_out:o0 + n_out + n_c]
        s0 = o0 + n_out + n_c
        scr, sems = refs[s0:s0 + n_scr], refs[s0 + n_scr:]
        i = pl.program_id(0)

        @pl.when(i == 0)
        def _():
            _xfers(kinds, csrc, cdst, *sems, start=True)

        body(*ins, *outs, *scr)

        @pl.when(i == n_steps - 1)
        def _():
            _xfers(kinds, csrc, cdst, *sems, start=False)

    hbm = pl.BlockSpec(memory_space=pl.ANY)
    outs = pl.pallas_call(
        wrapped, grid=grid, in_specs=list(in_specs) + [hbm] * n_c, out_specs=list(out_specs) + [hbm] * n_c,
        out_shape=list(out_shape) + [_xfer_out_shape(k, a) for k, a in comm],
        scratch_shapes=list(scratch) + _xfer_sems(n_c), name=name, compiler_params=_params(("arbitrary",)),
    )(*args, *[a for _, a in comm])
    return list(outs[:n_out]), list(outs[n_out:])


def _norm_mod_fwd(X, nw, mods, si, nxt, name):
    T = X.shape[0]

    def body(x_ref, w_ref, m_ref, h_ref):
        x = x_ref[...]
        r = lax.rsqrt(_rowmean(x * x) + EPS)
        m = m_ref[0]
        y = x * r * w_ref[...]
        h_ref[...] = (y * (1.0 + m[si + 1:si + 2, :]) + m[si:si + 1, :]).astype(BF16)

    return pl.pallas_call(
        body, grid=(T // ROW_TILE,), name=name,
        in_specs=[_row(D), _const((1, D)), _mod_spec(nxt)],
        out_specs=_row(D), out_shape=jax.ShapeDtypeStruct((T, D), BF16),
        compiler_params=_params(("parallel",)),
    )(X, nw, mods)


def _norm_mod_bwd(dh, X, nw, mods, dres, si, nxt, name):
    T = X.shape[0]

    def body(dh_ref, x_ref, w_ref, m_ref, dres_ref, dx_ref, st_ref):
        i = pl.program_id(0)

        @pl.when((i == 0) | (i == nxt))
        def _():
            st_ref[...] = jnp.zeros_like(st_ref)

        x = x_ref[...]
        r = lax.rsqrt(_rowmean(x * x) + EPS)
        xn = x * r
        w = w_ref[...]
        scale = m_ref[0][si + 1:si + 2, :]
        dhv = dh_ref[...]
        dy = dhv * (1.0 + scale)
        dxn = dy * w
        dx_ref[...] = dres_ref[...] + r * (dxn - xn * _rowmean(dxn * xn))
        st_ref[0, 0:1, :] += _colsum(dhv)
        st_ref[0, 1:2, :] += _colsum(dhv * xn * w)
        st_ref[0, 2:3, :] += _colsum(dy * xn)

    return pl.pallas_call(
        body, grid=(T // ROW_TILE,), name=name,
        in_specs=[_row(D), _row(D), _const((1, D)), _mod_spec(nxt), _row(D)],
        out_specs=[_row(D), _stat_spec(nxt)],
        out_shape=[jax.ShapeDtypeStruct((T, D), F32), jax.ShapeDtypeStruct((2, 8, D), F32)],
        compiler_params=_params(("arbitrary",)),
    )(dh, X, nw, mods, dres)


def _norm_mod_resid_bwd(dh, X, nw, mods, dres, si, y, gmods, gi, nxt, name):
    T = X.shape[0]

    def body(dh_ref, x_ref, w_ref, m_ref, dres_ref, y_ref, gm_ref, dx_ref, st_ref, dy_ref):
        i = pl.program_id(0)

        @pl.when((i == 0) | (i == nxt))
        def _():
            st_ref[...] = jnp.zeros_like(st_ref)

        x = x_ref[...]
        r = lax.rsqrt(_rowmean(x * x) + EPS)
        xn = x * r
        w = w_ref[...]
        scale = m_ref[0][si + 1:si + 2, :]
        dhv = dh_ref[...]
        dyn = dhv * (1.0 + scale)
        dxn = dyn * w
        dx = dres_ref[...] + r * (dxn - xn * _rowmean(dxn * xn))
        dx_ref[...] = dx
        dy_ref[...] = (dx * gm_ref[0][gi:gi + 1, :]).astype(BF16)
        st_ref[0, 0:1, :] += _colsum(dhv)
        st_ref[0, 1:2, :] += _colsum(dhv * xn * w)
        st_ref[0, 2:3, :] += _colsum(dyn * xn)
        st_ref[0, 3:4, :] += _colsum(dx * y_ref[...].astype(F32))

    return pl.pallas_call(
        body, grid=(T // ROW_TILE,), name=name,
        in_specs=[_row(D), _row(D), _const((1, D)), _mod_spec(nxt), _row(D), _row(D), _mod_spec(nxt)],
        out_specs=[_row(D), _stat_spec(nxt), _row(D)],
        out_shape=[jax.ShapeDtypeStruct((T, D), F32), jax.ShapeDtypeStruct((2, 8, D), F32),
                   jax.ShapeDtypeStruct((T, D), BF16)],
        compiler_params=_params(("arbitrary",)),
    )(dh, X, nw, mods, dres, y, gmods)


def _resid_bwd(dX, y, mods, gi, nxt, name):
    T = dX.shape[0]

    def body(dx_ref, y_ref, m_ref, dy_ref, st_ref):
        i = pl.program_id(0)

        @pl.when((i == 0) | (i == nxt))
        def _():
            st_ref[...] = jnp.zeros_like(st_ref)

        dx = dx_ref[...]
        dy_ref[...] = (dx * m_ref[0][gi:gi + 1, :]).astype(BF16)
        st_ref[0, 0:1, :] += _colsum(dx * y_ref[...].astype(F32))

    return pl.pallas_call(
        body, grid=(T // ROW_TILE,), name=name,
        in_specs=[_row(D), _row(D), _mod_spec(nxt)],
        out_specs=[_row(D), _stat_spec(nxt)],
        out_shape=[jax.ShapeDtypeStruct((T, D), BF16), jax.ShapeDtypeStruct((2, 8, D), F32)],
        compiler_params=_params(("arbitrary",)),
    )(dX, y, mods)


def _loss_head(X, tgt, fw, nxt, name):
    T = X.shape[0]

    def body(x_ref, t_ref, w_ref, dx_ref, st_ref):
        i = pl.program_id(0)

        @pl.when(i == 0)
        def _():
            st_ref[...] = jnp.zeros_like(st_ref)

        @pl.when(i < nxt)
        def _():
            x = x_ref[...]
            r = lax.rsqrt(_rowmean(x * x) + EPS)
            xn = x * r
            w = w_ref[...]
            err = xn * w - t_ref[...]
            dy = err * (1.0 / D)
            dxn = dy * w
            dx_ref[...] = r * (dxn - xn * _rowmean(dxn * xn))
            st_ref[0:1, :] += _colsum(dy * xn)
            st_ref[1:2, :] += (0.5 / D) * _colsum(err * err)

        @pl.when(i >= nxt)
        def _():
            dx_ref[...] = jnp.zeros_like(dx_ref)

    return pl.pallas_call(
        body, grid=(T // ROW_TILE,), name=name,
        in_specs=[_row(D), pl.BlockSpec((ROW_TILE, D), lambda i: (jnp.minimum(i, nxt - 1), 0)), _const((1, D))],
        out_specs=[_row(D), _const((8, D))],
        out_shape=[jax.ShapeDtypeStruct((T, D), F32), jax.ShapeDtypeStruct((8, D), F32)],
        compiler_params=_params(("arbitrary",)),
    )(X, tgt, fw)


def _mm_nn(A, B, out_dtype, name, add=None, comm=()):
    T, K = A.shape
    N = B.shape[1]

    tm = _mm_tile(T, MM_ROWS)
    if add is None:
        def body(a_ref, b_ref, o_ref):
            o_ref[...] = _dot(a_ref[...], b_ref[...]).astype(out_dtype)
        ins, specs = (A, B), [_row(K, tm), _const((K, N))]
    else:
        def body(a_ref, b_ref, c_ref, o_ref):
            o_ref[...] = (c_ref[...] + _dot(a_ref[...], b_ref[...])).astype(out_dtype)
        ins, specs = (A, B, add), [_row(K, tm), _const((K, N)), _row(N, tm)]

    return _call(body, grid=(T // tm,), name=name, in_specs=specs,
                 out_specs=[_row(N, tm)], out_shape=[jax.ShapeDtypeStruct((T, N), out_dtype)],
                 sem=("parallel",), args=ins, comm=comm)


def _mm_nn_resid(A, B, X, mods, gi, seq, name, comm=()):
    T, K = A.shape
    N = B.shape[1]
    tm = _mm_tile(T, MM_ROWS)

    def body(a_ref, b_ref, x_ref, m_ref, xo_ref, y_ref):
        acc = _dot(a_ref[...], b_ref[...])
        y_ref[...] = acc.astype(BF16)
        row = pl.program_id(0) * tm + lax.broadcasted_iota(jnp.int32, (tm, 1), 0)
        gate = jnp.where(row >= seq, m_ref[1][gi:gi + 1, :], m_ref[0][gi:gi + 1, :])
        xo_ref[...] = x_ref[...] + gate * acc

    return _call(body, grid=(T // tm,), name=name,
                 in_specs=[_row(K, tm), _const((K, N)), _row(N, tm), _const((2, 6, D))],
                 out_specs=[_row(N, tm), _row(N, tm)],
                 out_shape=[jax.ShapeDtypeStruct((T, N), F32), jax.ShapeDtypeStruct((T, N), BF16)],
                 sem=("parallel",), args=(A, B, X, mods), comm=comm)


def _mm_nt(A, B, name, comm=()):
    T, N = A.shape
    K = B.shape[0]
    tm = _mm_tile(T, MM_ROWS)

    def body(a_ref, b_ref, o_ref):
        o_ref[...] = _dot_nt(a_ref[...], b_ref[...])

    return _call(body, grid=(T // tm,), name=name, in_specs=[_row(N, tm), _const((K, N))], out_specs=[_row(K, tm)],
                 out_shape=[jax.ShapeDtypeStruct((T, K), F32)], sem=("parallel",), args=(A, B), comm=comm)


def _mm_tn(A, G, name):
    T, K = A.shape
    N = G.shape[1]
    wide = max(K, N)
    blk = next(b for b in (768, 512) if wide % b == 0)

    def body(a_ref, g_ref, o_ref):
        o_ref[...] = _dot_tn(a_ref[...], g_ref[...]).astype(BF16)

    whole = lambda cols: pl.BlockSpec((T, cols), lambda j: (0, 0))
    cols = lambda: pl.BlockSpec((T, blk), lambda j: (0, j))
    if N >= K:
        in_specs, out_spec = [whole(K), cols()], pl.BlockSpec((K, blk), lambda j: (0, j))
    else:
        in_specs, out_spec = [cols(), whole(N)], pl.BlockSpec((blk, N), lambda j: (j, 0))
    return pl.pallas_call(
        body, grid=(wide // blk,), name=name, in_specs=in_specs, out_specs=out_spec,
        out_shape=jax.ShapeDtypeStruct((K, N), BF16), compiler_params=_params(("parallel",)),
    )(A, G)


def _ffn_up(h, WgT, WuT, name, comm=()):
    T = h.shape[0]
    F = WgT.shape[0]

    def body(h_ref, wg_ref, wu_ref, gp_ref, up_ref, act_ref):
        hv = h_ref[...]
        gp = _dot_nt(hv, wg_ref[...])
        up = _dot_nt(hv, wu_ref[...])
        gp_ref[...] = gp.astype(BF16)
        up_ref[...] = up.astype(BF16)
        act_ref[...] = (gp * _sigmoid(gp) * up).astype(BF16)

    sd = jax.ShapeDtypeStruct((T, F), BF16)
    return _call(body, grid=(T // ROW_TILE,), name=name, in_specs=[_row(D), _const((F, D)), _const((F, D))],
                 out_specs=[_row(F), _row(F), _row(F)], out_shape=[sd, sd, sd], sem=("parallel",),
                 args=(h, WgT, WuT), comm=comm)


def _ffn_dact(dy, Wd, gp, up, name, comm=()):
    T = dy.shape[0]
    F = Wd.shape[0]

    def body(dy_ref, wd_ref, gp_ref, up_ref, dgp_ref, dup_ref):
        dact = _dot_nt(dy_ref[...], wd_ref[...])
        gpv = gp_ref[...].astype(F32)
        upv = up_ref[...].astype(F32)
        sg = _sigmoid(gpv)
        dup_ref[...] = (dact * gpv * sg).astype(BF16)
        dgp_ref[...] = (dact * upv * sg * (1.0 + gpv * (1.0 - sg))).astype(BF16)

    sd = jax.ShapeDtypeStruct((T, F), BF16)
    return _call(body, grid=(T // ROW_TILE,), name=name, in_specs=[_row(D), _const((F, D)), _row(F), _row(F)],
                 out_specs=[_row(F), _row(F)], out_shape=[sd, sd], sem=("parallel",),
                 args=(dy, Wd, gp, up), comm=comm)


N_TAB = 7


def _ret_tables(tab_ref, lgf, lgb):
    r = lax.broadcasted_iota(jnp.int32, (CHUNK, CHUNK), 0).astype(F32)
    c = lax.broadcasted_iota(jnp.int32, (CHUNK, CHUNK), 1).astype(F32)
    for d, lg in ((0, lgf), (1, lgb)):
        if d == 0:
            mask, expo, xe, ze = r >= c, r - c, r + 1.0, (CHUNK - 1.0) - r
        else:
            mask, expo, xe, ze = c > r, c - r - 1.0, (CHUNK - 1.0) - r, r
        e = jnp.where(mask, expo, 0.0)
        tab_ref[N_TAB * d + 0] = jnp.where(mask, jnp.exp(lg * e), 0.0)
        tab_ref[N_TAB * d + 1] = jnp.exp(lg * xe)
        tab_ref[N_TAB * d + 2] = jnp.exp(lg * ze)
        tab_ref[N_TAB * d + 3] = jnp.exp(jnp.full((CHUNK, CHUNK), lg * float(CHUNK), F32))
        tab_ref[N_TAB * d + 4] = e
        tab_ref[N_TAB * d + 5] = xe
        tab_ref[N_TAB * d + 6] = ze


def _rope(t, cosf, sgn):
    return t * cosf + pltpu.roll(t, HEAD_DIM // 2, 1) * sgn


def _rope_t(d, cosf, sgn):
    return d * cosf + pltpu.roll(d * sgn, HEAD_DIM // 2, 1)


def _chunk_of(d, j, nc, ncx):
    return lax.rem(j + ncx, nc) if d == 0 else nc - 1 - j


def _head_spec(T, col0):
    return pl.BlockSpec((T, HEAD_DIM), lambda h: (0, col0 + h))


def _ret_fwd(p, lg, cosf, sgn, seq, name, comm=()):
    T = p.shape[0]
    nc, ncx = T // CHUNK, seq // CHUNK

    def body(lg_ref, q_ref, k_ref, v_ref, g_ref, cos_ref, sgn_ref, y_ref, ro_ref, tab_ref, of_ref):
        h = pl.program_id(0)
        _ret_tables(tab_ref, lg_ref[0, h], lg_ref[1, h])

        def load(cidx):
            rows = pl.ds(pl.multiple_of(cidx * CHUNK, CHUNK), CHUNK)
            cs, sn = cos_ref[rows, :], sgn_ref[rows, :]
            return rows, _rope(q_ref[rows, :], cs, sn) * Q_SCALE, _rope(k_ref[rows, :], cs, sn), v_ref[rows, :]

        def chunk(d, q, k, v, S):
            b = N_TAB * d
            kb, vb = k.astype(BF16), v.astype(BF16)
            pm = _dot_nt(q.astype(BF16), kb) * tab_ref[b]
            o = _dot(pm.astype(BF16), vb) + _dot((q * tab_ref[b + 1]).astype(BF16), S.astype(BF16))
            return o, S * tab_ref[b + 3] + _dot_tn((k * tab_ref[b + 2]).astype(BF16), vb)

        def step(j, carry):
            Sf, Sb = carry
            rows, q, k, v = load(_chunk_of(0, j, nc, ncx))
            o, Sf = chunk(0, q, k, v, Sf)
            y_ref[rows, :] = o
            rows, q, k, v = load(_chunk_of(1, j, nc, ncx))
            o, Sb = chunk(1, q, k, v, Sb)
            of_ref[rows, :] = o
            return Sf, Sb

        zero = jnp.zeros((CHUNK, CHUNK), F32)
        lax.fori_loop(0, nc, step, (zero, zero))

        def finish(cidx, carry):
            rows = pl.ds(pl.multiple_of(cidx * CHUNK, CHUNK), CHUNK)
            y = y_ref[rows, :] + of_ref[rows, :]
            y_ref[rows, :] = y
            g = g_ref[rows, :]
            ro_ref[rows, :] = (g * _sigmoid(g) * y * lax.rsqrt(_rowmean(y * y) + EPS)).astype(BF16)
            return carry

        lax.fori_loop(0, nc, finish, 0)

    tbl = pl.BlockSpec((T, HEAD_DIM), lambda h: (0, 0))
    return _call(
        body, grid=(RET_HEADS,), name=name,
        in_specs=[pl.BlockSpec(memory_space=pltpu.SMEM), _head_spec(T, 0), _head_spec(T, 4), _head_spec(T, 8),
                  _head_spec(T, 12), tbl, tbl],
        out_specs=[_head_spec(T, 0), _head_spec(T, 0)],
        out_shape=[jax.ShapeDtypeStruct((T, RET_W), F32), jax.ShapeDtypeStruct((T, RET_W), BF16)],
        scratch=[pltpu.VMEM((2 * N_TAB, CHUNK, CHUNK), F32), pltpu.VMEM((T, HEAD_DIM), F32)],
        sem=("arbitrary",), args=(lg, p, p, p, p, cosf, sgn), comm=comm)


def _ret_gate_bwd(yret, p, dmix, name):
    T = yret.shape[0]

    def body(y_ref, g_ref, dm_ref, dy_ref, dg_ref):
        for hh in range(RET_HEADS):
            sl = slice(hh * HEAD_DIM, (hh + 1) * HEAD_DIM)
            y = y_ref[:, sl]
            r = lax.rsqrt(_rowmean(y * y) + EPS)
            yn = y * r
            g = g_ref[:, sl]
            sg = _sigmoid(g)
            dro = dm_ref[:, sl]
            dg_ref[:, sl] = (dro * yn * sg * (1.0 + g * (1.0 - sg))).astype(BF16)
            dyn = dro * g * sg
            dy_ref[:, sl] = r * (dyn - yn * _rowmean(dyn * yn))

    return pl.pallas_call(
        body, grid=(T // ROW_TILE,), name=name,
        in_specs=[_row(RET_W), pl.BlockSpec((ROW_TILE, RET_W), lambda i: (i, 3)), _row(RET_W)],
        out_specs=[_row(RET_W), _row(RET_W)],
        out_shape=[jax.ShapeDtypeStruct((T, RET_W), F32), jax.ShapeDtypeStruct((T, RET_W), BF16)],
        compiler_params=_params(("parallel",)),
    )(yret, p, dmix)


def _ret_bwd(p, lg, dy, cosf, sgn, seq, name, comm=()):
    T = p.shape[0]
    nc, ncx = T // CHUNK, seq // CHUNK

    def body(lg_ref, q_ref, k_ref, v_ref, dy_ref, cos_ref, sgn_ref, dq_ref, dk_ref, dv_ref, dlg_ref,
             tab_ref, st_ref, dqs, dks, dvs):
        h = pl.program_id(0)
        _ret_tables(tab_ref, lg_ref[0, h], lg_ref[1, h])
        dlg_ref[...] = jnp.zeros_like(dlg_ref)

        def load(cidx):
            rows = pl.ds(pl.multiple_of(cidx * CHUNK, CHUNK), CHUNK)
            cs, sn = cos_ref[rows, :], sgn_ref[rows, :]
            return rows, _rope(q_ref[rows, :], cs, sn) * Q_SCALE, _rope(k_ref[rows, :], cs, sn), v_ref[rows, :]

        def states(j, carry):
            out = []
            for d, S in enumerate(carry):
                b = N_TAB * d
                _, _, k, v = load(_chunk_of(d, j, nc, ncx))
                st_ref[d, j] = S
                out.append(S * tab_ref[b + 3] + _dot_tn((k * tab_ref[b + 2]).astype(BF16), v.astype(BF16)))
            return tuple(out)

        zero = jnp.zeros((CHUNK, CHUNK), F32)
        lax.fori_loop(0, nc, states, (zero, zero))

        def sweep_one(d, j, dS, acc):
            b = N_TAB * d
            rows, q, k, v = load(_chunk_of(d, j, nc, ncx))
            dO = dy_ref[rows, :]
            Sp = st_ref[d, j]
            qb, kb, vb, dOb = q.astype(BF16), k.astype(BF16), v.astype(BF16), dO.astype(BF16)
            Spb, dSb = Sp.astype(BF16), dS.astype(BF16)
            dmat, xi, ze, cd = tab_ref[b], tab_ref[b + 1], tab_ref[b + 2], tab_ref[b + 3]
            pm = _dot_nt(qb, kb) * dmat
            dpm = _dot_nt(dOb, vb)
            acc = acc + _colsum(dpm * pm * tab_ref[b + 4])
            dsc = (dpm * dmat).astype(BF16)
            qx = (q * xi).astype(BF16)
            kz = (k * ze).astype(BF16)
            dq = _dot(dsc, kb) + _dot_nt(dOb, Spb) * xi
            dk = _dot_tn(dsc, qb) + _dot_nt(vb, dSb) * ze
            dvst = _dot(kz, dSb)
            dv = _dot_tn(pm.astype(BF16), dOb) + dvst
            inter = _dot(qx, Spb)
            acc = acc + _colsum(dO * inter * tab_ref[b + 5])
            acc = acc + float(CHUNK) * _colsum(dS * cd * Sp) + _colsum(v * dvst * tab_ref[b + 6])
            dqs[d, rows, :] = dq.astype(BF16)
            dks[d, rows, :] = dk.astype(BF16)
            dvs[d, rows, :] = dv.astype(BF16)
            return dS * cd + _dot_tn(qx, dOb), acc

        def sweep(jj, carry):
            dSf, accf, dSb, accb = carry
            j = nc - 1 - jj
            dSf, accf = sweep_one(0, j, dSf, accf)
            dSb, accb = sweep_one(1, j, dSb, accb)
            return dSf, accf, dSb, accb

        zacc = jnp.zeros((1, CHUNK), F32)
        _, accf, _, accb = lax.fori_loop(0, nc, sweep, (zero, zacc, zero, zacc))
        dlg_ref[0, 0:1, :] = jnp.zeros((1, LANES), F32) + jnp.sum(accf)
        dlg_ref[0, 1:2, :] = jnp.zeros((1, LANES), F32) + jnp.sum(accb)

        def finish(cidx, carry):
            rows = pl.ds(pl.multiple_of(cidx * CHUNK, CHUNK), CHUNK)
            cs, sn = cos_ref[rows, :], sgn_ref[rows, :]
            both = lambda ref: ref[0, rows, :].astype(F32) + ref[1, rows, :].astype(F32)
            dq_ref[rows, :] = (_rope_t(both(dqs), cs, sn) * Q_SCALE).astype(BF16)
            dk_ref[rows, :] = _rope_t(both(dks), cs, sn).astype(BF16)
            dv_ref[rows, :] = both(dvs).astype(BF16)
            return carry

        lax.fori_loop(0, nc, finish, 0)

    tbl = pl.BlockSpec((T, HEAD_DIM), lambda h: (0, 0))
    sd = jax.ShapeDtypeStruct((T, RET_W), BF16)
    return _call(
        body, grid=(RET_HEADS,), name=name,
        in_specs=[pl.BlockSpec(memory_space=pltpu.SMEM), _head_spec(T, 0), _head_spec(T, 4), _head_spec(T, 8),
                  _head_spec(T, 0), tbl, tbl],
        out_specs=[_head_spec(T, 0), _head_spec(T, 0), _head_spec(T, 0),
                   pl.BlockSpec((1, 8, LANES), lambda h: (h, 0, 0))],
        out_shape=[sd, sd, sd, jax.ShapeDtypeStruct((RET_HEADS, 8, LANES), F32)],
        scratch=[pltpu.VMEM((2 * N_TAB, CHUNK, CHUNK), F32), pltpu.VMEM((2, nc, CHUNK, CHUNK), F32),
                 pltpu.VMEM((2, T, HEAD_DIM), BF16), pltpu.VMEM((2, T, HEAD_DIM), BF16),
                 pltpu.VMEM((2, T, HEAD_DIM), BF16)],
        sem=("arbitrary",), args=(lg, p, p, p, dy, cosf, sgn), comm=comm)


def _halo_specs(T, cols, colblock):
    per = ROW_TILE // HALO
    last = T // HALO - 1
    prv = pl.BlockSpec((HALO, cols), lambda i: (jnp.maximum(i * per - 1, 0), colblock))
    nxt = pl.BlockSpec((HALO, cols), lambda i: (jnp.minimum((i + 1) * per, last), colblock))
    return prv, nxt


def _seq_edges(i, nxt, nt):
    first = (i == 0) | (i == nxt)
    last = (i == nxt - 1) | (i == nt - 1)
    return first, last


def _fill_pad(pad_ref, prv, cur, nxt, first, last):
    pad_ref[0:HALO, :] = jnp.where(first, 0.0, prv)
    pad_ref[HALO:HALO + ROW_TILE, :] = cur
    pad_ref[HALO + ROW_TILE:2 * HALO + ROW_TILE, :] = jnp.where(last, 0.0, nxt)


SUBLANES = 8
SHIFT_ROWS = ROW_TILE + 2 * HALO - SUBLANES


def _fill_shifts(sh_ref, pad_ref):
    for b in range(SUBLANES):
        sh_ref[b] = pad_ref[b:b + SHIFT_ROWS, :]


TAP_ROWS = 32


def _window(sh_ref, o, r0=0, rows=ROW_TILE):
    b = o % SUBLANES
    return sh_ref[b, o - b + r0:o - b + r0 + rows, :]


def _tap_sum(w_ref, sh_ref, offset_of_tap, r0):
    acc = jnp.zeros((TAP_ROWS, CONV_CH), F32)
    for k in range(CONV_K):
        acc = acc + w_ref[k:k + 1, :] * _window(sh_ref, offset_of_tap(k), r0, TAP_ROWS)
    return acc


def _ln_stats(x):
    mu = _rowmean(x)
    xc = x - mu
    rs = lax.rsqrt(_rowmean(xc * xc) + EPS)
    return xc * rs, rs


def _conv_fwd(p, cw, lnw, lnb, nxt, name, comm=()):
    T = p.shape[0]
    nt = T // ROW_TILE
    a_col, g_col = 4, 5

    def body(a_ref, g_ref, ap_ref, gp_ref, an_ref, gn_ref, w_ref, lw_ref, lb_ref, cv_ref, co_ref, pad_ref, sh_ref):
        i = pl.program_id(0)
        first, last = _seq_edges(i, nxt, nt)
        glu = lambda a, g: a * _sigmoid(g)
        _fill_pad(pad_ref, glu(ap_ref[...], gp_ref[...]), glu(a_ref[...], g_ref[...]),
                  glu(an_ref[...], gn_ref[...]), first, last)
        _fill_shifts(sh_ref, pad_ref)
        for r0 in range(0, ROW_TILE, TAP_ROWS):
            cv_ref[r0:r0 + TAP_ROWS, :] = _tap_sum(w_ref, sh_ref, lambda k: k + 1, r0)
        xh, _ = _ln_stats(cv_ref[...])
        z = xh * lw_ref[...] + lb_ref[...]
        co_ref[...] = (z * _sigmoid(z)).astype(BF16)

    cur = lambda cb: pl.BlockSpec((ROW_TILE, CONV_CH), lambda i: (i, cb))
    ap, an = _halo_specs(T, CONV_CH, a_col)
    gp, gn = _halo_specs(T, CONV_CH, g_col)
    return _call(
        body, grid=(nt,), name=name,
        in_specs=[cur(a_col), cur(g_col), ap, gp, an, gn, _const((32, CONV_CH)), _const((1, CONV_CH)),
                  _const((1, CONV_CH))],
        out_specs=[_row(CONV_CH), _row(CONV_CH)],
        out_shape=[jax.ShapeDtypeStruct((T, CONV_CH), F32), jax.ShapeDtypeStruct((T, CONV_CH), BF16)],
        scratch=[pltpu.VMEM((ROW_TILE + 2 * HALO, CONV_CH), F32), pltpu.VMEM((SUBLANES, SHIFT_ROWS, CONV_CH), F32)],
        sem=("parallel",), args=(p, p, p, p, p, p, cw, lnw, lnb), comm=comm)


def _conv_bwd(p, cv, dmix, cw, lnw, lnb, nxt, name, comm=()):
    T = p.shape[0]
    nt = T // ROW_TILE
    a_col, g_col = 4, 5

    def body(a_ref, g_ref, ap_ref, gp_ref, an_ref, gn_ref, cv_ref, cvp_ref, cvn_ref, dc_ref, dcp_ref, dcn_ref,
             w_ref, lw_ref, lb_ref, dp_ref, dw_ref, dl_ref, upad_ref, dpad_ref, ush_ref, dsh_ref):
        i = pl.program_id(0)
        first, last = _seq_edges(i, nxt, nt)

        @pl.when(i == 0)
        def _():
            dw_ref[...] = jnp.zeros_like(dw_ref)
            dl_ref[...] = jnp.zeros_like(dl_ref)

        lw, lb = lw_ref[...], lb_ref[...]

        def ln_bwd(cvv, dco):
            xh, rs = _ln_stats(cvv)
            z = xh * lw + lb
            sg = _sigmoid(z)
            dz = dco * sg * (1.0 + z * (1.0 - sg))
            dxh = dz * lw
            return rs * (dxh - _rowmean(dxh) - xh * _rowmean(dxh * xh)), dz, xh

        dcv, dz, xh = ln_bwd(cv_ref[...], dc_ref[...])
        _fill_pad(dpad_ref, ln_bwd(cvp_ref[...], dcp_ref[...])[0], dcv, ln_bwd(cvn_ref[...], dcn_ref[...])[0],
                  first, last)
        a, g = a_ref[...], g_ref[...]
        sg = _sigmoid(g)
        glu = lambda av, gv: av * _sigmoid(gv)
        _fill_pad(upad_ref, glu(ap_ref[...], gp_ref[...]), a * sg, glu(an_ref[...], gn_ref[...]), first, last)
        _fill_shifts(dsh_ref, dpad_ref)
        _fill_shifts(ush_ref, upad_ref)
        for k in range(CONV_K):
            dw_ref[k:k + 1, :] += _colsum(dcv * _window(ush_ref, k + 1))
        dl_ref[0:1, :] += _colsum(dz * xh)
        dl_ref[1:2, :] += _colsum(dz)
        for r0 in range(0, ROW_TILE, TAP_ROWS):
            rs = slice(r0, r0 + TAP_ROWS)
            du = _tap_sum(w_ref, dsh_ref, lambda k: CONV_K - k, r0)
            ar, sr = a_ref[rs, :], _sigmoid(g_ref[rs, :])
            dp_ref[rs, 0:CONV_CH] = (du * sr).astype(BF16)
            dp_ref[rs, CONV_CH:2 * CONV_CH] = (du * ar * sr * (1.0 - sr)).astype(BF16)

    cur = lambda cb: pl.BlockSpec((ROW_TILE, CONV_CH), lambda i: (i, cb))
    ap, an = _halo_specs(T, CONV_CH, a_col)
    gp, gn = _halo_specs(T, CONV_CH, g_col)
    cvp, cvn = _halo_specs(T, CONV_CH, 0)
    dcp, dcn = _halo_specs(T, CONV_CH, 1)
    return _call(
        body, grid=(nt,), name=name,
        in_specs=[cur(a_col), cur(g_col), ap, gp, an, gn, cur(0), cvp, cvn, cur(1), dcp, dcn,
                  _const((32, CONV_CH)), _const((1, CONV_CH)), _const((1, CONV_CH))],
        out_specs=[_row(2 * CONV_CH), _const((32, CONV_CH)), _const((8, CONV_CH))],
        out_shape=[jax.ShapeDtypeStruct((T, 2 * CONV_CH), BF16), jax.ShapeDtypeStruct((32, CONV_CH), F32),
                   jax.ShapeDtypeStruct((8, CONV_CH), F32)],
        scratch=[pltpu.VMEM((ROW_TILE + 2 * HALO, CONV_CH), F32), pltpu.VMEM((ROW_TILE + 2 * HALO, CONV_CH), F32),
                 pltpu.VMEM((SUBLANES, SHIFT_ROWS, CONV_CH), F32), pltpu.VMEM((SUBLANES, SHIFT_ROWS, CONV_CH), F32)],
        sem=("arbitrary",), args=(p, p, p, p, p, p, cv, cv, cv, dmix, dmix, dmix, cw, lnw, lnb), comm=comm)


def _tile_positions(i, nxt, seq, ctx, offset, rows):
    is_ctx = i >= nxt
    pos0 = (i - jnp.where(is_ctx, nxt, 0)) * ROW_TILE + offset
    length = jnp.where(is_ctx, ctx, seq).astype(F32)
    pos = (pos0 + lax.broadcasted_iota(jnp.int32, (rows, 1), 0)).astype(F32)
    return pos, length


def _pool_count(pos, length, w):
    left = w // 2
    right = w - 1 - left
    return jnp.minimum(pos + right, length - 1.0) - jnp.maximum(pos - left, 0.0) + 1.0


def _gelu(x):
    return 0.5 * x * (1.0 + lax.erf(x * INV_SQRT2))


def _odd_fwd(p, pw, ps, slw, slb, sw, sbf, nxt, seq, ctx, name, comm=()):
    T = p.shape[0]
    nt = T // ROW_TILE

    def body(p_ref, pp_ref, pn_ref, pw_ref, ps_ref, lw_ref, lb_ref, sw_ref, sb_ref, o_ref, pad_ref):
        i = pl.program_id(0)
        first, last = _seq_edges(i, nxt, nt)
        _fill_pad(pad_ref, pp_ref[...], p_ref[:, 0:POOL_CH], pn_ref[...], first, last)
        pos, length = _tile_positions(i, nxt, seq, ctx, 0, ROW_TILE)
        for gi, w in enumerate(POOL_WINDOWS):
            sl = slice(gi * GROUP_CH, (gi + 1) * GROUP_CH)
            left = w // 2
            ssum = jnp.zeros((ROW_TILE, GROUP_CH), F32)
            for o in range(-left, w - left):
                ssum = ssum + pad_ref[HALO + o:HALO + o + ROW_TILE, sl]
            m = ssum / _pool_count(pos, length, w) - p_ref[:, sl]
            pre = _dot(m.astype(BF16), pw_ref[gi].astype(BF16))
            o_ref[:, sl] = (pre * ps_ref[:, sl]).astype(BF16)
        u = _gelu(p_ref[:, POOL_CH:POOL_CH + SG_CH])
        xh, _ = _ln_stats(_gelu(p_ref[:, POOL_CH + SG_CH:ODD_IN]))
        vln = xh * lw_ref[...] + lb_ref[...]
        for n in range(ROW_TILE // SG_CHUNK):
            rs = slice(n * SG_CHUNK, (n + 1) * SG_CHUNK)
            for gi in range(4):
                sl = slice(gi * GROUP_CH, (gi + 1) * GROUP_CH)
                s = _dot(sw_ref[gi].astype(BF16), vln[rs, sl].astype(BF16)) + sb_ref[:, sl]
                o_ref[rs, POOL_CH + gi * GROUP_CH:POOL_CH + (gi + 1) * GROUP_CH] = (u[rs, sl] * s).astype(BF16)

    pp, pn = _halo_specs(T, POOL_CH, 0)
    return _call(
        body, grid=(nt,), name=name,
        in_specs=[_row(ODD_IN), pp, pn, _const((4, GROUP_CH, GROUP_CH)), _const((1, POOL_CH)), _const((1, SG_CH)),
                  _const((1, SG_CH)), _const((4, SG_CHUNK, SG_CHUNK)), _const((SG_CHUNK, SG_CH))],
        out_specs=[_row(D)], out_shape=[jax.ShapeDtypeStruct((T, D), BF16)],
        scratch=[pltpu.VMEM((ROW_TILE + 2 * HALO, POOL_CH), F32)],
        sem=("parallel",), args=(p, p, p, pw, ps, slw, slb, sw, sbf), comm=comm)


def _odd_bwd(p, dmix, pw, ps, slw, slb, sw, sbf, nxt, seq, ctx, name, comm=()):
    T = p.shape[0]
    nt = T // ROW_TILE

    def body(p_ref, pp_ref, pn_ref, dm_ref, dmp_ref, dmn_ref, pw_ref, ps_ref, lw_ref, lb_ref, sw_ref, sb_ref,
             dp_ref, dpw_ref, dsw_ref, dv_ref, dsb_ref, pad_ref, dpad_ref):
        i = pl.program_id(0)
        first, last = _seq_edges(i, nxt, nt)

        @pl.when(i == 0)
        def _():
            dpw_ref[...] = jnp.zeros_like(dpw_ref)
            dsw_ref[...] = jnp.zeros_like(dsw_ref)
            dv_ref[...] = jnp.zeros_like(dv_ref)
            dsb_ref[...] = jnp.zeros_like(dsb_ref)

        _fill_pad(pad_ref, pp_ref[...], p_ref[:, 0:POOL_CH], pn_ref[...], first, last)
        pos, length = _tile_positions(i, nxt, seq, ctx, 0, ROW_TILE)
        pos_p, _ = _tile_positions(i, nxt, seq, ctx, -HALO, HALO)
        pos_n, _ = _tile_positions(i, nxt, seq, ctx, ROW_TILE, HALO)
        scale = ps_ref[...]
        for gi, w in enumerate(POOL_WINDOWS):
            sl = slice(gi * GROUP_CH, (gi + 1) * GROUP_CH)
            left = w // 2
            right = w - 1 - left
            ssum = jnp.zeros((ROW_TILE, GROUP_CH), F32)
            for o in range(-left, right + 1):
                ssum = ssum + pad_ref[HALO + o:HALO + o + ROW_TILE, sl]
            cnt = _pool_count(pos, length, w)
            m = ssum / cnt - p_ref[:, sl]
            wg = pw_ref[gi].astype(BF16)
            pre = _dot(m.astype(BF16), wg)
            dpo = dm_ref[:, sl]
            dv_ref[0:1, sl] += _colsum(dpo * pre)
            dpre = (dpo * scale[:, sl]).astype(BF16)
            dpw_ref[gi] += _dot_tn(m.astype(BF16), dpre)
            dmc = _dot_nt(dpre, wg)
            halo_dm = lambda ref, ps_: _dot_nt((ref[:, sl] * scale[:, sl]).astype(BF16), wg) / _pool_count(ps_, length, w)
            dpad_ref[0:HALO, sl] = jnp.where(first, 0.0, halo_dm(dmp_ref, pos_p))
            dpad_ref[HALO:HALO + ROW_TILE, sl] = dmc / cnt
            dpad_ref[HALO + ROW_TILE:2 * HALO + ROW_TILE, sl] = jnp.where(last, 0.0, halo_dm(dmn_ref, pos_n))
            atd = jnp.zeros((ROW_TILE, GROUP_CH), F32)
            for o in range(-right, left + 1):
                atd = atd + dpad_ref[HALO + o:HALO + o + ROW_TILE, sl]
            dp_ref[:, sl] = (atd - dmc).astype(BF16)

        pu = p_ref[:, POOL_CH:POOL_CH + SG_CH]
        pv = p_ref[:, POOL_CH + SG_CH:ODD_IN]
        u = _gelu(pu)
        xh, rs_ = _ln_stats(_gelu(pv))
        lw = lw_ref[...]
        vln = xh * lw + lb_ref[...]
        dgelu = lambda x: 0.5 * (1.0 + lax.erf(x * INV_SQRT2)) + x * jnp.exp(-0.5 * x * x) * INV_SQRT_2PI
        for n in range(ROW_TILE // SG_CHUNK):
            rs = slice(n * SG_CHUNK, (n + 1) * SG_CHUNK)
            dvl = []
            for gi in range(4):
                sl = slice(gi * GROUP_CH, (gi + 1) * GROUP_CH)
                wq = sw_ref[gi].astype(BF16)
                vb = vln[rs, sl].astype(BF16)
                s = _dot(wq, vb) + sb_ref[:, sl]
                dsg = dm_ref[rs, POOL_CH + gi * GROUP_CH:POOL_CH + (gi + 1) * GROUP_CH]
                ds = dsg * u[rs, sl]
                dsb_ref[:, sl] += ds
                dsw_ref[gi] += _dot_nt(ds.astype(BF16), vb)
                dvl.append(_dot_tn(wq, ds.astype(BF16)))
                dp_ref[rs, POOL_CH + gi * GROUP_CH:POOL_CH + (gi + 1) * GROUP_CH] = (
                    dsg * s * dgelu(pu[rs, sl])).astype(BF16)
            dvln = jnp.concatenate(dvl, axis=1)
            xhc = xh[rs, :]
            dv_ref[1:2, :] += _colsum(dvln * xhc)
            dv_ref[2:3, :] += _colsum(dvln)
            dxh = dvln * lw
            dvv = rs_[rs, :] * (dxh - _rowmean(dxh) - xhc * _rowmean(dxh * xhc))
            dp_ref[rs, POOL_CH + SG_CH:ODD_IN] = (dvv * dgelu(pv[rs, :])).astype(BF16)

    pp, pn = _halo_specs(T, POOL_CH, 0)
    dmp, dmn = _halo_specs(T, POOL_CH, 0)
    gsd = jax.ShapeDtypeStruct((4, GROUP_CH, GROUP_CH), F32)
    return _call(
        body, grid=(nt,), name=name,
        in_specs=[_row(ODD_IN), pp, pn, _row(D), dmp, dmn, _const((4, GROUP_CH, GROUP_CH)), _const((1, POOL_CH)),
                  _const((1, SG_CH)), _const((1, SG_CH)), _const((4, SG_CHUNK, SG_CHUNK)), _const((SG_CHUNK, SG_CH))],
        out_specs=[_row(ODD_IN), _const((4, GROUP_CH, GROUP_CH)), _const((4, SG_CHUNK, SG_CHUNK)), _const((8, POOL_CH)),
                   _const((SG_CHUNK, SG_CH))],
        out_shape=[jax.ShapeDtypeStruct((T, ODD_IN), BF16), gsd, gsd, jax.ShapeDtypeStruct((8, POOL_CH), F32),
                   jax.ShapeDtypeStruct((SG_CHUNK, SG_CH), F32)],
        scratch=[pltpu.VMEM((ROW_TILE + 2 * HALO, POOL_CH), F32), pltpu.VMEM((ROW_TILE + 2 * HALO, POOL_CH), F32)],
        sem=("arbitrary",), args=(p, p, p, dmix, dmix, dmix, pw, ps, slw, slb, sw, sbf), comm=comm)


def _ada_fwd(cs, aw, ab, name):
    cols = aw.shape[2]

    def body(c_ref, w_ref, b_ref, o_ref):
        c = c_ref[...]
        s = (c * _sigmoid(c)).astype(BF16)
        o_ref[0] = _dot(s, w_ref[0].astype(BF16)) + b_ref[0]

    return pl.pallas_call(
        body, grid=(DEPTH,), name=name,
        in_specs=[pl.BlockSpec((16, D), lambda i: (0, 0)), pl.BlockSpec((1, D, cols), lambda i: (i, 0, 0)),
                  pl.BlockSpec((1, 1, cols), lambda i: (i, 0, 0))],
        out_specs=pl.BlockSpec((1, 16, cols), lambda i: (i, 0, 0)),
        out_shape=jax.ShapeDtypeStruct((DEPTH, 16, cols), F32),
        compiler_params=_params(("parallel",)),
    )(cs, aw, ab)


def _ada_bwd(cs, G, aw, name):
    cols = aw.shape[2]

    def body(c_ref, g_ref, w_ref, gw_ref, cc_ref):
        i = pl.program_id(0)

        @pl.when(i == 0)
        def _():
            cc_ref[...] = jnp.zeros_like(cc_ref)

        c = c_ref[...]
        s = (c * _sigmoid(c)).astype(BF16)
        g = g_ref[0]
        dc = g[8:9, :]
        for d in range(9, 16):
            dc = dc + g[d:d + 1, :]
        row = lax.broadcasted_iota(jnp.int32, (8, cols), 0)
        dcrows = jnp.where(row == 0, dc, 0.0)
        dm = jnp.concatenate([g[0:8, :], dcrows], axis=0).astype(BF16)
        gw_ref[0] = _dot_tn(s, dm)
        cc_ref[...] += _dot_nt(dcrows.astype(BF16), w_ref[0].astype(BF16))

    return pl.pallas_call(
        body, grid=(DEPTH,), name=name,
        in_specs=[pl.BlockSpec((16, D), lambda i: (0, 0)), pl.BlockSpec((1, 16, cols), lambda i: (i, 0, 0)),
                  pl.BlockSpec((1, D, cols), lambda i: (i, 0, 0))],
        out_specs=[pl.BlockSpec((1, D, cols), lambda i: (i, 0, 0)), pl.BlockSpec((8, D), lambda i: (0, 0))],
        out_shape=[jax.ShapeDtypeStruct((DEPTH, D, cols), F32), jax.ShapeDtypeStruct((8, D), F32)],
        compiler_params=_params(("arbitrary",)),
    )(cs, G, aw)


def _row_tile_for(rows):
    for t in (512, 256, 128, 64, 32, 16, 8):
        if rows % t == 0:
            return t
    raise ValueError(f"rows={rows} is not a multiple of 8")


def _sum_devices(x, name):
    _, rows, cols = x.shape
    tr = _row_tile_for(rows)

    def body(x_ref, o_ref):
        acc = x_ref[0]
        for d in range(1, N_DEV):
            acc = acc + x_ref[d]
        o_ref[...] = acc

    return pl.pallas_call(
        body, grid=(rows // tr,), name=name,
        in_specs=[pl.BlockSpec((N_DEV, tr, cols), lambda i: (0, i, 0))],
        out_specs=pl.BlockSpec((tr, cols), lambda i: (i, 0)),
        out_shape=jax.ShapeDtypeStruct((rows, cols), F32),
        compiler_params=_params(("parallel",)),
    )(x)


def _adamw(w, g, m, v, name):
    rows, cols = w.shape
    tr = _row_tile_for(rows)

    def body(w_ref, g_ref, m_ref, v_ref, go_ref, d_ref, mo_ref, vo_ref):
        gv = g_ref[...]
        mn = ADAM_B1 * m_ref[...] + (1.0 - ADAM_B1) * gv
        vn = ADAM_B2 * v_ref[...] + (1.0 - ADAM_B2) * (gv * gv)
        go_ref[...] = gv
        mo_ref[...] = mn
        vo_ref[...] = vn
        d_ref[...] = -ADAM_LR * ((mn / ADAM_BC1) / (jnp.sqrt(vn / ADAM_BC2) + ADAM_EPS) + ADAM_WD * w_ref[...])

    blk = pl.BlockSpec((tr, cols), lambda i: (i, 0))
    sd = jax.ShapeDtypeStruct((rows, cols), F32)
    return pl.pallas_call(
        body, grid=(rows // tr,), name=name,
        in_specs=[blk, blk, blk, blk], out_specs=[blk, blk, blk, blk], out_shape=[sd, sd, sd, sd],
        compiler_params=_params(("parallel",)),
    )(w, g, m, v)


def _adamw_layers(w, pieces, m, v, name):
    L, a, b = w.shape
    bp = pieces[0].shape[2]
    tr = next(t for t in (256, 128, 64, 32, 16) if a % t == 0)

    def body(w_ref, m_ref, v_ref, *rest):
        p_refs, (go_ref, d_ref, mo_ref, vo_ref) = rest[:L], rest[L:]
        layer = pl.program_id(0)
        for k in range(L):
            @pl.when(layer == k)
            def _(k=k):
                gv = p_refs[k][0].astype(F32)
                for d in range(1, N_DEV):
                    gv = gv + p_refs[k][d].astype(F32)
                gv = gv[:, :b]
                mn = ADAM_B1 * m_ref[0] + (1.0 - ADAM_B1) * gv
                vn = ADAM_B2 * v_ref[0] + (1.0 - ADAM_B2) * (gv * gv)
                go_ref[0] = gv
                mo_ref[0] = mn
                vo_ref[0] = vn
                d_ref[0] = -ADAM_LR * ((mn / ADAM_BC1) / (jnp.sqrt(vn / ADAM_BC2) + ADAM_EPS) + ADAM_WD * w_ref[0])

    blk = pl.BlockSpec((1, tr, b), lambda l, i: (l, i, 0))
    pspecs = [pl.BlockSpec((N_DEV, tr, bp), lambda l, i, k=k: (0, jnp.where(l == k, i, 0), 0)) for k in range(L)]
    sd = jax.ShapeDtypeStruct((L, a, b), F32)
    return pl.pallas_call(
        body, grid=(L, a // tr), name=name,
        in_specs=[blk, blk, blk] + pspecs, out_specs=[blk, blk, blk, blk], out_shape=[sd, sd, sd, sd],
        compiler_params=_params(("arbitrary", "arbitrary")),
    )(w, m, v, *pieces)


def _pack_rows(shape):
    return -(-math.prod(shape) // (8 * LANES)) * 8


def _pack(arrs, row_mult):
    parts = []
    for a in arrs:
        n, rows = math.prod(a.shape), _pack_rows(a.shape)
        parts.append(jnp.pad(a.reshape(-1).astype(F32), (0, rows * LANES - n)).reshape(rows, LANES))
    total = sum(p.shape[0] for p in parts)
    if total % row_mult:
        parts.append(jnp.zeros((-total % row_mult, LANES), F32))
    return jnp.concatenate(parts, axis=0)


def _unpack(packed, shapes):
    out, r0 = [], 0
    lead = packed.shape[:-2]
    for s in shapes:
        n, rows = math.prod(s), _pack_rows(s)
        piece = packed[..., r0:r0 + rows, :].reshape(lead + (rows * LANES,))
        out.append(piece[..., :n].reshape(lead + tuple(s)))
        r0 += rows
    return out


def _rope_tables(seq, ctx):
    def angles(ps):
        parts = []
        for pvec, n in zip(ps, ROPE_PAIRS):
            freq = ROPE_BASE ** (-jnp.arange(n, dtype=F32) / n)
            parts.append(pvec[:, None] * freq[None, :])
        return jnp.concatenate(parts, axis=-1)

    rows = seq // GRID_W
    grid_r = jnp.broadcast_to(jnp.arange(rows, dtype=F32)[:, None], (rows, GRID_W)).reshape(-1)
    grid_c = jnp.broadcast_to(jnp.arange(GRID_W, dtype=F32)[None, :], (rows, GRID_W)).reshape(-1)
    zc = jnp.zeros((ctx,), F32)
    ang = jnp.concatenate([angles((jnp.full((seq,), ctx, F32), grid_r, grid_c)),
                           angles((jnp.arange(ctx, dtype=F32), zc, zc))], axis=0)
    cos, sin = jnp.cos(ang), jnp.sin(ang)
    return jnp.concatenate([cos, cos], axis=1), jnp.concatenate([-sin, sin], axis=1)


def _layer_gather(sh, even):
    return [("ag_cols" if even else "ag_rows", sh["in"]), ("ag_rows", sh["out"]), ("ag_rows", sh["gate"]),
            ("ag_rows", sh["up"]), ("ag_rows", sh["down"])]


def _sample_step(X0, tgt, mods, shards, S, seq, ctx):
    nxt = seq // ROW_TILE
    cosf, sgn = _rope_tables(seq, ctx)
    X = X0
    saved = []
    gathers = {}

    def start_gather(i, half, dep):
        comm = _layer_gather(shards[i], i % 2 == 0)
        comm = comm[:2] if half == "mix" else comm[2:]
        comm[0] = (comm[0][0], comm[0][1] + dep.astype(BF16))
        gathers[(i, half)], token = _xfer_start(comm, f"gather_{half}{i}_start")
        return token[0, 0]

    dep = mods[0, 0, 0, 0] * 0.0
    for i, half in ((0, "mix"), (0, "ffn"), (1, "mix"), (1, "ffn")):
        dep = start_gather(i, half, dep)
    w_in, w_out = _xfer_wait(gathers[(0, "mix")], mods, "gather_mix0_wait")
    for i in range(DEPTH):
        j, even = i // 2, i % 2 == 0
        t = f"l{i}_"
        last = i == DEPTH - 1
        if i in (1, 2):
            dep = w_out[0, 0].astype(F32) * 0.0
            for half in ("mix", "ffn"):
                dep = start_gather(i + 1, half, dep)
        md = mods[i] + dep
        h1 = _norm_mod_fwd(X, S["norm_w"][i, 0][None], md, 0, nxt, t + "norm1")
        if even:
            (p,), _ = _mm_nn(h1, w_in, F32, t + "proj_in")
            (yret, ro), _ = _ret_fwd(p, S["lg"][j], cosf, sgn, seq, t + "ret_fwd")
            (cv, co), _ = _conv_fwd(p, S["conv_w"][j], S["conv_ln_w"][j][None], S["conv_ln_b"][j][None], nxt,
                                    t + "conv_fwd")
            mix = jnp.concatenate([ro, co], axis=1)
            extra = (yret, cv)
        else:
            (p,), _ = _mm_nt(h1, w_in, t + "proj_in")
            sbf = jnp.repeat(S["sg_b"][j].T, GROUP_CH, axis=1)
            (mix,), _ = _odd_fwd(p, S["pool_w"][j], S["pool_scale"][j][None], S["sg_ln_w"][j][None],
                                 S["sg_ln_b"][j][None], S["sg_w"][j], sbf, nxt, seq, ctx, t + "odd_fwd")
            extra = (sbf,)
        (X1, y1), _ = _mm_nn_resid(mix, w_out, X, md, 2, seq, t + "proj_out")
        h2 = _norm_mod_fwd(X1, S["norm_w"][i, 1][None], md, 3, nxt, t + "norm2")
        wg, wu, wd = _xfer_wait(gathers[(i, "ffn")], h2, f"gather_ffn{i}_wait")
        (gp, up, act), _ = _ffn_up(h2, wg, wu, t + "ffn_up")
        (X2, y2), _ = _mm_nn_resid(act, wd, X1, md, 5, seq, t + "ffn_down")
        saved.append((X, h1, p, mix, extra, y1, X1, h2, gp, up, act, y2, (w_in, w_out, wg, wu, wd)))
        X = X2
        if not last:
            w_in, w_out = _xfer_wait(gathers[(i + 1, "mix")], X2, f"gather_mix{i + 1}_wait")

    dX, hst = _loss_head(X, tgt, S["final_norm_w"][None], nxt, "loss_head")
    loss = jnp.sum(hst[1])
    pieces = {}
    gS = {"final_norm_w": hst[0], "norm_w": [None] * DEPTH, "lg": [None] * 2, "conv_w": [None] * 2,
          "conv_ln_w": [None] * 2, "conv_ln_b": [None] * 2, "pool_w": [None] * 2, "pool_scale": [None] * 2,
          "sg_ln_w": [None] * 2, "sg_ln_b": [None] * 2, "sg_w": [None] * 2, "sg_b": [None] * 2}
    dmods = [None] * DEPTH
    in_flight = []

    def land(flights, after, tag):
        for handle, keys, name in flights:
            for key, got in zip(keys, _xfer_wait(handle, after, name + tag)):
                pieces[key] = got

    for i in reversed(range(DEPTH)):
        j, even = i // 2, i % 2 == 0
        md = mods[i]
        t = f"l{i}_"
        X_in, h1, p, mix, extra, y1, X1, h2, gp, up, act, y2, (w_in, w_out, wg, wu, wd) = saved[i]
        if i == DEPTH - 1:
            dyf, g2 = _resid_bwd(dX, y2, md, 5, nxt, t + "ffn_resid_bwd")
            g2 = g2[:, 0]
        (dgp, dup), _ = _ffn_dact(dyf, wd, gp, up, t + "ffn_dact")
        g_down = _mm_tn(act, dyf, t + "dw_down")
        g_gate = _mm_tn(dgp, h2, t + "dw_gate")
        g_up = _mm_tn(dup, h2, t + "dw_up")
        ffn_flight, token = _xfer_start([("a2a_rows", g_down), ("a2a_rows", g_gate), ("a2a_rows", g_up)],
                                        t + "scatter_ffn_start")
        (dh2,), _ = _mm_nn(dgp, wg, F32, t + "dh2_gate")
        (dh2,), _ = _mm_nn(dup, wu, F32, t + "dh2_up", add=dh2)
        dX1, s2, dym = _norm_mod_resid_bwd(dh2, X1, S["norm_w"][i, 1][None], md + token[0, 0], dX, 3, y1, md, 2,
                                           nxt, t + "norm2_bwd")
        (dmix,), _ = _mm_nt(dym, w_out, t + "dmix")
        g_out = _mm_tn(mix, dym, t + "dw_out")
        if even:
            yret, cv = extra
            dyr, dg = _ret_gate_bwd(yret, p, dmix, t + "ret_gate_bwd")
            (dq, dk, dv, dlg), _ = _ret_bwd(p, S["lg"][j], dyr, cosf, sgn, seq, t + "ret_bwd")
            early = [("ag_blk", _pack([jnp.stack(gS["pool_w"]), jnp.stack(gS["sg_w"])], 8))] if i == 0 else []
            (dpc, dcw, dln), got = _conv_bwd(
                p, cv, dmix, S["conv_w"][j], S["conv_ln_w"][j][None], S["conv_ln_b"][j][None], nxt,
                t + "conv_bwd", comm=early)
            if i == 0:
                early_all = got[0]
            dp = jnp.concatenate([dq, dk, dv, dg, dpc], axis=1)
            gS["lg"][j] = dlg[:, 0:2, 0].T
            gS["conv_w"][j], gS["conv_ln_w"][j], gS["conv_ln_b"][j] = dcw, dln[0], dln[1]
            g_in = _mm_tn(h1, dp, t + "dw_in")
            mix_flight, token = _xfer_start([("a2a_rows", g_out), ("a2a_cols", g_in)], t + "scatter_mix_start")
            (dh1,), _ = _mm_nt(dp, w_in, t + "dh1")
        else:
            (sbf,) = extra
            (dp, dpw, dsw, dvec, dsb), _ = _odd_bwd(
                p, dmix, S["pool_w"][j], S["pool_scale"][j][None], S["sg_ln_w"][j][None], S["sg_ln_b"][j][None],
                S["sg_w"][j], sbf, nxt, seq, ctx, t + "odd_bwd")
            gS["pool_w"][j], gS["sg_w"][j] = dpw, dsw
            gS["pool_scale"][j], gS["sg_ln_w"][j], gS["sg_ln_b"][j] = dvec[0], dvec[1], dvec[2]
            gS["sg_b"][j] = jnp.sum(dsb.reshape(SG_CHUNK, 4, GROUP_CH), axis=2).T
            g_in = _mm_tn(dp, h1, t + "dw_in")
            mix_flight, token = _xfer_start([("a2a_rows", g_out), ("a2a_rows", g_in)], t + "scatter_mix_start")
            (dh1,), _ = _mm_nn(dp, w_in, F32, t + "dh1")
        if i == 0:
            dX, s1 = _norm_mod_bwd(dh1, X_in, S["norm_w"][i, 0][None], md + token[0, 0], dX1, 0, nxt,
                                   t + "norm1_bwd")
            g2_below = None
        else:
            dX, s1, dyf = _norm_mod_resid_bwd(dh1, X_in, S["norm_w"][i, 0][None], md + token[0, 0], dX1, 0,
                                              saved[i - 1][11], mods[i - 1], 5, nxt, t + "norm1_bwd")
            g2_below = s1[:, 3]
        gS["norm_w"][i] = jnp.stack([s1[0, 2] + s1[1, 2], s2[0, 2] + s2[1, 2]])
        dmods[i] = jnp.stack([s1[:, 0], s1[:, 1], s2[:, 3], s2[:, 0], s2[:, 1], g2], axis=1)
        g2 = g2_below
        land(in_flight, dX, "_wait")
        in_flight = [(ffn_flight, [("down", i), ("gate", i), ("up", i)], t + "scatter_ffn"),
                     (mix_flight, [("out", i), ("in", i)], t + "scatter_mix")]
    land(in_flight[:1], dX, "_wait")
    gS = {k: (jnp.stack(v) if isinstance(v, list) else v) for k, v in gS.items()}
    return loss, dX, pieces, gS, jnp.stack(dmods), early_all, in_flight[1]


FF_SHARD = D_FF // N_DEV
FF_SHARD_PAD = 384


def kernel(x, c, ctx, c_ctx, ada_w, ada_b, norm_w, even_w_in, even_w_out, ret_decay_logit, conv_dw_w, conv_ln_w, conv_ln_b, odd_w_in, odd_w_out, pool_w, pool_scale, sg_ln_w, sg_ln_b, sg_w, sg_b, ffn_w_gate, ffn_w_up, ffn_w_down, final_norm_w, loss_target, m_c_ctx, m_ada_w, m_ada_b, m_norm_w, m_even_w_in, m_even_w_out, m_ret_decay_logit, m_conv_dw_w, m_conv_ln_w, m_conv_ln_b, m_odd_w_in, m_odd_w_out, m_pool_w, m_pool_scale, m_sg_ln_w, m_sg_ln_b, m_sg_w, m_sg_b, m_ffn_w_gate, m_ffn_w_up, m_ffn_w_down, m_final_norm_w, v_c_ctx, v_ada_w, v_ada_b, v_norm_w, v_even_w_in, v_even_w_out, v_ret_decay_logit, v_conv_dw_w, v_conv_ln_w, v_conv_ln_b, v_odd_w_in, v_odd_w_out, v_pool_w, v_pool_scale, v_sg_ln_w, v_sg_ln_b, v_sg_w, v_sg_b, v_ffn_w_gate, v_ffn_w_up, v_ffn_w_down, v_final_norm_w):
    seq, n_ctx = x.shape[1], ctx.shape[1]
    me = 4 * lax.axis_index("x") + 2 * lax.axis_index("y") + lax.axis_index("c")
    mcols = ada_w.shape[2]

    tr = lambda a: jnp.swapaxes(a, 1, 2)
    pad_rows = lambda a: jnp.pad(a, ((0, FF_SHARD_PAD - FF_SHARD), (0, 0))).astype(BF16)
    gate_t, up_t, odd_in_t = tr(ffn_w_gate), tr(ffn_w_up), tr(odd_w_in)
    shards = []
    for i in range(DEPTH):
        j, even = i // 2, i % 2 == 0
        w_in, w_out = (even_w_in[j], even_w_out[j]) if even else (odd_in_t[j], odd_w_out[j])
        shards.append({"in": w_in.astype(BF16), "out": w_out.astype(BF16), "gate": pad_rows(gate_t[i]),
                       "up": pad_rows(up_t[i]), "down": pad_rows(ffn_w_down[i])})

    small_shapes = [(D,), norm_w.shape, conv_dw_w.shape, pool_scale.shape, sg_ln_w.shape, sg_ln_b.shape]
    (sm,) = _exchange([("ag_blk", _pack([c, norm_w, conv_dw_w, pool_scale, sg_ln_w, sg_ln_b], 8))], "gather_small")
    c_all, nw_s, cw_s, ps_s, slw_s, slb_s = _unpack(sm, small_shapes)
    cat_last = lambda a: jnp.moveaxis(a, 0, -2).reshape(a.shape[1:-1] + (-1,))
    conv_w_full = cat_last(cw_s)
    S = {"norm_w": cat_last(nw_s), "lg": jax.nn.log_sigmoid(ret_decay_logit),
         "conv_w": jnp.pad(conv_w_full, ((0, 0), (0, 32 - CONV_K), (0, 0))),
         "conv_ln_w": conv_ln_w, "conv_ln_b": conv_ln_b, "pool_w": pool_w, "pool_scale": cat_last(ps_s),
         "sg_ln_w": cat_last(slw_s), "sg_ln_b": cat_last(slb_s), "sg_w": sg_w, "sg_b": sg_b,
         "final_norm_w": final_norm_w}

    cs = jnp.concatenate([c_all, c_ctx[None], jnp.zeros((7, D), F32)], axis=0)
    ab_loc = lax.dynamic_slice_in_dim(ada_b, me * mcols, mcols, axis=1)
    mod_loc = _ada_fwd(cs, ada_w, ab_loc[:, None, :], "ada_fwd")
    (mod_all,) = _exchange([("ag_blk", mod_loc.reshape(DEPTH * 16, mcols))], "gather_mod")
    mod_all = mod_all.reshape(N_DEV, DEPTH, 16, mcols).transpose(1, 2, 0, 3).reshape(DEPTH, 16, 6, D)
    mod_x = lax.dynamic_index_in_dim(mod_all, me, axis=1, keepdims=False)
    mods = jnp.stack([mod_x, mod_all[:, 8]], axis=1)

    X0 = jnp.concatenate([x[0], ctx[0]], axis=0)
    loss, dX, big, gS, dmods, early_all, late = _sample_step(X0, loss_target[0], mods, shards, S, seq, n_ctx)
    loss = lax.psum(loss, ("x", "y", "c"))
    grad_x = dX[:seq][None]

    dmod_gather, _ = _xfer_start([("ag_blk", dmods.reshape(DEPTH * 2, 6 * D))], "gather_dmod_start")

    def adam_big(key, layers, w, m, v, name):
        return _adamw_layers(w, [big[(key, l)] for l in layers], m, v, name)

    res = {}
    adam_t = lambda key, layers, w, m, v, name: [tr(o) for o in adam_big(key, layers, tr(w), tr(m), tr(v), name)]
    res["odd_w_in"] = adam_t("in", (1, 3), odd_w_in, m_odd_w_in, v_odd_w_in, "adam_odd_in")
    res["odd_w_out"] = adam_big("out", (1, 3), odd_w_out, m_odd_w_out, v_odd_w_out, "adam_odd_out")
    res["ffn_w_gate"] = adam_t("gate", (0, 1, 2, 3), ffn_w_gate, m_ffn_w_gate, v_ffn_w_gate, "adam_gate")
    (dm_all,) = _xfer_wait(dmod_gather, res["ffn_w_gate"][1], "gather_dmod_wait")
    dm_all = dm_all.reshape(N_DEV, DEPTH, 2, 6 * D)
    dm_sum = _sum_devices(dm_all.reshape(N_DEV, DEPTH * 2, 6 * D), "sum_dmod").reshape(DEPTH, 2, 6 * D)
    g_ada_b = dm_sum[:, 0] + dm_sum[:, 1]
    G = lax.dynamic_slice_in_dim(dm_all, me * mcols, mcols, axis=3).transpose(1, 2, 0, 3).reshape(DEPTH, 16, mcols)
    g_ada_w, ccp = _ada_bwd(cs, G, ada_w, "ada_bwd")
    sg_cc = _sigmoid(c_ctx)
    g_cctx_part = ccp[0] * sg_cc * (1.0 + c_ctx * (1.0 - sg_cc))

    dsig = _sigmoid(-ret_decay_logit)
    part = [g_cctx_part, gS["norm_w"], gS["lg"] * dsig, gS["conv_w"][:, :CONV_K], gS["conv_ln_w"], gS["conv_ln_b"],
            gS["pool_scale"], gS["sg_ln_w"], gS["sg_ln_b"], gS["sg_b"], gS["final_norm_w"]]
    part_shapes = [a.shape for a in part]
    part_gather, _ = _xfer_start([("ag_blk", _pack(part, 8))], "gather_small_grads_start")

    res["ffn_w_up"] = adam_t("up", (0, 1, 2, 3), ffn_w_up, m_ffn_w_up, v_ffn_w_up, "adam_up")
    (part_all,) = _xfer_wait(part_gather, res["ffn_w_up"][1], "gather_small_grads_wait")
    red = _sum_devices(part_all, "sum_small_grads")
    g_cctx, g_nw, g_rdl, g_cw, g_clw, g_clb, g_ps, g_slw, g_slb, g_sb, g_fnw = _unpack(red, part_shapes)
    g_pw, g_sw = _unpack(_sum_devices(early_all, "sum_early_grads"), [pool_w.shape, sg_w.shape])
    mine_last = lambda a, n: lax.dynamic_slice_in_dim(a, me * n, n, axis=a.ndim - 1)
    g_nw, g_cw = mine_last(g_nw, norm_w.shape[2]), mine_last(g_cw, conv_dw_w.shape[2])
    g_ps, g_slw, g_slb = (mine_last(a, pool_scale.shape[1]) for a in (g_ps, g_slw, g_slb))
    res["ffn_w_down"] = adam_big("down", (0, 1, 2, 3), ffn_w_down, m_ffn_w_down, v_ffn_w_down, "adam_down")
    late_handle, late_keys, late_name = late
    for key, got in zip(late_keys, _xfer_wait(late_handle, res["ffn_w_down"][1], late_name + "_wait")):
        big[key] = got
    res["even_w_in"] = adam_big("in", (0, 2), even_w_in, m_even_w_in, v_even_w_in, "adam_even_in")
    res["even_w_out"] = adam_big("out", (0, 2), even_w_out, m_even_w_out, v_even_w_out, "adam_even_out")
    flat2 = lambda a: a.reshape(-1, a.shape[-1])
    res["ada_w"] = [o.reshape(ada_w.shape) for o in
                    _adamw(flat2(ada_w), flat2(g_ada_w), flat2(m_ada_w), flat2(v_ada_w), "adam_ada_w")]

    names = ["c_ctx", "ada_b", "norm_w", "ret_decay_logit", "conv_dw_w", "conv_ln_w", "conv_ln_b", "pool_w",
             "pool_scale", "sg_ln_w", "sg_ln_b", "sg_w", "sg_b", "final_norm_w"]
    ws = [c_ctx, ada_b, norm_w, ret_decay_logit, conv_dw_w, conv_ln_w, conv_ln_b, pool_w, pool_scale, sg_ln_w,
          sg_ln_b, sg_w, sg_b, final_norm_w]
    gs = [g_cctx, g_ada_b, g_nw, g_rdl, g_cw, g_clw, g_clb, g_pw, g_ps, g_slw, g_slb, g_sw, g_sb, g_fnw]
    ms = [m_c_ctx, m_ada_b, m_norm_w, m_ret_decay_logit, m_conv_dw_w, m_conv_ln_w, m_conv_ln_b, m_pool_w,
          m_pool_scale, m_sg_ln_w, m_sg_ln_b, m_sg_w, m_sg_b, m_final_norm_w]
    vs = [v_c_ctx, v_ada_b, v_norm_w, v_ret_decay_logit, v_conv_dw_w, v_conv_ln_w, v_conv_ln_b, v_pool_w,
          v_pool_scale, v_sg_ln_w, v_sg_ln_b, v_sg_w, v_sg_b, v_final_norm_w]
    shapes = [a.shape for a in ws]
    gs = [g.reshape(s) for g, s in zip(gs, shapes)]
    pk = lambda arrs: _pack(arrs, 512)
    outs = _adamw(pk(ws), pk(gs), pk(ms), pk(vs), "adam_small")
    for k, o in enumerate(outs):
        for nm, arr in zip(names, _unpack(o, shapes)):
            res.setdefault(nm, [None] * 4)[k] = arr

    order = ["c_ctx", "ada_w", "ada_b", "norm_w", "even_w_in", "even_w_out", "ret_decay_logit", "conv_dw_w",
             "conv_ln_w", "conv_ln_b", "odd_w_in", "odd_w_out", "pool_w", "pool_scale", "sg_ln_w", "sg_ln_b",
             "sg_w", "sg_b", "ffn_w_gate", "ffn_w_up", "ffn_w_down", "final_norm_w"]
    return (loss, grad_x, *[res[n][0] for n in order], *[res[n][1] for n in order],
            *[res[n][2] for n in order], *[res[n][3] for n in order])
```
